```python
import math
import jax
import jax.numpy as jnp
from jax import lax
import numpy as np

D_MODEL = 2048
BATCH = 4
SEQ = 2048
DEPTH = 4
DEC_BATCH = 32
DEC_SEQ = 32
PAST_LEN = 2048

CHUNK = 64
D_MIX = D_MODEL
D_SSM = D_MIX // 2
D_ATTN = D_MIX - D_SSM
SSM_GROUP_CH = 16
SSM_GROUPS = D_SSM // SSM_GROUP_CH
SSM_STATE = 64
HEAD_DIM = 64
N_HEADS = D_ATTN // HEAD_DIM
KV_HEADS = 4
GQ = N_HEADS // KV_HEADS
D_KV = KV_HEADS * HEAD_DIM
WINDOW = 128
WINDOW_CHUNKS = WINDOW // CHUNK
N_BUCKETS = 32
REL_MAX_DIST = 128
D_IN = D_SSM + D_ATTN + 2 * D_KV
D_FF = 5632
N_EXPERTS = 8
TOP_K = 2
D_EXPERT = 7168
MOE_BLOCK = 128
PLE_DIM = 256
N_DENSE = (DEPTH + 1) // 2
N_MOE = DEPTH // 2
EPS = 1e-6
NEG_INF = -1e30
ATTN_SCALE = HEAD_DIM ** -0.5

kernel_name = "hymba_s5_swa_sink_moe_streaming_step"


def rms_norm(x, g):
    x32 = x.astype(jnp.float32)
    y = x32 * lax.rsqrt(jnp.mean(x32 * x32, axis=-1, keepdims=True) + EPS)
    return (y * g.astype(jnp.float32)).astype(x.dtype)


def rel_bucket(rel):
    half = N_BUCKETS // 2
    max_exact = half // 2
    n = jnp.abs(rel)
    nf = jnp.maximum(n, 1).astype(jnp.float32)
    large = max_exact + (jnp.log(nf / max_exact) / math.log(REL_MAX_DIST / max_exact) * (half - max_exact)).astype(jnp.int32)
    large = jnp.minimum(large, half - 1)
    return jnp.where(rel > 0, half, 0) + jnp.where(n < max_exact, n, large)


def rel_position_bias(n_q, n_k_past, n_k, table):
    rel = (jnp.arange(n_k) - n_k_past)[None, :] - jnp.arange(n_q)[:, None]
    bias = table.astype(jnp.float32)[rel_bucket(rel)]
    return jnp.transpose(bias, (2, 0, 1)).reshape(KV_HEADS, GQ, n_q, n_k)


def sink_softmax(s, sinks):
    sk = sinks.astype(jnp.float32).reshape(KV_HEADS, GQ, 1, 1)
    m = jnp.maximum(jnp.max(s, axis=-1, keepdims=True), sk)
    e = jnp.exp(s - m)
    return e / (jnp.sum(e, axis=-1, keepdims=True) + jnp.exp(sk - m))


def swa_prompt(q, k, v, sinks, table):
    b, l = q.shape[:2]
    nc = l // CHUNK
    band = (WINDOW_CHUNKS + 1) * CHUNK

    def band_view(t):
        tp = jnp.pad(t.astype(jnp.float32), ((0, 0), (WINDOW, 0), (0, 0), (0, 0)))
        tp = tp.reshape(b, nc + WINDOW_CHUNKS, CHUNK, KV_HEADS, HEAD_DIM)
        return jnp.concatenate([tp[:, i:i + nc] for i in range(WINDOW_CHUNKS + 1)], axis=2)

    kb, vb = band_view(k), band_view(v)
    qb = q.astype(jnp.float32).reshape(b, nc, CHUNK, KV_HEADS, GQ, HEAD_DIM)
    s = jnp.einsum("bcqkgd,bcjkd->bckgqj", qb, kb) * ATTN_SCALE
    s = s + rel_position_bias(CHUNK, WINDOW, band, table)
    key_pos = jnp.arange(nc)[:, None] * CHUNK - WINDOW + jnp.arange(band)[None, :]
    s = jnp.where((key_pos >= 0)[None, :, None, None, None, :], s, NEG_INF)
    pr = sink_softmax(s, sinks)
    o = jnp.einsum("bckgqj,bcjkd->bcqkgd", pr, vb)
    return o.reshape(b, l, D_ATTN).astype(q.dtype)


def swa_sample(q, k, v, k_cache, v_cache, sinks, table):
    b, n = q.shape[:2]
    n_past = k_cache.shape[1]
    kk = jnp.concatenate([k_cache.astype(jnp.float32), k.astype(jnp.float32)], axis=1)
    vv = jnp.concatenate([v_cache.astype(jnp.float32), v.astype(jnp.float32)], axis=1)
    qh = q.astype(jnp.float32).reshape(b, n, KV_HEADS, GQ, HEAD_DIM)
    s = jnp.einsum("bqkgd,bjkd->bkgqj", qh, kk) * ATTN_SCALE + rel_position_bias(n, n_past, n_past + n, table)
    pr = sink_softmax(s, sinks)
    o = jnp.einsum("bkgqj,bjkd->bqkgd", pr, vv)
    return o.reshape(b, n, D_ATTN).astype(q.dtype)


def s5_mixer(u, h0_re, h0_im, a_re, a_im, log_dt, b_re, b_im, c_re, c_im, d_skip, glu_w):
    b, l = u.shape[:2]
    f32 = jnp.float32
    u32 = u.astype(f32).reshape(b, l, SSM_GROUPS, SSM_GROUP_CH)
    a = lax.complex(a_re.astype(f32), a_im.astype(f32))
    dt = jnp.exp(log_dt.astype(f32))[:, None]
    a_bar = jnp.exp(a * dt)
    b_bar = ((a_bar - 1.0) / a)[:, :, None] * lax.complex(b_re.astype(f32), b_im.astype(f32))
    bu = jnp.einsum("hpg,blhg->lbhp", b_bar, u32.astype(jnp.complex64))
    a_seq = jnp.broadcast_to(a_bar, (l, 1) + a_bar.shape)

    def combine(left, right):
        a_l, b_l = left
        a_r, b_r = right
        return a_r * a_l, a_r * b_l + b_r

    a_cum, h = lax.associative_scan(combine, (a_seq, bu), axis=0)
    h = h + a_cum * lax.complex(h0_re.astype(f32), h0_im.astype(f32))[None]
    cmat = lax.complex(c_re.astype(f32), c_im.astype(f32))
    y = jnp.real(jnp.einsum("hgp,lbhp->blhg", cmat, h)) + d_skip.astype(f32).reshape(SSM_GROUPS, SSM_GROUP_CH) * u32
    z = jnp.einsum("blhg,hgo->blho", y, glu_w.astype(f32))
    out = z[..., :SSM_GROUP_CH] * jax.nn.sigmoid(z[..., SSM_GROUP_CH:])
    h_last = h[-1]
    return out.reshape(b, l, D_SSM).astype(u.dtype), jnp.real(h_last), jnp.imag(h_last)


def token_mixer(hn, w, i, h0_re, h0_im, k_cache, v_cache):
    b, l = hn.shape[:2]
    z = hn @ w["w_in"][i]
    u = z[..., :D_SSM]
    q = z[..., D_SSM:D_SSM + D_ATTN].reshape(b, l, N_HEADS, HEAD_DIM)
    k = z[..., D_SSM + D_ATTN:D_SSM + D_ATTN + D_KV].reshape(b, l, KV_HEADS, HEAD_DIM)
    v = z[..., D_SSM + D_ATTN + D_KV:].reshape(b, l, KV_HEADS, HEAD_DIM)
    y_ssm, hr, hi = s5_mixer(u, h0_re, h0_im, w["ssm_a_re"][i], w["ssm_a_im"][i], w["ssm_log_dt"][i],
                             w["ssm_b_re"][i], w["ssm_b_im"][i], w["ssm_c_re"][i], w["ssm_c_im"][i],
                             w["ssm_d"][i], w["ssm_glu_w"][i])
    if k_cache is None:
        y_att = swa_prompt(q, k, v, w["attn_sinks"][i], w["rel_bias"])
        k_rows, v_rows = k[:, -WINDOW:], v[:, -WINDOW:]
    else:
        y_att = swa_sample(q, k, v, k_cache, v_cache, w["attn_sinks"][i], w["rel_bias"])
        k_rows, v_rows = k, v
    merged = jnp.concatenate([rms_norm(y_ssm, w["norm_grp_ssm"][i]), rms_norm(y_att, w["norm_grp_attn"][i])], axis=-1)
    return merged @ w["w_out"][i], k_rows, v_rows, hr, hi


def swiglu(x, wg, wu, wd):
    return (jax.nn.silu(x @ wg) * (x @ wu)) @ wd


def moe_ffn(x, router_w, wg, wu, wd):
    b, l, d = x.shape
    xt = x.reshape(-1, d)
    n = xt.shape[0]
    logits = (xt @ router_w).astype(jnp.float32)
    top_val, top_idx = lax.top_k(logits, TOP_K)
    gates = jax.nn.softmax(top_val, axis=-1)
    e_flat = top_idx.reshape(-1)
    tok_flat = jnp.arange(n * TOP_K, dtype=jnp.int32) // TOP_K
    g_flat = gates.reshape(-1)
    order = jnp.argsort(e_flat)
    e_sorted, tok_sorted, g_sorted = e_flat[order], tok_flat[order], g_flat[order]
    counts = jnp.bincount(e_flat, length=N_EXPERTS)
    padded = (counts + MOE_BLOCK - 1) // MOE_BLOCK * MOE_BLOCK
    pad_end = jnp.cumsum(padded)
    pad_start = pad_end - padded
    start = jnp.cumsum(counts) - counts
    slot = pad_start[e_sorted] + (jnp.arange(n * TOP_K) - start[e_sorted])
    n_blocks = (n * TOP_K) // MOE_BLOCK + N_EXPERTS
    n_slots = n_blocks * MOE_BLOCK
    slot_tok = jnp.full((n_slots,), n, jnp.int32).at[slot].set(tok_sorted)
    slot_gate = jnp.zeros((n_slots,), jnp.float32).at[slot].set(g_sorted)
    block_expert = jnp.minimum(jnp.searchsorted(pad_end, jnp.arange(n_blocks) * MOE_BLOCK, side="right"), N_EXPERTS - 1)
    x_pad = jnp.concatenate([xt, jnp.zeros((1, d), xt.dtype)], axis=0)

    def run_block(args):
        toks, e = args
        return swiglu(x_pad[toks], wg[e], wu[e], wd[e])

    yb = lax.map(run_block, (slot_tok.reshape(n_blocks, MOE_BLOCK), block_expert))
    y = jnp.zeros((n + 1, d), xt.dtype).at[slot_tok].add(yb.reshape(-1, d) * slot_gate[:, None].astype(xt.dtype))
    return y[:n].reshape(b, l, d)


def run_trunk(x, p, cache_k, cache_v, st_re, st_im, w):
    b = x.shape[0]
    h = x
    ks, vs, rs, ims = [], [], [], []
    for i in range(DEPTH):
        if st_re is None:
            h0_re = jnp.zeros((b, SSM_GROUPS, SSM_STATE), jnp.float32)
            h0_im = jnp.zeros((b, SSM_GROUPS, SSM_STATE), jnp.float32)
            kc, vc = None, None
        else:
            h0_re, h0_im, kc, vc = st_re[i], st_im[i], cache_k[i], cache_v[i]
        mix, kr, vr, hr, hi = token_mixer(rms_norm(h, w["norm_mix"][i]), w, i, h0_re, h0_im, kc, vc)
        h = h + mix
        hn = rms_norm(h, w["norm_ffn"][i])
        j = i // 2
        if i % 2 == 0:
            f = swiglu(hn, w["ffn_w_gate"][j], w["ffn_w_up"][j], w["ffn_w_down"][j])
        else:
            f = moe_ffn(hn, w["router_w"][j], w["moe_w_gate"][j], w["moe_w_up"][j], w["moe_w_down"][j])
        h = h + f
        hn = rms_norm(h, w["norm_ple"][i])
        h = h + jax.nn.sigmoid(hn @ w["ple_w_gate"][i]) * (p[i] @ w["ple_w_proj"][i])
        ks.append(kr)
        vs.append(vr)
        rs.append(hr)
        ims.append(hi)
    y = rms_norm(h, w["norm_final"])
    return y, jnp.stack(ks), jnp.stack(vs), jnp.stack(rs), jnp.stack(ims)


def setup_inputs(seed: int = 0) -> dict:
    key = jax.random.key(seed)
    ks = iter(jax.random.split(key, 48))
    f32 = jnp.float32

    def nrm(shape, scale):
        return jax.random.normal(next(ks), shape, f32) * scale

    def gain(shape):
        return 1.0 + 0.02 * jax.random.normal(next(ks), shape, f32)

    cache_rows = min(WINDOW, PAST_LEN)
    n_idx = jnp.arange(SSM_STATE, dtype=f32)
    return {
        "x_prompt": nrm((BATCH, SEQ, D_MODEL), 1.0),
        "x_sample": nrm((DEC_BATCH, DEC_SEQ, D_MODEL), 1.0),
        "cache_k": nrm((DEPTH, DEC_BATCH, cache_rows, KV_HEADS, HEAD_DIM), 1.0),
        "cache_v": nrm((DEPTH, DEC_BATCH, cache_rows, KV_HEADS, HEAD_DIM), 1.0),
        "state_ssm_re": nrm((DEPTH, DEC_BATCH, SSM_GROUPS, SSM_STATE), 0.1),
        "state_ssm_im": nrm((DEPTH, DEC_BATCH, SSM_GROUPS, SSM_STATE), 0.1),
        "p_prompt": nrm((DEPTH, BATCH, SEQ, PLE_DIM), 1.0),
        "p_sample": nrm((DEPTH, DEC_BATCH, DEC_SEQ, PLE_DIM), 1.0),
        "norm_mix": gain((DEPTH, D_MODEL)),
        "w_in": nrm((DEPTH, D_MODEL, D_IN), D_MODEL ** -0.5),
        "ssm_a_re": -0.5 * (1.0 + nrm((DEPTH, SSM_GROUPS, SSM_STATE), 0.02)),
        "ssm_a_im": math.pi * n_idx + nrm((DEPTH, SSM_GROUPS, SSM_STATE), 0.02),
        "ssm_log_dt": jax.random.uniform(next(ks), (DEPTH, SSM_GROUPS), f32, math.log(1e-3), math.log(1e-1)),
        "ssm_b_re": nrm((DEPTH, SSM_GROUPS, SSM_STATE, SSM_GROUP_CH), (2 * SSM_GROUP_CH) ** -0.5),
        "ssm_b_im": nrm((DEPTH, SSM_GROUPS, SSM_STATE, SSM_GROUP_CH), (2 * SSM_GROUP_CH) ** -0.5),
        "ssm_c_re": nrm((DEPTH, SSM_GROUPS, SSM_GROUP_CH, SSM_STATE), (2 * SSM_STATE) ** -0.5),
        "ssm_c_im": nrm((DEPTH, SSM_GROUPS, SSM_GROUP_CH, SSM_STATE), (2 * SSM_STATE) ** -0.5),
        "ssm_d": nrm((DEPTH, D_SSM), 1.0),
        "ssm_glu_w": nrm((DEPTH, SSM_GROUPS, SSM_GROUP_CH, 2 * SSM_GROUP_CH), SSM_GROUP_CH ** -0.5),
        "attn_sinks": nrm((DEPTH, N_HEADS), 1.0),
        "rel_bias": nrm((N_BUCKETS, N_HEADS), 0.5),
        "norm_grp_ssm": gain((DEPTH, D_SSM)),
        "norm_grp_attn": gain((DEPTH, D_ATTN)),
        "w_out": nrm((DEPTH, D_MIX, D_MODEL), D_MIX ** -0.5),
        "norm_ffn": gain((DEPTH, D_MODEL)),
        "ffn_w_gate": nrm((N_DENSE, D_MODEL, D_FF), D_MODEL ** -0.5),
        "ffn_w_up": nrm((N_DENSE, D_MODEL, D_FF), D_MODEL ** -0.5),
        "ffn_w_down": nrm((N_DENSE, D_FF, D_MODEL), D_FF ** -0.5),
        "router_w": nrm((N_MOE, D_MODEL, N_EXPERTS), D_MODEL ** -0.5),
        "moe_w_gate": nrm((N_MOE, N_EXPERTS, D_MODEL, D_EXPERT), D_MODEL ** -0.5),
        "moe_w_up": nrm((N_MOE, N_EXPERTS, D_MODEL, D_EXPERT), D_MODEL ** -0.5),
        "moe_w_down": nrm((N_MOE, N_EXPERTS, D_EXPERT, D_MODEL), D_EXPERT ** -0.5),
        "norm_ple": gain((DEPTH, D_MODEL)),
        "ple_w_gate": nrm((DEPTH, D_MODEL, D_MODEL), D_MODEL ** -0.5),
        "ple_w_proj": nrm((DEPTH, PLE_DIM, D_MODEL), PLE_DIM ** -0.5),
        "norm_final": gain((D_MODEL,)),
    }


def reference(x_prompt, x_sample, cache_k, cache_v, state_ssm_re, state_ssm_im, p_prompt, p_sample,
              norm_mix, w_in, ssm_a_re, ssm_a_im, ssm_log_dt, ssm_b_re, ssm_b_im, ssm_c_re, ssm_c_im,
              ssm_d, ssm_glu_w, attn_sinks, rel_bias, norm_grp_ssm, norm_grp_attn, w_out, norm_ffn,
              ffn_w_gate, ffn_w_up, ffn_w_down, router_w, moe_w_gate, moe_w_up, moe_w_down,
              norm_ple, ple_w_gate, ple_w_proj, norm_final):
    w = {
        "norm_mix": norm_mix, "w_in": w_in, "ssm_a_re": ssm_a_re, "ssm_a_im": ssm_a_im,
        "ssm_log_dt": ssm_log_dt, "ssm_b_re": ssm_b_re, "ssm_b_im": ssm_b_im, "ssm_c_re": ssm_c_re,
        "ssm_c_im": ssm_c_im, "ssm_d": ssm_d, "ssm_glu_w": ssm_glu_w, "attn_sinks": attn_sinks,
        "rel_bias": rel_bias, "norm_grp_ssm": norm_grp_ssm, "norm_grp_attn": norm_grp_attn,
        "w_out": w_out, "norm_ffn": norm_ffn, "ffn_w_gate": ffn_w_gate, "ffn_w_up": ffn_w_up,
        "ffn_w_down": ffn_w_down, "router_w": router_w, "moe_w_gate": moe_w_gate, "moe_w_up": moe_w_up,
        "moe_w_down": moe_w_down, "norm_ple": norm_ple, "ple_w_gate": ple_w_gate,
        "ple_w_proj": ple_w_proj, "norm_final": norm_final,
    }
    y_prompt, prompt_k, prompt_v, prompt_ssm_re, prompt_ssm_im = run_trunk(
        x_prompt, p_prompt, None, None, None, None, w)
    y_sample, sample_k, sample_v, sample_ssm_re, sample_ssm_im = run_trunk(
        x_sample, p_sample, cache_k, cache_v, state_ssm_re, state_ssm_im, w)
    return (y_prompt, y_sample, prompt_k, prompt_v, prompt_ssm_re, prompt_ssm_im,
            sample_k, sample_v, sample_ssm_re, sample_ssm_im)
```

```python
import functools
import math

import numpy as np
import jax
import jax.numpy as jnp
from jax import lax
from jax.experimental import pallas as pl
from jax.experimental.pallas import tpu as pltpu

F32 = jnp.float32
BF16 = jnp.bfloat16
HIGHEST = lax.Precision.HIGHEST

CHUNK = 64
WINDOW = 128
HEAD_DIM = 64
KV_HEADS = 4
SSM_GROUP_CH = 16
SSM_STATE = 64
N_BUCKETS = 32
REL_MAX_DIST = 128
EPS = 1e-6
NEG_INF = -1e30
ATTN_SCALE = HEAD_DIM ** -0.5

S5_T = 16
S5_W = S5_T * SSM_GROUP_CH
MOE_TM = 256
MIB = 1024 * 1024


def _pick(n, pref, mult=8):
    best = None
    for d in range(mult, min(n, pref) + 1, mult):
        if n % d == 0:
            best = d
    return best if best is not None else n


def _params(sem, vmem_mib):
    return pltpu.CompilerParams(dimension_semantics=sem, vmem_limit_bytes=int(vmem_mib * MIB))


def _rms(x, g):
    return x * lax.rsqrt(jnp.mean(x * x, axis=-1, keepdims=True) + EPS) * g


def _norm_body(x_ref, g_ref, o_ref):
    o_ref[...] = _rms(x_ref[...], g_ref[...]).astype(o_ref.dtype)


def _rmsnorm(x, g, out_dtype):
    m, d = x.shape
    tm = _pick(m, 512)
    return pl.pallas_call(
        _norm_body,
        grid=(m // tm,),
        in_specs=[pl.BlockSpec((tm, d), lambda i: (i, 0)), pl.BlockSpec((1, d), lambda i: (0, 0))],
        out_specs=pl.BlockSpec((tm, d), lambda i: (i, 0)),
        out_shape=jax.ShapeDtypeStruct((m, d), out_dtype),
        compiler_params=_params(("arbitrary",), 40),
        name="rmsnorm",
    )(x, g.reshape(1, d))


def _norm_router_body(x_ref, g_ref, rw_ref, o_ref, idx_ref, gate_ref):
    y = _rms(x_ref[...], g_ref[...])
    o_ref[...] = y.astype(o_ref.dtype)
    logits = jnp.dot(y, rw_ref[...], precision=HIGHEST, preferred_element_type=F32)
    n_exp = logits.shape[-1]
    lane = lax.broadcasted_iota(jnp.int32, logits.shape, 1)
    m1 = jnp.max(logits, axis=-1, keepdims=True)
    i1 = jnp.min(jnp.where(logits == m1, lane, n_exp), axis=-1, keepdims=True)
    rest = jnp.where(lane == i1, -jnp.inf, logits)
    m2 = jnp.max(rest, axis=-1, keepdims=True)
    i2 = jnp.min(jnp.where(rest == m2, lane, n_exp), axis=-1, keepdims=True)
    e2 = jnp.exp(m2 - m1)
    den = 1.0 + e2
    idx_ref[...] = jnp.concatenate([i1, i2], axis=-1)
    gate_ref[...] = jnp.concatenate([1.0 / den, e2 / den], axis=-1)


def _rmsnorm_router(x, g, router_w):
    m, d = x.shape
    n_exp = router_w.shape[-1]
    tm = _pick(m, 512)
    return pl.pallas_call(
        _norm_router_body,
        grid=(m // tm,),
        in_specs=[pl.BlockSpec((tm, d), lambda i: (i, 0)),
                  pl.BlockSpec((1, d), lambda i: (0, 0)),
                  pl.BlockSpec((d, n_exp), lambda i: (0, 0))],
        out_specs=[pl.BlockSpec((tm, d), lambda i: (i, 0)),
                   pl.BlockSpec((tm, 2), lambda i: (i, 0)),
                   pl.BlockSpec((tm, 2), lambda i: (i, 0))],
        out_shape=[jax.ShapeDtypeStruct((m, d), BF16),
                   jax.ShapeDtypeStruct((m, 2), jnp.int32),
                   jax.ShapeDtypeStruct((m, 2), F32)],
        compiler_params=_params(("arbitrary",), 40),
        name="rmsnorm_router",
    )(x, g.reshape(1, d), router_w)


def _merge_norm_body(a_ref, b_ref, ga_ref, gb_ref, o_ref):
    da = a_ref.shape[-1]
    o_ref[:, :da] = _rms(a_ref[...], ga_ref[...]).astype(o_ref.dtype)
    o_ref[:, da:] = _rms(b_ref[...], gb_ref[...]).astype(o_ref.dtype)


def _merge_norm(a, b, ga, gb):
    m, da = a.shape
    db = b.shape[-1]
    tm = _pick(m, 512)
    return pl.pallas_call(
        _merge_norm_body,
        grid=(m // tm,),
        in_specs=[pl.BlockSpec((tm, da), lambda i: (i, 0)), pl.BlockSpec((tm, db), lambda i: (i, 0)),
                  pl.BlockSpec((1, da), lambda i: (0, 0)), pl.BlockSpec((1, db), lambda i: (0, 0))],
        out_specs=pl.BlockSpec((tm, da + db), lambda i: (i, 0)),
        out_shape=jax.ShapeDtypeStruct((m, da + db), BF16),
        compiler_params=_params(("arbitrary",), 40),
        name="merge_norm",
    )(a, b, ga.reshape(1, da), gb.reshape(1, db))


def _mm_body(*refs, has_res):
    if has_res:
        x_ref, w_ref, r_ref, o_ref, wb_ref = refs
    else:
        x_ref, w_ref, o_ref, wb_ref = refs

    @pl.when(pl.program_id(1) == 0)
    def _():
        wb_ref[...] = w_ref[...].astype(BF16)

    acc = jnp.dot(x_ref[...], wb_ref[...], preferred_element_type=F32)
    if has_res:
        acc = r_ref[...] + acc
    o_ref[...] = acc.astype(o_ref.dtype)


def _mm(x, w, lyr, *, tm, tn, out_dtype=F32, res=None, vmem_mib=48):
    m, k = x.shape
    n = w.shape[-1]
    tm, tn = _pick(m, tm), _pick(n, tn, 128)
    in_specs = [pl.BlockSpec((tm, k), lambda j, i: (i, 0)),
                pl.BlockSpec((None, k, tn), lambda j, i: (lyr, 0, j))]
    args = [x, w]
    if res is not None:
        in_specs.append(pl.BlockSpec((tm, tn), lambda j, i: (i, j)))
        args.append(res)
    return pl.pallas_call(
        functools.partial(_mm_body, has_res=res is not None),
        grid=(n // tn, m // tm),
        in_specs=in_specs,
        out_specs=pl.BlockSpec((tm, tn), lambda j, i: (i, j)),
        out_shape=jax.ShapeDtypeStruct((m, n), out_dtype),
        scratch_shapes=[pltpu.VMEM((k, tn), BF16)],
        compiler_params=_params(("arbitrary", "arbitrary"), vmem_mib),
        name="mm_res" if res is not None else "mm",
    )(*args)


def _swiglu_math(x, wg, wu):
    a = jnp.dot(x, wg, preferred_element_type=F32)
    b = jnp.dot(x, wu, preferred_element_type=F32)
    return (a * jax.nn.sigmoid(a) * b).astype(BF16)


def _swiglu_body(x_ref, wg_ref, wu_ref, o_ref, wgb_ref, wub_ref):
    @pl.when(pl.program_id(1) == 0)
    def _():
        wgb_ref[...] = wg_ref[...].astype(BF16)
        wub_ref[...] = wu_ref[...].astype(BF16)

    o_ref[...] = _swiglu_math(x_ref[...], wgb_ref[...], wub_ref[...])


def _swiglu(x, wg, wu, lyr, *, tm, tn, vmem_mib=48):
    m, k = x.shape
    n = wg.shape[-1]
    tm, tn = _pick(m, tm), _pick(n, tn, 128)
    wspec = pl.BlockSpec((None, k, tn), lambda j, i: (lyr, 0, j))
    return pl.pallas_call(
        _swiglu_body,
        grid=(n // tn, m // tm),
        in_specs=[pl.BlockSpec((tm, k), lambda j, i: (i, 0)), wspec, wspec],
        out_specs=pl.BlockSpec((tm, tn), lambda j, i: (i, j)),
        out_shape=jax.ShapeDtypeStruct((m, n), BF16),
        scratch_shapes=[pltpu.VMEM((k, tn), BF16), pltpu.VMEM((k, tn), BF16)],
        compiler_params=_params(("arbitrary", "arbitrary"), vmem_mib),
        name="swiglu",
    )(x, wg, wu)


def _ple_body(x_ref, p_ref, wg_ref, wp_ref, r_ref, o_ref, wgb_ref, wpb_ref):
    @pl.when(pl.program_id(1) == 0)
    def _():
        wgb_ref[...] = wg_ref[...].astype(BF16)
        wpb_ref[...] = wp_ref[...].astype(BF16)

    a = jnp.dot(x_ref[...], wgb_ref[...], preferred_element_type=F32)
    b = jnp.dot(p_ref[...].astype(BF16), wpb_ref[...], preferred_element_type=F32)
    o_ref[...] = r_ref[...] + jax.nn.sigmoid(a) * b


def _ple(x, p, wg, wp, res, lyr, *, tm, tn, vmem_mib=48):
    m, k = x.shape
    kp = p.shape[-1]
    n = wg.shape[-1]
    tm, tn = _pick(m, tm), _pick(n, tn, 128)
    return pl.pallas_call(
        _ple_body,
        grid=(n // tn, m // tm),
        in_specs=[pl.BlockSpec((tm, k), lambda j, i: (i, 0)),
                  pl.BlockSpec((None, tm, kp), lambda j, i: (lyr, i, 0)),
                  pl.BlockSpec((None, k, tn), lambda j, i: (lyr, 0, j)),
                  pl.BlockSpec((None, kp, tn), lambda j, i: (lyr, 0, j)),
                  pl.BlockSpec((tm, tn), lambda j, i: (i, j))],
        out_specs=pl.BlockSpec((tm, tn), lambda j, i: (i, j)),
        out_shape=jax.ShapeDtypeStruct((m, n), F32),
        scratch_shapes=[pltpu.VMEM((k, tn), BF16), pltpu.VMEM((kp, tn), BF16)],
        compiler_params=_params(("arbitrary", "arbitrary"), vmem_mib),
        name="ple",
    )(x, p, wg, wp, res)


def _expert_changed(e_ref, i):
    return jnp.logical_or(i == 0, e_ref[i] != e_ref[jnp.maximum(i - 1, 0)])


def _moe_up_body(e_ref, nu_ref, x_ref, wg_ref, wu_ref, o_ref, wgb_ref, wub_ref):
    i = pl.program_id(1)

    @pl.when(_expert_changed(e_ref, i))
    def _():
        wgb_ref[...] = wg_ref[...].astype(BF16)
        wub_ref[...] = wu_ref[...].astype(BF16)

    @pl.when(i < nu_ref[0])
    def _():
        o_ref[...] = _swiglu_math(x_ref[...], wgb_ref[...], wub_ref[...])

    @pl.when(i >= nu_ref[0])
    def _():
        o_ref[...] = jnp.zeros_like(o_ref)


def _moe_up(x_sorted, wg, wu, blk_expert, n_used, lyr, *, tn, vmem_mib=48):
    ns, k = x_sorted.shape
    n = wg.shape[-1]
    tm, tn = MOE_TM, _pick(n, tn, 128)
    wspec = pl.BlockSpec((None, None, k, tn), lambda j, i, e, nu: (lyr, e[i], 0, j))
    grid_spec = pltpu.PrefetchScalarGridSpec(
        num_scalar_prefetch=2,
        grid=(n // tn, ns // tm),
        in_specs=[pl.BlockSpec((tm, k), lambda j, i, e, nu: (i, 0)), wspec, wspec],
        out_specs=pl.BlockSpec((tm, tn), lambda j, i, e, nu: (i, j)),
        scratch_shapes=[pltpu.VMEM((k, tn), BF16), pltpu.VMEM((k, tn), BF16)],
    )
    return pl.pallas_call(
        _moe_up_body,
        grid_spec=grid_spec,
        out_shape=jax.ShapeDtypeStruct((ns, n), BF16),
        compiler_params=_params(("arbitrary", "arbitrary"), vmem_mib),
        name="moe_up",
    )(blk_expert, n_used, x_sorted, wg, wu)


def _moe_down_body(e_ref, nu_ref, h_ref, wd_ref, g_ref, o_ref, wdb_ref):
    i = pl.program_id(1)

    @pl.when(_expert_changed(e_ref, i))
    def _():
        wdb_ref[...] = wd_ref[...].astype(BF16)

    @pl.when(i < nu_ref[0])
    def _():
        o_ref[...] = jnp.dot(h_ref[...], wdb_ref[...], preferred_element_type=F32) * g_ref[...]

    @pl.when(i >= nu_ref[0])
    def _():
        o_ref[...] = jnp.zeros_like(o_ref)


def _moe_down(h_sorted, wd, slot_gate, blk_expert, n_used, lyr, *, tn, vmem_mib=48):
    ns, k = h_sorted.shape
    n = wd.shape[-1]
    tm, tn = MOE_TM, _pick(n, tn, 128)
    grid_spec = pltpu.PrefetchScalarGridSpec(
        num_scalar_prefetch=2,
        grid=(n // tn, ns // tm),
        in_specs=[pl.BlockSpec((tm, k), lambda j, i, e, nu: (i, 0)),
                  pl.BlockSpec((None, None, k, tn), lambda j, i, e, nu: (lyr, e[i], 0, j)),
                  pl.BlockSpec((tm, 1), lambda j, i, e, nu: (i, 0))],
        out_specs=pl.BlockSpec((tm, tn), lambda j, i, e, nu: (i, j)),
        scratch_shapes=[pltpu.VMEM((k, tn), BF16)],
    )
    return pl.pallas_call(
        _moe_down_body,
        grid_spec=grid_spec,
        out_shape=jax.ShapeDtypeStruct((ns, n), F32),
        compiler_params=_params(("arbitrary", "arbitrary"), vmem_mib),
        name="moe_down",
    )(blk_expert, n_used, h_sorted, wd, slot_gate)


def _moe_ffn(hn, e_idx, gates, wg, wu, wd, lyr):
    m, d = hn.shape
    n_exp = wg.shape[1]
    top_k = e_idx.shape[-1]
    tm = MOE_TM
    e_flat = e_idx.reshape(-1)
    g_flat = gates.reshape(-1)
    order = jnp.argsort(e_flat)
    e_sorted = e_flat[order]
    counts = jnp.bincount(e_flat, length=n_exp)
    padded = (counts + tm - 1) // tm * tm
    pad_end = jnp.cumsum(padded)
    pad_start = pad_end - padded
    start = jnp.cumsum(counts) - counts
    slot_sorted = pad_start[e_sorted] + (jnp.arange(m * top_k) - start[e_sorted])
    n_blocks = (m * top_k) // tm + n_exp
    n_slots = n_blocks * tm
    slot_of = jnp.zeros((m * top_k,), jnp.int32).at[order].set(slot_sorted.astype(jnp.int32))
    slot_tok = jnp.full((n_slots,), m, jnp.int32).at[slot_of].set(jnp.arange(m * top_k, dtype=jnp.int32) // top_k)
    slot_gate = jnp.zeros((n_slots,), F32).at[slot_of].set(g_flat)
    n_used = (pad_end[-1] // tm).astype(jnp.int32)
    blk = jnp.arange(n_blocks, dtype=jnp.int32)
    blk_expert = jnp.searchsorted(pad_end, jnp.minimum(blk, n_used - 1) * tm, side="right").astype(jnp.int32)
    blk_expert = jnp.minimum(blk_expert, n_exp - 1)

    x_pad = jnp.concatenate([hn, jnp.zeros((1, d), hn.dtype)], axis=0)
    x_sorted = x_pad[slot_tok]
    nu = n_used.reshape(1)
    h_sorted = _moe_up(x_sorted, wg, wu, blk_expert, nu, lyr, tn=512, vmem_mib=48)
    y_sorted = _moe_down(h_sorted, wd, slot_gate.reshape(n_slots, 1), blk_expert, nu, lyr, tn=256, vmem_mib=48)
    sl = slot_of.reshape(m, top_k)
    y = y_sorted[sl[:, 0]]
    for t in range(1, top_k):
        y = y + y_sorted[sl[:, t]]
    return y


def _rel_buckets(n_q, n_k_past, n_k):
    rel = (np.arange(n_k) - n_k_past)[None, :] - np.arange(n_q)[:, None]
    half = N_BUCKETS // 2
    max_exact = half // 2
    n = np.abs(rel)
    nf = np.maximum(n, 1).astype(np.float32)
    large = max_exact + (np.log(nf / np.float32(max_exact)) / np.float32(math.log(REL_MAX_DIST / max_exact))
                         * np.float32(half - max_exact)).astype(np.int32)
    large = np.minimum(large, half - 1)
    return (np.where(rel > 0, half, 0) + np.where(n < max_exact, n, large)).astype(np.int32)


def _build_bias(bias_ref, bucket_ref, table_ref):
    bucket = bucket_ref[...]
    for n in range(bias_ref.shape[0]):
        acc = jnp.zeros(bucket.shape, F32)
        for b in range(N_BUCKETS):
            acc = jnp.where(bucket == b, table_ref[b, n], acc)
        bias_ref[n] = acc


def _attn_heads(q, k, v, bias_ref, sinks_ref, key_valid):
    n_heads = q.shape[-1] // HEAD_DIM
    gq = n_heads // KV_HEADS
    outs = []
    for n in range(n_heads):
        kh = n // gq
        qn = q[:, n * HEAD_DIM:(n + 1) * HEAD_DIM]
        kn = k[:, kh * HEAD_DIM:(kh + 1) * HEAD_DIM]
        vn = v[:, kh * HEAD_DIM:(kh + 1) * HEAD_DIM]
        s = lax.dot_general(qn, kn, (((1,), (1,)), ((), ())), preferred_element_type=F32)
        s = s * ATTN_SCALE + bias_ref[n]
        if key_valid is not None:
            s = jnp.where(key_valid, s, NEG_INF)
        sink = sinks_ref[n]
        mx = jnp.maximum(jnp.max(s, axis=-1, keepdims=True), sink)
        e = jnp.exp(s - mx)
        den = jnp.sum(e, axis=-1, keepdims=True) + jnp.exp(sink - mx)
        p = (e / den).astype(BF16)
        outs.append(jnp.dot(p, vn, preferred_element_type=F32))
    return jnp.concatenate(outs, axis=-1)


def _attn_prompt_body(q_ref, kp_ref, kc_ref, vp_ref, vc_ref, bucket_ref, table_ref, sinks_ref,
                      o_ref, bias_ref, *, chunks):
    j = pl.program_id(1)

    @pl.when(jnp.logical_and(pl.program_id(0) == 0, j == 0))
    def _():
        _build_bias(bias_ref, bucket_ref, table_ref)

    q = q_ref[...].astype(BF16)
    k = jnp.concatenate([kp_ref[...], kc_ref[...]], axis=0).astype(BF16)
    v = jnp.concatenate([vp_ref[...], vc_ref[...]], axis=0).astype(BF16)
    band = WINDOW + CHUNK
    for c in range(chunks):
        lo = c * CHUNK
        key_valid = None
        if lo < WINDOW:
            key_pos = j * (chunks * CHUNK) + lo - WINDOW + lax.broadcasted_iota(jnp.int32, (1, band), 1)
            key_valid = key_pos >= 0
        o_ref[lo:lo + CHUNK, :] = _attn_heads(q[lo:lo + CHUNK], k[lo:lo + band], v[lo:lo + band],
                                              bias_ref, sinks_ref, key_valid)


def _attn_prompt(z, n_b, seq, d_ssm, d_attn, table, sinks):
    d_kv = KV_HEADS * HEAD_DIM
    assert d_ssm % d_attn == 0 and d_ssm % d_kv == 0 and d_attn % d_kv == 0
    qb = _pick(seq, 4 * CHUNK, 2 * CHUNK)
    assert qb % WINDOW == 0 and seq % qb == 0
    chunks = qb // CHUNK
    nj = seq // qb
    r = qb // WINDOW
    kcol = (d_ssm + d_attn) // d_kv
    n_heads = d_attn // HEAD_DIM
    band = WINDOW + CHUNK
    bucket = jnp.asarray(_rel_buckets(CHUNK, WINDOW, band))

    def prev(b, j):
        return b * (seq // WINDOW) + jnp.maximum(j * r - 1, 0)

    return pl.pallas_call(
        functools.partial(_attn_prompt_body, chunks=chunks),
        grid=(n_b, nj),
        in_specs=[pl.BlockSpec((qb, d_attn), lambda b, j: (b * nj + j, d_ssm // d_attn)),
                  pl.BlockSpec((WINDOW, d_kv), lambda b, j: (prev(b, j), kcol)),
                  pl.BlockSpec((qb, d_kv), lambda b, j: (b * nj + j, kcol)),
                  pl.BlockSpec((WINDOW, d_kv), lambda b, j: (prev(b, j), kcol + 1)),
                  pl.BlockSpec((qb, d_kv), lambda b, j: (b * nj + j, kcol + 1)),
                  pl.BlockSpec((CHUNK, band), lambda b, j: (0, 0)),
                  pl.BlockSpec(memory_space=pltpu.SMEM),
                  pl.BlockSpec(memory_space=pltpu.SMEM)],
        out_specs=pl.BlockSpec((qb, d_attn), lambda b, j: (b * nj + j, 0)),
        out_shape=jax.ShapeDtypeStruct((n_b * seq, d_attn), F32),
        scratch_shapes=[pltpu.VMEM((n_heads, CHUNK, band), F32)],
        compiler_params=_params(("arbitrary", "arbitrary"), 40),
        name="attn_prompt",
    )(z, z, z, z, z, bucket, table, sinks)


def _attn_sample_body(q_ref, kc_ref, kn_ref, vc_ref, vn_ref, bucket_ref, table_ref, sinks_ref,
                      o_ref, bias_ref):
    @pl.when(pl.program_id(0) == 0)
    def _():
        _build_bias(bias_ref, bucket_ref, table_ref)

    q = q_ref[...].astype(BF16)
    k = jnp.concatenate([kc_ref[...], kn_ref[...]], axis=0).astype(BF16)
    v = jnp.concatenate([vc_ref[...], vn_ref[...]], axis=0).astype(BF16)
    o_ref[...] = _attn_heads(q, k, v, bias_ref, sinks_ref, None)


def _attn_sample(z, row0, n_b, n_new, d_ssm, d_attn, cache_k, cache_v, lyr, table, sinks):
    d_kv = KV_HEADS * HEAD_DIM
    n_past = cache_k.shape[2]
    assert row0 % n_new == 0
    kcol = (d_ssm + d_attn) // d_kv
    n_heads = d_attn // HEAD_DIM
    n_k = n_past + n_new
    bucket = jnp.asarray(_rel_buckets(n_new, n_past, n_k))
    ck = cache_k.reshape(-1, d_kv)
    cv = cache_v.reshape(-1, d_kv)
    rb = row0 // n_new
    cache_spec = pl.BlockSpec((n_past, d_kv), lambda b: (lyr * n_b + b, 0))
    return pl.pallas_call(
        _attn_sample_body,
        grid=(n_b,),
        in_specs=[pl.BlockSpec((n_new, d_attn), lambda b: (rb + b, d_ssm // d_attn)),
                  cache_spec,
                  pl.BlockSpec((n_new, d_kv), lambda b: (rb + b, kcol)),
                  cache_spec,
                  pl.BlockSpec((n_new, d_kv), lambda b: (rb + b, kcol + 1)),
                  pl.BlockSpec((n_new, n_k), lambda b: (0, 0)),
                  pl.BlockSpec(memory_space=pltpu.SMEM),
                  pl.BlockSpec(memory_space=pltpu.SMEM)],
        out_specs=pl.BlockSpec((n_new, d_attn), lambda b: (b, 0)),
        out_shape=jax.ShapeDtypeStruct((n_b * n_new, d_attn), F32),
        scratch_shapes=[pltpu.VMEM((n_heads, n_new, n_k), F32)],
        compiler_params=_params(("arbitrary",), 40),
        name="attn_sample",
    )(z, ck, z, cv, z, bucket, table, sinks)


def _s5_tables_body(are_ref, aim_ref, ldt_ref, btr_ref, bti_ref, cre_ref, cim_ref, d_ref,
                    k_ref, we_ref, ws_ref, a_ref):
    g_ch, p = cre_ref.shape
    dt = jnp.exp(ldt_ref[...])
    lam_re = (are_ref[...] * dt)[None]
    lam_im = (aim_ref[...] * dt)[None]

    def powers(jj):
        mag = jnp.exp(jj * lam_re)
        return mag * jnp.cos(jj * lam_im), mag * jnp.sin(jj * lam_im)

    jf = lax.broadcasted_iota(jnp.int32, (S5_T + 1, 1, p), 0).astype(F32)
    pw_re, pw_im = powers(jf)
    rf = (S5_T - 1) - lax.broadcasted_iota(jnp.int32, (S5_T, 1, p), 0).astype(F32)
    rv_re, rv_im = powers(rf)

    a_re, a_im = are_ref[...], aim_ref[...]
    x, y = pw_re[1] - 1.0, pw_im[1]
    den = a_re * a_re + a_im * a_im
    co_re = (x * a_re + y * a_im) / den
    co_im = (y * a_re - x * a_im) / den
    bb_re = co_re * btr_ref[...] - co_im * bti_ref[...]
    bb_im = co_re * bti_ref[...] + co_im * btr_ref[...]

    c_re, c_im = cre_ref[...][None], cim_ref[...][None]
    ca_re = c_re * pw_re - c_im * pw_im
    ca_im = c_re * pw_im + c_im * pw_re

    dims = (((1,), (1,)), ((), ()))
    kt = (lax.dot_general(ca_re[:S5_T].reshape(S5_T * g_ch, p), bb_re, dims, precision=HIGHEST,
                          preferred_element_type=F32)
          - lax.dot_general(ca_im[:S5_T].reshape(S5_T * g_ch, p), bb_im, dims, precision=HIGHEST,
                            preferred_element_type=F32))
    eye = (lax.broadcasted_iota(jnp.int32, (g_ch, g_ch), 0)
           == lax.broadcasted_iota(jnp.int32, (g_ch, g_ch), 1))
    k_ref[...] = jnp.concatenate([kt[:g_ch] + jnp.where(eye, d_ref[...], 0.0), kt[g_ch:]], axis=0)

    e_re = rv_re * bb_re[None] - rv_im * bb_im[None]
    e_im = rv_re * bb_im[None] + rv_im * bb_re[None]
    we_ref[...] = jnp.concatenate([e_re.reshape(S5_T * g_ch, p), e_im.reshape(S5_T * g_ch, p)], axis=-1)
    ws_ref[...] = jnp.concatenate([ca_re[1:].reshape(S5_T * g_ch, p), -ca_im[1:].reshape(S5_T * g_ch, p)],
                                  axis=-1)
    a_ref[...] = jnp.concatenate([pw_re[S5_T], pw_im[S5_T]], axis=-1)


def _s5_tables(a_re, a_im, log_dt, b_re, b_im, c_re, c_im, d_skip):
    h, p = a_re.shape
    g = SSM_GROUP_CH
    vec = pl.BlockSpec((None, 1, p), lambda i: (i, 0, 0))
    mat = pl.BlockSpec((None, g, p), lambda i: (i, 0, 0))
    wide = pl.BlockSpec((None, S5_W, 2 * p), lambda i: (i, 0, 0))
    return pl.pallas_call(
        _s5_tables_body,
        grid=(h,),
        in_specs=[vec, vec, pl.BlockSpec((None, 1, 1), lambda i: (i, 0, 0)), mat, mat, mat, mat,
                  pl.BlockSpec((None, g, 1), lambda i: (i, 0, 0))],
        out_specs=[pl.BlockSpec((None, S5_W, g), lambda i: (i, 0, 0)), wide, wide,
                   pl.BlockSpec((None, 1, 2 * p), lambda i: (i, 0, 0))],
        out_shape=[jax.ShapeDtypeStruct((h, S5_W, g), F32),
                   jax.ShapeDtypeStruct((h, S5_W, 2 * p), F32),
                   jax.ShapeDtypeStruct((h, S5_W, 2 * p), F32),
                   jax.ShapeDtypeStruct((h, 1, 2 * p), F32)],
        compiler_params=_params(("arbitrary",), 32),
        name="s5_tables",
    )(a_re.reshape(h, 1, p), a_im.reshape(h, 1, p), log_dt.reshape(h, 1, 1),
      jnp.swapaxes(b_re, 1, 2), jnp.swapaxes(b_im, 1, 2), c_re, c_im, d_skip.reshape(h, g, 1))


def _split_mul(a, p):
    lane = lax.broadcasted_iota(jnp.int32, a.shape, a.ndim - 1)
    sw = pltpu.roll(a, p, axis=a.ndim - 1)
    return jnp.where(lane < p, a, sw), jnp.where(lane < p, -sw, a)


def _cmul(x, mr, mi, p):
    return x * mr + pltpu.roll(x, p, axis=x.ndim - 1) * mi


def _scan_chunks(e, a, nb, nc, h0, p):
    if h0 is not None or nc <= 4:
        prev = h0 if h0 is not None else jnp.zeros((nb, e.shape[-1]), F32)
        mr, mi = _split_mul(a, p)
        enter = []
        for c in range(nc):
            enter.append(prev)
            prev = _cmul(prev, mr, mi, p) + e[c * nb:(c + 1) * nb]
        return (jnp.concatenate(enter, axis=0) if nc > 1 else enter[0]), prev
    row = lax.broadcasted_iota(jnp.int32, e.shape, 0)
    x = e
    sh = 1
    while sh < nc:
        mr, mi = _split_mul(a, p)
        shifted = jnp.where(row >= sh * nb, pltpu.roll(x, sh * nb, axis=0), 0.0)
        x = x + _cmul(shifted, mr, mi, p)
        a = _cmul(a, mr, mi, p)
        sh *= 2
    enter = jnp.where(row >= nb, pltpu.roll(x, nb, axis=0), 0.0)
    return enter, x[(nc - 1) * nb:]


def _s5_body(u_ref, m_ref, ka_ref, kb_ref, we_ref, ws_ref, a_ref, h0_ref, y_ref, fp_ref, fs_ref,
             *, nb_p, nc_p, nb_s, nc_s):
    p = a_ref.shape[-1] // 2
    u = u_ref[...]
    dot = functools.partial(jnp.dot, precision=HIGHEST, preferred_element_type=F32)
    e = dot(u, we_ref[...])
    a = a_ref[...]
    rp = nb_p * nc_p
    enter_p, fin_p = _scan_chunks(e[:rp], a, nb_p, nc_p, None, p)
    enter_s, fin_s = _scan_chunks(e[rp:], a, nb_s, nc_s, h0_ref[...], p)
    fp_ref[...] = fin_p
    fs_ref[...] = fin_s
    enter = jnp.concatenate([enter_p, enter_s], axis=0)
    y = dot(u, m_ref[...]) + lax.dot_general(enter, ws_ref[...], (((1,), (1,)), ((), ())),
                                             precision=HIGHEST, preferred_element_type=F32)
    y_ref[...] = dot(y, ka_ref[...]) * jax.nn.sigmoid(dot(y, kb_ref[...]))


def _s5(u_flat, tabs, h0, nb_p, nc_p, nb_s, nc_s):
    m_toe, k_a, k_b, w_e, w_s, a_t = tabs
    h, r, w = u_flat.shape
    p2 = a_t.shape[-1]
    sq = pl.BlockSpec((None, w, w), lambda i: (i, 0, 0))
    wide = pl.BlockSpec((None, w, p2), lambda i: (i, 0, 0))
    rows = pl.BlockSpec((None, r, w), lambda i: (i, 0, 0))
    return pl.pallas_call(
        functools.partial(_s5_body, nb_p=nb_p, nc_p=nc_p, nb_s=nb_s, nc_s=nc_s),
        grid=(h,),
        in_specs=[rows, sq, sq, sq, wide, wide, pl.BlockSpec((None, 1, p2), lambda i: (i, 0, 0)),
                  pl.BlockSpec((None, nb_s, p2), lambda i: (i, 0, 0))],
        out_specs=[rows, pl.BlockSpec((None, nb_p, p2), lambda i: (i, 0, 0)),
                   pl.BlockSpec((None, nb_s, p2), lambda i: (i, 0, 0))],
        out_shape=[jax.ShapeDtypeStruct((h, r, w), F32),
                   jax.ShapeDtypeStruct((h, nb_p, p2), F32),
                   jax.ShapeDtypeStruct((h, nb_s, p2), F32)],
        compiler_params=_params(("arbitrary",), 40),
        name="s5",
    )(u_flat, m_toe, k_a, k_b, w_e, w_s, a_t, h0)


def _s5_layout_tables(k_tab, glu_w):
    h = k_tab.shape[0]
    g = SSM_GROUP_CH
    kt = k_tab.reshape(h, S5_T, g, g)
    lag = np.arange(S5_T)[None, :] - np.arange(S5_T)[:, None]
    m5 = kt[:, np.clip(lag, 0, S5_T - 1)]
    m5 = jnp.where((lag >= 0)[None, :, :, None, None], m5, 0.0)
    m_toe = m5.transpose(0, 1, 4, 2, 3).reshape(h, S5_W, S5_W)
    eye = jnp.eye(S5_T, dtype=F32)

    def diag(wm):
        return (eye[None, :, None, :, None] * wm[:, None, :, None, :]).reshape(h, S5_W, S5_W)

    return m_toe, diag(glu_w[..., :g]), diag(glu_w[..., g:])


def _to_flat(u, nb, seq):
    h = u.shape[-1] // SSM_GROUP_CH
    nc = seq // S5_T
    return (u.reshape(nb, nc, S5_T, h, SSM_GROUP_CH).transpose(3, 1, 0, 2, 4)
            .reshape(h, nc * nb, S5_W))


def _from_flat(y, nb, seq):
    h = y.shape[0]
    nc = seq // S5_T
    return (y.reshape(h, nc, nb, S5_T, SSM_GROUP_CH).transpose(2, 1, 3, 0, 4)
            .reshape(nb * seq, h * SSM_GROUP_CH))


def kernel(x_prompt, x_sample, cache_k, cache_v, state_ssm_re, state_ssm_im, p_prompt, p_sample,
           norm_mix, w_in, ssm_a_re, ssm_a_im, ssm_log_dt, ssm_b_re, ssm_b_im, ssm_c_re, ssm_c_im,
           ssm_d, ssm_glu_w, attn_sinks, rel_bias, norm_grp_ssm, norm_grp_attn, w_out, norm_ffn,
           ffn_w_gate, ffn_w_up, ffn_w_down, router_w, moe_w_gate, moe_w_up, moe_w_down,
           norm_ple, ple_w_gate, ple_w_proj, norm_final):
    nb_p, seq, d = x_prompt.shape
    nb_s, n_new, _ = x_sample.shape
    depth = w_in.shape[0]
    d_ssm = norm_grp_ssm.shape[-1]
    d_attn = norm_grp_attn.shape[-1]
    d_kv = KV_HEADS * HEAD_DIM
    n_grp = d_ssm // SSM_GROUP_CH
    p_st = ssm_a_re.shape[-1]
    mp, ms = nb_p * seq, nb_s * n_new
    nc_p, nc_s = seq // S5_T, n_new // S5_T
    win = min(WINDOW, seq)

    h = jnp.concatenate([x_prompt.reshape(mp, d), x_sample.reshape(ms, d)], axis=0)
    p_all = jnp.concatenate([p_prompt.reshape(depth, mp, -1), p_sample.reshape(depth, ms, -1)], axis=1)

    pk, pv, pre, pim, sk, sv, sre, sim = [], [], [], [], [], [], [], []
    for i in range(depth):
        xn = _rmsnorm(h, norm_mix[i], BF16)
        z = _mm(xn, w_in, i, tm=1024, tn=512)
        tabs = _s5_tables(ssm_a_re[i], ssm_a_im[i], ssm_log_dt[i], ssm_b_re[i], ssm_b_im[i],
                          ssm_c_re[i], ssm_c_im[i], ssm_d[i])
        k_tab, w_e, w_s, a_t = tabs
        m_toe, k_a, k_b = _s5_layout_tables(k_tab, ssm_glu_w[i])
        u_flat = jnp.concatenate([_to_flat(z[:mp, :d_ssm], nb_p, seq),
                                  _to_flat(z[mp:, :d_ssm], nb_s, n_new)], axis=1)
        h0 = jnp.concatenate([state_ssm_re[i], state_ssm_im[i]], axis=-1).transpose(1, 0, 2)
        y_flat, fin_p, fin_s = _s5(u_flat, (m_toe, k_a, k_b, w_e, w_s, a_t), h0, nb_p, nc_p, nb_s, nc_s)
        rp = nb_p * nc_p
        y_ssm = jnp.concatenate([_from_flat(y_flat[:, :rp], nb_p, seq),
                                 _from_flat(y_flat[:, rp:], nb_s, n_new)], axis=0)
        y_att = jnp.concatenate([
            _attn_prompt(z, nb_p, seq, d_ssm, d_attn, rel_bias, attn_sinks[i]),
            _attn_sample(z, mp, nb_s, n_new, d_ssm, d_attn, cache_k, cache_v, i, rel_bias, attn_sinks[i]),
        ], axis=0)
        merged = _merge_norm(y_ssm, y_att, norm_grp_ssm[i], norm_grp_attn[i])
        h = _mm(merged, w_out, i, tm=1024, tn=512, res=h)

        kv = z[:, d_ssm + d_attn:]
        kv_p = kv[:mp].reshape(nb_p, seq, 2, KV_HEADS, HEAD_DIM)[:, seq - win:]
        kv_s = kv[mp:].reshape(nb_s, n_new, 2, KV_HEADS, HEAD_DIM)
        pk.append(kv_p[:, :, 0]); pv.append(kv_p[:, :, 1])
        sk.append(kv_s[:, :, 0]); sv.append(kv_s[:, :, 1])
        fin_p = fin_p.transpose(1, 0, 2)
        fin_s = fin_s.transpose(1, 0, 2)
        pre.append(fin_p[..., :p_st]); pim.append(fin_p[..., p_st:])
        sre.append(fin_s[..., :p_st]); sim.append(fin_s[..., p_st:])

        j = i // 2
        if i % 2 == 0:
            hn = _rmsnorm(h, norm_ffn[i], BF16)
            t = _swiglu(hn, ffn_w_gate, ffn_w_up, j, tm=1024, tn=512)
            h = _mm(t, ffn_w_down, j, tm=512, tn=512, res=h, vmem_mib=52)
        else:
            hn, e_idx, gates = _rmsnorm_router(h, norm_ffn[i], router_w[j])
            h = h + _moe_ffn(hn, e_idx, gates, moe_w_gate, moe_w_up, moe_w_down, j)

        hn = _rmsnorm(h, norm_ple[i], BF16)
        h = _ple(hn, p_all, ple_w_gate, ple_w_proj, h, i, tm=1024, tn=512)

    y = _rmsnorm(h, norm_final, F32)
    return (y[:mp].reshape(nb_p, seq, d), y[mp:].reshape(nb_s, n_new, d),
            jnp.stack(pk), jnp.stack(pv), jnp.stack(pre), jnp.stack(pim),
            jnp.stack(sk), jnp.stack(sv), jnp.stack(sre), jnp.stack(sim))
```

```python
import functools
import math

import numpy as np
import jax
import jax.numpy as jnp
from jax import lax
from jax.experimental import pallas as pl
from jax.experimental.pallas import tpu as pltpu

F32 = jnp.float32
BF16 = jnp.bfloat16
HIGHEST = lax.Precision.HIGHEST

CHUNK = 64
WINDOW = 128
HEAD_DIM = 64
KV_HEADS = 4
SSM_GROUP_CH = 16
SSM_STATE = 64
N_BUCKETS = 32
REL_MAX_DIST = 128
EPS = 1e-6
NEG_INF = -1e30
ATTN_SCALE = HEAD_DIM ** -0.5

LANES = 128
S5_T = 16
S5_GB = LANES // SSM_GROUP_CH
S5_F = S5_T * LANES
MOE_TM = 512
MIB = 1024 * 1024


def _pick(n, pref, mult=8):
    best = None
    for d in range(mult, min(n, pref) + 1, mult):
        if n % d == 0:
            best = d
    return best if best is not None else n


def _params(sem, vmem_mib):
    return pltpu.CompilerParams(dimension_semantics=sem, vmem_limit_bytes=int(vmem_mib * MIB))


def _rms(x, g):
    return x * lax.rsqrt(jnp.mean(x * x, axis=-1, keepdims=True) + EPS) * g


def _norm_body(x_ref, g_ref, o_ref):
    o_ref[...] = _rms(x_ref[...], g_ref[...]).astype(o_ref.dtype)


def _rmsnorm(x, g, out_dtype):
    m, d = x.shape
    tm = _pick(m, 512)
    return pl.pallas_call(
        _norm_body,
        grid=(m // tm,),
        in_specs=[pl.BlockSpec((tm, d), lambda i: (i, 0)), pl.BlockSpec((1, d), lambda i: (0, 0))],
        out_specs=pl.BlockSpec((tm, d), lambda i: (i, 0)),
        out_shape=jax.ShapeDtypeStruct((m, d), out_dtype),
        compiler_params=_params(("arbitrary",), 40),
        name="rmsnorm",
    )(x, g.reshape(1, d))


def _norm_router_body(x_ref, g_ref, rw_ref, o_ref, idx_ref, gate_ref):
    y = _rms(x_ref[...], g_ref[...])
    o_ref[...] = y.astype(o_ref.dtype)
    logits = jnp.dot(y, rw_ref[...], precision=HIGHEST, preferred_element_type=F32)
    n_exp = logits.shape[-1]
    lane = lax.broadcasted_iota(jnp.int32, logits.shape, 1)
    m1 = jnp.max(logits, axis=-1, keepdims=True)
    i1 = jnp.min(jnp.where(logits == m1, lane, n_exp), axis=-1, keepdims=True)
    rest = jnp.where(lane == i1, -jnp.inf, logits)
    m2 = jnp.max(rest, axis=-1, keepdims=True)
    i2 = jnp.min(jnp.where(rest == m2, lane, n_exp), axis=-1, keepdims=True)
    e2 = jnp.exp(m2 - m1)
    den = 1.0 + e2
    idx_ref[...] = jnp.concatenate([i1, i2], axis=-1)
    gate_ref[...] = jnp.concatenate([1.0 / den, e2 / den], axis=-1)


def _rmsnorm_router(x, g, router_w):
    m, d = x.shape
    n_exp = router_w.shape[-1]
    tm = _pick(m, 512)
    return pl.pallas_call(
        _norm_router_body,
        grid=(m // tm,),
        in_specs=[pl.BlockSpec((tm, d), lambda i: (i, 0)),
                  pl.BlockSpec((1, d), lambda i: (0, 0)),
                  pl.BlockSpec((d, n_exp), lambda i: (0, 0))],
        out_specs=[pl.BlockSpec((tm, d), lambda i: (i, 0)),
                   pl.BlockSpec((tm, 2), lambda i: (i, 0)),
                   pl.BlockSpec((tm, 2), lambda i: (i, 0))],
        out_shape=[jax.ShapeDtypeStruct((m, d), BF16),
                   jax.ShapeDtypeStruct((m, 2), jnp.int32),
                   jax.ShapeDtypeStruct((m, 2), F32)],
        compiler_params=_params(("arbitrary",), 40),
        name="rmsnorm_router",
    )(x, g.reshape(1, d), router_w)


def _merge_norm_body(a_ref, b_ref, ga_ref, gb_ref, o_ref):
    da = a_ref.shape[-1]
    o_ref[:, :da] = _rms(a_ref[...], ga_ref[...]).astype(o_ref.dtype)
    o_ref[:, da:] = _rms(b_ref[...], gb_ref[...]).astype(o_ref.dtype)


def _merge_norm(a, b, ga, gb):
    m, da = a.shape
    db = b.shape[-1]
    tm = _pick(m, 512)
    return pl.pallas_call(
        _merge_norm_body,
        grid=(m // tm,),
        in_specs=[pl.BlockSpec((tm, da), lambda i: (i, 0)), pl.BlockSpec((tm, db), lambda i: (i, 0)),
                  pl.BlockSpec((1, da), lambda i: (0, 0)), pl.BlockSpec((1, db), lambda i: (0, 0))],
        out_specs=pl.BlockSpec((tm, da + db), lambda i: (i, 0)),
        out_shape=jax.ShapeDtypeStruct((m, da + db), BF16),
        compiler_params=_params(("arbitrary",), 40),
        name="merge_norm",
    )(a, b, ga.reshape(1, da), gb.reshape(1, db))


def _mm_body(*refs, has_res):
    if has_res:
        x_ref, w_ref, r_ref, o_ref, wb_ref = refs
    else:
        x_ref, w_ref, o_ref, wb_ref = refs

    @pl.when(pl.program_id(1) == 0)
    def _():
        wb_ref[...] = w_ref[...].astype(BF16)

    acc = jnp.dot(x_ref[...], wb_ref[...], preferred_element_type=F32)
    if has_res:
        acc = r_ref[...] + acc
    o_ref[...] = acc.astype(o_ref.dtype)


def _mm(x, w, lyr, *, tm, tn, out_dtype=F32, res=None, vmem_mib=48):
    m, k = x.shape
    n = w.shape[-1]
    tm, tn = _pick(m, tm), _pick(n, tn, 128)
    in_specs = [pl.BlockSpec((tm, k), lambda j, i: (i, 0)),
                pl.BlockSpec((None, k, tn), lambda j, i: (lyr, 0, j))]
    args = [x, w]
    if res is not None:
        in_specs.append(pl.BlockSpec((tm, tn), lambda j, i: (i, j)))
        args.append(res)
    return pl.pallas_call(
        functools.partial(_mm_body, has_res=res is not None),
        grid=(n // tn, m // tm),
        in_specs=in_specs,
        out_specs=pl.BlockSpec((tm, tn), lambda j, i: (i, j)),
        out_shape=jax.ShapeDtypeStruct((m, n), out_dtype),
        scratch_shapes=[pltpu.VMEM((k, tn), BF16)],
        compiler_params=_params(("arbitrary", "arbitrary"), vmem_mib),
        name="mm_res" if res is not None else "mm",
    )(*args)


def _swiglu_math(x, wg, wu):
    a = jnp.dot(x, wg, preferred_element_type=F32)
    b = jnp.dot(x, wu, preferred_element_type=F32)
    return (a * jax.nn.sigmoid(a) * b).astype(BF16)


def _swiglu_body(x_ref, wg_ref, wu_ref, o_ref, wgb_ref, wub_ref):
    @pl.when(pl.program_id(1) == 0)
    def _():
        wgb_ref[...] = wg_ref[...].astype(BF16)
        wub_ref[...] = wu_ref[...].astype(BF16)

    o_ref[...] = _swiglu_math(x_ref[...], wgb_ref[...], wub_ref[...])


def _swiglu(x, wg, wu, lyr, *, tm, tn, vmem_mib=48):
    m, k = x.shape
    n = wg.shape[-1]
    tm, tn = _pick(m, tm), _pick(n, tn, 128)
    wspec = pl.BlockSpec((None, k, tn), lambda j, i: (lyr, 0, j))
    return pl.pallas_call(
        _swiglu_body,
        grid=(n // tn, m // tm),
        in_specs=[pl.BlockSpec((tm, k), lambda j, i: (i, 0)), wspec, wspec],
        out_specs=pl.BlockSpec((tm, tn), lambda j, i: (i, j)),
        out_shape=jax.ShapeDtypeStruct((m, n), BF16),
        scratch_shapes=[pltpu.VMEM((k, tn), BF16), pltpu.VMEM((k, tn), BF16)],
        compiler_params=_params(("arbitrary", "arbitrary"), vmem_mib),
        name="swiglu",
    )(x, wg, wu)


def _ple_body(x_ref, p_ref, wg_ref, wp_ref, r_ref, o_ref, wgb_ref, wpb_ref):
    @pl.when(pl.program_id(1) == 0)
    def _():
        wgb_ref[...] = wg_ref[...].astype(BF16)
        wpb_ref[...] = wp_ref[...].astype(BF16)

    a = jnp.dot(x_ref[...], wgb_ref[...], preferred_element_type=F32)
    b = jnp.dot(p_ref[...].astype(BF16), wpb_ref[...], preferred_element_type=F32)
    o_ref[...] = r_ref[...] + jax.nn.sigmoid(a) * b


def _ple(x, p, wg, wp, res, lyr, *, tm, tn, vmem_mib=48):
    m, k = x.shape
    kp = p.shape[-1]
    n = wg.shape[-1]
    tm, tn = _pick(m, tm), _pick(n, tn, 128)
    return pl.pallas_call(
        _ple_body,
        grid=(n // tn, m // tm),
        in_specs=[pl.BlockSpec((tm, k), lambda j, i: (i, 0)),
                  pl.BlockSpec((None, tm, kp), lambda j, i: (lyr, i, 0)),
                  pl.BlockSpec((None, k, tn), lambda j, i: (lyr, 0, j)),
                  pl.BlockSpec((None, kp, tn), lambda j, i: (lyr, 0, j)),
                  pl.BlockSpec((tm, tn), lambda j, i: (i, j))],
        out_specs=pl.BlockSpec((tm, tn), lambda j, i: (i, j)),
        out_shape=jax.ShapeDtypeStruct((m, n), F32),
        scratch_shapes=[pltpu.VMEM((k, tn), BF16), pltpu.VMEM((kp, tn), BF16)],
        compiler_params=_params(("arbitrary", "arbitrary"), vmem_mib),
        name="ple",
    )(x, p, wg, wp, res)


def _expert_changed(e_ref, i):
    return jnp.logical_or(i == 0, e_ref[i] != e_ref[jnp.maximum(i - 1, 0)])


def _moe_up_body(e_ref, nu_ref, x_ref, wg_ref, wu_ref, o_ref, wgb_ref, wub_ref):
    i = pl.program_id(1)

    @pl.when(_expert_changed(e_ref, i))
    def _():
        wgb_ref[...] = wg_ref[...].astype(BF16)
        wub_ref[...] = wu_ref[...].astype(BF16)

    @pl.when(i < nu_ref[0])
    def _():
        o_ref[...] = _swiglu_math(x_ref[...], wgb_ref[...], wub_ref[...])

    @pl.when(i >= nu_ref[0])
    def _():
        o_ref[...] = jnp.zeros_like(o_ref)


def _moe_up(x_sorted, wg, wu, blk_expert, n_used, lyr, *, tn, vmem_mib):
    ns, k = x_sorted.shape
    n = wg.shape[-1]
    tm, tn = MOE_TM, _pick(n, tn, 128)
    wspec = pl.BlockSpec((None, None, k, tn), lambda j, i, e, nu: (lyr, e[i], 0, j))
    grid_spec = pltpu.PrefetchScalarGridSpec(
        num_scalar_prefetch=2,
        grid=(n // tn, ns // tm),
        in_specs=[pl.BlockSpec((tm, k), lambda j, i, e, nu: (i, 0)), wspec, wspec],
        out_specs=pl.BlockSpec((tm, tn), lambda j, i, e, nu: (i, j)),
        scratch_shapes=[pltpu.VMEM((k, tn), BF16), pltpu.VMEM((k, tn), BF16)],
    )
    return pl.pallas_call(
        _moe_up_body,
        grid_spec=grid_spec,
        out_shape=jax.ShapeDtypeStruct((ns, n), BF16),
        compiler_params=_params(("arbitrary", "arbitrary"), vmem_mib),
        name="moe_up",
    )(blk_expert, n_used, x_sorted, wg, wu)


def _moe_down_body(e_ref, nu_ref, h_ref, wd_ref, o_ref, wdb_ref):
    i = pl.program_id(1)

    @pl.when(_expert_changed(e_ref, i))
    def _():
        wdb_ref[...] = wd_ref[...].astype(BF16)

    @pl.when(i < nu_ref[0])
    def _():
        o_ref[...] = jnp.dot(h_ref[...], wdb_ref[...], preferred_element_type=F32)

    @pl.when(i >= nu_ref[0])
    def _():
        o_ref[...] = jnp.zeros_like(o_ref)


def _moe_down(h_sorted, wd, blk_expert, n_used, lyr, *, tn, vmem_mib):
    ns, k = h_sorted.shape
    n = wd.shape[-1]
    tm, tn = MOE_TM, _pick(n, tn, 128)
    grid_spec = pltpu.PrefetchScalarGridSpec(
        num_scalar_prefetch=2,
        grid=(n // tn, ns // tm),
        in_specs=[pl.BlockSpec((tm, k), lambda j, i, e, nu: (i, 0)),
                  pl.BlockSpec((None, None, k, tn), lambda j, i, e, nu: (lyr, e[i], 0, j))],
        out_specs=pl.BlockSpec((tm, tn), lambda j, i, e, nu: (i, j)),
        scratch_shapes=[pltpu.VMEM((k, tn), BF16)],
    )
    return pl.pallas_call(
        _moe_down_body,
        grid_spec=grid_spec,
        out_shape=jax.ShapeDtypeStruct((ns, n), F32),
        compiler_params=_params(("arbitrary", "arbitrary"), vmem_mib),
        name="moe_down",
    )(blk_expert, n_used, h_sorted, wd)


def _moe_ffn(hn, e_idx, gates, wg, wu, wd, lyr):
    m, d = hn.shape
    n_exp = wg.shape[1]
    top_k = e_idx.shape[-1]
    tm = MOE_TM
    n_asg = m * top_k
    e_flat = e_idx.reshape(n_asg)
    onehot = (e_flat[None, :] == jnp.arange(n_exp, dtype=jnp.int32)[:, None]).astype(jnp.int32)
    csum = jnp.cumsum(onehot, axis=1)
    counts = csum[:, -1]
    padded = (counts + tm - 1) // tm * tm
    pad_end = jnp.cumsum(padded)
    pad_start = pad_end - padded
    slot_of = jnp.sum(onehot * (pad_start[:, None] + csum - 1), axis=0).astype(jnp.int32)
    n_blocks = n_asg // tm + n_exp
    n_slots = n_blocks * tm
    slot_tok = jnp.full((n_slots,), m, jnp.int32).at[slot_of].set(
        jnp.arange(n_asg, dtype=jnp.int32) // top_k)
    n_used = (pad_end[-1] // tm).astype(jnp.int32)
    blk = jnp.arange(n_blocks, dtype=jnp.int32)
    blk_start = jnp.minimum(blk, n_used - 1) * tm
    blk_expert = jnp.sum((pad_end[None, :] <= blk_start[:, None]).astype(jnp.int32), axis=1)
    blk_expert = jnp.minimum(blk_expert, n_exp - 1).astype(jnp.int32)

    x_pad = jnp.concatenate([hn, jnp.zeros((1, d), hn.dtype)], axis=0)
    x_sorted = x_pad[slot_tok]
    nu = n_used.reshape(1)
    h_sorted = _moe_up(x_sorted, wg, wu, blk_expert, nu, lyr, tn=1024, vmem_mib=56)
    y_sorted = _moe_down(h_sorted, wd, blk_expert, nu, lyr, tn=256, vmem_mib=56)
    sl = slot_of.reshape(m, top_k)
    y = y_sorted[sl[:, 0]] * gates[:, 0:1]
    for t in range(1, top_k):
        y = y + y_sorted[sl[:, t]] * gates[:, t:t + 1]
    return y


def _rel_buckets(n_q, n_k_past, n_k):
    rel = (np.arange(n_k) - n_k_past)[None, :] - np.arange(n_q)[:, None]
    half = N_BUCKETS // 2
    max_exact = half // 2
    n = np.abs(rel)
    nf = np.maximum(n, 1).astype(np.float32)
    large = max_exact + (np.log(nf / np.float32(max_exact)) / np.float32(math.log(REL_MAX_DIST / max_exact))
                         * np.float32(half - max_exact)).astype(np.int32)
    large = np.minimum(large, half - 1)
    return (np.where(rel > 0, half, 0) + np.where(n < max_exact, n, large)).astype(np.int32)


def _build_bias(bias_ref, sink_ref, bucket_ref, table_ref, sinks_ref):
    bucket = bucket_ref[...]
    nq = bucket.shape[1]
    gq = bias_ref.shape[2] // nq
    for kh in range(bias_ref.shape[0]):
        for g in range(gq):
            n = kh * gq + g
            acc = jnp.zeros(bucket.shape, F32)
            for b in range(N_BUCKETS):
                acc = jnp.where(bucket == b, table_ref[b, n], acc)
            bias_ref[kh, :, g * nq:(g + 1) * nq] = acc
            sink_ref[kh, :, g * nq:(g + 1) * nq] = jnp.full((1, nq), sinks_ref[n], F32)


def _attn_items(items, bias_ref, sink_ref):
    nq = items[0][0].shape[0]
    gq = items[0][0].shape[-1] // HEAD_DIM // KV_HEADS
    head = lambda x, n: x[:, n * HEAD_DIM:(n + 1) * HEAD_DIM]
    scores = []
    for q, k, _, valid in items:
        for kh in range(KV_HEADS):
            q4 = jnp.concatenate([head(q, kh * gq + g) for g in range(gq)], axis=0)
            s = lax.dot_general(head(k, kh), q4, (((1,), (1,)), ((), ())), preferred_element_type=F32)
            scores.append((s, kh, valid))
    probs = []
    for s, kh, valid in scores:
        s = s * ATTN_SCALE + bias_ref[kh]
        if valid is not None:
            s = jnp.where(valid, s, NEG_INF)
        sink = sink_ref[kh]
        mx = jnp.maximum(jnp.max(s, axis=0, keepdims=True), sink)
        e = jnp.exp(s - mx)
        den = jnp.sum(e, axis=0, keepdims=True) + jnp.exp(sink - mx)
        probs.append((e * (1.0 / den)).astype(BF16))
    outs = []
    for i, (_, _, v, _) in enumerate(items):
        heads = []
        for kh in range(KV_HEADS):
            o4 = lax.dot_general(probs[i * KV_HEADS + kh], head(v, kh), (((0,), (0,)), ((), ())),
                                 preferred_element_type=F32)
            heads.extend(o4[g * nq:(g + 1) * nq] for g in range(gq))
        outs.append(jnp.concatenate(heads, axis=-1))
    return outs


def _attn_prompt_body(q_ref, kp_ref, kc_ref, vp_ref, vc_ref, bucket_ref, table_ref, sinks_ref,
                      o_ref, bias_ref, sink_ref, *, chunks):
    j = pl.program_id(1)

    @pl.when(jnp.logical_and(pl.program_id(0) == 0, j == 0))
    def _():
        _build_bias(bias_ref, sink_ref, bucket_ref, table_ref, sinks_ref)

    q = q_ref[...].astype(BF16)
    k = jnp.concatenate([kp_ref[...], kc_ref[...]], axis=0).astype(BF16)
    v = jnp.concatenate([vp_ref[...], vc_ref[...]], axis=0).astype(BF16)
    band = WINDOW + CHUNK
    items = []
    for c in range(chunks):
        lo = c * CHUNK
        key_valid = None
        if lo < WINDOW:
            key_pos = j * (chunks * CHUNK) + lo - WINDOW + lax.broadcasted_iota(jnp.int32, (band, 1), 0)
            key_valid = key_pos >= 0
        items.append((q[lo:lo + CHUNK], k[lo:lo + band], v[lo:lo + band], key_valid))
    for c, o in enumerate(_attn_items(items, bias_ref, sink_ref)):
        o_ref[c * CHUNK:(c + 1) * CHUNK, :] = o


def _attn_prompt(z, m_total, n_b, seq, d_ssm, d_attn, table, sinks):
    d_kv = KV_HEADS * HEAD_DIM
    assert d_ssm % d_attn == 0 and d_ssm % d_kv == 0 and d_attn % d_kv == 0
    qb = _pick(seq, 4 * CHUNK, 2 * CHUNK)
    assert qb % WINDOW == 0 and seq % qb == 0
    chunks = qb // CHUNK
    nj = seq // qb
    r = qb // WINDOW
    kcol = (d_ssm + d_attn) // d_kv
    gq = d_attn // HEAD_DIM // KV_HEADS
    band = WINDOW + CHUNK
    bucket = jnp.asarray(_rel_buckets(CHUNK, WINDOW, band).T)

    def prev(b, j):
        return b * (seq // WINDOW) + jnp.maximum(j * r - 1, 0)

    return pl.pallas_call(
        functools.partial(_attn_prompt_body, chunks=chunks),
        grid=(n_b, nj),
        in_specs=[pl.BlockSpec((qb, d_attn), lambda b, j: (b * nj + j, d_ssm // d_attn)),
                  pl.BlockSpec((WINDOW, d_kv), lambda b, j: (prev(b, j), kcol)),
                  pl.BlockSpec((qb, d_kv), lambda b, j: (b * nj + j, kcol)),
                  pl.BlockSpec((WINDOW, d_kv), lambda b, j: (prev(b, j), kcol + 1)),
                  pl.BlockSpec((qb, d_kv), lambda b, j: (b * nj + j, kcol + 1)),
                  pl.BlockSpec((band, CHUNK), lambda b, j: (0, 0)),
                  pl.BlockSpec(memory_space=pltpu.SMEM),
                  pl.BlockSpec(memory_space=pltpu.SMEM)],
        out_specs=pl.BlockSpec((qb, d_attn), lambda b, j: (b * nj + j, 0)),
        out_shape=jax.ShapeDtypeStruct((m_total, d_attn), F32),
        scratch_shapes=[pltpu.VMEM((KV_HEADS, band, gq * CHUNK), F32),
                        pltpu.VMEM((KV_HEADS, 1, gq * CHUNK), F32)],
        compiler_params=_params(("arbitrary", "arbitrary"), 40),
        name="attn_prompt",
    )(z, z, z, z, z, bucket, table, sinks)


def _attn_sample_body(q_ref, kc_ref, kn_ref, vc_ref, vn_ref, bucket_ref, table_ref, sinks_ref, _,
                      o_ref, bias_ref, sink_ref, *, n_bs, n_new, n_past):
    @pl.when(pl.program_id(0) == 0)
    def _():
        _build_bias(bias_ref, sink_ref, bucket_ref, table_ref, sinks_ref)

    q = q_ref[...].astype(BF16)
    kc, kn = kc_ref[...].astype(BF16), kn_ref[...].astype(BF16)
    vc, vn = vc_ref[...].astype(BF16), vn_ref[...].astype(BF16)
    items = []
    for b in range(n_bs):
        new = slice(b * n_new, (b + 1) * n_new)
        past = slice(b * n_past, (b + 1) * n_past)
        items.append((q[new], jnp.concatenate([kc[past], kn[new]], axis=0),
                      jnp.concatenate([vc[past], vn[new]], axis=0), None))
    for b, o in enumerate(_attn_items(items, bias_ref, sink_ref)):
        o_ref[b * n_new:(b + 1) * n_new, :] = o


def _attn_sample(z, y_att, row0, n_b, n_new, d_ssm, d_attn, cache_k, cache_v, lyr, table, sinks):
    d_kv = KV_HEADS * HEAD_DIM
    n_past = cache_k.shape[2]
    n_bs = _pick(n_b, 4, 1)
    rows = n_bs * n_new
    assert row0 % rows == 0
    kcol = (d_ssm + d_attn) // d_kv
    gq = d_attn // HEAD_DIM // KV_HEADS
    n_k = n_past + n_new
    bucket = jnp.asarray(_rel_buckets(n_new, n_past, n_k).T)
    ck = cache_k.reshape(-1, d_kv)
    cv = cache_v.reshape(-1, d_kv)
    rb = row0 // rows
    nsteps = n_b // n_bs
    cache_spec = pl.BlockSpec((n_bs * n_past, d_kv), lambda b: (lyr * nsteps + b, 0))
    return pl.pallas_call(
        functools.partial(_attn_sample_body, n_bs=n_bs, n_new=n_new, n_past=n_past),
        grid=(nsteps,),
        in_specs=[pl.BlockSpec((rows, d_attn), lambda b: (rb + b, d_ssm // d_attn)),
                  cache_spec,
                  pl.BlockSpec((rows, d_kv), lambda b: (rb + b, kcol)),
                  cache_spec,
                  pl.BlockSpec((rows, d_kv), lambda b: (rb + b, kcol + 1)),
                  pl.BlockSpec((n_k, n_new), lambda b: (0, 0)),
                  pl.BlockSpec(memory_space=pltpu.SMEM),
                  pl.BlockSpec(memory_space=pltpu.SMEM),
                  pl.BlockSpec(memory_space=pl.ANY)],
        out_specs=pl.BlockSpec((rows, d_attn), lambda b: (rb + b, 0)),
        out_shape=jax.ShapeDtypeStruct(y_att.shape, F32),
        scratch_shapes=[pltpu.VMEM((KV_HEADS, n_k, gq * n_new), F32),
                        pltpu.VMEM((KV_HEADS, 1, gq * n_new), F32)],
        input_output_aliases={8: 0},
        compiler_params=_params(("arbitrary",), 40),
        name="attn_sample",
    )(z, ck, z, cv, z, bucket, table, sinks, y_att)


def _tile_lanes(x, reps):
    return jnp.concatenate([x] * reps, axis=-1)


def _s5_tables_body(are_ref, aim_ref, ldt_ref, btr_ref, bti_ref, cre_ref, cim_ref, d_ref,
                    dk_ref, we_ref, ws_ref, a_ref):
    rows, p = cre_ref.shape
    gb = rows // SSM_GROUP_CH
    a_re, a_im = are_ref[...], aim_ref[...]
    dt = jnp.exp(ldt_ref[...])
    mag = jnp.exp(a_re * dt)
    ab_re, ab_im = mag * jnp.cos(a_im * dt), mag * jnp.sin(a_im * dt)

    x, y = ab_re - 1.0, ab_im
    den = a_re * a_re + a_im * a_im
    co_re = (x * a_re + y * a_im) / den
    co_im = (y * a_re - x * a_im) / den
    bb_re = co_re * btr_ref[...] - co_im * bti_ref[...]
    bb_im = co_re * bti_ref[...] + co_im * btr_ref[...]
    c_re, c_im = cre_ref[...], cim_ref[...]

    r_idx = lax.broadcasted_iota(jnp.int32, (rows, 1), 0)
    own_state = (r_idx // SSM_GROUP_CH) == (lax.broadcasted_iota(jnp.int32, (1, gb * p), 1) // p)
    same_group = (r_idx // SSM_GROUP_CH) == (lax.broadcasted_iota(jnp.int32, (1, rows), 1) // SSM_GROUP_CH)
    diag = r_idx == lax.broadcasted_iota(jnp.int32, (1, rows), 1)

    def place(v_re, v_im):
        return jnp.concatenate([jnp.where(own_state, _tile_lanes(v_re, gb), 0.0),
                                jnp.where(own_state, _tile_lanes(v_im, gb), 0.0)], axis=-1)

    dims = (((1,), (1,)), ((), ()))
    bb_cat = jnp.concatenate([bb_re, -bb_im], axis=-1)
    pw_re, pw_im = jnp.ones_like(ab_re), jnp.zeros_like(ab_re)
    for j in range(S5_T + 1):
        ca_re = c_re * pw_re - c_im * pw_im
        ca_im = c_re * pw_im + c_im * pw_re
        if j < S5_T:
            kj = lax.dot_general(bb_cat, jnp.concatenate([ca_re, ca_im], axis=-1), dims,
                                 precision=HIGHEST, preferred_element_type=F32)
            kj = jnp.where(same_group, kj, 0.0)
            if j == 0:
                kj = kj + jnp.where(diag, d_ref[...], 0.0)
            dk_ref[j] = kj.astype(dk_ref.dtype)
            s = S5_T - 1 - j
            we_ref[s * rows:(s + 1) * rows, :] = place(
                pw_re * bb_re - pw_im * bb_im, pw_re * bb_im + pw_im * bb_re).astype(we_ref.dtype)
        if j >= 1:
            ws_ref[(j - 1) * rows:j * rows, :] = place(ca_re, -ca_im).astype(ws_ref.dtype)
        if j == S5_T:
            first = r_idx == (lax.broadcasted_iota(jnp.int32, (1, gb * p), 1) // p) * SSM_GROUP_CH
            a_ref[...] = jnp.concatenate(
                [jnp.sum(jnp.where(first, _tile_lanes(pw_re, gb), 0.0), axis=0, keepdims=True),
                 jnp.sum(jnp.where(first, _tile_lanes(pw_im, gb), 0.0), axis=0, keepdims=True)], axis=-1)
        pw_re, pw_im = pw_re * ab_re - pw_im * ab_im, pw_re * ab_im + pw_im * ab_re


def _s5_tables(a_re, a_im, log_dt, b_re, b_im, c_re, c_im, d_skip):
    h, p = a_re.shape
    g = SSM_GROUP_CH
    hb = h // S5_GB
    sw = 2 * S5_GB * p
    rep = lambda v: jnp.repeat(v, g, axis=0).reshape(hb, LANES, -1)
    blk = lambda w: pl.BlockSpec((None, LANES, w), lambda i: (i, 0, 0))
    wide = pl.BlockSpec((None, S5_F, sw), lambda i: (i, 0, 0))
    return pl.pallas_call(
        _s5_tables_body,
        grid=(hb,),
        in_specs=[blk(p), blk(p), blk(1), blk(p), blk(p), blk(p), blk(p), blk(1)],
        out_specs=[pl.BlockSpec((None, S5_T, LANES, LANES), lambda i: (i, 0, 0, 0)), wide, wide,
                   pl.BlockSpec((None, 1, sw), lambda i: (i, 0, 0))],
        out_shape=[jax.ShapeDtypeStruct((hb, S5_T, LANES, LANES), BF16),
                   jax.ShapeDtypeStruct((hb, S5_F, sw), BF16),
                   jax.ShapeDtypeStruct((hb, S5_F, sw), BF16),
                   jax.ShapeDtypeStruct((hb, 1, sw), F32)],
        compiler_params=_params(("arbitrary",), 40),
        name="s5_tables",
    )(rep(a_re), rep(a_im), rep(log_dt.reshape(h, 1)),
      jnp.swapaxes(b_re, 1, 2).reshape(hb, LANES, p), jnp.swapaxes(b_im, 1, 2).reshape(hb, LANES, p),
      c_re.reshape(hb, LANES, p), c_im.reshape(hb, LANES, p), d_skip.reshape(hb, LANES, 1))


def _split_mul(a, p):
    lane = lax.broadcasted_iota(jnp.int32, a.shape, a.ndim - 1)
    sw = pltpu.roll(a, p, axis=a.ndim - 1)
    return jnp.where(lane < p, a, sw), jnp.where(lane < p, -sw, a)


def _cmul(x, mr, mi, p):
    return x * mr + pltpu.roll(x, p, axis=x.ndim - 1) * mi


def _build_toeplitz(m_ref, dk_ref):
    m_ref[...] = jnp.zeros_like(m_ref)
    for s in range(S5_T):
        for t in range(s, S5_T):
            m_ref[s * LANES:(s + 1) * LANES, t * LANES:(t + 1) * LANES] = dk_ref[t - s]


def _s5_outputs(ucat, enter, m_ref, ws_ref):
    return (jnp.dot(ucat, m_ref[...], preferred_element_type=F32)
            + lax.dot_general(enter.astype(BF16), ws_ref[...], (((1,), (1,)), ((), ())),
                              preferred_element_type=F32))


def _glu_rows(o_ref, ga_ref, gb_ref, tile):
    ga, gb = ga_ref[...], gb_ref[...]
    tile = _pick(o_ref.shape[0], tile)
    for r0 in range(0, o_ref.shape[0], tile):
        y = o_ref[r0:r0 + tile, :].astype(BF16)
        o_ref[r0:r0 + tile, :] = (jnp.dot(y, ga, preferred_element_type=F32)
                                  * jax.nn.sigmoid(jnp.dot(y, gb, preferred_element_type=F32)))


def _s5_prompt_body(z_ref, dk_ref, we_ref, ws_ref, a_ref, ga_ref, gb_ref, o_ref, fin_ref, m_ref,
                    *, nb, nc):
    @pl.when(pl.program_id(1) == 0)
    def _():
        _build_toeplitz(m_ref, dk_ref)

    r = nb * nc
    p = a_ref.shape[-1] // 2
    ucat = jnp.concatenate([z_ref[pl.ds(t, r, stride=S5_T), :].astype(BF16) for t in range(S5_T)], axis=-1)
    x = jnp.dot(ucat, we_ref[...], preferred_element_type=F32)
    a = a_ref[...]
    pos = lax.rem(lax.broadcasted_iota(jnp.int32, (r, 1), 0), nc)
    sh = 1
    while sh < nc:
        mr, mi = _split_mul(a, p)
        shifted = jnp.where(pos >= sh, pltpu.roll(x, sh, axis=0), 0.0)
        x = x + _cmul(shifted, mr, mi, p)
        a = _cmul(a, mr, mi, p)
        sh *= 2
    enter = jnp.where(pos >= 1, pltpu.roll(x, 1, axis=0), 0.0)
    fin_ref[...] = jnp.concatenate([x[(b + 1) * nc - 1:(b + 1) * nc] for b in range(nb)], axis=0)
    y = _s5_outputs(ucat, enter, m_ref, ws_ref)
    for t in range(S5_T):
        o_ref[pl.ds(t, r, stride=S5_T), :] = y[:, t * LANES:(t + 1) * LANES]
    _glu_rows(o_ref, ga_ref, gb_ref, 512)


def _s5_sample_body(z_ref, dk_ref, we_ref, ws_ref, a_ref, ga_ref, gb_ref, h0_ref, _, o_ref, fin_ref, m_ref,
                    *, nb, nc):
    _build_toeplitz(m_ref, dk_ref)
    p = a_ref.shape[-1] // 2
    seq = nc * S5_T
    ucat = jnp.concatenate(
        [jnp.concatenate([z_ref[pl.ds(c * S5_T + t, nb, stride=seq), :].astype(BF16) for t in range(S5_T)],
                         axis=-1) for c in range(nc)], axis=0)
    e = jnp.dot(ucat, we_ref[...], preferred_element_type=F32)
    mr, mi = _split_mul(a_ref[...], p)
    state = h0_ref[...]
    enter = []
    for c in range(nc):
        enter.append(state)
        state = _cmul(state, mr, mi, p) + e[c * nb:(c + 1) * nb]
    fin_ref[...] = state
    y = _s5_outputs(ucat, jnp.concatenate(enter, axis=0), m_ref, ws_ref)
    for c in range(nc):
        for t in range(S5_T):
            o_ref[pl.ds(c * S5_T + t, nb, stride=seq), :] = y[c * nb:(c + 1) * nb, t * LANES:(t + 1) * LANES]
    _glu_rows(o_ref, ga_ref, gb_ref, 512)


def _s5_prompt(z, m_total, n_b, seq, tabs, g_a, g_b):
    dk, w_e, w_s, a_t = tabs
    hb, sw = a_t.shape[0], a_t.shape[-1]
    nbs = _pick(n_b, 2, 1)
    nc = seq // S5_T
    tab = lambda arr: pl.BlockSpec((None,) + arr.shape[1:], lambda i, b: (i,) + (0,) * (arr.ndim - 1))
    return pl.pallas_call(
        functools.partial(_s5_prompt_body, nb=nbs, nc=nc),
        grid=(hb, n_b // nbs),
        in_specs=[pl.BlockSpec((nbs * seq, LANES), lambda i, b: (b, i)),
                  tab(dk), tab(w_e), tab(w_s), tab(a_t), tab(g_a), tab(g_b)],
        out_specs=[pl.BlockSpec((nbs * seq, LANES), lambda i, b: (b, i)),
                   pl.BlockSpec((None, None, nbs, sw), lambda i, b: (i, b, 0, 0))],
        out_shape=[jax.ShapeDtypeStruct((m_total, hb * LANES), F32),
                   jax.ShapeDtypeStruct((hb, n_b // nbs, nbs, sw), F32)],
        scratch_shapes=[pltpu.VMEM((S5_F, S5_F), BF16)],
        compiler_params=_params(("arbitrary", "arbitrary"), 52),
        name="s5_prompt",
    )(z, dk, w_e, w_s, a_t, g_a, g_b)


def _s5_sample(z, y_ssm, row0, n_b, n_new, tabs, g_a, g_b, h0):
    dk, w_e, w_s, a_t = tabs
    hb, sw = a_t.shape[0], a_t.shape[-1]
    rows = n_b * n_new
    assert row0 % rows == 0
    rb = row0 // rows
    tab = lambda arr: pl.BlockSpec((None,) + arr.shape[1:], lambda i: (i,) + (0,) * (arr.ndim - 1))
    return pl.pallas_call(
        functools.partial(_s5_sample_body, nb=n_b, nc=n_new // S5_T),
        grid=(hb,),
        in_specs=[pl.BlockSpec((rows, LANES), lambda i: (rb, i)),
                  tab(dk), tab(w_e), tab(w_s), tab(a_t), tab(g_a), tab(g_b), tab(h0),
                  pl.BlockSpec(memory_space=pl.ANY)],
        out_specs=[pl.BlockSpec((rows, LANES), lambda i: (rb, i)),
                   pl.BlockSpec((None, n_b, sw), lambda i: (i, 0, 0))],
        out_shape=[jax.ShapeDtypeStruct(y_ssm.shape, F32),
                   jax.ShapeDtypeStruct((hb, n_b, sw), F32)],
        scratch_shapes=[pltpu.VMEM((S5_F, S5_F), BF16)],
        input_output_aliases={8: 0},
        compiler_params=_params(("arbitrary",), 52),
        name="s5_sample",
    )(z, dk, w_e, w_s, a_t, g_a, g_b, h0, y_ssm)


def _glu_blocks(glu_w):
    h, g, _ = glu_w.shape
    hb = h // S5_GB
    eye = jnp.eye(S5_GB, dtype=F32)

    def diag(wm):
        wm = wm.reshape(hb, S5_GB, g, g)
        return (eye[None, :, None, :, None] * wm[:, :, :, None, :]).reshape(hb, LANES, LANES).astype(BF16)

    return diag(glu_w[..., :g]), diag(glu_w[..., g:])


def _state_to_lanes(re, im):
    n, h, p = re.shape
    hb = h // S5_GB
    f = lambda v: v.reshape(n, hb, S5_GB * p).transpose(1, 0, 2)
    return jnp.concatenate([f(re), f(im)], axis=-1)


def _state_from_lanes(s):
    hb, n, sw = s.shape
    half = sw // 2
    f = lambda v: v.transpose(1, 0, 2).reshape(n, hb * S5_GB, half // S5_GB)
    return f(s[..., :half]), f(s[..., half:])


def kernel(x_prompt, x_sample, cache_k, cache_v, state_ssm_re, state_ssm_im, p_prompt, p_sample,
           norm_mix, w_in, ssm_a_re, ssm_a_im, ssm_log_dt, ssm_b_re, ssm_b_im, ssm_c_re, ssm_c_im,
           ssm_d, ssm_glu_w, attn_sinks, rel_bias, norm_grp_ssm, norm_grp_attn, w_out, norm_ffn,
           ffn_w_gate, ffn_w_up, ffn_w_down, router_w, moe_w_gate, moe_w_up, moe_w_down,
           norm_ple, ple_w_gate, ple_w_proj, norm_final):
    nb_p, seq, d = x_prompt.shape
    nb_s, n_new, _ = x_sample.shape
    depth = w_in.shape[0]
    d_ssm = norm_grp_ssm.shape[-1]
    d_attn = norm_grp_attn.shape[-1]
    mp, ms = nb_p * seq, nb_s * n_new
    m = mp + ms
    win = min(WINDOW, seq)
    assert d_ssm % LANES == 0 and seq % S5_T == 0 and n_new % S5_T == 0

    h = jnp.concatenate([x_prompt.reshape(mp, d), x_sample.reshape(ms, d)], axis=0)
    p_all = jnp.concatenate([p_prompt.reshape(depth, mp, -1), p_sample.reshape(depth, ms, -1)], axis=1)

    pk, pv, pre, pim, sk, sv, sre, sim = [], [], [], [], [], [], [], []
    for i in range(depth):
        xn = _rmsnorm(h, norm_mix[i], BF16)
        z = _mm(xn, w_in, i, tm=1024, tn=512)
        tabs = _s5_tables(ssm_a_re[i], ssm_a_im[i], ssm_log_dt[i], ssm_b_re[i], ssm_b_im[i],
                          ssm_c_re[i], ssm_c_im[i], ssm_d[i])
        g_a, g_b = _glu_blocks(ssm_glu_w[i])
        y_ssm, fin_p = _s5_prompt(z, m, nb_p, seq, tabs, g_a, g_b)
        y_ssm, fin_s = _s5_sample(z, y_ssm, mp, nb_s, n_new, tabs, g_a, g_b,
                                  _state_to_lanes(state_ssm_re[i], state_ssm_im[i]))
        y_att = _attn_prompt(z, m, nb_p, seq, d_ssm, d_attn, rel_bias, attn_sinks[i])
        y_att = _attn_sample(z, y_att, mp, nb_s, n_new, d_ssm, d_attn, cache_k, cache_v, i, rel_bias,
                             attn_sinks[i])
        merged = _merge_norm(y_ssm, y_att, norm_grp_ssm[i], norm_grp_attn[i])
        h = _mm(merged, w_out, i, tm=1024, tn=512, res=h)

        kv = z[:, d_ssm + d_attn:]
        kv_p = kv[:mp].reshape(nb_p, seq, 2, KV_HEADS, HEAD_DIM)[:, seq - win:]
        kv_s = kv[mp:].reshape(nb_s, n_new, 2, KV_HEADS, HEAD_DIM)
        pk.append(kv_p[:, :, 0]); pv.append(kv_p[:, :, 1])
        sk.append(kv_s[:, :, 0]); sv.append(kv_s[:, :, 1])
        f_re, f_im = _state_from_lanes(fin_p.reshape(fin_p.shape[0], nb_p, -1))
        pre.append(f_re); pim.append(f_im)
        f_re, f_im = _state_from_lanes(fin_s)
        sre.append(f_re); sim.append(f_im)

        j = i // 2
        if i % 2 == 0:
            hn = _rmsnorm(h, norm_ffn[i], BF16)
            t = _swiglu(hn, ffn_w_gate, ffn_w_up, j, tm=1024, tn=512)
            h = _mm(t, ffn_w_down, j, tm=512, tn=512, res=h, vmem_mib=52)
        else:
            hn, e_idx, gates = _rmsnorm_router(h, norm_ffn[i], router_w[j])
            h = h + _moe_ffn(hn, e_idx, gates, moe_w_gate, moe_w_up, moe_w_down, j)

        hn = _rmsnorm(h, norm_ple[i], BF16)
        h = _ple(hn, p_all, ple_w_gate, ple_w_proj, h, i, tm=1024, tn=512)

    y = _rmsnorm(h, norm_final, F32)
    return (y[:mp].reshape(nb_p, seq, d), y[mp:].reshape(nb_s, n_new, d),
            jnp.stack(pk), jnp.stack(pv), jnp.stack(pre), jnp.stack(pim),
            jnp.stack(sk), jnp.stack(sv), jnp.stack(sre), jnp.stack(sim))
```

```python
import functools
import math

import numpy as np
import jax
import jax.numpy as jnp
from jax import lax
from jax.experimental import pallas as pl
from jax.experimental.pallas import tpu as pltpu

F32 = jnp.float32
BF16 = jnp.bfloat16
HIGHEST = lax.Precision.HIGHEST

CHUNK = 64
WINDOW = 128
HEAD_DIM = 64
KV_HEADS = 4
SSM_GROUP_CH = 16
SSM_STATE = 64
N_BUCKETS = 32
REL_MAX_DIST = 128
EPS = 1e-6
NEG_INF = -1e30
ATTN_SCALE = HEAD_DIM ** -0.5

LANES = 128
S5_T = 16
S5_GB = LANES // SSM_GROUP_CH
S5_F = S5_T * LANES
MOE_TM = 512
MOE_SUB = 256
MIB = 1024 * 1024


def _pick(n, pref, mult=8):
    best = None
    for d in range(mult, min(n, pref) + 1, mult):
        if n % d == 0:
            best = d
    return best if best is not None else n


def _params(sem, vmem_mib):
    return pltpu.CompilerParams(dimension_semantics=sem, vmem_limit_bytes=int(vmem_mib * MIB))


def _rms(x, g):
    return x * lax.rsqrt(jnp.mean(x * x, axis=-1, keepdims=True) + EPS) * g


def _norm_body(x_ref, g_ref, o_ref):
    o_ref[...] = _rms(x_ref[...], g_ref[...]).astype(o_ref.dtype)


def _rmsnorm(x, g, out_dtype):
    m, d = x.shape
    tm = _pick(m, 512)
    return pl.pallas_call(
        _norm_body,
        grid=(m // tm,),
        in_specs=[pl.BlockSpec((tm, d), lambda i: (i, 0)), pl.BlockSpec((1, d), lambda i: (0, 0))],
        out_specs=pl.BlockSpec((tm, d), lambda i: (i, 0)),
        out_shape=jax.ShapeDtypeStruct((m, d), out_dtype),
        compiler_params=_params(("arbitrary",), 40),
        name="rmsnorm",
    )(x, g.reshape(1, d))


def _norm_router_body(x_ref, g_ref, rw_ref, o_ref, idx_ref, gate_ref):
    y = _rms(x_ref[...], g_ref[...])
    o_ref[...] = y.astype(o_ref.dtype)
    logits = jnp.dot(y, rw_ref[...], precision=HIGHEST, preferred_element_type=F32)
    n_exp = logits.shape[-1]
    lane = lax.broadcasted_iota(jnp.int32, logits.shape, 1)
    m1 = jnp.max(logits, axis=-1, keepdims=True)
    i1 = jnp.min(jnp.where(logits == m1, lane, n_exp), axis=-1, keepdims=True)
    rest = jnp.where(lane == i1, -jnp.inf, logits)
    m2 = jnp.max(rest, axis=-1, keepdims=True)
    i2 = jnp.min(jnp.where(rest == m2, lane, n_exp), axis=-1, keepdims=True)
    e2 = jnp.exp(m2 - m1)
    den = 1.0 + e2
    idx_ref[...] = jnp.concatenate([i1, i2], axis=-1)
    gate_ref[...] = jnp.concatenate([1.0 / den, e2 / den], axis=-1)


def _rmsnorm_router(x, g, router_w):
    m, d = x.shape
    n_exp = router_w.shape[-1]
    tm = _pick(m, 512)
    return pl.pallas_call(
        _norm_router_body,
        grid=(m // tm,),
        in_specs=[pl.BlockSpec((tm, d), lambda i: (i, 0)),
                  pl.BlockSpec((1, d), lambda i: (0, 0)),
                  pl.BlockSpec((d, n_exp), lambda i: (0, 0))],
        out_specs=[pl.BlockSpec((tm, d), lambda i: (i, 0)),
                   pl.BlockSpec((tm, 2), lambda i: (i, 0)),
                   pl.BlockSpec((tm, 2), lambda i: (i, 0))],
        out_shape=[jax.ShapeDtypeStruct((m, d), F32),
                   jax.ShapeDtypeStruct((m, 2), jnp.int32),
                   jax.ShapeDtypeStruct((m, 2), F32)],
        compiler_params=_params(("arbitrary",), 40),
        name="rmsnorm_router",
    )(x, g.reshape(1, d), router_w)


def _merge_norm_body(a_ref, b_ref, ga_ref, gb_ref, o_ref):
    da = a_ref.shape[-1]
    o_ref[:, :da] = _rms(a_ref[...], ga_ref[...]).astype(o_ref.dtype)
    o_ref[:, da:] = _rms(b_ref[...], gb_ref[...]).astype(o_ref.dtype)


def _merge_norm(a, b, ga, gb):
    m, da = a.shape
    db = b.shape[-1]
    tm = _pick(m, 512)
    return pl.pallas_call(
        _merge_norm_body,
        grid=(m // tm,),
        in_specs=[pl.BlockSpec((tm, da), lambda i: (i, 0)), pl.BlockSpec((tm, db), lambda i: (i, 0)),
                  pl.BlockSpec((1, da), lambda i: (0, 0)), pl.BlockSpec((1, db), lambda i: (0, 0))],
        out_specs=pl.BlockSpec((tm, da + db), lambda i: (i, 0)),
        out_shape=jax.ShapeDtypeStruct((m, da + db), BF16),
        compiler_params=_params(("arbitrary",), 40),
        name="merge_norm",
    )(a, b, ga.reshape(1, da), gb.reshape(1, db))


def _mm_body(*refs, has_res):
    if has_res:
        x_ref, w_ref, r_ref, o_ref, wb_ref = refs
    else:
        x_ref, w_ref, o_ref, wb_ref = refs

    @pl.when(pl.program_id(1) == 0)
    def _():
        wb_ref[...] = w_ref[...].astype(BF16)

    acc = jnp.dot(x_ref[...], wb_ref[...], preferred_element_type=F32)
    if has_res:
        acc = r_ref[...] + acc
    o_ref[...] = acc.astype(o_ref.dtype)


def _mm(x, w, lyr, *, tm, tn, out_dtype=F32, res=None, vmem_mib=48):
    m, k = x.shape
    n = w.shape[-1]
    tm, tn = _pick(m, tm), _pick(n, tn, 128)
    in_specs = [pl.BlockSpec((tm, k), lambda j, i: (i, 0)),
                pl.BlockSpec((None, k, tn), lambda j, i: (lyr, 0, j))]
    args = [x, w]
    if res is not None:
        in_specs.append(pl.BlockSpec((tm, tn), lambda j, i: (i, j)))
        args.append(res)
    return pl.pallas_call(
        functools.partial(_mm_body, has_res=res is not None),
        grid=(n // tn, m // tm),
        in_specs=in_specs,
        out_specs=pl.BlockSpec((tm, tn), lambda j, i: (i, j)),
        out_shape=jax.ShapeDtypeStruct((m, n), out_dtype),
        scratch_shapes=[pltpu.VMEM((k, tn), BF16)],
        compiler_params=_params(("arbitrary", "arbitrary"), vmem_mib),
        name="mm_res" if res is not None else "mm",
    )(*args)


def _swiglu_math(x, wg, wu):
    a = jnp.dot(x, wg, preferred_element_type=F32)
    b = jnp.dot(x, wu, preferred_element_type=F32)
    return (a * jax.nn.sigmoid(a) * b).astype(BF16)


def _swiglu_body(x_ref, wg_ref, wu_ref, o_ref, wgb_ref, wub_ref):
    @pl.when(pl.program_id(1) == 0)
    def _():
        wgb_ref[...] = wg_ref[...].astype(BF16)
        wub_ref[...] = wu_ref[...].astype(BF16)

    o_ref[...] = _swiglu_math(x_ref[...], wgb_ref[...], wub_ref[...])


def _swiglu(x, wg, wu, lyr, *, tm, tn, vmem_mib=48):
    m, k = x.shape
    n = wg.shape[-1]
    tm, tn = _pick(m, tm), _pick(n, tn, 128)
    wspec = pl.BlockSpec((None, k, tn), lambda j, i: (lyr, 0, j))
    return pl.pallas_call(
        _swiglu_body,
        grid=(n // tn, m // tm),
        in_specs=[pl.BlockSpec((tm, k), lambda j, i: (i, 0)), wspec, wspec],
        out_specs=pl.BlockSpec((tm, tn), lambda j, i: (i, j)),
        out_shape=jax.ShapeDtypeStruct((m, n), BF16),
        scratch_shapes=[pltpu.VMEM((k, tn), BF16), pltpu.VMEM((k, tn), BF16)],
        compiler_params=_params(("arbitrary", "arbitrary"), vmem_mib),
        name="swiglu",
    )(x, wg, wu)


def _ple_body(x_ref, p_ref, wg_ref, wp_ref, r_ref, o_ref, wgb_ref, wpb_ref):
    @pl.when(pl.program_id(1) == 0)
    def _():
        wgb_ref[...] = wg_ref[...].astype(BF16)
        wpb_ref[...] = wp_ref[...].astype(BF16)

    a = jnp.dot(x_ref[...], wgb_ref[...], preferred_element_type=F32)
    b = jnp.dot(p_ref[...].astype(BF16), wpb_ref[...], preferred_element_type=F32)
    o_ref[...] = r_ref[...] + jax.nn.sigmoid(a) * b


def _ple(x, p, wg, wp, res, lyr, *, tm, tn, vmem_mib=48):
    m, k = x.shape
    kp = p.shape[-1]
    n = wg.shape[-1]
    tm, tn = _pick(m, tm), _pick(n, tn, 128)
    return pl.pallas_call(
        _ple_body,
        grid=(n // tn, m // tm),
        in_specs=[pl.BlockSpec((tm, k), lambda j, i: (i, 0)),
                  pl.BlockSpec((None, tm, kp), lambda j, i: (lyr, i, 0)),
                  pl.BlockSpec((None, k, tn), lambda j, i: (lyr, 0, j)),
                  pl.BlockSpec((None, kp, tn), lambda j, i: (lyr, 0, j)),
                  pl.BlockSpec((tm, tn), lambda j, i: (i, j))],
        out_specs=pl.BlockSpec((tm, tn), lambda j, i: (i, j)),
        out_shape=jax.ShapeDtypeStruct((m, n), F32),
        scratch_shapes=[pltpu.VMEM((k, tn), BF16), pltpu.VMEM((kp, tn), BF16)],
        compiler_params=_params(("arbitrary", "arbitrary"), vmem_mib),
        name="ple",
    )(x, p, wg, wp, res)


def _expert_changed(e_ref, blk, i):
    return jnp.logical_or(i == 0, e_ref[blk] != e_ref[jnp.maximum(blk - 1, 0)])


def _moe_up_body(e_ref, act_ref, x_ref, wg_ref, wu_ref, o_ref, wgb_ref, wub_ref):
    i = pl.program_id(1)

    @pl.when(_expert_changed(e_ref, i, i))
    def _():
        wgb_ref[...] = wg_ref[...].astype(BF16)
        wub_ref[...] = wu_ref[...].astype(BF16)

    per = x_ref.shape[0] // MOE_SUB
    for s in range(per):
        rows = slice(s * MOE_SUB, (s + 1) * MOE_SUB)

        @pl.when(act_ref[i * per + s] != 0)
        def _():
            o_ref[rows, :] = _swiglu_math(x_ref[rows, :].astype(BF16), wgb_ref[...], wub_ref[...])

        @pl.when(act_ref[i * per + s] == 0)
        def _():
            o_ref[rows, :] = jnp.zeros((MOE_SUB, o_ref.shape[1]), o_ref.dtype)


def _moe_up(x_sorted, wg, wu, blk_expert, sub_active, lyr, *, tn, vmem_mib):
    ns, k = x_sorted.shape
    n = wg.shape[-1]
    tm, tn = MOE_TM, _pick(n, tn, 128)
    wspec = pl.BlockSpec((None, None, k, tn), lambda j, i, e, act: (lyr, e[i], 0, j))
    grid_spec = pltpu.PrefetchScalarGridSpec(
        num_scalar_prefetch=2,
        grid=(n // tn, ns // tm),
        in_specs=[pl.BlockSpec((tm, k), lambda j, i, e, act: (i, 0)), wspec, wspec],
        out_specs=pl.BlockSpec((tm, tn), lambda j, i, e, act: (i, j)),
        scratch_shapes=[pltpu.VMEM((k, tn), BF16), pltpu.VMEM((k, tn), BF16)],
    )
    return pl.pallas_call(
        _moe_up_body,
        grid_spec=grid_spec,
        out_shape=jax.ShapeDtypeStruct((ns, n), BF16),
        compiler_params=_params(("arbitrary", "arbitrary"), vmem_mib),
        name="moe_up",
    )(blk_expert, sub_active, x_sorted, wg, wu)


def _moe_down_body(e_ref, act_ref, h_ref, wd_ref, o_ref, wdb_ref, *, per):
    i = pl.program_id(1)
    blk = i // per

    @pl.when(jnp.logical_and(_expert_changed(e_ref, blk, i), i % per == 0))
    def _():
        wdb_ref[...] = wd_ref[...].astype(BF16)

    @pl.when(act_ref[i] != 0)
    def _():
        o_ref[...] = jnp.dot(h_ref[...], wdb_ref[...], preferred_element_type=F32)

    @pl.when(act_ref[i] == 0)
    def _():
        o_ref[...] = jnp.zeros_like(o_ref)


def _moe_down(h_sorted, wd, blk_expert, sub_active, lyr, *, tn, vmem_mib):
    ns, k = h_sorted.shape
    n = wd.shape[-1]
    tm, tn = MOE_SUB, _pick(n, tn, 128)
    per = MOE_TM // MOE_SUB
    grid_spec = pltpu.PrefetchScalarGridSpec(
        num_scalar_prefetch=2,
        grid=(n // tn, ns // tm),
        in_specs=[pl.BlockSpec((tm, k), lambda j, i, e, act: (i, 0)),
                  pl.BlockSpec((None, None, k, tn), lambda j, i, e, act: (lyr, e[i // per], 0, j))],
        out_specs=pl.BlockSpec((tm, tn), lambda j, i, e, act: (i, j)),
        scratch_shapes=[pltpu.VMEM((k, tn), BF16)],
    )
    return pl.pallas_call(
        functools.partial(_moe_down_body, per=per),
        grid_spec=grid_spec,
        out_shape=jax.ShapeDtypeStruct((ns, n), F32),
        compiler_params=_params(("arbitrary", "arbitrary"), vmem_mib),
        name="moe_down",
    )(blk_expert, sub_active, h_sorted, wd)


def _moe_ffn(hn, e_idx, gates, wg, wu, wd, lyr):
    m, d = hn.shape
    n_exp = wg.shape[1]
    top_k = e_idx.shape[-1]
    tm = MOE_TM
    n_asg = m * top_k
    e_flat = e_idx.reshape(n_asg)
    onehot = (e_flat[None, :] == jnp.arange(n_exp, dtype=jnp.int32)[:, None]).astype(jnp.int32)
    csum = jnp.cumsum(onehot, axis=1)
    counts = csum[:, -1]
    padded = (counts + tm - 1) // tm * tm
    pad_end = jnp.cumsum(padded)
    pad_start = pad_end - padded
    slot_of = jnp.sum(onehot * (pad_start[:, None] + csum - 1), axis=0).astype(jnp.int32)
    n_blocks = n_asg // tm + n_exp
    n_slots = n_blocks * tm
    slot_tok = jnp.zeros((n_slots,), jnp.int32).at[slot_of].set(
        jnp.arange(n_asg, dtype=jnp.int32) // top_k)
    n_used = (pad_end[-1] // tm).astype(jnp.int32)
    blk = jnp.arange(n_blocks, dtype=jnp.int32)
    blk_start = jnp.minimum(blk, n_used - 1) * tm
    blk_expert = jnp.sum((pad_end[None, :] <= blk_start[:, None]).astype(jnp.int32), axis=1)
    blk_expert = jnp.minimum(blk_expert, n_exp - 1).astype(jnp.int32)
    sub_start = jnp.arange(n_slots // MOE_SUB, dtype=jnp.int32) * MOE_SUB
    sub_expert = jnp.repeat(blk_expert, tm // MOE_SUB)
    real_end = (pad_start + counts)[sub_expert]
    sub_active = jnp.logical_and(sub_start < real_end, sub_start < pad_end[-1]).astype(jnp.int32)

    x_sorted = hn[slot_tok]
    h_sorted = _moe_up(x_sorted, wg, wu, blk_expert, sub_active, lyr, tn=896, vmem_mib=56)
    y_sorted = _moe_down(h_sorted, wd, blk_expert, sub_active, lyr, tn=512, vmem_mib=56)
    sl = slot_of.reshape(m, top_k)
    y = y_sorted[sl[:, 0]] * gates[:, 0:1]
    for t in range(1, top_k):
        y = y + y_sorted[sl[:, t]] * gates[:, t:t + 1]
    return y


def _rel_buckets(n_q, n_k_past, n_k):
    rel = (np.arange(n_k) - n_k_past)[None, :] - np.arange(n_q)[:, None]
    half = N_BUCKETS // 2
    max_exact = half // 2
    n = np.abs(rel)
    nf = np.maximum(n, 1).astype(np.float32)
    large = max_exact + (np.log(nf / np.float32(max_exact)) / np.float32(math.log(REL_MAX_DIST / max_exact))
                         * np.float32(half - max_exact)).astype(np.int32)
    large = np.minimum(large, half - 1)
    return (np.where(rel > 0, half, 0) + np.where(n < max_exact, n, large)).astype(np.int32)


def _build_bias(bias_ref, sink_ref, bucket_ref, table_ref, sinks_ref):
    bucket = bucket_ref[...]
    nq = bucket.shape[1]
    gq = bias_ref.shape[2] // nq
    for kh in range(bias_ref.shape[0]):
        for g in range(gq):
            n = kh * gq + g
            acc = jnp.zeros(bucket.shape, F32)
            for b in range(N_BUCKETS):
                acc = jnp.where(bucket == b, table_ref[b, n], acc)
            bias_ref[kh, :, g * nq:(g + 1) * nq] = acc
            sink_ref[kh, :, g * nq:(g + 1) * nq] = jnp.full((1, nq), sinks_ref[n], F32)


def _attn_items(items, bias_ref, sink_ref):
    nq = items[0][0].shape[0]
    gq = items[0][0].shape[-1] // HEAD_DIM // KV_HEADS
    head = lambda x, n: x[:, n * HEAD_DIM:(n + 1) * HEAD_DIM]
    scores = []
    for q, k, _, valid in items:
        for kh in range(KV_HEADS):
            q4 = jnp.concatenate([head(q, kh * gq + g) for g in range(gq)], axis=0)
            s = lax.dot_general(head(k, kh), q4, (((1,), (1,)), ((), ())), preferred_element_type=F32)
            scores.append((s, kh, valid))
    probs = []
    for s, kh, valid in scores:
        s = s * ATTN_SCALE + bias_ref[kh]
        if valid is not None:
            s = jnp.where(valid, s, NEG_INF)
        sink = sink_ref[kh]
        mx = jnp.maximum(jnp.max(s, axis=0, keepdims=True), sink)
        e = jnp.exp(s - mx)
        den = jnp.sum(e, axis=0, keepdims=True) + jnp.exp(sink - mx)
        probs.append((e * (1.0 / den)).astype(BF16))
    outs = []
    for i, (_, _, v, _) in enumerate(items):
        heads = []
        for kh in range(KV_HEADS):
            o4 = lax.dot_general(probs[i * KV_HEADS + kh], head(v, kh), (((0,), (0,)), ((), ())),
                                 preferred_element_type=F32)
            heads.extend(o4[g * nq:(g + 1) * nq] for g in range(gq))
        outs.append(jnp.concatenate(heads, axis=-1))
    return outs


def _attn_prompt_body(q_ref, kp_ref, kc_ref, vp_ref, vc_ref, bucket_ref, table_ref, sinks_ref,
                      o_ref, bias_ref, sink_ref, *, chunks):
    j = pl.program_id(1)

    @pl.when(jnp.logical_and(pl.program_id(0) == 0, j == 0))
    def _():
        _build_bias(bias_ref, sink_ref, bucket_ref, table_ref, sinks_ref)

    q = q_ref[...].astype(BF16)
    k = jnp.concatenate([kp_ref[...], kc_ref[...]], axis=0).astype(BF16)
    v = jnp.concatenate([vp_ref[...], vc_ref[...]], axis=0).astype(BF16)
    band = WINDOW + CHUNK
    items = []
    for c in range(chunks):
        lo = c * CHUNK
        key_valid = None
        if lo < WINDOW:
            key_pos = j * (chunks * CHUNK) + lo - WINDOW + lax.broadcasted_iota(jnp.int32, (band, 1), 0)
            key_valid = key_pos >= 0
        items.append((q[lo:lo + CHUNK], k[lo:lo + band], v[lo:lo + band], key_valid))
    for c, o in enumerate(_attn_items(items, bias_ref, sink_ref)):
        o_ref[c * CHUNK:(c + 1) * CHUNK, :] = o


def _attn_prompt(z, m_total, n_b, seq, d_ssm, d_attn, table, sinks):
    d_kv = KV_HEADS * HEAD_DIM
    assert d_ssm % d_attn == 0 and d_ssm % d_kv == 0 and d_attn % d_kv == 0
    qb = _pick(seq, 4 * CHUNK, 2 * CHUNK)
    assert qb % WINDOW == 0 and seq % qb == 0
    chunks = qb // CHUNK
    nj = seq // qb
    r = qb // WINDOW
    kcol = (d_ssm + d_attn) // d_kv
    gq = d_attn // HEAD_DIM // KV_HEADS
    band = WINDOW + CHUNK
    bucket = jnp.asarray(_rel_buckets(CHUNK, WINDOW, band).T)

    def prev(b, j):
        return b * (seq // WINDOW) + jnp.maximum(j * r - 1, 0)

    return pl.pallas_call(
        functools.partial(_attn_prompt_body, chunks=chunks),
        grid=(n_b, nj),
        in_specs=[pl.BlockSpec((qb, d_attn), lambda b, j: (b * nj + j, d_ssm // d_attn)),
                  pl.BlockSpec((WINDOW, d_kv), lambda b, j: (prev(b, j), kcol)),
                  pl.BlockSpec((qb, d_kv), lambda b, j: (b * nj + j, kcol)),
                  pl.BlockSpec((WINDOW, d_kv), lambda b, j: (prev(b, j), kcol + 1)),
                  pl.BlockSpec((qb, d_kv), lambda b, j: (b * nj + j, kcol + 1)),
                  pl.BlockSpec((band, CHUNK), lambda b, j: (0, 0)),
                  pl.BlockSpec(memory_space=pltpu.SMEM),
                  pl.BlockSpec(memory_space=pltpu.SMEM)],
        out_specs=pl.BlockSpec((qb, d_attn), lambda b, j: (b * nj + j, 0)),
        out_shape=jax.ShapeDtypeStruct((m_total, d_attn), F32),
        scratch_shapes=[pltpu.VMEM((KV_HEADS, band, gq * CHUNK), F32),
                        pltpu.VMEM((KV_HEADS, 1, gq * CHUNK), F32)],
        compiler_params=_params(("arbitrary", "arbitrary"), 40),
        name="attn_prompt",
    )(z, z, z, z, z, bucket, table, sinks)


def _attn_sample_body(q_ref, kc_ref, kn_ref, vc_ref, vn_ref, bucket_ref, table_ref, sinks_ref, _,
                      o_ref, bias_ref, sink_ref, *, n_bs, n_new, n_past):
    @pl.when(pl.program_id(0) == 0)
    def _():
        _build_bias(bias_ref, sink_ref, bucket_ref, table_ref, sinks_ref)

    q = q_ref[...].astype(BF16)
    kc, kn = kc_ref[...].astype(BF16), kn_ref[...].astype(BF16)
    vc, vn = vc_ref[...].astype(BF16), vn_ref[...].astype(BF16)
    items = []
    for b in range(n_bs):
        new = slice(b * n_new, (b + 1) * n_new)
        past = slice(b * n_past, (b + 1) * n_past)
        items.append((q[new], jnp.concatenate([kc[past], kn[new]], axis=0),
                      jnp.concatenate([vc[past], vn[new]], axis=0), None))
    for b, o in enumerate(_attn_items(items, bias_ref, sink_ref)):
        o_ref[b * n_new:(b + 1) * n_new, :] = o


def _attn_sample(z, y_att, row0, n_b, n_new, d_ssm, d_attn, cache_k, cache_v, lyr, table, sinks):
    d_kv = KV_HEADS * HEAD_DIM
    n_past = cache_k.shape[2]
    n_bs = _pick(n_b, 4, 1)
    rows = n_bs * n_new
    assert row0 % rows == 0
    kcol = (d_ssm + d_attn) // d_kv
    gq = d_attn // HEAD_DIM // KV_HEADS
    n_k = n_past + n_new
    bucket = jnp.asarray(_rel_buckets(n_new, n_past, n_k).T)
    ck = cache_k.reshape(-1, d_kv)
    cv = cache_v.reshape(-1, d_kv)
    rb = row0 // rows
    nsteps = n_b // n_bs
    cache_spec = pl.BlockSpec((n_bs * n_past, d_kv), lambda b: (lyr * nsteps + b, 0))
    return pl.pallas_call(
        functools.partial(_attn_sample_body, n_bs=n_bs, n_new=n_new, n_past=n_past),
        grid=(nsteps,),
        in_specs=[pl.BlockSpec((rows, d_attn), lambda b: (rb + b, d_ssm // d_attn)),
                  cache_spec,
                  pl.BlockSpec((rows, d_kv), lambda b: (rb + b, kcol)),
                  cache_spec,
                  pl.BlockSpec((rows, d_kv), lambda b: (rb + b, kcol + 1)),
                  pl.BlockSpec((n_k, n_new), lambda b: (0, 0)),
                  pl.BlockSpec(memory_space=pltpu.SMEM),
                  pl.BlockSpec(memory_space=pltpu.SMEM),
                  pl.BlockSpec(memory_space=pl.ANY)],
        out_specs=pl.BlockSpec((rows, d_attn), lambda b: (rb + b, 0)),
        out_shape=jax.ShapeDtypeStruct(y_att.shape, F32),
        scratch_shapes=[pltpu.VMEM((KV_HEADS, n_k, gq * n_new), F32),
                        pltpu.VMEM((KV_HEADS, 1, gq * n_new), F32)],
        input_output_aliases={8: 0},
        compiler_params=_params(("arbitrary",), 40),
        name="attn_sample",
    )(z, ck, z, cv, z, bucket, table, sinks, y_att)


def _tile_lanes(x, reps):
    return jnp.concatenate([x] * reps, axis=-1)


def _s5_tables_body(are_ref, aim_ref, ldt_ref, btr_ref, bti_ref, cre_ref, cim_ref, d_ref,
                    dk_ref, we_ref, ws_ref, a_ref):
    rows, p = cre_ref.shape
    gb = rows // SSM_GROUP_CH
    a_re, a_im = are_ref[...], aim_ref[...]
    dt = jnp.exp(ldt_ref[...])
    mag = jnp.exp(a_re * dt)
    ab_re, ab_im = mag * jnp.cos(a_im * dt), mag * jnp.sin(a_im * dt)

    x, y = ab_re - 1.0, ab_im
    den = a_re * a_re + a_im * a_im
    co_re = (x * a_re + y * a_im) / den
    co_im = (y * a_re - x * a_im) / den
    bb_re = co_re * btr_ref[...] - co_im * bti_ref[...]
    bb_im = co_re * bti_ref[...] + co_im * btr_ref[...]
    c_re, c_im = cre_ref[...], cim_ref[...]

    r_idx = lax.broadcasted_iota(jnp.int32, (rows, 1), 0)
    own_state = (r_idx // SSM_GROUP_CH) == (lax.broadcasted_iota(jnp.int32, (1, gb * p), 1) // p)
    same_group = (r_idx // SSM_GROUP_CH) == (lax.broadcasted_iota(jnp.int32, (1, rows), 1) // SSM_GROUP_CH)
    diag = r_idx == lax.broadcasted_iota(jnp.int32, (1, rows), 1)

    def place(v_re, v_im):
        return jnp.concatenate([jnp.where(own_state, _tile_lanes(v_re, gb), 0.0),
                                jnp.where(own_state, _tile_lanes(v_im, gb), 0.0)], axis=-1)

    dims = (((1,), (1,)), ((), ()))
    bb_cat = jnp.concatenate([bb_re, -bb_im], axis=-1)
    pw_re, pw_im = jnp.ones_like(ab_re), jnp.zeros_like(ab_re)
    for j in range(S5_T + 1):
        ca_re = c_re * pw_re - c_im * pw_im
        ca_im = c_re * pw_im + c_im * pw_re
        if j < S5_T:
            kj = lax.dot_general(bb_cat, jnp.concatenate([ca_re, ca_im], axis=-1), dims,
                                 precision=HIGHEST, preferred_element_type=F32)
            kj = jnp.where(same_group, kj, 0.0)
            if j == 0:
                kj = kj + jnp.where(diag, d_ref[...], 0.0)
            dk_ref[j] = kj.astype(dk_ref.dtype)
            s = S5_T - 1 - j
            we_ref[s * rows:(s + 1) * rows, :] = place(
                pw_re * bb_re - pw_im * bb_im, pw_re * bb_im + pw_im * bb_re).astype(we_ref.dtype)
        if j >= 1:
            ws_ref[(j - 1) * rows:j * rows, :] = place(ca_re, -ca_im).astype(ws_ref.dtype)
        if j == S5_T:
            first = r_idx == (lax.broadcasted_iota(jnp.int32, (1, gb * p), 1) // p) * SSM_GROUP_CH
            a_ref[...] = jnp.concatenate(
                [jnp.sum(jnp.where(first, _tile_lanes(pw_re, gb), 0.0), axis=0, keepdims=True),
                 jnp.sum(jnp.where(first, _tile_lanes(pw_im, gb), 0.0), axis=0, keepdims=True)], axis=-1)
        pw_re, pw_im = pw_re * ab_re - pw_im * ab_im, pw_re * ab_im + pw_im * ab_re


def _s5_tables(a_re, a_im, log_dt, b_re, b_im, c_re, c_im, d_skip):
    h, p = a_re.shape
    g = SSM_GROUP_CH
    hb = h // S5_GB
    sw = 2 * S5_GB * p
    rep = lambda v: jnp.repeat(v, g, axis=0).reshape(hb, LANES, -1)
    blk = lambda w: pl.BlockSpec((None, LANES, w), lambda i: (i, 0, 0))
    wide = pl.BlockSpec((None, S5_F, sw), lambda i: (i, 0, 0))
    return pl.pallas_call(
        _s5_tables_body,
        grid=(hb,),
        in_specs=[blk(p), blk(p), blk(1), blk(p), blk(p), blk(p), blk(p), blk(1)],
        out_specs=[pl.BlockSpec((None, S5_T, LANES, LANES), lambda i: (i, 0, 0, 0)), wide, wide,
                   pl.BlockSpec((None, 1, sw), lambda i: (i, 0, 0))],
        out_shape=[jax.ShapeDtypeStruct((hb, S5_T, LANES, LANES), BF16),
                   jax.ShapeDtypeStruct((hb, S5_F, sw), BF16),
                   jax.ShapeDtypeStruct((hb, S5_F, sw), BF16),
                   jax.ShapeDtypeStruct((hb, 1, sw), F32)],
        compiler_params=_params(("arbitrary",), 40),
        name="s5_tables",
    )(rep(a_re), rep(a_im), rep(log_dt.reshape(h, 1)),
      jnp.swapaxes(b_re, 1, 2).reshape(hb, LANES, p), jnp.swapaxes(b_im, 1, 2).reshape(hb, LANES, p),
      c_re.reshape(hb, LANES, p), c_im.reshape(hb, LANES, p), d_skip.reshape(hb, LANES, 1))


def _split_mul(a, p):
    lane = lax.broadcasted_iota(jnp.int32, a.shape, a.ndim - 1)
    sw = pltpu.roll(a, p, axis=a.ndim - 1)
    return jnp.where(lane < p, a, sw), jnp.where(lane < p, -sw, a)


def _cmul(x, mr, mi, p):
    return x * mr + pltpu.roll(x, p, axis=x.ndim - 1) * mi


def _build_toeplitz(m_ref, dk_ref):
    m_ref[...] = jnp.zeros_like(m_ref)
    for s in range(S5_T):
        for t in range(s, S5_T):
            m_ref[s * LANES:(s + 1) * LANES, t * LANES:(t + 1) * LANES] = dk_ref[t - s]


def _s5_outputs(ucat, enter, m_ref, ws_ref):
    return (jnp.dot(ucat, m_ref[...], preferred_element_type=F32)
            + lax.dot_general(enter.astype(BF16), ws_ref[...], (((1,), (1,)), ((), ())),
                              preferred_element_type=F32))


def _glu_rows(o_ref, ga_ref, gb_ref, tile):
    ga, gb = ga_ref[...], gb_ref[...]
    tile = _pick(o_ref.shape[0], tile)
    for r0 in range(0, o_ref.shape[0], tile):
        y = o_ref[r0:r0 + tile, :].astype(BF16)
        o_ref[r0:r0 + tile, :] = (jnp.dot(y, ga, preferred_element_type=F32)
                                  * jax.nn.sigmoid(jnp.dot(y, gb, preferred_element_type=F32)))


def _s5_prompt_body(z_ref, dk_ref, we_ref, ws_ref, a_ref, ga_ref, gb_ref, o_ref, fin_ref, m_ref,
                    *, nb, nc):
    @pl.when(pl.program_id(1) == 0)
    def _():
        _build_toeplitz(m_ref, dk_ref)

    r = nb * nc
    p = a_ref.shape[-1] // 2
    ucat = jnp.concatenate([z_ref[pl.ds(t, r, stride=S5_T), :].astype(BF16) for t in range(S5_T)], axis=-1)
    x = jnp.dot(ucat, we_ref[...], preferred_element_type=F32)
    a = a_ref[...]
    pos = lax.rem(lax.broadcasted_iota(jnp.int32, (r, 1), 0), nc)
    sh = 1
    while sh < nc:
        mr, mi = _split_mul(a, p)
        shifted = jnp.where(pos >= sh, pltpu.roll(x, sh, axis=0), 0.0)
        x = x + _cmul(shifted, mr, mi, p)
        a = _cmul(a, mr, mi, p)
        sh *= 2
    enter = jnp.where(pos >= 1, pltpu.roll(x, 1, axis=0), 0.0)
    fin_ref[...] = jnp.concatenate([x[(b + 1) * nc - 1:(b + 1) * nc] for b in range(nb)], axis=0)
    y = _s5_outputs(ucat, enter, m_ref, ws_ref)
    for t in range(S5_T):
        o_ref[pl.ds(t, r, stride=S5_T), :] = y[:, t * LANES:(t + 1) * LANES]
    _glu_rows(o_ref, ga_ref, gb_ref, 512)


def _s5_sample_body(z_ref, dk_ref, we_ref, ws_ref, a_ref, ga_ref, gb_ref, h0_ref, _, o_ref, fin_ref, m_ref,
                    *, nb, nc):
    _build_toeplitz(m_ref, dk_ref)
    p = a_ref.shape[-1] // 2
    seq = nc * S5_T
    ucat = jnp.concatenate(
        [jnp.concatenate([z_ref[pl.ds(c * S5_T + t, nb, stride=seq), :].astype(BF16) for t in range(S5_T)],
                         axis=-1) for c in range(nc)], axis=0)
    e = jnp.dot(ucat, we_ref[...], preferred_element_type=F32)
    mr, mi = _split_mul(a_ref[...], p)
    state = h0_ref[...]
    enter = []
    for c in range(nc):
        enter.append(state)
        state = _cmul(state, mr, mi, p) + e[c * nb:(c + 1) * nb]
    fin_ref[...] = state
    y = _s5_outputs(ucat, jnp.concatenate(enter, axis=0), m_ref, ws_ref)
    for c in range(nc):
        for t in range(S5_T):
            o_ref[pl.ds(c * S5_T + t, nb, stride=seq), :] = y[c * nb:(c + 1) * nb, t * LANES:(t + 1) * LANES]
    _glu_rows(o_ref, ga_ref, gb_ref, 512)


def _s5_prompt(z, m_total, n_b, seq, tabs, g_a, g_b):
    dk, w_e, w_s, a_t = tabs
    hb, sw = a_t.shape[0], a_t.shape[-1]
    nbs = _pick(n_b, 2, 1)
    nc = seq // S5_T
    tab = lambda arr: pl.BlockSpec((None,) + arr.shape[1:], lambda i, b: (i,) + (0,) * (arr.ndim - 1))
    return pl.pallas_call(
        functools.partial(_s5_prompt_body, nb=nbs, nc=nc),
        grid=(hb, n_b // nbs),
        in_specs=[pl.BlockSpec((nbs * seq, LANES), lambda i, b: (b, i)),
                  tab(dk), tab(w_e), tab(w_s), tab(a_t), tab(g_a), tab(g_b)],
        out_specs=[pl.BlockSpec((nbs * seq, LANES), lambda i, b: (b, i)),
                   pl.BlockSpec((None, None, nbs, sw), lambda i, b: (i, b, 0, 0))],
        out_shape=[jax.ShapeDtypeStruct((m_total, hb * LANES), F32),
                   jax.ShapeDtypeStruct((hb, n_b // nbs, nbs, sw), F32)],
        scratch_shapes=[pltpu.VMEM((S5_F, S5_F), BF16)],
        compiler_params=_params(("arbitrary", "arbitrary"), 52),
        name="s5_prompt",
    )(z, dk, w_e, w_s, a_t, g_a, g_b)


def _s5_sample(z, y_ssm, row0, n_b, n_new, tabs, g_a, g_b, h0):
    dk, w_e, w_s, a_t = tabs
    hb, sw = a_t.shape[0], a_t.shape[-1]
    rows = n_b * n_new
    assert row0 % rows == 0
    rb = row0 // rows
    tab = lambda arr: pl.BlockSpec((None,) + arr.shape[1:], lambda i: (i,) + (0,) * (arr.ndim - 1))
    return pl.pallas_call(
        functools.partial(_s5_sample_body, nb=n_b, nc=n_new // S5_T),
        grid=(hb,),
        in_specs=[pl.BlockSpec((rows, LANES), lambda i: (rb, i)),
                  tab(dk), tab(w_e), tab(w_s), tab(a_t), tab(g_a), tab(g_b), tab(h0),
                  pl.BlockSpec(memory_space=pl.ANY)],
        out_specs=[pl.BlockSpec((rows, LANES), lambda i: (rb, i)),
                   pl.BlockSpec((None, n_b, sw), lambda i: (i, 0, 0))],
        out_shape=[jax.ShapeDtypeStruct(y_ssm.shape, F32),
                   jax.ShapeDtypeStruct((hb, n_b, sw), F32)],
        scratch_shapes=[pltpu.VMEM((S5_F, S5_F), BF16)],
        input_output_aliases={8: 0},
        compiler_params=_params(("arbitrary",), 52),
        name="s5_sample",
    )(z, dk, w_e, w_s, a_t, g_a, g_b, h0, y_ssm)


def _glu_blocks(glu_w):
    h, g, _ = glu_w.shape
    hb = h // S5_GB
    eye = jnp.eye(S5_GB, dtype=F32)

    def diag(wm):
        wm = wm.reshape(hb, S5_GB, g, g)
        return (eye[None, :, None, :, None] * wm[:, :, :, None, :]).reshape(hb, LANES, LANES).astype(BF16)

    return diag(glu_w[..., :g]), diag(glu_w[..., g:])


def _state_to_lanes(re, im):
    n, h, p = re.shape
    hb = h // S5_GB
    f = lambda v: v.reshape(n, hb, S5_GB * p).transpose(1, 0, 2)
    return jnp.concatenate([f(re), f(im)], axis=-1)


def _state_from_lanes(s):
    hb, n, sw = s.shape
    half = sw // 2
    f = lambda v: v.transpose(1, 0, 2).reshape(n, hb * S5_GB, half // S5_GB)
    return f(s[..., :half]), f(s[..., half:])


def kernel(x_prompt, x_sample, cache_k, cache_v, state_ssm_re, state_ssm_im, p_prompt, p_sample,
           norm_mix, w_in, ssm_a_re, ssm_a_im, ssm_log_dt, ssm_b_re, ssm_b_im, ssm_c_re, ssm_c_im,
           ssm_d, ssm_glu_w, attn_sinks, rel_bias, norm_grp_ssm, norm_grp_attn, w_out, norm_ffn,
           ffn_w_gate, ffn_w_up, ffn_w_down, router_w, moe_w_gate, moe_w_up, moe_w_down,
           norm_ple, ple_w_gate, ple_w_proj, norm_final):
    nb_p, seq, d = x_prompt.shape
    nb_s, n_new, _ = x_sample.shape
    depth = w_in.shape[0]
    d_ssm = norm_grp_ssm.shape[-1]
    d_attn = norm_grp_attn.shape[-1]
    mp, ms = nb_p * seq, nb_s * n_new
    m = mp + ms
    win = min(WINDOW, seq)
    assert d_ssm % LANES == 0 and seq % S5_T == 0 and n_new % S5_T == 0

    h = jnp.concatenate([x_prompt.reshape(mp, d), x_sample.reshape(ms, d)], axis=0)
    p_all = jnp.concatenate([p_prompt.reshape(depth, mp, -1), p_sample.reshape(depth, ms, -1)], axis=1)

    pk, pv, pre, pim, sk, sv, sre, sim = [], [], [], [], [], [], [], []
    for i in range(depth):
        xn = _rmsnorm(h, norm_mix[i], BF16)
        z = _mm(xn, w_in, i, tm=1024, tn=1280, vmem_mib=56)
        tabs = _s5_tables(ssm_a_re[i], ssm_a_im[i], ssm_log_dt[i], ssm_b_re[i], ssm_b_im[i],
                          ssm_c_re[i], ssm_c_im[i], ssm_d[i])
        g_a, g_b = _glu_blocks(ssm_glu_w[i])
        y_ssm, fin_p = _s5_prompt(z, m, nb_p, seq, tabs, g_a, g_b)
        y_ssm, fin_s = _s5_sample(z, y_ssm, mp, nb_s, n_new, tabs, g_a, g_b,
                                  _state_to_lanes(state_ssm_re[i], state_ssm_im[i]))
        y_att = _attn_prompt(z, m, nb_p, seq, d_ssm, d_attn, rel_bias, attn_sinks[i])
        y_att = _attn_sample(z, y_att, mp, nb_s, n_new, d_ssm, d_attn, cache_k, cache_v, i, rel_bias,
                             attn_sinks[i])
        merged = _merge_norm(y_ssm, y_att, norm_grp_ssm[i], norm_grp_attn[i])
        h = _mm(merged, w_out, i, tm=1024, tn=1024, res=h, vmem_mib=56)

        kv = z[:, d_ssm + d_attn:]
        kv_p = kv[:mp].reshape(nb_p, seq, 2, KV_HEADS, HEAD_DIM)[:, seq - win:]
        kv_s = kv[mp:].reshape(nb_s, n_new, 2, KV_HEADS, HEAD_DIM)
        pk.append(kv_p[:, :, 0]); pv.append(kv_p[:, :, 1])
        sk.append(kv_s[:, :, 0]); sv.append(kv_s[:, :, 1])
        f_re, f_im = _state_from_lanes(fin_p.reshape(fin_p.shape[0], nb_p, -1))
        pre.append(f_re); pim.append(f_im)
        f_re, f_im = _state_from_lanes(fin_s)
        sre.append(f_re); sim.append(f_im)

        j = i // 2
        if i % 2 == 0:
            hn = _rmsnorm(h, norm_ffn[i], BF16)
            t = _swiglu(hn, ffn_w_gate, ffn_w_up, j, tm=1024, tn=512)
            h = _mm(t, ffn_w_down, j, tm=512, tn=512, res=h, vmem_mib=52)
        else:
            hn, e_idx, gates = _rmsnorm_router(h, norm_ffn[i], router_w[j])
            h = h + _moe_ffn(hn, e_idx, gates, moe_w_gate, moe_w_up, moe_w_down, j)

        hn = _rmsnorm(h, norm_ple[i], BF16)
        h = _ple(hn, p_all, ple_w_gate, ple_w_proj, h, i, tm=1024, tn=1024, vmem_mib=56)

    y = _rmsnorm(h, norm_final, F32)
    return (y[:mp].reshape(nb_p, seq, d), y[mp:].reshape(nb_s, n_new, d),
            jnp.stack(pk), jnp.stack(pv), jnp.stack(pre), jnp.stack(pim),
            jnp.stack(sk), jnp.stack(sv), jnp.stack(sre), jnp.stack(sim))
```

```python
import functools
import math

import numpy as np
import jax
import jax.numpy as jnp
from jax import lax
from jax.experimental import pallas as pl
from jax.experimental.pallas import tpu as pltpu

F32 = jnp.float32
BF16 = jnp.bfloat16
HIGHEST = lax.Precision.HIGHEST

CHUNK = 64
WINDOW = 128
HEAD_DIM = 64
KV_HEADS = 4
SSM_GROUP_CH = 16
SSM_STATE = 64
N_BUCKETS = 32
REL_MAX_DIST = 128
EPS = 1e-6
NEG_INF = -1e30
ATTN_SCALE = HEAD_DIM ** -0.5

LANES = 128
S5_T = 16
S5_GB = LANES // SSM_GROUP_CH
S5_F = S5_T * LANES
MOE_TM = 512
MOE_SUB = 256
MIB = 1024 * 1024


def _pick(n, pref, mult=8):
    best = None
    for d in range(mult, min(n, pref) + 1, mult):
        if n % d == 0:
            best = d
    return best if best is not None else n


def _params(sem, vmem_mib):
    return pltpu.CompilerParams(dimension_semantics=sem, vmem_limit_bytes=int(vmem_mib * MIB))


def _rms(x, g):
    return x * lax.rsqrt(jnp.mean(x * x, axis=-1, keepdims=True) + EPS) * g


def _norm_body(x_ref, g_ref, o_ref):
    o_ref[...] = _rms(x_ref[...], g_ref[...]).astype(o_ref.dtype)


def _rmsnorm(x, g, out_dtype):
    m, d = x.shape
    tm = _pick(m, 512)
    return pl.pallas_call(
        _norm_body,
        grid=(m // tm,),
        in_specs=[pl.BlockSpec((tm, d), lambda i: (i, 0)), pl.BlockSpec((1, d), lambda i: (0, 0))],
        out_specs=pl.BlockSpec((tm, d), lambda i: (i, 0)),
        out_shape=jax.ShapeDtypeStruct((m, d), out_dtype),
        compiler_params=_params(("arbitrary",), 40),
        name="rmsnorm",
    )(x, g.reshape(1, d))


def _norm_router_body(x_ref, g_ref, rw_ref, o_ref, idx_ref, gate_ref):
    y = _rms(x_ref[...], g_ref[...])
    o_ref[...] = y.astype(o_ref.dtype)
    logits = jnp.dot(y, rw_ref[...], precision=HIGHEST, preferred_element_type=F32)
    n_exp = logits.shape[-1]
    lane = lax.broadcasted_iota(jnp.int32, logits.shape, 1)
    m1 = jnp.max(logits, axis=-1, keepdims=True)
    i1 = jnp.min(jnp.where(logits == m1, lane, n_exp), axis=-1, keepdims=True)
    rest = jnp.where(lane == i1, -jnp.inf, logits)
    m2 = jnp.max(rest, axis=-1, keepdims=True)
    i2 = jnp.min(jnp.where(rest == m2, lane, n_exp), axis=-1, keepdims=True)
    e2 = jnp.exp(m2 - m1)
    den = 1.0 + e2
    idx_ref[...] = jnp.concatenate([i1, i2], axis=-1)
    gate_ref[...] = jnp.concatenate([1.0 / den, e2 / den], axis=-1)


def _rmsnorm_router(x, g, router_w):
    m, d = x.shape
    n_exp = router_w.shape[-1]
    tm = _pick(m, 512)
    return pl.pallas_call(
        _norm_router_body,
        grid=(m // tm,),
        in_specs=[pl.BlockSpec((tm, d), lambda i: (i, 0)),
                  pl.BlockSpec((1, d), lambda i: (0, 0)),
                  pl.BlockSpec((d, n_exp), lambda i: (0, 0))],
        out_specs=[pl.BlockSpec((tm, d), lambda i: (i, 0)),
                   pl.BlockSpec((tm, 2), lambda i: (i, 0)),
                   pl.BlockSpec((tm, 2), lambda i: (i, 0))],
        out_shape=[jax.ShapeDtypeStruct((m, d), F32),
                   jax.ShapeDtypeStruct((m, 2), jnp.int32),
                   jax.ShapeDtypeStruct((m, 2), F32)],
        compiler_params=_params(("arbitrary",), 40),
        name="rmsnorm_router",
    )(x, g.reshape(1, d), router_w)


def _merge_norm_body(ap_ref, as_ref, bp_ref, bs_ref, ga_ref, gb_ref, o_ref, *, np_tiles):
    da = ap_ref.shape[-1]

    def emit(a_ref, b_ref):
        o_ref[:, :da] = _rms(a_ref[...], ga_ref[...]).astype(o_ref.dtype)
        o_ref[:, da:] = _rms(b_ref[...], gb_ref[...]).astype(o_ref.dtype)

    @pl.when(pl.program_id(0) < np_tiles)
    def _():
        emit(ap_ref, bp_ref)

    @pl.when(pl.program_id(0) >= np_tiles)
    def _():
        emit(as_ref, bs_ref)


def _merge_norm(a_p, a_s, b_p, b_s, ga, gb):
    (mp, da), (ms, db) = a_p.shape, b_s.shape
    tm = _pick(math.gcd(mp, ms), 512)
    npt, nst = mp // tm, ms // tm
    pspec = lambda w: pl.BlockSpec((tm, w), lambda i: (jnp.minimum(i, npt - 1), 0))
    sspec = lambda w: pl.BlockSpec((tm, w), lambda i: (jnp.maximum(i - npt, 0), 0))
    return pl.pallas_call(
        functools.partial(_merge_norm_body, np_tiles=npt),
        grid=(npt + nst,),
        in_specs=[pspec(da), sspec(da), pspec(db), sspec(db),
                  pl.BlockSpec((1, da), lambda i: (0, 0)), pl.BlockSpec((1, db), lambda i: (0, 0))],
        out_specs=pl.BlockSpec((tm, da + db), lambda i: (i, 0)),
        out_shape=jax.ShapeDtypeStruct((mp + ms, da + db), BF16),
        compiler_params=_params(("arbitrary",), 40),
        name="merge_norm",
    )(a_p, a_s, b_p, b_s, ga.reshape(1, da), gb.reshape(1, db))


def _mm_body(*refs, has_res):
    if has_res:
        x_ref, w_ref, r_ref, o_ref, wb_ref = refs
    else:
        x_ref, w_ref, o_ref, wb_ref = refs

    @pl.when(pl.program_id(1) == 0)
    def _():
        wb_ref[...] = w_ref[...].astype(BF16)

    acc = jnp.dot(x_ref[...], wb_ref[...], preferred_element_type=F32)
    if has_res:
        acc = r_ref[...] + acc
    o_ref[...] = acc.astype(o_ref.dtype)


def _mm(x, w, lyr, *, tm, tn, out_dtype=F32, res=None, vmem_mib=48):
    m, k = x.shape
    n = w.shape[-1]
    tm, tn = _pick(m, tm), _pick(n, tn, 128)
    in_specs = [pl.BlockSpec((tm, k), lambda j, i: (i, 0)),
                pl.BlockSpec((None, k, tn), lambda j, i: (lyr, 0, j))]
    args = [x, w]
    if res is not None:
        in_specs.append(pl.BlockSpec((tm, tn), lambda j, i: (i, j)))
        args.append(res)
    return pl.pallas_call(
        functools.partial(_mm_body, has_res=res is not None),
        grid=(n // tn, m // tm),
        in_specs=in_specs,
        out_specs=pl.BlockSpec((tm, tn), lambda j, i: (i, j)),
        out_shape=jax.ShapeDtypeStruct((m, n), out_dtype),
        scratch_shapes=[pltpu.VMEM((k, tn), BF16)],
        compiler_params=_params(("arbitrary", "arbitrary"), vmem_mib),
        name="mm_res" if res is not None else "mm",
    )(*args)


def _swiglu_math(x, wg, wu):
    a = jnp.dot(x, wg, preferred_element_type=F32)
    b = jnp.dot(x, wu, preferred_element_type=F32)
    return (a * jax.nn.sigmoid(a) * b).astype(BF16)


def _swiglu_body(x_ref, wg_ref, wu_ref, o_ref, wgb_ref, wub_ref):
    @pl.when(pl.program_id(1) == 0)
    def _():
        wgb_ref[...] = wg_ref[...].astype(BF16)
        wub_ref[...] = wu_ref[...].astype(BF16)

    o_ref[...] = _swiglu_math(x_ref[...], wgb_ref[...], wub_ref[...])


def _swiglu(x, wg, wu, lyr, *, tm, tn, vmem_mib=48):
    m, k = x.shape
    n = wg.shape[-1]
    tm, tn = _pick(m, tm), _pick(n, tn, 128)
    wspec = pl.BlockSpec((None, k, tn), lambda j, i: (lyr, 0, j))
    return pl.pallas_call(
        _swiglu_body,
        grid=(n // tn, m // tm),
        in_specs=[pl.BlockSpec((tm, k), lambda j, i: (i, 0)), wspec, wspec],
        out_specs=pl.BlockSpec((tm, tn), lambda j, i: (i, j)),
        out_shape=jax.ShapeDtypeStruct((m, n), BF16),
        scratch_shapes=[pltpu.VMEM((k, tn), BF16), pltpu.VMEM((k, tn), BF16)],
        compiler_params=_params(("arbitrary", "arbitrary"), vmem_mib),
        name="swiglu",
    )(x, wg, wu)


def _ple_body(x_ref, p_ref, wg_ref, wp_ref, r_ref, o_ref, wgb_ref, wpb_ref):
    @pl.when(pl.program_id(1) == 0)
    def _():
        wgb_ref[...] = wg_ref[...].astype(BF16)
        wpb_ref[...] = wp_ref[...].astype(BF16)

    a = jnp.dot(x_ref[...], wgb_ref[...], preferred_element_type=F32)
    b = jnp.dot(p_ref[...].astype(BF16), wpb_ref[...], preferred_element_type=F32)
    o_ref[...] = r_ref[...] + jax.nn.sigmoid(a) * b


def _ple(x, p, wg, wp, res, lyr, *, tm, tn, vmem_mib=48):
    m, k = x.shape
    kp = p.shape[-1]
    n = wg.shape[-1]
    tm, tn = _pick(m, tm), _pick(n, tn, 128)
    return pl.pallas_call(
        _ple_body,
        grid=(n // tn, m // tm),
        in_specs=[pl.BlockSpec((tm, k), lambda j, i: (i, 0)),
                  pl.BlockSpec((None, tm, kp), lambda j, i: (lyr, i, 0)),
                  pl.BlockSpec((None, k, tn), lambda j, i: (lyr, 0, j)),
                  pl.BlockSpec((None, kp, tn), lambda j, i: (lyr, 0, j)),
                  pl.BlockSpec((tm, tn), lambda j, i: (i, j))],
        out_specs=pl.BlockSpec((tm, tn), lambda j, i: (i, j)),
        out_shape=jax.ShapeDtypeStruct((m, n), F32),
        scratch_shapes=[pltpu.VMEM((k, tn), BF16), pltpu.VMEM((kp, tn), BF16)],
        compiler_params=_params(("arbitrary", "arbitrary"), vmem_mib),
        name="ple",
    )(x, p, wg, wp, res)


def _expert_changed(e_ref, blk, i):
    return jnp.logical_or(i == 0, e_ref[blk] != e_ref[jnp.maximum(blk - 1, 0)])


def _moe_up_body(e_ref, act_ref, x_ref, wg_ref, wu_ref, o_ref, wgb_ref, wub_ref):
    i = pl.program_id(1)

    @pl.when(_expert_changed(e_ref, i, i))
    def _():
        wgb_ref[...] = wg_ref[...].astype(BF16)
        wub_ref[...] = wu_ref[...].astype(BF16)

    tm = x_ref.shape[0]
    per = tm // MOE_SUB
    n_act = act_ref[i * per]
    for s in range(1, per):
        n_act = n_act + act_ref[i * per + s]
    for c in range(per + 1):
        rows = c * MOE_SUB

        @pl.when(n_act == c)
        def _():
            if rows:
                o_ref[:rows, :] = _swiglu_math(x_ref[:rows, :], wgb_ref[...], wub_ref[...])
            if rows < tm:
                o_ref[rows:, :] = jnp.zeros((tm - rows, o_ref.shape[1]), o_ref.dtype)


def _moe_up(x_sorted, wg, wu, blk_expert, sub_active, lyr, *, tn, vmem_mib):
    ns, k = x_sorted.shape
    n = wg.shape[-1]
    tm, tn = MOE_TM, _pick(n, tn, 128)
    wspec = pl.BlockSpec((None, None, k, tn), lambda j, i, e, act: (lyr, e[i], 0, j))
    grid_spec = pltpu.PrefetchScalarGridSpec(
        num_scalar_prefetch=2,
        grid=(n // tn, ns // tm),
        in_specs=[pl.BlockSpec((tm, k), lambda j, i, e, act: (i, 0)), wspec, wspec],
        out_specs=pl.BlockSpec((tm, tn), lambda j, i, e, act: (i, j)),
        scratch_shapes=[pltpu.VMEM((k, tn), BF16), pltpu.VMEM((k, tn), BF16)],
    )
    return pl.pallas_call(
        _moe_up_body,
        grid_spec=grid_spec,
        out_shape=jax.ShapeDtypeStruct((ns, n), BF16),
        compiler_params=_params(("arbitrary", "arbitrary"), vmem_mib),
        name="moe_up",
    )(blk_expert, sub_active, x_sorted, wg, wu)


def _moe_down_body(e_ref, act_ref, h_ref, wd_ref, o_ref, wdb_ref, *, per):
    i = pl.program_id(1)
    blk = i // per

    @pl.when(jnp.logical_and(_expert_changed(e_ref, blk, i), i % per == 0))
    def _():
        wdb_ref[...] = wd_ref[...].astype(BF16)

    @pl.when(act_ref[i] != 0)
    def _():
        o_ref[...] = jnp.dot(h_ref[...], wdb_ref[...], preferred_element_type=F32)

    @pl.when(act_ref[i] == 0)
    def _():
        o_ref[...] = jnp.zeros_like(o_ref)


def _moe_down(h_sorted, wd, blk_expert, sub_active, lyr, *, tn, vmem_mib):
    ns, k = h_sorted.shape
    n = wd.shape[-1]
    tm, tn = MOE_SUB, _pick(n, tn, 128)
    per = MOE_TM // MOE_SUB
    grid_spec = pltpu.PrefetchScalarGridSpec(
        num_scalar_prefetch=2,
        grid=(n // tn, ns // tm),
        in_specs=[pl.BlockSpec((tm, k), lambda j, i, e, act: (i, 0)),
                  pl.BlockSpec((None, None, k, tn), lambda j, i, e, act: (lyr, e[i // per], 0, j))],
        out_specs=pl.BlockSpec((tm, tn), lambda j, i, e, act: (i, j)),
        scratch_shapes=[pltpu.VMEM((k, tn), BF16)],
    )
    return pl.pallas_call(
        functools.partial(_moe_down_body, per=per),
        grid_spec=grid_spec,
        out_shape=jax.ShapeDtypeStruct((ns, n), F32),
        compiler_params=_params(("arbitrary", "arbitrary"), vmem_mib),
        name="moe_down",
    )(blk_expert, sub_active, h_sorted, wd)


def _row_gather_copy(x_hbm, buf, sem, slot, src_row, dst_row, rows):
    return pltpu.make_async_copy(x_hbm.at[pl.ds(src_row, rows)], buf.at[slot, pl.ds(dst_row, rows)],
                                 sem.at[slot])


def _moe_gather_body(tok_ref, x_hbm, o_ref, buf, sem):
    i = pl.program_id(0)
    tm = o_ref.shape[0]

    def issue(blk, slot):
        def body(r, carry):
            _row_gather_copy(x_hbm, buf, sem, slot, tok_ref[blk * tm + r], r, 1).start()
            return carry
        lax.fori_loop(0, tm, body, 0, unroll=8)

    @pl.when(i == 0)
    def _():
        issue(0, 0)

    @pl.when(i + 1 < pl.num_programs(0))
    def _():
        issue(i + 1, (i + 1) % 2)

    slot = i % 2
    _row_gather_copy(x_hbm, buf, sem, slot, 0, 0, tm).wait()
    o_ref[...] = buf[slot].astype(o_ref.dtype)


def _moe_gather(x, slot_tok):
    _, d = x.shape
    n_slots = slot_tok.shape[0]
    tm = MOE_TM
    grid_spec = pltpu.PrefetchScalarGridSpec(
        num_scalar_prefetch=1,
        grid=(n_slots // tm,),
        in_specs=[pl.BlockSpec(memory_space=pl.ANY)],
        out_specs=pl.BlockSpec((tm, d), lambda i, tok: (i, 0)),
        scratch_shapes=[pltpu.VMEM((2, tm, d), F32), pltpu.SemaphoreType.DMA((2,))],
    )
    return pl.pallas_call(
        _moe_gather_body,
        grid_spec=grid_spec,
        out_shape=jax.ShapeDtypeStruct((n_slots, d), BF16),
        compiler_params=_params(("arbitrary",), 32),
        name="moe_gather",
    )(slot_tok, x)


def _moe_ffn(hn, e_idx, gates, wg, wu, wd, lyr):
    m, d = hn.shape
    n_exp = wg.shape[1]
    top_k = e_idx.shape[-1]
    tm = MOE_TM
    n_asg = m * top_k
    e_flat = e_idx.reshape(n_asg)
    onehot = (e_flat[None, :] == jnp.arange(n_exp, dtype=jnp.int32)[:, None]).astype(jnp.int32)
    csum = jnp.cumsum(onehot, axis=1)
    counts = csum[:, -1]
    padded = (counts + tm - 1) // tm * tm
    pad_end = jnp.cumsum(padded)
    pad_start = pad_end - padded
    slot_of = jnp.sum(onehot * (pad_start[:, None] + csum - 1), axis=0).astype(jnp.int32)
    n_blocks = n_asg // tm + n_exp
    n_slots = n_blocks * tm
    slot_tok = jnp.zeros((n_slots,), jnp.int32).at[slot_of].set(
        jnp.arange(n_asg, dtype=jnp.int32) // top_k)
    n_used = (pad_end[-1] // tm).astype(jnp.int32)
    blk = jnp.arange(n_blocks, dtype=jnp.int32)
    blk_start = jnp.minimum(blk, n_used - 1) * tm
    blk_expert = jnp.sum((pad_end[None, :] <= blk_start[:, None]).astype(jnp.int32), axis=1)
    blk_expert = jnp.minimum(blk_expert, n_exp - 1).astype(jnp.int32)
    sub_start = jnp.arange(n_slots // MOE_SUB, dtype=jnp.int32) * MOE_SUB
    sub_expert = jnp.repeat(blk_expert, tm // MOE_SUB)
    real_end = (pad_start + counts)[sub_expert]
    sub_active = jnp.logical_and(sub_start < real_end, sub_start < pad_end[-1]).astype(jnp.int32)

    x_sorted = _moe_gather(hn, slot_tok)
    h_sorted = _moe_up(x_sorted, wg, wu, blk_expert, sub_active, lyr, tn=1024, vmem_mib=56)
    y_sorted = _moe_down(h_sorted, wd, blk_expert, sub_active, lyr, tn=512, vmem_mib=56)
    sl = slot_of.reshape(m, top_k)
    y = y_sorted[sl[:, 0]] * gates[:, 0:1]
    for t in range(1, top_k):
        y = y + y_sorted[sl[:, t]] * gates[:, t:t + 1]
    return y


def _rel_buckets(n_q, n_k_past, n_k):
    rel = (np.arange(n_k) - n_k_past)[None, :] - np.arange(n_q)[:, None]
    half = N_BUCKETS // 2
    max_exact = half // 2
    n = np.abs(rel)
    nf = np.maximum(n, 1).astype(np.float32)
    large = max_exact + (np.log(nf / np.float32(max_exact)) / np.float32(math.log(REL_MAX_DIST / max_exact))
                         * np.float32(half - max_exact)).astype(np.int32)
    large = np.minimum(large, half - 1)
    return (np.where(rel > 0, half, 0) + np.where(n < max_exact, n, large)).astype(np.int32)


def _build_bias(bias_ref, sink_ref, bucket_ref, table_ref, sinks_ref):
    bucket = bucket_ref[...]
    nq = bucket.shape[1]
    gq = bias_ref.shape[2] // nq
    for kh in range(bias_ref.shape[0]):
        for g in range(gq):
            n = kh * gq + g
            acc = jnp.zeros(bucket.shape, F32)
            for b in range(N_BUCKETS):
                acc = jnp.where(bucket == b, table_ref[b, n], acc)
            bias_ref[kh, :, g * nq:(g + 1) * nq] = acc
            sink_ref[kh, :, g * nq:(g + 1) * nq] = jnp.full((1, nq), sinks_ref[n], F32)


def _attn_items(items, bias_ref, sink_ref):
    nq = items[0][0].shape[0]
    gq = items[0][0].shape[-1] // HEAD_DIM // KV_HEADS
    head = lambda x, n: x[:, n * HEAD_DIM:(n + 1) * HEAD_DIM]
    scores = []
    for q, k, _, valid in items:
        for kh in range(KV_HEADS):
            q4 = jnp.concatenate([head(q, kh * gq + g) for g in range(gq)], axis=0)
            s = lax.dot_general(head(k, kh), q4, (((1,), (1,)), ((), ())), preferred_element_type=F32)
            scores.append((s, kh, valid))
    probs = []
    for s, kh, valid in scores:
        s = s * ATTN_SCALE + bias_ref[kh]
        if valid is not None:
            s = jnp.where(valid, s, NEG_INF)
        sink = sink_ref[kh]
        mx = jnp.maximum(jnp.max(s, axis=0, keepdims=True), sink)
        e = jnp.exp(s - mx)
        den = jnp.sum(e, axis=0, keepdims=True) + jnp.exp(sink - mx)
        probs.append((e * (1.0 / den)).astype(BF16))
    outs = []
    for i, (_, _, v, _) in enumerate(items):
        heads = []
        for kh in range(KV_HEADS):
            o4 = lax.dot_general(probs[i * KV_HEADS + kh], head(v, kh), (((0,), (0,)), ((), ())),
                                 preferred_element_type=F32)
            heads.extend(o4[g * nq:(g + 1) * nq] for g in range(gq))
        outs.append(jnp.concatenate(heads, axis=-1))
    return outs


def _attn_prompt_body(q_ref, kp_ref, kc_ref, vp_ref, vc_ref, bucket_ref, table_ref, sinks_ref,
                      o_ref, bias_ref, sink_ref, *, chunks):
    j = pl.program_id(1)

    @pl.when(jnp.logical_and(pl.program_id(0) == 0, j == 0))
    def _():
        _build_bias(bias_ref, sink_ref, bucket_ref, table_ref, sinks_ref)

    q = q_ref[...].astype(BF16)
    k = jnp.concatenate([kp_ref[...], kc_ref[...]], axis=0).astype(BF16)
    v = jnp.concatenate([vp_ref[...], vc_ref[...]], axis=0).astype(BF16)
    band = WINDOW + CHUNK
    items = []
    for c in range(chunks):
        lo = c * CHUNK
        key_valid = None
        if lo < WINDOW:
            key_pos = j * (chunks * CHUNK) + lo - WINDOW + lax.broadcasted_iota(jnp.int32, (band, 1), 0)
            key_valid = key_pos >= 0
        items.append((q[lo:lo + CHUNK], k[lo:lo + band], v[lo:lo + band], key_valid))
    for c, o in enumerate(_attn_items(items, bias_ref, sink_ref)):
        o_ref[c * CHUNK:(c + 1) * CHUNK, :] = o


def _attn_prompt(z, n_b, seq, d_ssm, d_attn, table, sinks):
    d_kv = KV_HEADS * HEAD_DIM
    assert d_ssm % d_attn == 0 and d_ssm % d_kv == 0 and d_attn % d_kv == 0
    qb = _pick(seq, 4 * CHUNK, 2 * CHUNK)
    assert qb % WINDOW == 0 and seq % qb == 0
    chunks = qb // CHUNK
    nj = seq // qb
    r = qb // WINDOW
    kcol = (d_ssm + d_attn) // d_kv
    gq = d_attn // HEAD_DIM // KV_HEADS
    band = WINDOW + CHUNK
    bucket = jnp.asarray(_rel_buckets(CHUNK, WINDOW, band).T)

    def prev(b, j):
        return b * (seq // WINDOW) + jnp.maximum(j * r - 1, 0)

    return pl.pallas_call(
        functools.partial(_attn_prompt_body, chunks=chunks),
        grid=(n_b, nj),
        in_specs=[pl.BlockSpec((qb, d_attn), lambda b, j: (b * nj + j, d_ssm // d_attn)),
                  pl.BlockSpec((WINDOW, d_kv), lambda b, j: (prev(b, j), kcol)),
                  pl.BlockSpec((qb, d_kv), lambda b, j: (b * nj + j, kcol)),
                  pl.BlockSpec((WINDOW, d_kv), lambda b, j: (prev(b, j), kcol + 1)),
                  pl.BlockSpec((qb, d_kv), lambda b, j: (b * nj + j, kcol + 1)),
                  pl.BlockSpec((band, CHUNK), lambda b, j: (0, 0)),
                  pl.BlockSpec(memory_space=pltpu.SMEM),
                  pl.BlockSpec(memory_space=pltpu.SMEM)],
        out_specs=pl.BlockSpec((qb, d_attn), lambda b, j: (b * nj + j, 0)),
        out_shape=jax.ShapeDtypeStruct((n_b * seq, d_attn), F32),
        scratch_shapes=[pltpu.VMEM((KV_HEADS, band, gq * CHUNK), F32),
                        pltpu.VMEM((KV_HEADS, 1, gq * CHUNK), F32)],
        compiler_params=_params(("arbitrary", "arbitrary"), 40),
        name="attn_prompt",
    )(z, z, z, z, z, bucket, table, sinks)


def _attn_sample_body(q_ref, kc_ref, kn_ref, vc_ref, vn_ref, bucket_ref, table_ref, sinks_ref,
                      o_ref, bias_ref, sink_ref, *, n_bs, n_new, n_past):
    @pl.when(pl.program_id(0) == 0)
    def _():
        _build_bias(bias_ref, sink_ref, bucket_ref, table_ref, sinks_ref)

    q = q_ref[...].astype(BF16)
    kc, kn = kc_ref[...].astype(BF16), kn_ref[...].astype(BF16)
    vc, vn = vc_ref[...].astype(BF16), vn_ref[...].astype(BF16)
    items = []
    for b in range(n_bs):
        new = slice(b * n_new, (b + 1) * n_new)
        past = slice(b * n_past, (b + 1) * n_past)
        items.append((q[new], jnp.concatenate([kc[past], kn[new]], axis=0),
                      jnp.concatenate([vc[past], vn[new]], axis=0), None))
    for b, o in enumerate(_attn_items(items, bias_ref, sink_ref)):
        o_ref[b * n_new:(b + 1) * n_new, :] = o


def _attn_sample(z, row0, n_b, n_new, d_ssm, d_attn, cache_k, cache_v, lyr, table, sinks):
    d_kv = KV_HEADS * HEAD_DIM
    n_past = cache_k.shape[2]
    n_bs = _pick(n_b, 4, 1)
    rows = n_bs * n_new
    assert row0 % rows == 0
    kcol = (d_ssm + d_attn) // d_kv
    gq = d_attn // HEAD_DIM // KV_HEADS
    n_k = n_past + n_new
    bucket = jnp.asarray(_rel_buckets(n_new, n_past, n_k).T)
    ck = cache_k.reshape(-1, d_kv)
    cv = cache_v.reshape(-1, d_kv)
    rb = row0 // rows
    nsteps = n_b // n_bs
    cache_spec = pl.BlockSpec((n_bs * n_past, d_kv), lambda b: (lyr * nsteps + b, 0))
    return pl.pallas_call(
        functools.partial(_attn_sample_body, n_bs=n_bs, n_new=n_new, n_past=n_past),
        grid=(nsteps,),
        in_specs=[pl.BlockSpec((rows, d_attn), lambda b: (rb + b, d_ssm // d_attn)),
                  cache_spec,
                  pl.BlockSpec((rows, d_kv), lambda b: (rb + b, kcol)),
                  cache_spec,
                  pl.BlockSpec((rows, d_kv), lambda b: (rb + b, kcol + 1)),
                  pl.BlockSpec((n_k, n_new), lambda b: (0, 0)),
                  pl.BlockSpec(memory_space=pltpu.SMEM),
                  pl.BlockSpec(memory_space=pltpu.SMEM)],
        out_specs=pl.BlockSpec((rows, d_attn), lambda b: (b, 0)),
        out_shape=jax.ShapeDtypeStruct((n_b * n_new, d_attn), F32),
        scratch_shapes=[pltpu.VMEM((KV_HEADS, n_k, gq * n_new), F32),
                        pltpu.VMEM((KV_HEADS, 1, gq * n_new), F32)],
        compiler_params=_params(("arbitrary",), 40),
        name="attn_sample",
    )(z, ck, z, cv, z, bucket, table, sinks)


def _tile_lanes(x, reps):
    return jnp.concatenate([x] * reps, axis=-1)


def _s5_tables_body(are_ref, aim_ref, ldt_ref, btr_ref, bti_ref, cre_ref, cim_ref, d_ref,
                    dk_ref, we_ref, ws_ref, a_ref):
    rows, p = cre_ref.shape
    gb = rows // SSM_GROUP_CH
    a_re, a_im = are_ref[...], aim_ref[...]
    dt = jnp.exp(ldt_ref[...])
    mag = jnp.exp(a_re * dt)
    ab_re, ab_im = mag * jnp.cos(a_im * dt), mag * jnp.sin(a_im * dt)

    x, y = ab_re - 1.0, ab_im
    den = a_re * a_re + a_im * a_im
    co_re = (x * a_re + y * a_im) / den
    co_im = (y * a_re - x * a_im) / den
    bb_re = co_re * btr_ref[...] - co_im * bti_ref[...]
    bb_im = co_re * bti_ref[...] + co_im * btr_ref[...]
    c_re, c_im = cre_ref[...], cim_ref[...]

    r_idx = lax.broadcasted_iota(jnp.int32, (rows, 1), 0)
    own_state = (r_idx // SSM_GROUP_CH) == (lax.broadcasted_iota(jnp.int32, (1, gb * p), 1) // p)
    same_group = (r_idx // SSM_GROUP_CH) == (lax.broadcasted_iota(jnp.int32, (1, rows), 1) // SSM_GROUP_CH)
    diag = r_idx == lax.broadcasted_iota(jnp.int32, (1, rows), 1)

    def place(v_re, v_im):
        return jnp.concatenate([jnp.where(own_state, _tile_lanes(v_re, gb), 0.0),
                                jnp.where(own_state, _tile_lanes(v_im, gb), 0.0)], axis=-1)

    dims = (((1,), (1,)), ((), ()))
    bb_cat = jnp.concatenate([bb_re, -bb_im], axis=-1)
    pw_re, pw_im = jnp.ones_like(ab_re), jnp.zeros_like(ab_re)
    for j in range(S5_T + 1):
        ca_re = c_re * pw_re - c_im * pw_im
        ca_im = c_re * pw_im + c_im * pw_re
        if j < S5_T:
            kj = lax.dot_general(bb_cat, jnp.concatenate([ca_re, ca_im], axis=-1), dims,
                                 precision=HIGHEST, preferred_element_type=F32)
            kj = jnp.where(same_group, kj, 0.0)
            if j == 0:
                kj = kj + jnp.where(diag, d_ref[...], 0.0)
            dk_ref[j] = kj.astype(dk_ref.dtype)
            s = S5_T - 1 - j
            we_ref[s * rows:(s + 1) * rows, :] = place(
                pw_re * bb_re - pw_im * bb_im, pw_re * bb_im + pw_im * bb_re).astype(we_ref.dtype)
        if j >= 1:
            ws_ref[(j - 1) * rows:j * rows, :] = place(ca_re, -ca_im).astype(ws_ref.dtype)
        if j == S5_T:
            first = r_idx == (lax.broadcasted_iota(jnp.int32, (1, gb * p), 1) // p) * SSM_GROUP_CH
            a_ref[...] = jnp.concatenate(
                [jnp.sum(jnp.where(first, _tile_lanes(pw_re, gb), 0.0), axis=0, keepdims=True),
                 jnp.sum(jnp.where(first, _tile_lanes(pw_im, gb), 0.0), axis=0, keepdims=True)], axis=-1)
        pw_re, pw_im = pw_re * ab_re - pw_im * ab_im, pw_re * ab_im + pw_im * ab_re


def _s5_tables(a_re, a_im, log_dt, b_re, b_im, c_re, c_im, d_skip):
    h, p = a_re.shape
    g = SSM_GROUP_CH
    hb = h // S5_GB
    sw = 2 * S5_GB * p
    rep = lambda v: jnp.repeat(v, g, axis=0).reshape(hb, LANES, -1)
    blk = lambda w: pl.BlockSpec((None, LANES, w), lambda i: (i, 0, 0))
    wide = pl.BlockSpec((None, S5_F, sw), lambda i: (i, 0, 0))
    return pl.pallas_call(
        _s5_tables_body,
        grid=(hb,),
        in_specs=[blk(p), blk(p), blk(1), blk(p), blk(p), blk(p), blk(p), blk(1)],
        out_specs=[pl.BlockSpec((None, S5_T, LANES, LANES), lambda i: (i, 0, 0, 0)), wide, wide,
                   pl.BlockSpec((None, 1, sw), lambda i: (i, 0, 0))],
        out_shape=[jax.ShapeDtypeStruct((hb, S5_T, LANES, LANES), BF16),
                   jax.ShapeDtypeStruct((hb, S5_F, sw), BF16),
                   jax.ShapeDtypeStruct((hb, S5_F, sw), BF16),
                   jax.ShapeDtypeStruct((hb, 1, sw), F32)],
        compiler_params=_params(("arbitrary",), 40),
        name="s5_tables",
    )(rep(a_re), rep(a_im), rep(log_dt.reshape(h, 1)),
      jnp.swapaxes(b_re, 1, 2).reshape(hb, LANES, p), jnp.swapaxes(b_im, 1, 2).reshape(hb, LANES, p),
      c_re.reshape(hb, LANES, p), c_im.reshape(hb, LANES, p), d_skip.reshape(hb, LANES, 1))


def _split_mul(a, p):
    lane = lax.broadcasted_iota(jnp.int32, a.shape, a.ndim - 1)
    sw = pltpu.roll(a, p, axis=a.ndim - 1)
    return jnp.where(lane < p, a, sw), jnp.where(lane < p, -sw, a)


def _cmul(x, mr, mi, p):
    return x * mr + pltpu.roll(x, p, axis=x.ndim - 1) * mi


def _build_toeplitz(m_ref, dk_ref):
    m_ref[...] = jnp.zeros_like(m_ref)
    for s in range(S5_T):
        for t in range(s, S5_T):
            m_ref[s * LANES:(s + 1) * LANES, t * LANES:(t + 1) * LANES] = dk_ref[t - s]


def _s5_outputs(ucat, enter, m_ref, ws_ref):
    return (jnp.dot(ucat, m_ref[...], preferred_element_type=F32)
            + lax.dot_general(enter.astype(BF16), ws_ref[...], (((1,), (1,)), ((), ())),
                              preferred_element_type=F32))


def _glu_rows(o_ref, ga_ref, gb_ref, tile):
    ga, gb = ga_ref[...], gb_ref[...]
    tile = _pick(o_ref.shape[0], tile)
    for r0 in range(0, o_ref.shape[0], tile):
        y = o_ref[r0:r0 + tile, :].astype(BF16)
        o_ref[r0:r0 + tile, :] = (jnp.dot(y, ga, preferred_element_type=F32)
                                  * jax.nn.sigmoid(jnp.dot(y, gb, preferred_element_type=F32)))


def _s5_prompt_body(z_ref, dk_ref, we_ref, ws_ref, a_ref, ga_ref, gb_ref, o_ref, fin_ref, m_ref,
                    *, nb, nc):
    @pl.when(pl.program_id(1) == 0)
    def _():
        _build_toeplitz(m_ref, dk_ref)

    r = nb * nc
    p = a_ref.shape[-1] // 2
    ucat = jnp.concatenate([z_ref[pl.ds(t, r, stride=S5_T), :].astype(BF16) for t in range(S5_T)], axis=-1)
    x = jnp.dot(ucat, we_ref[...], preferred_element_type=F32)
    a = a_ref[...]
    pos = lax.rem(lax.broadcasted_iota(jnp.int32, (r, 1), 0), nc)
    sh = 1
    while sh < nc:
        mr, mi = _split_mul(a, p)
        shifted = jnp.where(pos >= sh, pltpu.roll(x, sh, axis=0), 0.0)
        x = x + _cmul(shifted, mr, mi, p)
        a = _cmul(a, mr, mi, p)
        sh *= 2
    enter = jnp.where(pos >= 1, pltpu.roll(x, 1, axis=0), 0.0)
    fin_ref[...] = jnp.concatenate([x[(b + 1) * nc - 1:(b + 1) * nc] for b in range(nb)], axis=0)
    y = _s5_outputs(ucat, enter, m_ref, ws_ref)
    for t in range(S5_T):
        o_ref[pl.ds(t, r, stride=S5_T), :] = y[:, t * LANES:(t + 1) * LANES]
    _glu_rows(o_ref, ga_ref, gb_ref, 512)


def _s5_sample_body(z_ref, dk_ref, we_ref, ws_ref, a_ref, ga_ref, gb_ref, h0_ref, o_ref, fin_ref, m_ref,
                    *, nb, nc):
    _build_toeplitz(m_ref, dk_ref)
    p = a_ref.shape[-1] // 2
    seq = nc * S5_T
    ucat = jnp.concatenate(
        [jnp.concatenate([z_ref[pl.ds(c * S5_T + t, nb, stride=seq), :].astype(BF16) for t in range(S5_T)],
                         axis=-1) for c in range(nc)], axis=0)
    e = jnp.dot(ucat, we_ref[...], preferred_element_type=F32)
    mr, mi = _split_mul(a_ref[...], p)
    state = h0_ref[...]
    enter = []
    for c in range(nc):
        enter.append(state)
        state = _cmul(state, mr, mi, p) + e[c * nb:(c + 1) * nb]
    fin_ref[...] = state
    y = _s5_outputs(ucat, jnp.concatenate(enter, axis=0), m_ref, ws_ref)
    for c in range(nc):
        for t in range(S5_T):
            o_ref[pl.ds(c * S5_T + t, nb, stride=seq), :] = y[c * nb:(c + 1) * nb, t * LANES:(t + 1) * LANES]
    _glu_rows(o_ref, ga_ref, gb_ref, 512)


def _s5_prompt(z, n_b, seq, tabs, g_a, g_b):
    dk, w_e, w_s, a_t = tabs
    hb, sw = a_t.shape[0], a_t.shape[-1]
    nbs = _pick(n_b, 2, 1)
    nc = seq // S5_T
    tab = lambda arr: pl.BlockSpec((None,) + arr.shape[1:], lambda i, b: (i,) + (0,) * (arr.ndim - 1))
    return pl.pallas_call(
        functools.partial(_s5_prompt_body, nb=nbs, nc=nc),
        grid=(hb, n_b // nbs),
        in_specs=[pl.BlockSpec((nbs * seq, LANES), lambda i, b: (b, i)),
                  tab(dk), tab(w_e), tab(w_s), tab(a_t), tab(g_a), tab(g_b)],
        out_specs=[pl.BlockSpec((nbs * seq, LANES), lambda i, b: (b, i)),
                   pl.BlockSpec((None, None, nbs, sw), lambda i, b: (i, b, 0, 0))],
        out_shape=[jax.ShapeDtypeStruct((n_b * seq, hb * LANES), F32),
                   jax.ShapeDtypeStruct((hb, n_b // nbs, nbs, sw), F32)],
        scratch_shapes=[pltpu.VMEM((S5_F, S5_F), BF16)],
        compiler_params=_params(("arbitrary", "arbitrary"), 52),
        name="s5_prompt",
    )(z, dk, w_e, w_s, a_t, g_a, g_b)


def _s5_sample(z, row0, n_b, n_new, tabs, g_a, g_b, h0):
    dk, w_e, w_s, a_t = tabs
    hb, sw = a_t.shape[0], a_t.shape[-1]
    rows = n_b * n_new
    assert row0 % rows == 0
    rb = row0 // rows
    tab = lambda arr: pl.BlockSpec((None,) + arr.shape[1:], lambda i: (i,) + (0,) * (arr.ndim - 1))
    return pl.pallas_call(
        functools.partial(_s5_sample_body, nb=n_b, nc=n_new // S5_T),
        grid=(hb,),
        in_specs=[pl.BlockSpec((rows, LANES), lambda i: (rb, i)),
                  tab(dk), tab(w_e), tab(w_s), tab(a_t), tab(g_a), tab(g_b), tab(h0)],
        out_specs=[pl.BlockSpec((rows, LANES), lambda i: (0, i)),
                   pl.BlockSpec((None, n_b, sw), lambda i: (i, 0, 0))],
        out_shape=[jax.ShapeDtypeStruct((rows, hb * LANES), F32),
                   jax.ShapeDtypeStruct((hb, n_b, sw), F32)],
        scratch_shapes=[pltpu.VMEM((S5_F, S5_F), BF16)],
        compiler_params=_params(("arbitrary",), 52),
        name="s5_sample",
    )(z, dk, w_e, w_s, a_t, g_a, g_b, h0)


def _glu_blocks(glu_w):
    h, g, _ = glu_w.shape
    hb = h // S5_GB
    eye = jnp.eye(S5_GB, dtype=F32)

    def diag(wm):
        wm = wm.reshape(hb, S5_GB, g, g)
        return (eye[None, :, None, :, None] * wm[:, :, :, None, :]).reshape(hb, LANES, LANES).astype(BF16)

    return diag(glu_w[..., :g]), diag(glu_w[..., g:])


def _state_to_lanes(re, im):
    n, h, p = re.shape
    hb = h // S5_GB
    f = lambda v: v.reshape(n, hb, S5_GB * p).transpose(1, 0, 2)
    return jnp.concatenate([f(re), f(im)], axis=-1)


def _state_from_lanes(s):
    hb, n, sw = s.shape
    half = sw // 2
    f = lambda v: v.transpose(1, 0, 2).reshape(n, hb * S5_GB, half // S5_GB)
    return f(s[..., :half]), f(s[..., half:])


def kernel(x_prompt, x_sample, cache_k, cache_v, state_ssm_re, state_ssm_im, p_prompt, p_sample,
           norm_mix, w_in, ssm_a_re, ssm_a_im, ssm_log_dt, ssm_b_re, ssm_b_im, ssm_c_re, ssm_c_im,
           ssm_d, ssm_glu_w, attn_sinks, rel_bias, norm_grp_ssm, norm_grp_attn, w_out, norm_ffn,
           ffn_w_gate, ffn_w_up, ffn_w_down, router_w, moe_w_gate, moe_w_up, moe_w_down,
           norm_ple, ple_w_gate, ple_w_proj, norm_final):
    nb_p, seq, d = x_prompt.shape
    nb_s, n_new, _ = x_sample.shape
    depth = w_in.shape[0]
    d_ssm = norm_grp_ssm.shape[-1]
    d_attn = norm_grp_attn.shape[-1]
    mp, ms = nb_p * seq, nb_s * n_new
    m = mp + ms
    win = min(WINDOW, seq)
    assert d_ssm % LANES == 0 and seq % S5_T == 0 and n_new % S5_T == 0

    h = jnp.concatenate([x_prompt.reshape(mp, d), x_sample.reshape(ms, d)], axis=0)
    p_all = jnp.concatenate([p_prompt.reshape(depth, mp, -1), p_sample.reshape(depth, ms, -1)], axis=1)

    pk, pv, pre, pim, sk, sv, sre, sim = [], [], [], [], [], [], [], []
    for i in range(depth):
        xn = _rmsnorm(h, norm_mix[i], BF16)
        z = _mm(xn, w_in, i, tm=1024, tn=1280, vmem_mib=56)
        tabs = _s5_tables(ssm_a_re[i], ssm_a_im[i], ssm_log_dt[i], ssm_b_re[i], ssm_b_im[i],
                          ssm_c_re[i], ssm_c_im[i], ssm_d[i])
        g_a, g_b = _glu_blocks(ssm_glu_w[i])
        ssm_p, fin_p = _s5_prompt(z, nb_p, seq, tabs, g_a, g_b)
        ssm_s, fin_s = _s5_sample(z, mp, nb_s, n_new, tabs, g_a, g_b,
                                  _state_to_lanes(state_ssm_re[i], state_ssm_im[i]))
        att_p = _attn_prompt(z, nb_p, seq, d_ssm, d_attn, rel_bias, attn_sinks[i])
        att_s = _attn_sample(z, mp, nb_s, n_new, d_ssm, d_attn, cache_k, cache_v, i, rel_bias,
                             attn_sinks[i])
        merged = _merge_norm(ssm_p, ssm_s, att_p, att_s, norm_grp_ssm[i], norm_grp_attn[i])
        h = _mm(merged, w_out, i, tm=1024, tn=1024, res=h, vmem_mib=56)

        kv = z[:, d_ssm + d_attn:]
        kv_p = kv[:mp].reshape(nb_p, seq, 2, KV_HEADS, HEAD_DIM)[:, seq - win:]
        kv_s = kv[mp:].reshape(nb_s, n_new, 2, KV_HEADS, HEAD_DIM)
        pk.append(kv_p[:, :, 0]); pv.append(kv_p[:, :, 1])
        sk.append(kv_s[:, :, 0]); sv.append(kv_s[:, :, 1])
        f_re, f_im = _state_from_lanes(fin_p.reshape(fin_p.shape[0], nb_p, -1))
        pre.append(f_re); pim.append(f_im)
        f_re, f_im = _state_from_lanes(fin_s)
        sre.append(f_re); sim.append(f_im)

        j = i // 2
        if i % 2 == 0:
            hn = _rmsnorm(h, norm_ffn[i], BF16)
            t = _swiglu(hn, ffn_w_gate, ffn_w_up, j, tm=1024, tn=512)
            h = _mm(t, ffn_w_down, j, tm=512, tn=512, res=h, vmem_mib=52)
        else:
            hn, e_idx, gates = _rmsnorm_router(h, norm_ffn[i], router_w[j])
            h = h + _moe_ffn(hn, e_idx, gates, moe_w_gate, moe_w_up, moe_w_down, j)

        hn = _rmsnorm(h, norm_ple[i], BF16)
        h = _ple(hn, p_all, ple_w_gate, ple_w_proj, h, i, tm=1024, tn=1024, vmem_mib=56)

    y = _rmsnorm(h, norm_final, F32)
    return (y[:mp].reshape(nb_p, seq, d), y[mp:].reshape(nb_s, n_new, d),
            jnp.stack(pk), jnp.stack(pv), jnp.stack(pre), jnp.stack(pim),
            jnp.stack(sk), jnp.stack(sv), jnp.stack(sre), jnp.stack(sim))
```

```python
import functools
import math

import numpy as np
import jax
import jax.numpy as jnp
from jax import lax
from jax.experimental import pallas as pl
from jax.experimental.pallas import tpu as pltpu

F32 = jnp.float32
BF16 = jnp.bfloat16
HIGHEST = lax.Precision.HIGHEST

CHUNK = 64
WINDOW = 128
HEAD_DIM = 64
KV_HEADS = 4
SSM_GROUP_CH = 16
SSM_STATE = 64
N_BUCKETS = 32
REL_MAX_DIST = 128
EPS = 1e-6
NEG_INF = -1e30
ATTN_SCALE = HEAD_DIM ** -0.5

LANES = 128
S5_T = 16
S5_GB = LANES // SSM_GROUP_CH
S5_F = S5_T * LANES
MOE_TM = 512
MOE_SUB = 256
MIB = 1024 * 1024


def _pick(n, pref, mult=8):
    best = None
    for d in range(mult, min(n, pref) + 1, mult):
        if n % d == 0:
            best = d
    return best if best is not None else n


def _params(sem, vmem_mib):
    return pltpu.CompilerParams(dimension_semantics=sem, vmem_limit_bytes=int(vmem_mib * MIB))


def _rms(x, g):
    return x * lax.rsqrt(jnp.mean(x * x, axis=-1, keepdims=True) + EPS) * g


def _norm_body(x_ref, g_ref, o_ref):
    o_ref[...] = _rms(x_ref[...], g_ref[...]).astype(o_ref.dtype)


def _rmsnorm(x, g, out_dtype):
    m, d = x.shape
    tm = _pick(m, 512)
    return pl.pallas_call(
        _norm_body,
        grid=(m // tm,),
        in_specs=[pl.BlockSpec((tm, d), lambda i: (i, 0)), pl.BlockSpec((1, d), lambda i: (0, 0))],
        out_specs=pl.BlockSpec((tm, d), lambda i: (i, 0)),
        out_shape=jax.ShapeDtypeStruct((m, d), out_dtype),
        compiler_params=_params(("arbitrary",), 40),
        name="rmsnorm",
    )(x, g.reshape(1, d))


def _norm_router_body(x_ref, g_ref, rw_ref, o_ref, idx_ref, gate_ref):
    y = _rms(x_ref[...], g_ref[...])
    tm, d = y.shape
    pieces = d // LANES
    for s in range(pieces):
        o_ref[pl.ds(s, tm, stride=pieces), :] = y[:, s * LANES:(s + 1) * LANES]
    logits = jnp.dot(y, rw_ref[...], precision=HIGHEST, preferred_element_type=F32)
    n_exp = logits.shape[-1]
    lane = lax.broadcasted_iota(jnp.int32, logits.shape, 1)
    m1 = jnp.max(logits, axis=-1, keepdims=True)
    i1 = jnp.min(jnp.where(logits == m1, lane, n_exp), axis=-1, keepdims=True)
    rest = jnp.where(lane == i1, -jnp.inf, logits)
    m2 = jnp.max(rest, axis=-1, keepdims=True)
    i2 = jnp.min(jnp.where(rest == m2, lane, n_exp), axis=-1, keepdims=True)
    e2 = jnp.exp(m2 - m1)
    den = 1.0 + e2
    idx_ref[...] = jnp.concatenate([i1, i2], axis=-1)
    gate_ref[...] = jnp.concatenate([1.0 / den, e2 / den], axis=-1)


def _rmsnorm_router(x, g, router_w):
    m, d = x.shape
    n_exp = router_w.shape[-1]
    tm = _pick(m, 512)
    return pl.pallas_call(
        _norm_router_body,
        grid=(m // tm,),
        in_specs=[pl.BlockSpec((tm, d), lambda i: (i, 0)),
                  pl.BlockSpec((1, d), lambda i: (0, 0)),
                  pl.BlockSpec((d, n_exp), lambda i: (0, 0))],
        out_specs=[pl.BlockSpec((tm * (d // LANES), LANES), lambda i: (i, 0)),
                   pl.BlockSpec((tm, 2), lambda i: (i, 0)),
                   pl.BlockSpec((tm, 2), lambda i: (i, 0))],
        out_shape=[jax.ShapeDtypeStruct((m * (d // LANES), LANES), F32),
                   jax.ShapeDtypeStruct((m, 2), jnp.int32),
                   jax.ShapeDtypeStruct((m, 2), F32)],
        compiler_params=_params(("arbitrary",), 40),
        name="rmsnorm_router",
    )(x, g.reshape(1, d), router_w)


def _merge_norm_body(ap_ref, as_ref, bp_ref, bs_ref, ga_ref, gb_ref, o_ref, *, np_tiles):
    da = ap_ref.shape[-1]

    def emit(a_ref, b_ref):
        o_ref[:, :da] = _rms(a_ref[...], ga_ref[...]).astype(o_ref.dtype)
        o_ref[:, da:] = _rms(b_ref[...], gb_ref[...]).astype(o_ref.dtype)

    @pl.when(pl.program_id(0) < np_tiles)
    def _():
        emit(ap_ref, bp_ref)

    @pl.when(pl.program_id(0) >= np_tiles)
    def _():
        emit(as_ref, bs_ref)


def _merge_norm(a_p, a_s, b_p, b_s, ga, gb):
    (mp, da), (ms, db) = a_p.shape, b_s.shape
    tm = _pick(math.gcd(mp, ms), 512)
    npt, nst = mp // tm, ms // tm
    pspec = lambda w: pl.BlockSpec((tm, w), lambda i: (jnp.minimum(i, npt - 1), 0))
    sspec = lambda w: pl.BlockSpec((tm, w), lambda i: (jnp.maximum(i - npt, 0), 0))
    return pl.pallas_call(
        functools.partial(_merge_norm_body, np_tiles=npt),
        grid=(npt + nst,),
        in_specs=[pspec(da), sspec(da), pspec(db), sspec(db),
                  pl.BlockSpec((1, da), lambda i: (0, 0)), pl.BlockSpec((1, db), lambda i: (0, 0))],
        out_specs=pl.BlockSpec((tm, da + db), lambda i: (i, 0)),
        out_shape=jax.ShapeDtypeStruct((mp + ms, da + db), BF16),
        compiler_params=_params(("arbitrary",), 40),
        name="merge_norm",
    )(a_p, a_s, b_p, b_s, ga.reshape(1, da), gb.reshape(1, db))


def _mm_body(*refs, has_res):
    if has_res:
        x_ref, w_ref, r_ref, o_ref, wb_ref = refs
    else:
        x_ref, w_ref, o_ref, wb_ref = refs

    @pl.when(pl.program_id(1) == 0)
    def _():
        wb_ref[...] = w_ref[...].astype(BF16)

    acc = jnp.dot(x_ref[...], wb_ref[...], preferred_element_type=F32)
    if has_res:
        acc = r_ref[...] + acc
    o_ref[...] = acc.astype(o_ref.dtype)


def _mm(x, w, lyr, *, tm, tn, out_dtype=F32, res=None, vmem_mib=48):
    m, k = x.shape
    n = w.shape[-1]
    tm, tn = _pick(m, tm), _pick(n, tn, 128)
    in_specs = [pl.BlockSpec((tm, k), lambda j, i: (i, 0)),
                pl.BlockSpec((None, k, tn), lambda j, i: (lyr, 0, j))]
    args = [x, w]
    if res is not None:
        in_specs.append(pl.BlockSpec((tm, tn), lambda j, i: (i, j)))
        args.append(res)
    return pl.pallas_call(
        functools.partial(_mm_body, has_res=res is not None),
        grid=(n // tn, m // tm),
        in_specs=in_specs,
        out_specs=pl.BlockSpec((tm, tn), lambda j, i: (i, j)),
        out_shape=jax.ShapeDtypeStruct((m, n), out_dtype),
        scratch_shapes=[pltpu.VMEM((k, tn), BF16)],
        compiler_params=_params(("arbitrary", "arbitrary"), vmem_mib),
        name="mm_res" if res is not None else "mm",
    )(*args)


def _swiglu_math(x, wg, wu):
    a = jnp.dot(x, wg, preferred_element_type=F32)
    b = jnp.dot(x, wu, preferred_element_type=F32)
    return (a * jax.nn.sigmoid(a) * b).astype(BF16)


def _swiglu_body(x_ref, wg_ref, wu_ref, o_ref, wgb_ref, wub_ref):
    @pl.when(pl.program_id(1) == 0)
    def _():
        wgb_ref[...] = wg_ref[...].astype(BF16)
        wub_ref[...] = wu_ref[...].astype(BF16)

    o_ref[...] = _swiglu_math(x_ref[...], wgb_ref[...], wub_ref[...])


def _swiglu(x, wg, wu, lyr, *, tm, tn, vmem_mib=48):
    m, k = x.shape
    n = wg.shape[-1]
    tm, tn = _pick(m, tm), _pick(n, tn, 128)
    wspec = pl.BlockSpec((None, k, tn), lambda j, i: (lyr, 0, j))
    return pl.pallas_call(
        _swiglu_body,
        grid=(n // tn, m // tm),
        in_specs=[pl.BlockSpec((tm, k), lambda j, i: (i, 0)), wspec, wspec],
        out_specs=pl.BlockSpec((tm, tn), lambda j, i: (i, j)),
        out_shape=jax.ShapeDtypeStruct((m, n), BF16),
        scratch_shapes=[pltpu.VMEM((k, tn), BF16), pltpu.VMEM((k, tn), BF16)],
        compiler_params=_params(("arbitrary", "arbitrary"), vmem_mib),
        name="swiglu",
    )(x, wg, wu)


def _ple_body(x_ref, p_ref, wg_ref, wp_ref, r_ref, o_ref, wgb_ref, wpb_ref):
    @pl.when(pl.program_id(1) == 0)
    def _():
        wgb_ref[...] = wg_ref[...].astype(BF16)
        wpb_ref[...] = wp_ref[...].astype(BF16)

    a = jnp.dot(x_ref[...], wgb_ref[...], preferred_element_type=F32)
    b = jnp.dot(p_ref[...].astype(BF16), wpb_ref[...], preferred_element_type=F32)
    o_ref[...] = r_ref[...] + jax.nn.sigmoid(a) * b


def _ple(x, p, wg, wp, res, lyr, *, tm, tn, vmem_mib=48):
    m, k = x.shape
    kp = p.shape[-1]
    n = wg.shape[-1]
    tm, tn = _pick(m, tm), _pick(n, tn, 128)
    return pl.pallas_call(
        _ple_body,
        grid=(n // tn, m // tm),
        in_specs=[pl.BlockSpec((tm, k), lambda j, i: (i, 0)),
                  pl.BlockSpec((None, tm, kp), lambda j, i: (lyr, i, 0)),
                  pl.BlockSpec((None, k, tn), lambda j, i: (lyr, 0, j)),
                  pl.BlockSpec((None, kp, tn), lambda j, i: (lyr, 0, j)),
                  pl.BlockSpec((tm, tn), lambda j, i: (i, j))],
        out_specs=pl.BlockSpec((tm, tn), lambda j, i: (i, j)),
        out_shape=jax.ShapeDtypeStruct((m, n), F32),
        scratch_shapes=[pltpu.VMEM((k, tn), BF16), pltpu.VMEM((kp, tn), BF16)],
        compiler_params=_params(("arbitrary", "arbitrary"), vmem_mib),
        name="ple",
    )(x, p, wg, wp, res)


def _expert_changed(e_ref, blk, i):
    return jnp.logical_or(i == 0, e_ref[blk] != e_ref[jnp.maximum(blk - 1, 0)])


def _moe_up_body(e_ref, act_ref, x_ref, wg_ref, wu_ref, o_ref, wgb_ref, wub_ref):
    i = pl.program_id(1)

    @pl.when(_expert_changed(e_ref, i, i))
    def _():
        wgb_ref[...] = wg_ref[...].astype(BF16)
        wub_ref[...] = wu_ref[...].astype(BF16)

    tm = x_ref.shape[0]
    per = tm // MOE_SUB
    n_act = act_ref[i * per]
    for s in range(1, per):
        n_act = n_act + act_ref[i * per + s]
    for c in range(per + 1):
        rows = c * MOE_SUB

        @pl.when(n_act == c)
        def _():
            if rows:
                o_ref[:rows, :] = _swiglu_math(x_ref[:rows, :], wgb_ref[...], wub_ref[...])
            if rows < tm:
                o_ref[rows:, :] = jnp.zeros((tm - rows, o_ref.shape[1]), o_ref.dtype)


def _moe_up(x_sorted, wg, wu, blk_expert, sub_active, lyr, *, tn, vmem_mib):
    ns, k = x_sorted.shape
    n = wg.shape[-1]
    tm, tn = MOE_TM, _pick(n, tn, 128)
    wspec = pl.BlockSpec((None, None, k, tn), lambda j, i, e, act: (lyr, e[i], 0, j))
    grid_spec = pltpu.PrefetchScalarGridSpec(
        num_scalar_prefetch=2,
        grid=(n // tn, ns // tm),
        in_specs=[pl.BlockSpec((tm, k), lambda j, i, e, act: (i, 0)), wspec, wspec],
        out_specs=pl.BlockSpec((tm, tn), lambda j, i, e, act: (i, j)),
        scratch_shapes=[pltpu.VMEM((k, tn), BF16), pltpu.VMEM((k, tn), BF16)],
    )
    return pl.pallas_call(
        _moe_up_body,
        grid_spec=grid_spec,
        out_shape=jax.ShapeDtypeStruct((ns, n), BF16),
        compiler_params=_params(("arbitrary", "arbitrary"), vmem_mib),
        name="moe_up",
    )(blk_expert, sub_active, x_sorted, wg, wu)


def _moe_down_body(e_ref, act_ref, h_ref, wd_ref, o_ref, wdb_ref, *, per):
    i = pl.program_id(1)
    blk = i // per

    @pl.when(jnp.logical_and(_expert_changed(e_ref, blk, i), i % per == 0))
    def _():
        wdb_ref[...] = wd_ref[...].astype(BF16)

    @pl.when(act_ref[i] != 0)
    def _():
        o_ref[...] = jnp.dot(h_ref[...], wdb_ref[...], preferred_element_type=F32)

    @pl.when(act_ref[i] == 0)
    def _():
        o_ref[...] = jnp.zeros_like(o_ref)


def _moe_down(h_sorted, wd, blk_expert, sub_active, lyr, *, tn, vmem_mib):
    ns, k = h_sorted.shape
    n = wd.shape[-1]
    tm, tn = MOE_SUB, _pick(n, tn, 128)
    per = MOE_TM // MOE_SUB
    grid_spec = pltpu.PrefetchScalarGridSpec(
        num_scalar_prefetch=2,
        grid=(n // tn, ns // tm),
        in_specs=[pl.BlockSpec((tm, k), lambda j, i, e, act: (i, 0)),
                  pl.BlockSpec((None, None, k, tn), lambda j, i, e, act: (lyr, e[i // per], 0, j))],
        out_specs=pl.BlockSpec((tm, tn), lambda j, i, e, act: (i, j)),
        scratch_shapes=[pltpu.VMEM((k, tn), BF16)],
    )
    return pl.pallas_call(
        functools.partial(_moe_down_body, per=per),
        grid_spec=grid_spec,
        out_shape=jax.ShapeDtypeStruct((ns, n), F32),
        compiler_params=_params(("arbitrary", "arbitrary"), vmem_mib),
        name="moe_down",
    )(blk_expert, sub_active, h_sorted, wd)


def _row_gather_copy(x_hbm, buf, sem, slot, src_row, dst_row, rows):
    return pltpu.make_async_copy(x_hbm.at[pl.ds(src_row, rows)], buf.at[slot, pl.ds(dst_row, rows)],
                                 sem.at[slot])


def _moe_gather_body(tok_ref, x_hbm, o_ref, buf, sem):
    i = pl.program_id(0)
    tm = o_ref.shape[0]
    pieces = o_ref.shape[1] // LANES

    def issue(blk, slot):
        def body(r, carry):
            _row_gather_copy(x_hbm, buf, sem, slot, tok_ref[blk * tm + r] * pieces, r * pieces, pieces).start()
            return carry
        lax.fori_loop(0, tm, body, 0, unroll=8)

    @pl.when(i == 0)
    def _():
        issue(0, 0)

    @pl.when(i + 1 < pl.num_programs(0))
    def _():
        issue(i + 1, (i + 1) % 2)

    slot = i % 2
    _row_gather_copy(x_hbm, buf, sem, slot, 0, 0, tm * pieces).wait()
    cur = buf.at[slot]
    o_ref[...] = jnp.concatenate([cur[pl.ds(s, tm, stride=pieces), :] for s in range(pieces)],
                                 axis=-1).astype(o_ref.dtype)


def _moe_gather(x_rows, slot_tok, d):
    n_slots = slot_tok.shape[0]
    tm = MOE_TM
    grid_spec = pltpu.PrefetchScalarGridSpec(
        num_scalar_prefetch=1,
        grid=(n_slots // tm,),
        in_specs=[pl.BlockSpec(memory_space=pl.ANY)],
        out_specs=pl.BlockSpec((tm, d), lambda i, tok: (i, 0)),
        scratch_shapes=[pltpu.VMEM((2, tm * (d // LANES), LANES), F32), pltpu.SemaphoreType.DMA((2,))],
    )
    return pl.pallas_call(
        _moe_gather_body,
        grid_spec=grid_spec,
        out_shape=jax.ShapeDtypeStruct((n_slots, d), BF16),
        compiler_params=_params(("arbitrary",), 32),
        name="moe_gather",
    )(slot_tok, x_rows)


def _moe_ffn(hn_rows, e_idx, gates, wg, wu, wd, lyr):
    m = e_idx.shape[0]
    d = wg.shape[2]
    n_exp = wg.shape[1]
    top_k = e_idx.shape[-1]
    tm = MOE_TM
    n_asg = m * top_k
    e_flat = e_idx.reshape(n_asg)
    onehot = (e_flat[None, :] == jnp.arange(n_exp, dtype=jnp.int32)[:, None]).astype(jnp.int32)
    csum = jnp.cumsum(onehot, axis=1)
    counts = csum[:, -1]
    padded = (counts + tm - 1) // tm * tm
    pad_end = jnp.cumsum(padded)
    pad_start = pad_end - padded
    slot_of = jnp.sum(onehot * (pad_start[:, None] + csum - 1), axis=0).astype(jnp.int32)
    n_blocks = n_asg // tm + n_exp
    n_slots = n_blocks * tm
    slot_tok = jnp.zeros((n_slots,), jnp.int32).at[slot_of].set(
        jnp.arange(n_asg, dtype=jnp.int32) // top_k)
    n_used = (pad_end[-1] // tm).astype(jnp.int32)
    blk = jnp.arange(n_blocks, dtype=jnp.int32)
    blk_start = jnp.minimum(blk, n_used - 1) * tm
    blk_expert = jnp.sum((pad_end[None, :] <= blk_start[:, None]).astype(jnp.int32), axis=1)
    blk_expert = jnp.minimum(blk_expert, n_exp - 1).astype(jnp.int32)
    sub_start = jnp.arange(n_slots // MOE_SUB, dtype=jnp.int32) * MOE_SUB
    sub_expert = jnp.repeat(blk_expert, tm // MOE_SUB)
    real_end = (pad_start + counts)[sub_expert]
    sub_active = jnp.logical_and(sub_start < real_end, sub_start < pad_end[-1]).astype(jnp.int32)

    x_sorted = _moe_gather(hn_rows, slot_tok, d)
    h_sorted = _moe_up(x_sorted, wg, wu, blk_expert, sub_active, lyr, tn=1024, vmem_mib=56)
    y_sorted = _moe_down(h_sorted, wd, blk_expert, sub_active, lyr, tn=512, vmem_mib=56)
    sl = slot_of.reshape(m, top_k)
    y = y_sorted[sl[:, 0]] * gates[:, 0:1]
    for t in range(1, top_k):
        y = y + y_sorted[sl[:, t]] * gates[:, t:t + 1]
    return y


def _rel_buckets(n_q, n_k_past, n_k):
    rel = (np.arange(n_k) - n_k_past)[None, :] - np.arange(n_q)[:, None]
    half = N_BUCKETS // 2
    max_exact = half // 2
    n = np.abs(rel)
    nf = np.maximum(n, 1).astype(np.float32)
    large = max_exact + (np.log(nf / np.float32(max_exact)) / np.float32(math.log(REL_MAX_DIST / max_exact))
                         * np.float32(half - max_exact)).astype(np.int32)
    large = np.minimum(large, half - 1)
    return (np.where(rel > 0, half, 0) + np.where(n < max_exact, n, large)).astype(np.int32)


def _build_bias(bias_ref, sink_ref, bucket_ref, table_ref, sinks_ref):
    bucket = bucket_ref[...]
    nq = bucket.shape[1]
    gq = bias_ref.shape[2] // nq
    for kh in range(bias_ref.shape[0]):
        for g in range(gq):
            n = kh * gq + g
            acc = jnp.zeros(bucket.shape, F32)
            for b in range(N_BUCKETS):
                acc = jnp.where(bucket == b, table_ref[b, n], acc)
            bias_ref[kh, :, g * nq:(g + 1) * nq] = acc
            sink_ref[kh, :, g * nq:(g + 1) * nq] = jnp.full((1, nq), sinks_ref[n], F32)


def _attn_items(items, bias_ref, sink_ref):
    nq = items[0][0].shape[0]
    gq = items[0][0].shape[-1] // HEAD_DIM // KV_HEADS
    head = lambda x, n: x[:, n * HEAD_DIM:(n + 1) * HEAD_DIM]
    scores = []
    for q, k, _, valid in items:
        for kh in range(KV_HEADS):
            q4 = jnp.concatenate([head(q, kh * gq + g) for g in range(gq)], axis=0)
            s = lax.dot_general(head(k, kh), q4, (((1,), (1,)), ((), ())), preferred_element_type=F32)
            scores.append((s, kh, valid))
    probs = []
    for s, kh, valid in scores:
        s = s * ATTN_SCALE + bias_ref[kh]
        if valid is not None:
            s = jnp.where(valid, s, NEG_INF)
        sink = sink_ref[kh]
        mx = jnp.maximum(jnp.max(s, axis=0, keepdims=True), sink)
        e = jnp.exp(s - mx)
        den = jnp.sum(e, axis=0, keepdims=True) + jnp.exp(sink - mx)
        probs.append((e * (1.0 / den)).astype(BF16))
    outs = []
    for i, (_, _, v, _) in enumerate(items):
        heads = []
        for kh in range(KV_HEADS):
            o4 = lax.dot_general(probs[i * KV_HEADS + kh], head(v, kh), (((0,), (0,)), ((), ())),
                                 preferred_element_type=F32)
            heads.extend(o4[g * nq:(g + 1) * nq] for g in range(gq))
        outs.append(jnp.concatenate(heads, axis=-1))
    return outs


def _attn_prompt_body(q_ref, kp_ref, kc_ref, vp_ref, vc_ref, bucket_ref, table_ref, sinks_ref,
                      o_ref, bias_ref, sink_ref, *, chunks):
    j = pl.program_id(1)

    @pl.when(jnp.logical_and(pl.program_id(0) == 0, j == 0))
    def _():
        _build_bias(bias_ref, sink_ref, bucket_ref, table_ref, sinks_ref)

    q = q_ref[...].astype(BF16)
    k = jnp.concatenate([kp_ref[...], kc_ref[...]], axis=0).astype(BF16)
    v = jnp.concatenate([vp_ref[...], vc_ref[...]], axis=0).astype(BF16)
    band = WINDOW + CHUNK
    items = []
    for c in range(chunks):
        lo = c * CHUNK
        key_valid = None
        if lo < WINDOW:
            key_pos = j * (chunks * CHUNK) + lo - WINDOW + lax.broadcasted_iota(jnp.int32, (band, 1), 0)
            key_valid = key_pos >= 0
        items.append((q[lo:lo + CHUNK], k[lo:lo + band], v[lo:lo + band], key_valid))
    for c, o in enumerate(_attn_items(items, bias_ref, sink_ref)):
        o_ref[c * CHUNK:(c + 1) * CHUNK, :] = o


def _attn_prompt(z, n_b, seq, d_ssm, d_attn, table, sinks):
    d_kv = KV_HEADS * HEAD_DIM
    assert d_ssm % d_attn == 0 and d_ssm % d_kv == 0 and d_attn % d_kv == 0
    qb = _pick(seq, 4 * CHUNK, 2 * CHUNK)
    assert qb % WINDOW == 0 and seq % qb == 0
    chunks = qb // CHUNK
    nj = seq // qb
    r = qb // WINDOW
    kcol = (d_ssm + d_attn) // d_kv
    gq = d_attn // HEAD_DIM // KV_HEADS
    band = WINDOW + CHUNK
    bucket = jnp.asarray(_rel_buckets(CHUNK, WINDOW, band).T)

    def prev(b, j):
        return b * (seq // WINDOW) + jnp.maximum(j * r - 1, 0)

    return pl.pallas_call(
        functools.partial(_attn_prompt_body, chunks=chunks),
        grid=(n_b, nj),
        in_specs=[pl.BlockSpec((qb, d_attn), lambda b, j: (b * nj + j, d_ssm // d_attn)),
                  pl.BlockSpec((WINDOW, d_kv), lambda b, j: (prev(b, j), kcol)),
                  pl.BlockSpec((qb, d_kv), lambda b, j: (b * nj + j, kcol)),
                  pl.BlockSpec((WINDOW, d_kv), lambda b, j: (prev(b, j), kcol + 1)),
                  pl.BlockSpec((qb, d_kv), lambda b, j: (b * nj + j, kcol + 1)),
                  pl.BlockSpec((band, CHUNK), lambda b, j: (0, 0)),
                  pl.BlockSpec(memory_space=pltpu.SMEM),
                  pl.BlockSpec(memory_space=pltpu.SMEM)],
        out_specs=pl.BlockSpec((qb, d_attn), lambda b, j: (b * nj + j, 0)),
        out_shape=jax.ShapeDtypeStruct((n_b * seq, d_attn), F32),
        scratch_shapes=[pltpu.VMEM((KV_HEADS, band, gq * CHUNK), F32),
                        pltpu.VMEM((KV_HEADS, 1, gq * CHUNK), F32)],
        compiler_params=_params(("arbitrary", "arbitrary"), 40),
        name="attn_prompt",
    )(z, z, z, z, z, bucket, table, sinks)


def _attn_sample_body(q_ref, kc_ref, kn_ref, vc_ref, vn_ref, bucket_ref, table_ref, sinks_ref,
                      o_ref, bias_ref, sink_ref, *, n_bs, n_new, n_past):
    @pl.when(pl.program_id(0) == 0)
    def _():
        _build_bias(bias_ref, sink_ref, bucket_ref, table_ref, sinks_ref)

    q = q_ref[...].astype(BF16)
    kc, kn = kc_ref[...].astype(BF16), kn_ref[...].astype(BF16)
    vc, vn = vc_ref[...].astype(BF16), vn_ref[...].astype(BF16)
    items = []
    for b in range(n_bs):
        new = slice(b * n_new, (b + 1) * n_new)
        past = slice(b * n_past, (b + 1) * n_past)
        items.append((q[new], jnp.concatenate([kc[past], kn[new]], axis=0),
                      jnp.concatenate([vc[past], vn[new]], axis=0), None))
    for b, o in enumerate(_attn_items(items, bias_ref, sink_ref)):
        o_ref[b * n_new:(b + 1) * n_new, :] = o


def _attn_sample(z, row0, n_b, n_new, d_ssm, d_attn, cache_k, cache_v, lyr, table, sinks):
    d_kv = KV_HEADS * HEAD_DIM
    n_past = cache_k.shape[2]
    n_bs = _pick(n_b, 4, 1)
    rows = n_bs * n_new
    assert row0 % rows == 0
    kcol = (d_ssm + d_attn) // d_kv
    gq = d_attn // HEAD_DIM // KV_HEADS
    n_k = n_past + n_new
    bucket = jnp.asarray(_rel_buckets(n_new, n_past, n_k).T)
    ck = cache_k.reshape(-1, d_kv)
    cv = cache_v.reshape(-1, d_kv)
    rb = row0 // rows
    nsteps = n_b // n_bs
    cache_spec = pl.BlockSpec((n_bs * n_past, d_kv), lambda b: (lyr * nsteps + b, 0))
    return pl.pallas_call(
        functools.partial(_attn_sample_body, n_bs=n_bs, n_new=n_new, n_past=n_past),
        grid=(nsteps,),
        in_specs=[pl.BlockSpec((rows, d_attn), lambda b: (rb + b, d_ssm // d_attn)),
                  cache_spec,
                  pl.BlockSpec((rows, d_kv), lambda b: (rb + b, kcol)),
                  cache_spec,
                  pl.BlockSpec((rows, d_kv), lambda b: (rb + b, kcol + 1)),
                  pl.BlockSpec((n_k, n_new), lambda b: (0, 0)),
                  pl.BlockSpec(memory_space=pltpu.SMEM),
                  pl.BlockSpec(memory_space=pltpu.SMEM)],
        out_specs=pl.BlockSpec((rows, d_attn), lambda b: (b, 0)),
        out_shape=jax.ShapeDtypeStruct((n_b * n_new, d_attn), F32),
        scratch_shapes=[pltpu.VMEM((KV_HEADS, n_k, gq * n_new), F32),
                        pltpu.VMEM((KV_HEADS, 1, gq * n_new), F32)],
        compiler_params=_params(("arbitrary",), 40),
        name="attn_sample",
    )(z, ck, z, cv, z, bucket, table, sinks)


def _tile_lanes(x, reps):
    return jnp.concatenate([x] * reps, axis=-1)


def _s5_tables_body(are_ref, aim_ref, ldt_ref, btr_ref, bti_ref, cre_ref, cim_ref, d_ref,
                    dk_ref, we_ref, ws_ref, a_ref):
    rows, p = cre_ref.shape
    gb = rows // SSM_GROUP_CH
    a_re, a_im = are_ref[...], aim_ref[...]
    dt = jnp.exp(ldt_ref[...])
    mag = jnp.exp(a_re * dt)
    ab_re, ab_im = mag * jnp.cos(a_im * dt), mag * jnp.sin(a_im * dt)

    x, y = ab_re - 1.0, ab_im
    den = a_re * a_re + a_im * a_im
    co_re = (x * a_re + y * a_im) / den
    co_im = (y * a_re - x * a_im) / den
    bb_re = co_re * btr_ref[...] - co_im * bti_ref[...]
    bb_im = co_re * bti_ref[...] + co_im * btr_ref[...]
    c_re, c_im = cre_ref[...], cim_ref[...]

    r_idx = lax.broadcasted_iota(jnp.int32, (rows, 1), 0)
    own_state = (r_idx // SSM_GROUP_CH) == (lax.broadcasted_iota(jnp.int32, (1, gb * p), 1) // p)
    same_group = (r_idx // SSM_GROUP_CH) == (lax.broadcasted_iota(jnp.int32, (1, rows), 1) // SSM_GROUP_CH)
    diag = r_idx == lax.broadcasted_iota(jnp.int32, (1, rows), 1)

    def place(v_re, v_im):
        return jnp.concatenate([jnp.where(own_state, _tile_lanes(v_re, gb), 0.0),
                                jnp.where(own_state, _tile_lanes(v_im, gb), 0.0)], axis=-1)

    dims = (((1,), (1,)), ((), ()))
    bb_cat = jnp.concatenate([bb_re, -bb_im], axis=-1)
    pw_re, pw_im = jnp.ones_like(ab_re), jnp.zeros_like(ab_re)
    for j in range(S5_T + 1):
        ca_re = c_re * pw_re - c_im * pw_im
        ca_im = c_re * pw_im + c_im * pw_re
        if j < S5_T:
            kj = lax.dot_general(bb_cat, jnp.concatenate([ca_re, ca_im], axis=-1), dims,
                                 precision=HIGHEST, preferred_element_type=F32)
            kj = jnp.where(same_group, kj, 0.0)
            if j == 0:
                kj = kj + jnp.where(diag, d_ref[...], 0.0)
            dk_ref[j] = kj.astype(dk_ref.dtype)
            s = S5_T - 1 - j
            we_ref[s * rows:(s + 1) * rows, :] = place(
                pw_re * bb_re - pw_im * bb_im, pw_re * bb_im + pw_im * bb_re).astype(we_ref.dtype)
        if j >= 1:
            ws_ref[(j - 1) * rows:j * rows, :] = place(ca_re, -ca_im).astype(ws_ref.dtype)
        if j == S5_T:
            first = r_idx == (lax.broadcasted_iota(jnp.int32, (1, gb * p), 1) // p) * SSM_GROUP_CH
            a_ref[...] = jnp.concatenate(
                [jnp.sum(jnp.where(first, _tile_lanes(pw_re, gb), 0.0), axis=0, keepdims=True),
                 jnp.sum(jnp.where(first, _tile_lanes(pw_im, gb), 0.0), axis=0, keepdims=True)], axis=-1)
        pw_re, pw_im = pw_re * ab_re - pw_im * ab_im, pw_re * ab_im + pw_im * ab_re


def _s5_tables(a_re, a_im, log_dt, b_re, b_im, c_re, c_im, d_skip):
    h, p = a_re.shape
    g = SSM_GROUP_CH
    hb = h // S5_GB
    sw = 2 * S5_GB * p
    rep = lambda v: jnp.repeat(v, g, axis=0).reshape(hb, LANES, -1)
    blk = lambda w: pl.BlockSpec((None, LANES, w), lambda i: (i, 0, 0))
    wide = pl.BlockSpec((None, S5_F, sw), lambda i: (i, 0, 0))
    return pl.pallas_call(
        _s5_tables_body,
        grid=(hb,),
        in_specs=[blk(p), blk(p), blk(1), blk(p), blk(p), blk(p), blk(p), blk(1)],
        out_specs=[pl.BlockSpec((None, S5_T, LANES, LANES), lambda i: (i, 0, 0, 0)), wide, wide,
                   pl.BlockSpec((None, 1, sw), lambda i: (i, 0, 0))],
        out_shape=[jax.ShapeDtypeStruct((hb, S5_T, LANES, LANES), BF16),
                   jax.ShapeDtypeStruct((hb, S5_F, sw), BF16),
                   jax.ShapeDtypeStruct((hb, S5_F, sw), BF16),
                   jax.ShapeDtypeStruct((hb, 1, sw), F32)],
        compiler_params=_params(("arbitrary",), 40),
        name="s5_tables",
    )(rep(a_re), rep(a_im), rep(log_dt.reshape(h, 1)),
      jnp.swapaxes(b_re, 1, 2).reshape(hb, LANES, p), jnp.swapaxes(b_im, 1, 2).reshape(hb, LANES, p),
      c_re.reshape(hb, LANES, p), c_im.reshape(hb, LANES, p), d_skip.reshape(hb, LANES, 1))


def _split_mul(a, p):
    lane = lax.broadcasted_iota(jnp.int32, a.shape, a.ndim - 1)
    sw = pltpu.roll(a, p, axis=a.ndim - 1)
    return jnp.where(lane < p, a, sw), jnp.where(lane < p, -sw, a)


def _cmul(x, mr, mi, p):
    return x * mr + pltpu.roll(x, p, axis=x.ndim - 1) * mi


def _build_toeplitz(m_ref, dk_ref):
    m_ref[...] = jnp.zeros_like(m_ref)
    for s in range(S5_T):
        for t in range(s, S5_T):
            m_ref[s * LANES:(s + 1) * LANES, t * LANES:(t + 1) * LANES] = dk_ref[t - s]


def _s5_outputs(ucat, enter, m_ref, ws_ref):
    return (jnp.dot(ucat, m_ref[...], preferred_element_type=F32)
            + lax.dot_general(enter.astype(BF16), ws_ref[...], (((1,), (1,)), ((), ())),
                              preferred_element_type=F32))


def _glu_rows(o_ref, ga_ref, gb_ref, tile):
    ga, gb = ga_ref[...], gb_ref[...]
    tile = _pick(o_ref.shape[0], tile)
    for r0 in range(0, o_ref.shape[0], tile):
        y = o_ref[r0:r0 + tile, :].astype(BF16)
        o_ref[r0:r0 + tile, :] = (jnp.dot(y, ga, preferred_element_type=F32)
                                  * jax.nn.sigmoid(jnp.dot(y, gb, preferred_element_type=F32)))


def _s5_prompt_body(z_ref, dk_ref, we_ref, ws_ref, a_ref, ga_ref, gb_ref, o_ref, fin_ref, m_ref,
                    *, nb, nc):
    @pl.when(pl.program_id(1) == 0)
    def _():
        _build_toeplitz(m_ref, dk_ref)

    r = nb * nc
    p = a_ref.shape[-1] // 2
    ucat = jnp.concatenate([z_ref[pl.ds(t, r, stride=S5_T), :].astype(BF16) for t in range(S5_T)], axis=-1)
    x = jnp.dot(ucat, we_ref[...], preferred_element_type=F32)
    a = a_ref[...]
    pos = lax.rem(lax.broadcasted_iota(jnp.int32, (r, 1), 0), nc)
    sh = 1
    while sh < nc:
        mr, mi = _split_mul(a, p)
        shifted = jnp.where(pos >= sh, pltpu.roll(x, sh, axis=0), 0.0)
        x = x + _cmul(shifted, mr, mi, p)
        a = _cmul(a, mr, mi, p)
        sh *= 2
    enter = jnp.where(pos >= 1, pltpu.roll(x, 1, axis=0), 0.0)
    fin_ref[...] = jnp.concatenate([x[(b + 1) * nc - 1:(b + 1) * nc] for b in range(nb)], axis=0)
    y = _s5_outputs(ucat, enter, m_ref, ws_ref)
    for t in range(S5_T):
        o_ref[pl.ds(t, r, stride=S5_T), :] = y[:, t * LANES:(t + 1) * LANES]
    _glu_rows(o_ref, ga_ref, gb_ref, 512)


def _s5_sample_body(z_ref, dk_ref, we_ref, ws_ref, a_ref, ga_ref, gb_ref, h0_ref, o_ref, fin_ref, m_ref,
                    *, nb, nc):
    _build_toeplitz(m_ref, dk_ref)
    p = a_ref.shape[-1] // 2
    seq = nc * S5_T
    ucat = jnp.concatenate(
        [jnp.concatenate([z_ref[pl.ds(c * S5_T + t, nb, stride=seq), :].astype(BF16) for t in range(S5_T)],
                         axis=-1) for c in range(nc)], axis=0)
    e = jnp.dot(ucat, we_ref[...], preferred_element_type=F32)
    mr, mi = _split_mul(a_ref[...], p)
    state = h0_ref[...]
    enter = []
    for c in range(nc):
        enter.append(state)
        state = _cmul(state, mr, mi, p) + e[c * nb:(c + 1) * nb]
    fin_ref[...] = state
    y = _s5_outputs(ucat, jnp.concatenate(enter, axis=0), m_ref, ws_ref)
    for c in range(nc):
        for t in range(S5_T):
            o_ref[pl.ds(c * S5_T + t, nb, stride=seq), :] = y[c * nb:(c + 1) * nb, t * LANES:(t + 1) * LANES]
    _glu_rows(o_ref, ga_ref, gb_ref, 512)


def _s5_prompt(z, n_b, seq, tabs, g_a, g_b, lyr, hb):
    dk, w_e, w_s, a_t = tabs
    sw = a_t.shape[-1]
    nbs = _pick(n_b, 2, 1)
    nc = seq // S5_T
    tab = lambda arr: pl.BlockSpec((None,) + arr.shape[1:],
                                   lambda i, b: (lyr * hb + i,) + (0,) * (arr.ndim - 1))
    return pl.pallas_call(
        functools.partial(_s5_prompt_body, nb=nbs, nc=nc),
        grid=(hb, n_b // nbs),
        in_specs=[pl.BlockSpec((nbs * seq, LANES), lambda i, b: (b, i)),
                  tab(dk), tab(w_e), tab(w_s), tab(a_t), tab(g_a), tab(g_b)],
        out_specs=[pl.BlockSpec((nbs * seq, LANES), lambda i, b: (b, i)),
                   pl.BlockSpec((None, None, nbs, sw), lambda i, b: (i, b, 0, 0))],
        out_shape=[jax.ShapeDtypeStruct((n_b * seq, hb * LANES), F32),
                   jax.ShapeDtypeStruct((hb, n_b // nbs, nbs, sw), F32)],
        scratch_shapes=[pltpu.VMEM((S5_F, S5_F), BF16)],
        compiler_params=_params(("arbitrary", "arbitrary"), 52),
        name="s5_prompt",
    )(z, dk, w_e, w_s, a_t, g_a, g_b)


def _s5_sample(z, row0, n_b, n_new, tabs, g_a, g_b, h0, lyr, hb):
    dk, w_e, w_s, a_t = tabs
    sw = a_t.shape[-1]
    rows = n_b * n_new
    assert row0 % rows == 0
    rb = row0 // rows
    tab = lambda arr, off=lyr * hb: pl.BlockSpec((None,) + arr.shape[1:],
                                                 lambda i: (off + i,) + (0,) * (arr.ndim - 1))
    return pl.pallas_call(
        functools.partial(_s5_sample_body, nb=n_b, nc=n_new // S5_T),
        grid=(hb,),
        in_specs=[pl.BlockSpec((rows, LANES), lambda i: (rb, i)),
                  tab(dk), tab(w_e), tab(w_s), tab(a_t), tab(g_a), tab(g_b), tab(h0, 0)],
        out_specs=[pl.BlockSpec((rows, LANES), lambda i: (0, i)),
                   pl.BlockSpec((None, n_b, sw), lambda i: (i, 0, 0))],
        out_shape=[jax.ShapeDtypeStruct((rows, hb * LANES), F32),
                   jax.ShapeDtypeStruct((hb, n_b, sw), F32)],
        scratch_shapes=[pltpu.VMEM((S5_F, S5_F), BF16)],
        compiler_params=_params(("arbitrary",), 52),
        name="s5_sample",
    )(z, dk, w_e, w_s, a_t, g_a, g_b, h0)


def _glu_blocks(glu_w):
    h, g, _ = glu_w.shape
    hb = h // S5_GB
    eye = jnp.eye(S5_GB, dtype=F32)

    def diag(wm):
        wm = wm.reshape(hb, S5_GB, g, g)
        return (eye[None, :, None, :, None] * wm[:, :, :, None, :]).reshape(hb, LANES, LANES).astype(BF16)

    return diag(glu_w[..., :g]), diag(glu_w[..., g:])


def _state_to_lanes(re, im):
    n, h, p = re.shape
    hb = h // S5_GB
    f = lambda v: v.reshape(n, hb, S5_GB * p).transpose(1, 0, 2)
    return jnp.concatenate([f(re), f(im)], axis=-1)


def _state_from_lanes(s):
    nl, hb, n, sw = s.shape
    half = sw // 2
    f = lambda v: v.transpose(0, 2, 1, 3).reshape(nl, n, hb * S5_GB, half // S5_GB)
    return f(s[..., :half]), f(s[..., half:])


def kernel(x_prompt, x_sample, cache_k, cache_v, state_ssm_re, state_ssm_im, p_prompt, p_sample,
           norm_mix, w_in, ssm_a_re, ssm_a_im, ssm_log_dt, ssm_b_re, ssm_b_im, ssm_c_re, ssm_c_im,
           ssm_d, ssm_glu_w, attn_sinks, rel_bias, norm_grp_ssm, norm_grp_attn, w_out, norm_ffn,
           ffn_w_gate, ffn_w_up, ffn_w_down, router_w, moe_w_gate, moe_w_up, moe_w_down,
           norm_ple, ple_w_gate, ple_w_proj, norm_final):
    nb_p, seq, d = x_prompt.shape
    nb_s, n_new, _ = x_sample.shape
    depth = w_in.shape[0]
    d_ssm = norm_grp_ssm.shape[-1]
    d_attn = norm_grp_attn.shape[-1]
    mp, ms = nb_p * seq, nb_s * n_new
    win = min(WINDOW, seq)
    assert d_ssm % LANES == 0 and seq % S5_T == 0 and n_new % S5_T == 0

    h = jnp.concatenate([x_prompt.reshape(mp, d), x_sample.reshape(ms, d)], axis=0)
    p_all = jnp.concatenate([p_prompt.reshape(depth, mp, -1), p_sample.reshape(depth, ms, -1)], axis=1)

    n_grp = ssm_a_re.shape[1]
    hb = n_grp // S5_GB
    flat = lambda v: v.reshape((depth * n_grp,) + v.shape[2:])
    tabs = _s5_tables(flat(ssm_a_re), flat(ssm_a_im), ssm_log_dt.reshape(-1), flat(ssm_b_re), flat(ssm_b_im),
                      flat(ssm_c_re), flat(ssm_c_im), ssm_d.reshape(-1))
    g_a, g_b = _glu_blocks(flat(ssm_glu_w))

    kv_p, kv_s, fin_ps, fin_ss = [], [], [], []
    for i in range(depth):
        xn = _rmsnorm(h, norm_mix[i], BF16)
        z = _mm(xn, w_in, i, tm=1024, tn=1280, vmem_mib=56)
        ssm_p, fin_p = _s5_prompt(z, nb_p, seq, tabs, g_a, g_b, i, hb)
        ssm_s, fin_s = _s5_sample(z, mp, nb_s, n_new, tabs, g_a, g_b,
                                  _state_to_lanes(state_ssm_re[i], state_ssm_im[i]), i, hb)
        att_p = _attn_prompt(z, nb_p, seq, d_ssm, d_attn, rel_bias, attn_sinks[i])
        att_s = _attn_sample(z, mp, nb_s, n_new, d_ssm, d_attn, cache_k, cache_v, i, rel_bias,
                             attn_sinks[i])
        merged = _merge_norm(ssm_p, ssm_s, att_p, att_s, norm_grp_ssm[i], norm_grp_attn[i])
        h = _mm(merged, w_out, i, tm=1024, tn=1024, res=h, vmem_mib=56)

        kv_p.append(z[:mp].reshape(nb_p, seq, -1)[:, seq - win:, d_ssm + d_attn:])
        kv_s.append(z[mp:, d_ssm + d_attn:])
        fin_ps.append(fin_p.reshape(hb, nb_p, -1))
        fin_ss.append(fin_s)

        j = i // 2
        if i % 2 == 0:
            hn = _rmsnorm(h, norm_ffn[i], BF16)
            t = _swiglu(hn, ffn_w_gate, ffn_w_up, j, tm=1024, tn=512)
            h = _mm(t, ffn_w_down, j, tm=512, tn=512, res=h, vmem_mib=52)
        else:
            hn, e_idx, gates = _rmsnorm_router(h, norm_ffn[i], router_w[j])
            h = h + _moe_ffn(hn, e_idx, gates, moe_w_gate, moe_w_up, moe_w_down, j)

        hn = _rmsnorm(h, norm_ple[i], BF16)
        h = _ple(hn, p_all, ple_w_gate, ple_w_proj, h, i, tm=1024, tn=1024, vmem_mib=56)

    y = _rmsnorm(h, norm_final, F32)
    kv_p = jnp.stack(kv_p).reshape(depth, nb_p, win, 2, KV_HEADS, HEAD_DIM)
    kv_s = jnp.stack(kv_s).reshape(depth, nb_s, n_new, 2, KV_HEADS, HEAD_DIM)
    p_re, p_im = _state_from_lanes(jnp.stack(fin_ps))
    s_re, s_im = _state_from_lanes(jnp.stack(fin_ss))
    return (y[:mp].reshape(nb_p, seq, d), y[mp:].reshape(nb_s, n_new, d),
            kv_p[:, :, :, 0], kv_p[:, :, :, 1], p_re, p_im,
            kv_s[:, :, :, 0], kv_s[:, :, :, 1], s_re, s_im)
```

```python
import functools
import math

import numpy as np
import jax
import jax.numpy as jnp
from jax import lax
from jax.experimental import pallas as pl
from jax.experimental.pallas import tpu as pltpu

F32 = jnp.float32
BF16 = jnp.bfloat16
HIGHEST = lax.Precision.HIGHEST

CHUNK = 64
WINDOW = 128
HEAD_DIM = 64
KV_HEADS = 4
SSM_GROUP_CH = 16
SSM_STATE = 64
N_BUCKETS = 32
REL_MAX_DIST = 128
EPS = 1e-6
NEG_INF = -1e30
ATTN_SCALE = HEAD_DIM ** -0.5

LANES = 128
S5_T = 16
S5_GB = LANES // SSM_GROUP_CH
S5_F = S5_T * LANES
MOE_TM = 512
MOE_SUB = 256
MIB = 1024 * 1024


def _pick(n, pref, mult=8):
    best = None
    for d in range(mult, min(n, pref) + 1, mult):
        if n % d == 0:
            best = d
    return best if best is not None else n


def _params(sem, vmem_mib):
    return pltpu.CompilerParams(dimension_semantics=sem, vmem_limit_bytes=int(vmem_mib * MIB))


def _rms(x, g):
    return x * lax.rsqrt(jnp.mean(x * x, axis=-1, keepdims=True) + EPS) * g


def _norm_body(x_ref, g_ref, o_ref):
    o_ref[...] = _rms(x_ref[...], g_ref[...]).astype(o_ref.dtype)


def _rmsnorm(x, g, out_dtype):
    m, d = x.shape
    tm = _pick(m, 512)
    return pl.pallas_call(
        _norm_body,
        grid=(m // tm,),
        in_specs=[pl.BlockSpec((tm, d), lambda i: (i, 0)), pl.BlockSpec((1, d), lambda i: (0, 0))],
        out_specs=pl.BlockSpec((tm, d), lambda i: (i, 0)),
        out_shape=jax.ShapeDtypeStruct((m, d), out_dtype),
        compiler_params=_params(("arbitrary",), 40),
        name="rmsnorm",
    )(x, g.reshape(1, d))


def _norm_router_body(x_ref, g_ref, rw_ref, o_ref, idx_ref, gate_ref):
    y = _rms(x_ref[...], g_ref[...])
    tm, d = y.shape
    pieces = d // LANES
    for s in range(pieces):
        o_ref[pl.ds(s, tm, stride=pieces), :] = y[:, s * LANES:(s + 1) * LANES]
    n_exp = rw_ref.shape[0]
    logits = jnp.concatenate([jnp.sum(y * rw_ref[e:e + 1, :], axis=-1, keepdims=True) for e in range(n_exp)],
                             axis=-1)
    lane = lax.broadcasted_iota(jnp.int32, logits.shape, 1)
    m1 = jnp.max(logits, axis=-1, keepdims=True)
    i1 = jnp.min(jnp.where(logits == m1, lane, n_exp), axis=-1, keepdims=True)
    rest = jnp.where(lane == i1, -jnp.inf, logits)
    m2 = jnp.max(rest, axis=-1, keepdims=True)
    i2 = jnp.min(jnp.where(rest == m2, lane, n_exp), axis=-1, keepdims=True)
    e2 = jnp.exp(m2 - m1)
    den = 1.0 + e2
    idx_ref[...] = jnp.concatenate([i1, i2], axis=-1)
    gate_ref[...] = jnp.concatenate([1.0 / den, e2 / den], axis=-1)


def _rmsnorm_router(x, g, router_w):
    m, d = x.shape
    n_exp = router_w.shape[-1]
    tm = _pick(m, 512)
    return pl.pallas_call(
        _norm_router_body,
        grid=(m // tm,),
        in_specs=[pl.BlockSpec((tm, d), lambda i: (i, 0)),
                  pl.BlockSpec((1, d), lambda i: (0, 0)),
                  pl.BlockSpec((n_exp, d), lambda i: (0, 0))],
        out_specs=[pl.BlockSpec((tm * (d // LANES), LANES), lambda i: (i, 0)),
                   pl.BlockSpec((tm, 2), lambda i: (i, 0)),
                   pl.BlockSpec((tm, 2), lambda i: (i, 0))],
        out_shape=[jax.ShapeDtypeStruct((m * (d // LANES), LANES), F32),
                   jax.ShapeDtypeStruct((m, 2), jnp.int32),
                   jax.ShapeDtypeStruct((m, 2), F32)],
        compiler_params=_params(("arbitrary",), 40),
        name="rmsnorm_router",
    )(x, g.reshape(1, d), router_w.T)


def _merge_norm_body(ap_ref, as_ref, bp_ref, bs_ref, ga_ref, gb_ref, o_ref, *, np_tiles):
    da = ap_ref.shape[-1]

    def emit(a_ref, b_ref):
        o_ref[:, :da] = _rms(a_ref[...], ga_ref[...]).astype(o_ref.dtype)
        o_ref[:, da:] = _rms(b_ref[...], gb_ref[...]).astype(o_ref.dtype)

    @pl.when(pl.program_id(0) < np_tiles)
    def _():
        emit(ap_ref, bp_ref)

    @pl.when(pl.program_id(0) >= np_tiles)
    def _():
        emit(as_ref, bs_ref)


def _merge_norm(a_p, a_s, b_p, b_s, ga, gb):
    (mp, da), (ms, db) = a_p.shape, b_s.shape
    tm = _pick(math.gcd(mp, ms), 512)
    npt, nst = mp // tm, ms // tm
    pspec = lambda w: pl.BlockSpec((tm, w), lambda i: (jnp.minimum(i, npt - 1), 0))
    sspec = lambda w: pl.BlockSpec((tm, w), lambda i: (jnp.maximum(i - npt, 0), 0))
    return pl.pallas_call(
        functools.partial(_merge_norm_body, np_tiles=npt),
        grid=(npt + nst,),
        in_specs=[pspec(da), sspec(da), pspec(db), sspec(db),
                  pl.BlockSpec((1, da), lambda i: (0, 0)), pl.BlockSpec((1, db), lambda i: (0, 0))],
        out_specs=pl.BlockSpec((tm, da + db), lambda i: (i, 0)),
        out_shape=jax.ShapeDtypeStruct((mp + ms, da + db), BF16),
        compiler_params=_params(("arbitrary",), 40),
        name="merge_norm",
    )(a_p, a_s, b_p, b_s, ga.reshape(1, da), gb.reshape(1, db))


def _mm_body(*refs, has_res):
    if has_res:
        x_ref, w_ref, r_ref, o_ref, wb_ref = refs
    else:
        x_ref, w_ref, o_ref, wb_ref = refs

    @pl.when(pl.program_id(1) == 0)
    def _():
        wb_ref[...] = w_ref[...].astype(BF16)

    acc = jnp.dot(x_ref[...], wb_ref[...], preferred_element_type=F32)
    if has_res:
        acc = r_ref[...] + acc
    o_ref[...] = acc.astype(o_ref.dtype)


def _mm(x, w, lyr, *, tm, tn, out_dtype=F32, res=None, vmem_mib=48):
    m, k = x.shape
    n = w.shape[-1]
    tm, tn = _pick(m, tm), _pick(n, tn, 128)
    in_specs = [pl.BlockSpec((tm, k), lambda j, i: (i, 0)),
                pl.BlockSpec((None, k, tn), lambda j, i: (lyr, 0, j))]
    args = [x, w]
    if res is not None:
        in_specs.append(pl.BlockSpec((tm, tn), lambda j, i: (i, j)))
        args.append(res)
    return pl.pallas_call(
        functools.partial(_mm_body, has_res=res is not None),
        grid=(n // tn, m // tm),
        in_specs=in_specs,
        out_specs=pl.BlockSpec((tm, tn), lambda j, i: (i, j)),
        out_shape=jax.ShapeDtypeStruct((m, n), out_dtype),
        scratch_shapes=[pltpu.VMEM((k, tn), BF16)],
        compiler_params=_params(("arbitrary", "arbitrary"), vmem_mib),
        name="mm_res" if res is not None else "mm",
    )(*args)


def _swiglu_math(x, wg, wu):
    a = jnp.dot(x, wg, preferred_element_type=F32)
    b = jnp.dot(x, wu, preferred_element_type=F32)
    return (a * jax.nn.sigmoid(a) * b).astype(BF16)


def _swiglu_body(x_ref, wg_ref, wu_ref, o_ref, wgb_ref, wub_ref):
    @pl.when(pl.program_id(1) == 0)
    def _():
        wgb_ref[...] = wg_ref[...].astype(BF16)
        wub_ref[...] = wu_ref[...].astype(BF16)

    o_ref[...] = _swiglu_math(x_ref[...], wgb_ref[...], wub_ref[...])


def _swiglu(x, wg, wu, lyr, *, tm, tn, vmem_mib=48):
    m, k = x.shape
    n = wg.shape[-1]
    tm, tn = _pick(m, tm), _pick(n, tn, 128)
    wspec = pl.BlockSpec((None, k, tn), lambda j, i: (lyr, 0, j))
    return pl.pallas_call(
        _swiglu_body,
        grid=(n // tn, m // tm),
        in_specs=[pl.BlockSpec((tm, k), lambda j, i: (i, 0)), wspec, wspec],
        out_specs=pl.BlockSpec((tm, tn), lambda j, i: (i, j)),
        out_shape=jax.ShapeDtypeStruct((m, n), BF16),
        scratch_shapes=[pltpu.VMEM((k, tn), BF16), pltpu.VMEM((k, tn), BF16)],
        compiler_params=_params(("arbitrary", "arbitrary"), vmem_mib),
        name="swiglu",
    )(x, wg, wu)


def _ple_body(x_ref, p_ref, wg_ref, wp_ref, r_ref, o_ref, wgb_ref, wpb_ref):
    @pl.when(pl.program_id(1) == 0)
    def _():
        wgb_ref[...] = wg_ref[...].astype(BF16)
        wpb_ref[...] = wp_ref[...].astype(BF16)

    a = jnp.dot(x_ref[...], wgb_ref[...], preferred_element_type=F32)
    b = jnp.dot(p_ref[...].astype(BF16), wpb_ref[...], preferred_element_type=F32)
    o_ref[...] = r_ref[...] + jax.nn.sigmoid(a) * b


def _ple(x, p, wg, wp, res, lyr, *, tm, tn, vmem_mib=48):
    m, k = x.shape
    kp = p.shape[-1]
    n = wg.shape[-1]
    tm, tn = _pick(m, tm), _pick(n, tn, 128)
    return pl.pallas_call(
        _ple_body,
        grid=(n // tn, m // tm),
        in_specs=[pl.BlockSpec((tm, k), lambda j, i: (i, 0)),
                  pl.BlockSpec((None, tm, kp), lambda j, i: (lyr, i, 0)),
                  pl.BlockSpec((None, k, tn), lambda j, i: (lyr, 0, j)),
                  pl.BlockSpec((None, kp, tn), lambda j, i: (lyr, 0, j)),
                  pl.BlockSpec((tm, tn), lambda j, i: (i, j))],
        out_specs=pl.BlockSpec((tm, tn), lambda j, i: (i, j)),
        out_shape=jax.ShapeDtypeStruct((m, n), F32),
        scratch_shapes=[pltpu.VMEM((k, tn), BF16), pltpu.VMEM((kp, tn), BF16)],
        compiler_params=_params(("arbitrary", "arbitrary"), vmem_mib),
        name="ple",
    )(x, p, wg, wp, res)


def _moe_weights(w_hbms, stages, bf16s, sem, e_ref, start_ref, nxt_ref, lyr):
    j, i = pl.program_id(0), pl.program_id(1)
    tn = stages[0].shape[1]

    def copies(e, col):
        col = pl.multiple_of(col, LANES)
        return [pltpu.make_async_copy(w.at[lyr, e, :, pl.ds(col, tn)], st, sem.at[n])
                for n, (w, st) in enumerate(zip(w_hbms, stages))]

    @pl.when(jnp.logical_and(j == 0, i == 0))
    def _():
        for cp in copies(e_ref[0], 0):
            cp.start()

    @pl.when(start_ref[i] == 1)
    def _():
        for cp in copies(e_ref[i], j * tn):
            cp.wait()
        rows = 128

        def cast_chunk(c, carry):
            r = pl.multiple_of(c * rows, rows)
            for st, wb in zip(stages, bf16s):
                wb[pl.ds(r, rows), :] = st[pl.ds(r, rows), :].astype(BF16)
            return carry
        lax.fori_loop(0, stages[0].shape[0] // rows, cast_chunk, 0)
        nxt = nxt_ref[i]

        @pl.when(nxt >= 0)
        def _():
            for cp in copies(nxt, j * tn):
                cp.start()

        @pl.when(jnp.logical_and(nxt < 0, j + 1 < pl.num_programs(0)))
        def _():
            for cp in copies(e_ref[0], (j + 1) * tn):
                cp.start()


def _active_prefix(act_ref, i, per):
    n_act = act_ref[i * per]
    for s in range(1, per):
        n_act = n_act + act_ref[i * per + s]
    return n_act


def _prefix_rows(o_ref, n_act, fn):
    tm = o_ref.shape[0]
    for c in range(tm // MOE_SUB + 1):
        rows = c * MOE_SUB

        @pl.when(n_act == c)
        def _():
            if rows:
                o_ref[:rows, :] = fn(rows)
            if rows < tm:
                o_ref[rows:, :] = jnp.zeros((tm - rows, o_ref.shape[1]), o_ref.dtype)


def _moe_up_body(e_ref, act_ref, start_ref, nxt_ref, x_ref, wg_hbm, wu_hbm, o_ref,
                 wgb_ref, wub_ref, sg_ref, su_ref, sem, *, lyr):
    _moe_weights((wg_hbm, wu_hbm), (sg_ref, su_ref), (wgb_ref, wub_ref), sem, e_ref, start_ref, nxt_ref, lyr)
    n_act = _active_prefix(act_ref, pl.program_id(1), x_ref.shape[0] // MOE_SUB)
    _prefix_rows(o_ref, n_act, lambda rows: _swiglu_math(x_ref[:rows, :], wgb_ref[...], wub_ref[...]))


def _moe_down_body(e_ref, act_ref, start_ref, nxt_ref, h_ref, wd_hbm, o_ref, wdb_ref, sd_ref, sem, *, lyr):
    _moe_weights((wd_hbm,), (sd_ref,), (wdb_ref,), sem, e_ref, start_ref, nxt_ref, lyr)
    n_act = _active_prefix(act_ref, pl.program_id(1), h_ref.shape[0] // MOE_SUB)
    _prefix_rows(o_ref, n_act,
                 lambda rows: jnp.dot(h_ref[:rows, :], wdb_ref[...], preferred_element_type=F32))


def _moe_matmul(body, x, ws, sched, lyr, *, tn, out_dtype, vmem_mib, name):
    ns, k = x.shape
    n = ws[0].shape[-1]
    tm, tn = MOE_TM, _pick(n, tn, 128)
    grid_spec = pltpu.PrefetchScalarGridSpec(
        num_scalar_prefetch=len(sched),
        grid=(n // tn, ns // tm),
        in_specs=[pl.BlockSpec((tm, k), lambda j, i, *_: (i, 0))]
                 + [pl.BlockSpec(memory_space=pl.ANY)] * len(ws),
        out_specs=pl.BlockSpec((tm, tn), lambda j, i, *_: (i, j)),
        scratch_shapes=[pltpu.VMEM((k, tn), BF16)] * len(ws) + [pltpu.VMEM((k, tn), F32)] * len(ws)
                       + [pltpu.SemaphoreType.DMA((len(ws),))],
    )
    return pl.pallas_call(
        functools.partial(body, lyr=lyr),
        grid_spec=grid_spec,
        out_shape=jax.ShapeDtypeStruct((ns, n), out_dtype),
        compiler_params=_params(("arbitrary", "arbitrary"), vmem_mib),
        name=name,
    )(*sched, x, *ws)


def _row_gather_copy(x_hbm, buf, sem, slot, src_row, dst_row, rows):
    return pltpu.make_async_copy(x_hbm.at[pl.ds(src_row, rows)], buf.at[slot, pl.ds(dst_row, rows)],
                                 sem.at[slot])


def _moe_gather_body(tok_ref, x_hbm, o_ref, buf, sem):
    i = pl.program_id(0)
    tm = o_ref.shape[0]
    pieces = o_ref.shape[1] // LANES

    def issue(blk, slot):
        def body(r, carry):
            _row_gather_copy(x_hbm, buf, sem, slot, tok_ref[blk * tm + r] * pieces, r * pieces, pieces).start()
            return carry
        lax.fori_loop(0, tm, body, 0, unroll=8)

    @pl.when(i == 0)
    def _():
        issue(0, 0)

    @pl.when(i + 1 < pl.num_programs(0))
    def _():
        issue(i + 1, (i + 1) % 2)

    slot = i % 2
    _row_gather_copy(x_hbm, buf, sem, slot, 0, 0, tm * pieces).wait()
    cur = buf.at[slot]
    o_ref[...] = jnp.concatenate([cur[pl.ds(s, tm, stride=pieces), :] for s in range(pieces)],
                                 axis=-1).astype(o_ref.dtype)


def _moe_gather(x_rows, slot_tok, d):
    n_slots = slot_tok.shape[0]
    tm = MOE_TM
    grid_spec = pltpu.PrefetchScalarGridSpec(
        num_scalar_prefetch=1,
        grid=(n_slots // tm,),
        in_specs=[pl.BlockSpec(memory_space=pl.ANY)],
        out_specs=pl.BlockSpec((tm, d), lambda i, tok: (i, 0)),
        scratch_shapes=[pltpu.VMEM((2, tm * (d // LANES), LANES), F32), pltpu.SemaphoreType.DMA((2,))],
    )
    return pl.pallas_call(
        _moe_gather_body,
        grid_spec=grid_spec,
        out_shape=jax.ShapeDtypeStruct((n_slots, d), BF16),
        compiler_params=_params(("arbitrary",), 32),
        name="moe_gather",
    )(slot_tok, x_rows)


def _moe_ffn(hn_rows, e_idx, gates, wg, wu, wd, lyr):
    m = e_idx.shape[0]
    d = wg.shape[2]
    n_exp = wg.shape[1]
    top_k = e_idx.shape[-1]
    tm = MOE_TM
    n_asg = m * top_k
    e_flat = e_idx.reshape(n_asg)
    onehot = (e_flat[None, :] == jnp.arange(n_exp, dtype=jnp.int32)[:, None]).astype(jnp.int32)
    csum = jnp.cumsum(onehot, axis=1)
    counts = csum[:, -1]
    padded = (counts + tm - 1) // tm * tm
    pad_end = jnp.cumsum(padded)
    pad_start = pad_end - padded
    slot_of = jnp.sum(onehot * (pad_start[:, None] + csum - 1), axis=0).astype(jnp.int32)
    n_blocks = n_asg // tm + n_exp
    n_slots = n_blocks * tm
    slot_tok = jnp.zeros((n_slots,), jnp.int32).at[slot_of].set(
        jnp.arange(n_asg, dtype=jnp.int32) // top_k)
    n_used = (pad_end[-1] // tm).astype(jnp.int32)
    blk = jnp.arange(n_blocks, dtype=jnp.int32)
    blk_start = jnp.minimum(blk, n_used - 1) * tm
    blk_expert = jnp.sum((pad_end[None, :] <= blk_start[:, None]).astype(jnp.int32), axis=1)
    blk_expert = jnp.minimum(blk_expert, n_exp - 1).astype(jnp.int32)
    sub_start = jnp.arange(n_slots // MOE_SUB, dtype=jnp.int32) * MOE_SUB
    sub_expert = jnp.repeat(blk_expert, tm // MOE_SUB)
    real_end = (pad_start + counts)[sub_expert]
    sub_active = jnp.logical_and(sub_start < real_end, sub_start < pad_end[-1]).astype(jnp.int32)

    run_start = jnp.concatenate([jnp.ones((1,), jnp.int32),
                                 (blk_expert[1:] != blk_expert[:-1]).astype(jnp.int32)])
    later_start = jnp.where(run_start == 1, blk, n_blocks)
    nxt_blk = lax.cummin(jnp.concatenate([later_start[1:], jnp.full((1,), n_blocks, jnp.int32)]),
                         axis=0, reverse=True)
    run_next = jnp.where(nxt_blk < n_blocks, blk_expert[jnp.minimum(nxt_blk, n_blocks - 1)], -1)
    sched = (blk_expert, sub_active, run_start, run_next.astype(jnp.int32))

    x_sorted = _moe_gather(hn_rows, slot_tok, d)
    h_sorted = _moe_matmul(_moe_up_body, x_sorted, (wg, wu), sched, lyr, tn=1024, out_dtype=BF16,
                           vmem_mib=48, name="moe_up")
    y_sorted = _moe_matmul(_moe_down_body, h_sorted, (wd,), sched, lyr, tn=512, out_dtype=F32,
                           vmem_mib=48, name="moe_down")
    sl = slot_of.reshape(m, top_k)
    y = y_sorted[sl[:, 0]] * gates[:, 0:1]
    for t in range(1, top_k):
        y = y + y_sorted[sl[:, t]] * gates[:, t:t + 1]
    return y


def _rel_buckets(n_q, n_k_past, n_k):
    rel = (np.arange(n_k) - n_k_past)[None, :] - np.arange(n_q)[:, None]
    half = N_BUCKETS // 2
    max_exact = half // 2
    n = np.abs(rel)
    nf = np.maximum(n, 1).astype(np.float32)
    large = max_exact + (np.log(nf / np.float32(max_exact)) / np.float32(math.log(REL_MAX_DIST / max_exact))
                         * np.float32(half - max_exact)).astype(np.int32)
    large = np.minimum(large, half - 1)
    return (np.where(rel > 0, half, 0) + np.where(n < max_exact, n, large)).astype(np.int32)


def _build_bias(bias_ref, sink_ref, bucket_ref, table_ref, sinks_ref):
    bucket = bucket_ref[...]
    nq = bucket.shape[1]
    gq = bias_ref.shape[2] // nq
    for kh in range(bias_ref.shape[0]):
        for g in range(gq):
            n = kh * gq + g
            acc = jnp.zeros(bucket.shape, F32)
            for b in range(N_BUCKETS):
                acc = jnp.where(bucket == b, table_ref[b, n], acc)
            bias_ref[kh, :, g * nq:(g + 1) * nq] = acc
            sink_ref[kh, :, g * nq:(g + 1) * nq] = jnp.full((1, nq), sinks_ref[n], F32)


def _attn_items(items, bias_ref, sink_ref):
    nq = items[0][0].shape[0]
    gq = items[0][0].shape[-1] // HEAD_DIM // KV_HEADS
    head = lambda x, n: x[:, n * HEAD_DIM:(n + 1) * HEAD_DIM]
    scores = []
    for q, k, _, valid in items:
        for kh in range(KV_HEADS):
            q4 = jnp.concatenate([head(q, kh * gq + g) for g in range(gq)], axis=0)
            s = lax.dot_general(head(k, kh), q4, (((1,), (1,)), ((), ())), preferred_element_type=F32)
            scores.append((s, kh, valid))
    probs = []
    for s, kh, valid in scores:
        s = s * ATTN_SCALE + bias_ref[kh]
        if valid is not None:
            s = jnp.where(valid, s, NEG_INF)
        sink = sink_ref[kh]
        mx = jnp.maximum(jnp.max(s, axis=0, keepdims=True), sink)
        e = jnp.exp(s - mx)
        den = jnp.sum(e, axis=0, keepdims=True) + jnp.exp(sink - mx)
        probs.append((e * (1.0 / den)).astype(BF16))
    outs = []
    for i, (_, _, v, _) in enumerate(items):
        heads = []
        for kh in range(KV_HEADS):
            o4 = lax.dot_general(probs[i * KV_HEADS + kh], head(v, kh), (((0,), (0,)), ((), ())),
                                 preferred_element_type=F32)
            heads.extend(o4[g * nq:(g + 1) * nq] for g in range(gq))
        outs.append(jnp.concatenate(heads, axis=-1))
    return outs


def _attn_prompt_body(q_ref, kp_ref, kc_ref, vp_ref, vc_ref, bucket_ref, table_ref, sinks_ref,
                      o_ref, bias_ref, sink_ref, *, chunks):
    j = pl.program_id(1)

    @pl.when(jnp.logical_and(pl.program_id(0) == 0, j == 0))
    def _():
        _build_bias(bias_ref, sink_ref, bucket_ref, table_ref, sinks_ref)

    q = q_ref[...].astype(BF16)
    k = jnp.concatenate([kp_ref[...], kc_ref[...]], axis=0).astype(BF16)
    v = jnp.concatenate([vp_ref[...], vc_ref[...]], axis=0).astype(BF16)
    band = WINDOW + CHUNK
    items = []
    for c in range(chunks):
        lo = c * CHUNK
        key_valid = None
        if lo < WINDOW:
            key_pos = j * (chunks * CHUNK) + lo - WINDOW + lax.broadcasted_iota(jnp.int32, (band, 1), 0)
            key_valid = key_pos >= 0
        items.append((q[lo:lo + CHUNK], k[lo:lo + band], v[lo:lo + band], key_valid))
    for c, o in enumerate(_attn_items(items, bias_ref, sink_ref)):
        o_ref[c * CHUNK:(c + 1) * CHUNK, :] = o


def _attn_prompt(z, n_b, seq, d_ssm, d_attn, table, sinks):
    d_kv = KV_HEADS * HEAD_DIM
    assert d_ssm % d_attn == 0 and d_ssm % d_kv == 0 and d_attn % d_kv == 0
    qb = _pick(seq, 4 * CHUNK, 2 * CHUNK)
    assert qb % WINDOW == 0 and seq % qb == 0
    chunks = qb // CHUNK
    nj = seq // qb
    r = qb // WINDOW
    kcol = (d_ssm + d_attn) // d_kv
    gq = d_attn // HEAD_DIM // KV_HEADS
    band = WINDOW + CHUNK
    bucket = jnp.asarray(_rel_buckets(CHUNK, WINDOW, band).T)

    def prev(b, j):
        return b * (seq // WINDOW) + jnp.maximum(j * r - 1, 0)

    return pl.pallas_call(
        functools.partial(_attn_prompt_body, chunks=chunks),
        grid=(n_b, nj),
        in_specs=[pl.BlockSpec((qb, d_attn), lambda b, j: (b * nj + j, d_ssm // d_attn)),
                  pl.BlockSpec((WINDOW, d_kv), lambda b, j: (prev(b, j), kcol)),
                  pl.BlockSpec((qb, d_kv), lambda b, j: (b * nj + j, kcol)),
                  pl.BlockSpec((WINDOW, d_kv), lambda b, j: (prev(b, j), kcol + 1)),
                  pl.BlockSpec((qb, d_kv), lambda b, j: (b * nj + j, kcol + 1)),
                  pl.BlockSpec((band, CHUNK), lambda b, j: (0, 0)),
                  pl.BlockSpec(memory_space=pltpu.SMEM),
                  pl.BlockSpec(memory_space=pltpu.SMEM)],
        out_specs=pl.BlockSpec((qb, d_attn), lambda b, j: (b * nj + j, 0)),
        out_shape=jax.ShapeDtypeStruct((n_b * seq, d_attn), F32),
        scratch_shapes=[pltpu.VMEM((KV_HEADS, band, gq * CHUNK), F32),
                        pltpu.VMEM((KV_HEADS, 1, gq * CHUNK), F32)],
        compiler_params=_params(("arbitrary", "arbitrary"), 40),
        name="attn_prompt",
    )(z, z, z, z, z, bucket, table, sinks)


def _attn_sample_body(q_ref, kc_ref, kn_ref, vc_ref, vn_ref, bucket_ref, table_ref, sinks_ref,
                      o_ref, bias_ref, sink_ref, *, n_bs, n_new, n_past):
    @pl.when(pl.program_id(0) == 0)
    def _():
        _build_bias(bias_ref, sink_ref, bucket_ref, table_ref, sinks_ref)

    q = q_ref[...].astype(BF16)
    kc, kn = kc_ref[...].astype(BF16), kn_ref[...].astype(BF16)
    vc, vn = vc_ref[...].astype(BF16), vn_ref[...].astype(BF16)
    items = []
    for b in range(n_bs):
        new = slice(b * n_new, (b + 1) * n_new)
        past = slice(b * n_past, (b + 1) * n_past)
        items.append((q[new], jnp.concatenate([kc[past], kn[new]], axis=0),
                      jnp.concatenate([vc[past], vn[new]], axis=0), None))
    for b, o in enumerate(_attn_items(items, bias_ref, sink_ref)):
        o_ref[b * n_new:(b + 1) * n_new, :] = o


def _attn_sample(z, row0, n_b, n_new, d_ssm, d_attn, cache_k, cache_v, lyr, table, sinks):
    d_kv = KV_HEADS * HEAD_DIM
    n_past = cache_k.shape[2]
    n_bs = _pick(n_b, 4, 1)
    rows = n_bs * n_new
    assert row0 % rows == 0
    kcol = (d_ssm + d_attn) // d_kv
    gq = d_attn // HEAD_DIM // KV_HEADS
    n_k = n_past + n_new
    bucket = jnp.asarray(_rel_buckets(n_new, n_past, n_k).T)
    ck = cache_k.reshape(-1, d_kv)
    cv = cache_v.reshape(-1, d_kv)
    rb = row0 // rows
    nsteps = n_b // n_bs
    cache_spec = pl.BlockSpec((n_bs * n_past, d_kv), lambda b: (lyr * nsteps + b, 0))
    return pl.pallas_call(
        functools.partial(_attn_sample_body, n_bs=n_bs, n_new=n_new, n_past=n_past),
        grid=(nsteps,),
        in_specs=[pl.BlockSpec((rows, d_attn), lambda b: (rb + b, d_ssm // d_attn)),
                  cache_spec,
                  pl.BlockSpec((rows, d_kv), lambda b: (rb + b, kcol)),
                  cache_spec,
                  pl.BlockSpec((rows, d_kv), lambda b: (rb + b, kcol + 1)),
                  pl.BlockSpec((n_k, n_new), lambda b: (0, 0)),
                  pl.BlockSpec(memory_space=pltpu.SMEM),
                  pl.BlockSpec(memory_space=pltpu.SMEM)],
        out_specs=pl.BlockSpec((rows, d_attn), lambda b: (b, 0)),
        out_shape=jax.ShapeDtypeStruct((n_b * n_new, d_attn), F32),
        scratch_shapes=[pltpu.VMEM((KV_HEADS, n_k, gq * n_new), F32),
                        pltpu.VMEM((KV_HEADS, 1, gq * n_new), F32)],
        compiler_params=_params(("arbitrary",), 40),
        name="attn_sample",
    )(z, ck, z, cv, z, bucket, table, sinks)


def _tile_lanes(x, reps):
    return jnp.concatenate([x] * reps, axis=-1)


def _s5_tables_body(are_ref, aim_ref, ldt_ref, btr_ref, bti_ref, cre_ref, cim_ref, d_ref,
                    dk_ref, we_ref, ws_ref, a_ref):
    rows, p = cre_ref.shape
    gb = rows // SSM_GROUP_CH
    a_re, a_im = are_ref[...], aim_ref[...]
    dt = jnp.exp(ldt_ref[...])
    mag = jnp.exp(a_re * dt)
    ab_re, ab_im = mag * jnp.cos(a_im * dt), mag * jnp.sin(a_im * dt)

    x, y = ab_re - 1.0, ab_im
    den = a_re * a_re + a_im * a_im
    co_re = (x * a_re + y * a_im) / den
    co_im = (y * a_re - x * a_im) / den
    bb_re = co_re * btr_ref[...] - co_im * bti_ref[...]
    bb_im = co_re * bti_ref[...] + co_im * btr_ref[...]
    c_re, c_im = cre_ref[...], cim_ref[...]

    r_idx = lax.broadcasted_iota(jnp.int32, (rows, 1), 0)
    own_state = (r_idx // SSM_GROUP_CH) == (lax.broadcasted_iota(jnp.int32, (1, gb * p), 1) // p)
    same_group = (r_idx // SSM_GROUP_CH) == (lax.broadcasted_iota(jnp.int32, (1, rows), 1) // SSM_GROUP_CH)
    diag = r_idx == lax.broadcasted_iota(jnp.int32, (1, rows), 1)

    def place(v_re, v_im):
        return jnp.concatenate([jnp.where(own_state, _tile_lanes(v_re, gb), 0.0),
                                jnp.where(own_state, _tile_lanes(v_im, gb), 0.0)], axis=-1)

    dims = (((1,), (1,)), ((), ()))
    bb_cat = jnp.concatenate([bb_re, -bb_im], axis=-1)
    pw_re, pw_im = jnp.ones_like(ab_re), jnp.zeros_like(ab_re)
    for j in range(S5_T + 1):
        ca_re = c_re * pw_re - c_im * pw_im
        ca_im = c_re * pw_im + c_im * pw_re
        if j < S5_T:
            kj = lax.dot_general(bb_cat, jnp.concatenate([ca_re, ca_im], axis=-1), dims,
                                 precision=HIGHEST, preferred_element_type=F32)
            kj = jnp.where(same_group, kj, 0.0)
            if j == 0:
                kj = kj + jnp.where(diag, d_ref[...], 0.0)
            dk_ref[j] = kj.astype(dk_ref.dtype)
            s = S5_T - 1 - j
            we_ref[s * rows:(s + 1) * rows, :] = place(
                pw_re * bb_re - pw_im * bb_im, pw_re * bb_im + pw_im * bb_re).astype(we_ref.dtype)
        if j >= 1:
            ws_ref[(j - 1) * rows:j * rows, :] = place(ca_re, -ca_im).astype(ws_ref.dtype)
        if j == S5_T:
            first = r_idx == (lax.broadcasted_iota(jnp.int32, (1, gb * p), 1) // p) * SSM_GROUP_CH
            a_ref[...] = jnp.concatenate(
                [jnp.sum(jnp.where(first, _tile_lanes(pw_re, gb), 0.0), axis=0, keepdims=True),
                 jnp.sum(jnp.where(first, _tile_lanes(pw_im, gb), 0.0), axis=0, keepdims=True)], axis=-1)
        pw_re, pw_im = pw_re * ab_re - pw_im * ab_im, pw_re * ab_im + pw_im * ab_re


def _s5_tables(a_re, a_im, log_dt, b_re, b_im, c_re, c_im, d_skip):
    h, p = a_re.shape
    g = SSM_GROUP_CH
    hb = h // S5_GB
    sw = 2 * S5_GB * p
    rep = lambda v: jnp.repeat(v, g, axis=0).reshape(hb, LANES, -1)
    blk = lambda w: pl.BlockSpec((None, LANES, w), lambda i: (i, 0, 0))
    wide = pl.BlockSpec((None, S5_F, sw), lambda i: (i, 0, 0))
    return pl.pallas_call(
        _s5_tables_body,
        grid=(hb,),
        in_specs=[blk(p), blk(p), blk(1), blk(p), blk(p), blk(p), blk(p), blk(1)],
        out_specs=[pl.BlockSpec((None, S5_T, LANES, LANES), lambda i: (i, 0, 0, 0)), wide, wide,
                   pl.BlockSpec((None, 1, sw), lambda i: (i, 0, 0))],
        out_shape=[jax.ShapeDtypeStruct((hb, S5_T, LANES, LANES), BF16),
                   jax.ShapeDtypeStruct((hb, S5_F, sw), BF16),
                   jax.ShapeDtypeStruct((hb, S5_F, sw), BF16),
                   jax.ShapeDtypeStruct((hb, 1, sw), F32)],
        compiler_params=_params(("arbitrary",), 40),
        name="s5_tables",
    )(rep(a_re), rep(a_im), rep(log_dt.reshape(h, 1)),
      jnp.swapaxes(b_re, 1, 2).reshape(hb, LANES, p), jnp.swapaxes(b_im, 1, 2).reshape(hb, LANES, p),
      c_re.reshape(hb, LANES, p), c_im.reshape(hb, LANES, p), d_skip.reshape(hb, LANES, 1))


def _split_mul(a, p):
    lane = lax.broadcasted_iota(jnp.int32, a.shape, a.ndim - 1)
    sw = pltpu.roll(a, p, axis=a.ndim - 1)
    return jnp.where(lane < p, a, sw), jnp.where(lane < p, -sw, a)


def _cmul(x, mr, mi, p):
    return x * mr + pltpu.roll(x, p, axis=x.ndim - 1) * mi


def _build_toeplitz(m_ref, dk_ref):
    m_ref[...] = jnp.zeros_like(m_ref)
    for s in range(S5_T):
        for t in range(s, S5_T):
            m_ref[s * LANES:(s + 1) * LANES, t * LANES:(t + 1) * LANES] = dk_ref[t - s]


def _s5_outputs(ucat, enter, m_ref, ws_ref):
    return (jnp.dot(ucat, m_ref[...], preferred_element_type=F32)
            + lax.dot_general(enter.astype(BF16), ws_ref[...], (((1,), (1,)), ((), ())),
                              preferred_element_type=F32))


def _glu_rows(o_ref, ga_ref, gb_ref, tile):
    ga, gb = ga_ref[...], gb_ref[...]
    tile = _pick(o_ref.shape[0], tile)
    for r0 in range(0, o_ref.shape[0], tile):
        y = o_ref[r0:r0 + tile, :].astype(BF16)
        o_ref[r0:r0 + tile, :] = (jnp.dot(y, ga, preferred_element_type=F32)
                                  * jax.nn.sigmoid(jnp.dot(y, gb, preferred_element_type=F32)))


def _s5_prompt_body(z_ref, dk_ref, we_ref, ws_ref, a_ref, ga_ref, gb_ref, o_ref, fin_ref, m_ref,
                    *, nb, nc):
    @pl.when(pl.program_id(1) == 0)
    def _():
        _build_toeplitz(m_ref, dk_ref)

    r = nb * nc
    p = a_ref.shape[-1] // 2
    ucat = jnp.concatenate([z_ref[pl.ds(t, r, stride=S5_T), :].astype(BF16) for t in range(S5_T)], axis=-1)
    x = jnp.dot(ucat, we_ref[...], preferred_element_type=F32)
    a = a_ref[...]
    pos = lax.rem(lax.broadcasted_iota(jnp.int32, (r, 1), 0), nc)
    sh = 1
    while sh < nc:
        mr, mi = _split_mul(a, p)
        shifted = jnp.where(pos >= sh, pltpu.roll(x, sh, axis=0), 0.0)
        x = x + _cmul(shifted, mr, mi, p)
        a = _cmul(a, mr, mi, p)
        sh *= 2
    enter = jnp.where(pos >= 1, pltpu.roll(x, 1, axis=0), 0.0)
    fin_ref[...] = jnp.concatenate([x[(b + 1) * nc - 1:(b + 1) * nc] for b in range(nb)], axis=0)
    y = _s5_outputs(ucat, enter, m_ref, ws_ref)
    for t in range(S5_T):
        o_ref[pl.ds(t, r, stride=S5_T), :] = y[:, t * LANES:(t + 1) * LANES]
    _glu_rows(o_ref, ga_ref, gb_ref, 512)


def _s5_sample_body(z_ref, dk_ref, we_ref, ws_ref, a_ref, ga_ref, gb_ref, h0_ref, o_ref, fin_ref, m_ref,
                    *, nb, nc):
    _build_toeplitz(m_ref, dk_ref)
    p = a_ref.shape[-1] // 2
    seq = nc * S5_T
    ucat = jnp.concatenate(
        [jnp.concatenate([z_ref[pl.ds(c * S5_T + t, nb, stride=seq), :].astype(BF16) for t in range(S5_T)],
                         axis=-1) for c in range(nc)], axis=0)
    e = jnp.dot(ucat, we_ref[...], preferred_element_type=F32)
    mr, mi = _split_mul(a_ref[...], p)
    state = h0_ref[...]
    enter = []
    for c in range(nc):
        enter.append(state)
        state = _cmul(state, mr, mi, p) + e[c * nb:(c + 1) * nb]
    fin_ref[...] = state
    y = _s5_outputs(ucat, jnp.concatenate(enter, axis=0), m_ref, ws_ref)
    for c in range(nc):
        for t in range(S5_T):
            o_ref[pl.ds(c * S5_T + t, nb, stride=seq), :] = y[c * nb:(c + 1) * nb, t * LANES:(t + 1) * LANES]
    _glu_rows(o_ref, ga_ref, gb_ref, 512)


def _s5_prompt(z, n_b, seq, tabs, g_a, g_b, lyr, hb):
    dk, w_e, w_s, a_t = tabs
    sw = a_t.shape[-1]
    nbs = _pick(n_b, 2, 1)
    nc = seq // S5_T
    tab = lambda arr: pl.BlockSpec((None,) + arr.shape[1:],
                                   lambda i, b: (lyr * hb + i,) + (0,) * (arr.ndim - 1))
    return pl.pallas_call(
        functools.partial(_s5_prompt_body, nb=nbs, nc=nc),
        grid=(hb, n_b // nbs),
        in_specs=[pl.BlockSpec((nbs * seq, LANES), lambda i, b: (b, i)),
                  tab(dk), tab(w_e), tab(w_s), tab(a_t), tab(g_a), tab(g_b)],
        out_specs=[pl.BlockSpec((nbs * seq, LANES), lambda i, b: (b, i)),
                   pl.BlockSpec((None, None, nbs, sw), lambda i, b: (i, b, 0, 0))],
        out_shape=[jax.ShapeDtypeStruct((n_b * seq, hb * LANES), F32),
                   jax.ShapeDtypeStruct((hb, n_b // nbs, nbs, sw), F32)],
        scratch_shapes=[pltpu.VMEM((S5_F, S5_F), BF16)],
        compiler_params=_params(("arbitrary", "arbitrary"), 52),
        name="s5_prompt",
    )(z, dk, w_e, w_s, a_t, g_a, g_b)


def _s5_sample(z, row0, n_b, n_new, tabs, g_a, g_b, h0, lyr, hb):
    dk, w_e, w_s, a_t = tabs
    sw = a_t.shape[-1]
    rows = n_b * n_new
    assert row0 % rows == 0
    rb = row0 // rows
    tab = lambda arr, off=lyr * hb: pl.BlockSpec((None,) + arr.shape[1:],
                                                 lambda i: (off + i,) + (0,) * (arr.ndim - 1))
    return pl.pallas_call(
        functools.partial(_s5_sample_body, nb=n_b, nc=n_new // S5_T),
        grid=(hb,),
        in_specs=[pl.BlockSpec((rows, LANES), lambda i: (rb, i)),
                  tab(dk), tab(w_e), tab(w_s), tab(a_t), tab(g_a), tab(g_b), tab(h0, 0)],
        out_specs=[pl.BlockSpec((rows, LANES), lambda i: (0, i)),
                   pl.BlockSpec((None, n_b, sw), lambda i: (i, 0, 0))],
        out_shape=[jax.ShapeDtypeStruct((rows, hb * LANES), F32),
                   jax.ShapeDtypeStruct((hb, n_b, sw), F32)],
        scratch_shapes=[pltpu.VMEM((S5_F, S5_F), BF16)],
        compiler_params=_params(("arbitrary",), 52),
        name="s5_sample",
    )(z, dk, w_e, w_s, a_t, g_a, g_b, h0)


def _glu_blocks(glu_w):
    h, g, _ = glu_w.shape
    hb = h // S5_GB
    eye = jnp.eye(S5_GB, dtype=F32)

    def diag(wm):
        wm = wm.reshape(hb, S5_GB, g, g)
        return (eye[None, :, None, :, None] * wm[:, :, :, None, :]).reshape(hb, LANES, LANES).astype(BF16)

    return diag(glu_w[..., :g]), diag(glu_w[..., g:])


def _state_to_lanes(re, im):
    n, h, p = re.shape
    hb = h // S5_GB
    f = lambda v: v.reshape(n, hb, S5_GB * p).transpose(1, 0, 2)
    return jnp.concatenate([f(re), f(im)], axis=-1)


def _state_from_lanes(s):
    nl, hb, n, sw = s.shape
    half = sw // 2
    f = lambda v: v.transpose(0, 2, 1, 3).reshape(nl, n, hb * S5_GB, half // S5_GB)
    return f(s[..., :half]), f(s[..., half:])


def kernel(x_prompt, x_sample, cache_k, cache_v, state_ssm_re, state_ssm_im, p_prompt, p_sample,
           norm_mix, w_in, ssm_a_re, ssm_a_im, ssm_log_dt, ssm_b_re, ssm_b_im, ssm_c_re, ssm_c_im,
           ssm_d, ssm_glu_w, attn_sinks, rel_bias, norm_grp_ssm, norm_grp_attn, w_out, norm_ffn,
           ffn_w_gate, ffn_w_up, ffn_w_down, router_w, moe_w_gate, moe_w_up, moe_w_down,
           norm_ple, ple_w_gate, ple_w_proj, norm_final):
    nb_p, seq, d = x_prompt.shape
    nb_s, n_new, _ = x_sample.shape
    depth = w_in.shape[0]
    d_ssm = norm_grp_ssm.shape[-1]
    d_attn = norm_grp_attn.shape[-1]
    mp, ms = nb_p * seq, nb_s * n_new
    win = min(WINDOW, seq)
    assert d_ssm % LANES == 0 and seq % S5_T == 0 and n_new % S5_T == 0

    h = jnp.concatenate([x_prompt.reshape(mp, d), x_sample.reshape(ms, d)], axis=0)
    p_all = jnp.concatenate([p_prompt.reshape(depth, mp, -1), p_sample.reshape(depth, ms, -1)], axis=1)

    n_grp = ssm_a_re.shape[1]
    hb = n_grp // S5_GB
    flat = lambda v: v.reshape((depth * n_grp,) + v.shape[2:])
    tabs = _s5_tables(flat(ssm_a_re), flat(ssm_a_im), ssm_log_dt.reshape(-1), flat(ssm_b_re), flat(ssm_b_im),
                      flat(ssm_c_re), flat(ssm_c_im), ssm_d.reshape(-1))
    g_a, g_b = _glu_blocks(flat(ssm_glu_w))

    kv_p, kv_s, fin_ps, fin_ss = [], [], [], []
    for i in range(depth):
        xn = _rmsnorm(h, norm_mix[i], BF16)
        z = _mm(xn, w_in, i, tm=1024, tn=1280, vmem_mib=56)
        ssm_p, fin_p = _s5_prompt(z, nb_p, seq, tabs, g_a, g_b, i, hb)
        ssm_s, fin_s = _s5_sample(z, mp, nb_s, n_new, tabs, g_a, g_b,
                                  _state_to_lanes(state_ssm_re[i], state_ssm_im[i]), i, hb)
        att_p = _attn_prompt(z, nb_p, seq, d_ssm, d_attn, rel_bias, attn_sinks[i])
        att_s = _attn_sample(z, mp, nb_s, n_new, d_ssm, d_attn, cache_k, cache_v, i, rel_bias,
                             attn_sinks[i])
        merged = _merge_norm(ssm_p, ssm_s, att_p, att_s, norm_grp_ssm[i], norm_grp_attn[i])
        h = _mm(merged, w_out, i, tm=1024, tn=1024, res=h, vmem_mib=56)

        kv_p.append(jnp.stack([z[(b + 1) * seq - win:(b + 1) * seq, d_ssm + d_attn:] for b in range(nb_p)]))
        kv_s.append(z[mp:, d_ssm + d_attn:])
        fin_ps.append(fin_p.reshape(hb, nb_p, -1))
        fin_ss.append(fin_s)

        j = i // 2
        if i % 2 == 0:
            hn = _rmsnorm(h, norm_ffn[i], BF16)
            t = _swiglu(hn, ffn_w_gate, ffn_w_up, j, tm=1024, tn=512)
            h = _mm(t, ffn_w_down, j, tm=512, tn=512, res=h, vmem_mib=52)
        else:
            hn, e_idx, gates = _rmsnorm_router(h, norm_ffn[i], router_w[j])
            h = h + _moe_ffn(hn, e_idx, gates, moe_w_gate, moe_w_up, moe_w_down, j)

        hn = _rmsnorm(h, norm_ple[i], BF16)
        h = _ple(hn, p_all, ple_w_gate, ple_w_proj, h, i, tm=1024, tn=1024, vmem_mib=56)

    y = _rmsnorm(h, norm_final, F32)
    kv_p = jnp.stack(kv_p).reshape(depth, nb_p, win, 2, KV_HEADS, HEAD_DIM)
    kv_s = jnp.stack(kv_s).reshape(depth, nb_s, n_new, 2, KV_HEADS, HEAD_DIM)
    p_re, p_im = _state_from_lanes(jnp.stack(fin_ps))
    s_re, s_im = _state_from_lanes(jnp.stack(fin_ss))
    return (y[:mp].reshape(nb_p, seq, d), y[mp:].reshape(nb_s, n_new, d),
            kv_p[:, :, :, 0], kv_p[:, :, :, 1], p_re, p_im,
            kv_s[:, :, :, 0], kv_s[:, :, :, 1], s_re, s_im)
```

```python
import functools
import math

import numpy as np
import jax
import jax.numpy as jnp
from jax import lax
from jax.experimental import pallas as pl
from jax.experimental.pallas import tpu as pltpu

F32 = jnp.float32
BF16 = jnp.bfloat16
HIGHEST = lax.Precision.HIGHEST

CHUNK = 64
WINDOW = 128
HEAD_DIM = 64
KV_HEADS = 4
SSM_GROUP_CH = 16
SSM_STATE = 64
N_BUCKETS = 32
REL_MAX_DIST = 128
EPS = 1e-6
NEG_INF = -1e30
ATTN_SCALE = HEAD_DIM ** -0.5

LANES = 128
S5_T = 16
S5_GB = LANES // SSM_GROUP_CH
S5_F = S5_T * LANES
MOE_TM = 512
MOE_SUB = 256
MIB = 1024 * 1024


def _pick(n, pref, mult=8):
    best = None
    for d in range(mult, min(n, pref) + 1, mult):
        if n % d == 0:
            best = d
    return best if best is not None else n


def _params(sem, vmem_mib):
    return pltpu.CompilerParams(dimension_semantics=sem, vmem_limit_bytes=int(vmem_mib * MIB))


def _rms(x, g):
    return x * lax.rsqrt(jnp.mean(x * x, axis=-1, keepdims=True) + EPS) * g


def _norm_body(x_ref, g_ref, o_ref):
    o_ref[...] = _rms(x_ref[...], g_ref[...]).astype(o_ref.dtype)


def _rmsnorm(x, g, out_dtype):
    m, d = x.shape
    tm = _pick(m, 512)
    return pl.pallas_call(
        _norm_body,
        grid=(m // tm,),
        in_specs=[pl.BlockSpec((tm, d), lambda i: (i, 0)), pl.BlockSpec((1, d), lambda i: (0, 0))],
        out_specs=pl.BlockSpec((tm, d), lambda i: (i, 0)),
        out_shape=jax.ShapeDtypeStruct((m, d), out_dtype),
        compiler_params=_params(("arbitrary",), 40),
        name="rmsnorm",
    )(x, g.reshape(1, d))


def _norm_split_body(x_ref, g_ref, op_ref, os_ref, *, np_tiles):
    y = _rms(x_ref[...], g_ref[...])

    @pl.when(pl.program_id(0) < np_tiles)
    def _():
        op_ref[...] = y

    @pl.when(pl.program_id(0) >= np_tiles)
    def _():
        os_ref[...] = y


def _rmsnorm_split(x, g, mp):
    m, d = x.shape
    ms = m - mp
    tm = _pick(math.gcd(mp, ms), 512)
    npt = mp // tm
    return pl.pallas_call(
        functools.partial(_norm_split_body, np_tiles=npt),
        grid=(m // tm,),
        in_specs=[pl.BlockSpec((tm, d), lambda i: (i, 0)), pl.BlockSpec((1, d), lambda i: (0, 0))],
        out_specs=[pl.BlockSpec((tm, d), lambda i: (jnp.minimum(i, npt - 1), 0)),
                   pl.BlockSpec((tm, d), lambda i: (jnp.maximum(i - npt, 0), 0))],
        out_shape=[jax.ShapeDtypeStruct((mp, d), F32), jax.ShapeDtypeStruct((ms, d), F32)],
        compiler_params=_params(("arbitrary",), 40),
        name="rmsnorm_split",
    )(x, g.reshape(1, d))


def _norm_router_body(x_ref, g_ref, rw_ref, o_ref, idx_ref, gate_ref):
    y = _rms(x_ref[...], g_ref[...])
    tm, d = y.shape
    pieces = d // LANES
    for s in range(pieces):
        o_ref[pl.ds(s, tm, stride=pieces), :] = y[:, s * LANES:(s + 1) * LANES]
    n_exp = rw_ref.shape[0]
    logits = jnp.concatenate([jnp.sum(y * rw_ref[e:e + 1, :], axis=-1, keepdims=True) for e in range(n_exp)],
                             axis=-1)
    lane = lax.broadcasted_iota(jnp.int32, logits.shape, 1)
    m1 = jnp.max(logits, axis=-1, keepdims=True)
    i1 = jnp.min(jnp.where(logits == m1, lane, n_exp), axis=-1, keepdims=True)
    rest = jnp.where(lane == i1, -jnp.inf, logits)
    m2 = jnp.max(rest, axis=-1, keepdims=True)
    i2 = jnp.min(jnp.where(rest == m2, lane, n_exp), axis=-1, keepdims=True)
    e2 = jnp.exp(m2 - m1)
    den = 1.0 + e2
    idx_ref[...] = jnp.concatenate([i1, i2], axis=-1)
    gate_ref[...] = jnp.concatenate([1.0 / den, e2 / den], axis=-1)


def _rmsnorm_router(x, g, router_w):
    m, d = x.shape
    n_exp = router_w.shape[-1]
    tm = _pick(m, 512)
    return pl.pallas_call(
        _norm_router_body,
        grid=(m // tm,),
        in_specs=[pl.BlockSpec((tm, d), lambda i: (i, 0)),
                  pl.BlockSpec((1, d), lambda i: (0, 0)),
                  pl.BlockSpec((n_exp, d), lambda i: (0, 0))],
        out_specs=[pl.BlockSpec((tm * (d // LANES), LANES), lambda i: (i, 0)),
                   pl.BlockSpec((tm, 2), lambda i: (i, 0)),
                   pl.BlockSpec((tm, 2), lambda i: (i, 0))],
        out_shape=[jax.ShapeDtypeStruct((m * (d // LANES), LANES), F32),
                   jax.ShapeDtypeStruct((m, 2), jnp.int32),
                   jax.ShapeDtypeStruct((m, 2), F32)],
        compiler_params=_params(("arbitrary",), 40),
        name="rmsnorm_router",
    )(x, g.reshape(1, d), router_w.T)


def _merge_norm_body(ap_ref, as_ref, bp_ref, bs_ref, ga_ref, gb_ref, o_ref, *, np_tiles):
    da = ap_ref.shape[-1]

    def emit(a_ref, b_ref):
        o_ref[:, :da] = _rms(a_ref[...], ga_ref[...]).astype(o_ref.dtype)
        o_ref[:, da:] = _rms(b_ref[...], gb_ref[...]).astype(o_ref.dtype)

    @pl.when(pl.program_id(0) < np_tiles)
    def _():
        emit(ap_ref, bp_ref)

    @pl.when(pl.program_id(0) >= np_tiles)
    def _():
        emit(as_ref, bs_ref)


def _merge_norm(a_p, a_s, b_p, b_s, ga, gb):
    (mp, da), (ms, db) = a_p.shape, b_s.shape
    tm = _pick(math.gcd(mp, ms), 512)
    npt, nst = mp // tm, ms // tm
    pspec = lambda w: pl.BlockSpec((tm, w), lambda i: (jnp.minimum(i, npt - 1), 0))
    sspec = lambda w: pl.BlockSpec((tm, w), lambda i: (jnp.maximum(i - npt, 0), 0))
    return pl.pallas_call(
        functools.partial(_merge_norm_body, np_tiles=npt),
        grid=(npt + nst,),
        in_specs=[pspec(da), sspec(da), pspec(db), sspec(db),
                  pl.BlockSpec((1, da), lambda i: (0, 0)), pl.BlockSpec((1, db), lambda i: (0, 0))],
        out_specs=pl.BlockSpec((tm, da + db), lambda i: (i, 0)),
        out_shape=jax.ShapeDtypeStruct((mp + ms, da + db), BF16),
        compiler_params=_params(("arbitrary",), 40),
        name="merge_norm",
    )(a_p, a_s, b_p, b_s, ga.reshape(1, da), gb.reshape(1, db))


def _mm_body(*refs, has_res):
    if has_res:
        x_ref, w_ref, r_ref, o_ref, wb_ref = refs
    else:
        x_ref, w_ref, o_ref, wb_ref = refs

    @pl.when(pl.program_id(1) == 0)
    def _():
        wb_ref[...] = w_ref[...].astype(BF16)

    acc = jnp.dot(x_ref[...], wb_ref[...], preferred_element_type=F32)
    if has_res:
        acc = r_ref[...] + acc
    o_ref[...] = acc.astype(o_ref.dtype)


def _mm(x, w, lyr, *, tm, tn, out_dtype=F32, res=None, vmem_mib=48):
    m, k = x.shape
    n = w.shape[-1]
    tm, tn = _pick(m, tm), _pick(n, tn, 128)
    in_specs = [pl.BlockSpec((tm, k), lambda j, i: (i, 0)),
                pl.BlockSpec((None, k, tn), lambda j, i: (lyr, 0, j))]
    args = [x, w]
    if res is not None:
        in_specs.append(pl.BlockSpec((tm, tn), lambda j, i: (i, j)))
        args.append(res)
    return pl.pallas_call(
        functools.partial(_mm_body, has_res=res is not None),
        grid=(n // tn, m // tm),
        in_specs=in_specs,
        out_specs=pl.BlockSpec((tm, tn), lambda j, i: (i, j)),
        out_shape=jax.ShapeDtypeStruct((m, n), out_dtype),
        scratch_shapes=[pltpu.VMEM((k, tn), BF16)],
        compiler_params=_params(("arbitrary", "arbitrary"), vmem_mib),
        name="mm_res" if res is not None else "mm",
    )(*args)


def _swiglu_math(x, wg, wu):
    a = jnp.dot(x, wg, preferred_element_type=F32)
    b = jnp.dot(x, wu, preferred_element_type=F32)
    return (a * jax.nn.sigmoid(a) * b).astype(BF16)


def _swiglu_body(x_ref, wg_ref, wu_ref, o_ref, wgb_ref, wub_ref):
    @pl.when(pl.program_id(1) == 0)
    def _():
        wgb_ref[...] = wg_ref[...].astype(BF16)
        wub_ref[...] = wu_ref[...].astype(BF16)

    o_ref[...] = _swiglu_math(x_ref[...], wgb_ref[...], wub_ref[...])


def _swiglu(x, wg, wu, lyr, *, tm, tn, vmem_mib=48):
    m, k = x.shape
    n = wg.shape[-1]
    tm, tn = _pick(m, tm), _pick(n, tn, 128)
    wspec = pl.BlockSpec((None, k, tn), lambda j, i: (lyr, 0, j))
    return pl.pallas_call(
        _swiglu_body,
        grid=(n // tn, m // tm),
        in_specs=[pl.BlockSpec((tm, k), lambda j, i: (i, 0)), wspec, wspec],
        out_specs=pl.BlockSpec((tm, tn), lambda j, i: (i, j)),
        out_shape=jax.ShapeDtypeStruct((m, n), BF16),
        scratch_shapes=[pltpu.VMEM((k, tn), BF16), pltpu.VMEM((k, tn), BF16)],
        compiler_params=_params(("arbitrary", "arbitrary"), vmem_mib),
        name="swiglu",
    )(x, wg, wu)


def _ple_body(x_ref, p_ref, wg_ref, wp_ref, r_ref, o_ref, wgb_ref, wpb_ref):
    @pl.when(pl.program_id(1) == 0)
    def _():
        wgb_ref[...] = wg_ref[...].astype(BF16)
        wpb_ref[...] = wp_ref[...].astype(BF16)

    a = jnp.dot(x_ref[...], wgb_ref[...], preferred_element_type=F32)
    b = jnp.dot(p_ref[...].astype(BF16), wpb_ref[...], preferred_element_type=F32)
    o_ref[...] = r_ref[...] + jax.nn.sigmoid(a) * b


def _ple(x, p, wg, wp, res, lyr, *, tm, tn, vmem_mib=48):
    m, k = x.shape
    kp = p.shape[-1]
    n = wg.shape[-1]
    tm, tn = _pick(m, tm), _pick(n, tn, 128)
    return pl.pallas_call(
        _ple_body,
        grid=(n // tn, m // tm),
        in_specs=[pl.BlockSpec((tm, k), lambda j, i: (i, 0)),
                  pl.BlockSpec((None, tm, kp), lambda j, i: (lyr, i, 0)),
                  pl.BlockSpec((None, k, tn), lambda j, i: (lyr, 0, j)),
                  pl.BlockSpec((None, kp, tn), lambda j, i: (lyr, 0, j)),
                  pl.BlockSpec((tm, tn), lambda j, i: (i, j))],
        out_specs=pl.BlockSpec((tm, tn), lambda j, i: (i, j)),
        out_shape=jax.ShapeDtypeStruct((m, n), F32),
        scratch_shapes=[pltpu.VMEM((k, tn), BF16), pltpu.VMEM((kp, tn), BF16)],
        compiler_params=_params(("arbitrary", "arbitrary"), vmem_mib),
        name="ple",
    )(x, p, wg, wp, res)


def _moe_weights(w_hbms, stages, bf16s, sem, e_ref, start_ref, nxt_ref, lyr):
    j, i = pl.program_id(0), pl.program_id(1)
    tn = stages[0].shape[1]

    def copies(e, col):
        col = pl.multiple_of(col, LANES)
        return [pltpu.make_async_copy(w.at[lyr, e, :, pl.ds(col, tn)], st, sem.at[n])
                for n, (w, st) in enumerate(zip(w_hbms, stages))]

    @pl.when(jnp.logical_and(j == 0, i == 0))
    def _():
        for cp in copies(e_ref[0], 0):
            cp.start()

    @pl.when(start_ref[i] == 1)
    def _():
        for cp in copies(e_ref[i], j * tn):
            cp.wait()
        rows = 128

        def cast_chunk(c, carry):
            r = pl.multiple_of(c * rows, rows)
            for st, wb in zip(stages, bf16s):
                wb[pl.ds(r, rows), :] = st[pl.ds(r, rows), :].astype(BF16)
            return carry
        lax.fori_loop(0, stages[0].shape[0] // rows, cast_chunk, 0)
        nxt = nxt_ref[i]

        @pl.when(nxt >= 0)
        def _():
            for cp in copies(nxt, j * tn):
                cp.start()

        @pl.when(jnp.logical_and(nxt < 0, j + 1 < pl.num_programs(0)))
        def _():
            for cp in copies(e_ref[0], (j + 1) * tn):
                cp.start()


def _active_prefix(act_ref, i, per):
    n_act = act_ref[i * per]
    for s in range(1, per):
        n_act = n_act + act_ref[i * per + s]
    return n_act


def _prefix_rows(o_ref, n_act, fn):
    tm = o_ref.shape[0]
    for c in range(tm // MOE_SUB + 1):
        rows = c * MOE_SUB

        @pl.when(n_act == c)
        def _():
            if rows:
                o_ref[:rows, :] = fn(rows)
            if rows < tm:
                o_ref[rows:, :] = jnp.zeros((tm - rows, o_ref.shape[1]), o_ref.dtype)


def _moe_up_body(e_ref, act_ref, start_ref, nxt_ref, x_ref, wg_hbm, wu_hbm, o_ref,
                 wgb_ref, wub_ref, sg_ref, su_ref, sem, *, lyr):
    _moe_weights((wg_hbm, wu_hbm), (sg_ref, su_ref), (wgb_ref, wub_ref), sem, e_ref, start_ref, nxt_ref, lyr)
    n_act = _active_prefix(act_ref, pl.program_id(1), x_ref.shape[0] // MOE_SUB)
    _prefix_rows(o_ref, n_act, lambda rows: _swiglu_math(x_ref[:rows, :], wgb_ref[...], wub_ref[...]))


def _moe_down_body(e_ref, act_ref, start_ref, nxt_ref, h_ref, wd_hbm, o_ref, wdb_ref, sd_ref, sem, *, lyr):
    _moe_weights((wd_hbm,), (sd_ref,), (wdb_ref,), sem, e_ref, start_ref, nxt_ref, lyr)
    n_act = _active_prefix(act_ref, pl.program_id(1), h_ref.shape[0] // MOE_SUB)
    _prefix_rows(o_ref, n_act,
                 lambda rows: jnp.dot(h_ref[:rows, :], wdb_ref[...], preferred_element_type=F32))


def _moe_matmul(body, x, ws, sched, lyr, *, tn, out_dtype, vmem_mib, name):
    ns, k = x.shape
    n = ws[0].shape[-1]
    tm, tn = MOE_TM, _pick(n, tn, 128)
    grid_spec = pltpu.PrefetchScalarGridSpec(
        num_scalar_prefetch=len(sched),
        grid=(n // tn, ns // tm),
        in_specs=[pl.BlockSpec((tm, k), lambda j, i, *_: (i, 0))]
                 + [pl.BlockSpec(memory_space=pl.ANY)] * len(ws),
        out_specs=pl.BlockSpec((tm, tn), lambda j, i, *_: (i, j)),
        scratch_shapes=[pltpu.VMEM((k, tn), BF16)] * len(ws) + [pltpu.VMEM((k, tn), F32)] * len(ws)
                       + [pltpu.SemaphoreType.DMA((len(ws),))],
    )
    return pl.pallas_call(
        functools.partial(body, lyr=lyr),
        grid_spec=grid_spec,
        out_shape=jax.ShapeDtypeStruct((ns, n), out_dtype),
        compiler_params=_params(("arbitrary", "arbitrary"), vmem_mib),
        name=name,
    )(*sched, x, *ws)


def _row_gather_copy(x_hbm, buf, sem, slot, src_row, dst_row, rows):
    return pltpu.make_async_copy(x_hbm.at[pl.ds(src_row, rows)], buf.at[slot, pl.ds(dst_row, rows)],
                                 sem.at[slot])


def _moe_gather_body(tok_ref, x_hbm, o_ref, buf, sem):
    i = pl.program_id(0)
    tm = o_ref.shape[0]
    pieces = o_ref.shape[1] // LANES

    def issue(blk, slot):
        def body(r2, carry):
            for pri in range(2):
                r = r2 * 2 + pri
                _row_gather_copy(x_hbm, buf, sem, slot, tok_ref[blk * tm + r] * pieces, r * pieces,
                                 pieces).start(priority=pri)
            return carry
        lax.fori_loop(0, tm // 2, body, 0, unroll=4)

    @pl.when(i == 0)
    def _():
        issue(0, 0)

    @pl.when(i + 1 < pl.num_programs(0))
    def _():
        issue(i + 1, (i + 1) % 2)

    slot = i % 2
    _row_gather_copy(x_hbm, buf, sem, slot, 0, 0, tm * pieces).wait()
    cur = buf.at[slot]
    o_ref[...] = jnp.concatenate([cur[pl.ds(s, tm, stride=pieces), :] for s in range(pieces)],
                                 axis=-1).astype(o_ref.dtype)


def _moe_gather(x_rows, slot_tok, d):
    n_slots = slot_tok.shape[0]
    tm = MOE_TM
    grid_spec = pltpu.PrefetchScalarGridSpec(
        num_scalar_prefetch=1,
        grid=(n_slots // tm,),
        in_specs=[pl.BlockSpec(memory_space=pl.ANY)],
        out_specs=pl.BlockSpec((tm, d), lambda i, tok: (i, 0)),
        scratch_shapes=[pltpu.VMEM((2, tm * (d // LANES), LANES), F32), pltpu.SemaphoreType.DMA((2,))],
    )
    return pl.pallas_call(
        _moe_gather_body,
        grid_spec=grid_spec,
        out_shape=jax.ShapeDtypeStruct((n_slots, d), BF16),
        compiler_params=_params(("arbitrary",), 32),
        name="moe_gather",
    )(slot_tok, x_rows)


def _moe_ffn(hn_rows, e_idx, gates, wg, wu, wd, lyr):
    m = e_idx.shape[0]
    d = wg.shape[2]
    n_exp = wg.shape[1]
    top_k = e_idx.shape[-1]
    tm = MOE_TM
    n_asg = m * top_k
    e_flat = e_idx.reshape(n_asg)
    onehot = (e_flat[None, :] == jnp.arange(n_exp, dtype=jnp.int32)[:, None]).astype(jnp.int32)
    counts = jnp.sum(onehot, axis=1)
    start = jnp.cumsum(counts) - counts
    padded = (counts + tm - 1) // tm * tm
    pad_end = jnp.cumsum(padded)
    pad_start = pad_end - padded
    order = jnp.argsort(e_flat).astype(jnp.int32)
    rank = jnp.argsort(order).astype(jnp.int32)
    slot_of = rank + jnp.sum(onehot * (pad_start - start)[:, None], axis=0).astype(jnp.int32)
    n_blocks = n_asg // tm + n_exp
    n_slots = n_blocks * tm
    n_used = (pad_end[-1] // tm).astype(jnp.int32)
    blk = jnp.arange(n_blocks, dtype=jnp.int32)
    blk_start = jnp.minimum(blk, n_used - 1) * tm
    blk_expert = jnp.sum((pad_end[None, :] <= blk_start[:, None]).astype(jnp.int32), axis=1)
    blk_expert = jnp.minimum(blk_expert, n_exp - 1).astype(jnp.int32)
    sub_start = jnp.arange(n_slots // MOE_SUB, dtype=jnp.int32) * MOE_SUB
    sub_expert = jnp.repeat(blk_expert, tm // MOE_SUB)
    real_end = (pad_start + counts)[sub_expert]
    sub_active = jnp.logical_and(sub_start < real_end, sub_start < pad_end[-1]).astype(jnp.int32)
    slot_r = jnp.arange(n_slots, dtype=jnp.int32) - jnp.repeat(pad_start[sub_expert], MOE_SUB)
    slot_src = jnp.repeat(start[sub_expert], MOE_SUB) + slot_r
    slot_real = jnp.logical_and(slot_r >= 0, slot_r < jnp.repeat(counts[sub_expert], MOE_SUB))
    slot_tok = jnp.where(slot_real, order[jnp.clip(slot_src, 0, n_asg - 1)] // top_k, 0).astype(jnp.int32)

    run_start = jnp.concatenate([jnp.ones((1,), jnp.int32),
                                 (blk_expert[1:] != blk_expert[:-1]).astype(jnp.int32)])
    later_start = jnp.where(run_start == 1, blk, n_blocks)
    nxt_blk = lax.cummin(jnp.concatenate([later_start[1:], jnp.full((1,), n_blocks, jnp.int32)]),
                         axis=0, reverse=True)
    run_next = jnp.where(nxt_blk < n_blocks, blk_expert[jnp.minimum(nxt_blk, n_blocks - 1)], -1)
    sched = (blk_expert, sub_active, run_start, run_next.astype(jnp.int32))

    x_sorted = _moe_gather(hn_rows, slot_tok, d)
    h_sorted = _moe_matmul(_moe_up_body, x_sorted, (wg, wu), sched, lyr, tn=1024, out_dtype=BF16,
                           vmem_mib=48, name="moe_up")
    y_sorted = _moe_matmul(_moe_down_body, h_sorted, (wd,), sched, lyr, tn=512, out_dtype=F32,
                           vmem_mib=48, name="moe_down")
    sl = slot_of.reshape(m, top_k)
    y = y_sorted[sl[:, 0]] * gates[:, 0:1]
    for t in range(1, top_k):
        y = y + y_sorted[sl[:, t]] * gates[:, t:t + 1]
    return y


def _rel_buckets(n_q, n_k_past, n_k):
    rel = (np.arange(n_k) - n_k_past)[None, :] - np.arange(n_q)[:, None]
    half = N_BUCKETS // 2
    max_exact = half // 2
    n = np.abs(rel)
    nf = np.maximum(n, 1).astype(np.float32)
    large = max_exact + (np.log(nf / np.float32(max_exact)) / np.float32(math.log(REL_MAX_DIST / max_exact))
                         * np.float32(half - max_exact)).astype(np.int32)
    large = np.minimum(large, half - 1)
    return (np.where(rel > 0, half, 0) + np.where(n < max_exact, n, large)).astype(np.int32)


def _build_bias(bias_ref, sink_ref, bucket_ref, table_ref, sinks_ref):
    bucket = bucket_ref[...]
    nq = bucket.shape[1]
    gq = bias_ref.shape[2] // nq
    for kh in range(bias_ref.shape[0]):
        for g in range(gq):
            n = kh * gq + g
            acc = jnp.zeros(bucket.shape, F32)
            for b in range(N_BUCKETS):
                acc = jnp.where(bucket == b, table_ref[b, n], acc)
            bias_ref[kh, :, g * nq:(g + 1) * nq] = acc
            sink_ref[kh, :, g * nq:(g + 1) * nq] = jnp.full((1, nq), sinks_ref[n], F32)


def _attn_items(items, bias_ref, sink_ref):
    nq = items[0][0].shape[0]
    gq = items[0][0].shape[-1] // HEAD_DIM // KV_HEADS
    head = lambda x, n: x[:, n * HEAD_DIM:(n + 1) * HEAD_DIM]
    scores = []
    for q, k, _, valid in items:
        for kh in range(KV_HEADS):
            q4 = jnp.concatenate([head(q, kh * gq + g) for g in range(gq)], axis=0)
            s = lax.dot_general(head(k, kh), q4, (((1,), (1,)), ((), ())), preferred_element_type=F32)
            scores.append((s, kh, valid))
    probs = []
    for s, kh, valid in scores:
        s = s * ATTN_SCALE + bias_ref[kh]
        if valid is not None:
            s = jnp.where(valid, s, NEG_INF)
        sink = sink_ref[kh]
        mx = jnp.maximum(jnp.max(s, axis=0, keepdims=True), sink)
        e = jnp.exp(s - mx)
        den = jnp.sum(e, axis=0, keepdims=True) + jnp.exp(sink - mx)
        probs.append((e * (1.0 / den)).astype(BF16))
    outs = []
    for i, (_, _, v, _) in enumerate(items):
        heads = []
        for kh in range(KV_HEADS):
            o4 = lax.dot_general(probs[i * KV_HEADS + kh], head(v, kh), (((0,), (0,)), ((), ())),
                                 preferred_element_type=F32)
            heads.extend(o4[g * nq:(g + 1) * nq] for g in range(gq))
        outs.append(jnp.concatenate(heads, axis=-1))
    return outs


def _attn_prompt_body(q_ref, kp_ref, kc_ref, vp_ref, vc_ref, bucket_ref, table_ref, sinks_ref,
                      o_ref, bias_ref, sink_ref, *, chunks):
    j = pl.program_id(1)

    @pl.when(jnp.logical_and(pl.program_id(0) == 0, j == 0))
    def _():
        _build_bias(bias_ref, sink_ref, bucket_ref, table_ref, sinks_ref)

    q = q_ref[...].astype(BF16)
    k = jnp.concatenate([kp_ref[...], kc_ref[...]], axis=0).astype(BF16)
    v = jnp.concatenate([vp_ref[...], vc_ref[...]], axis=0).astype(BF16)
    band = WINDOW + CHUNK
    items = []
    for c in range(chunks):
        lo = c * CHUNK
        key_valid = None
        if lo < WINDOW:
            key_pos = j * (chunks * CHUNK) + lo - WINDOW + lax.broadcasted_iota(jnp.int32, (band, 1), 0)
            key_valid = key_pos >= 0
        items.append((q[lo:lo + CHUNK], k[lo:lo + band], v[lo:lo + band], key_valid))
    for c, o in enumerate(_attn_items(items, bias_ref, sink_ref)):
        o_ref[c * CHUNK:(c + 1) * CHUNK, :] = o


def _attn_prompt(z, n_b, seq, d_ssm, d_attn, table, sinks):
    d_kv = KV_HEADS * HEAD_DIM
    assert d_ssm % d_attn == 0 and d_ssm % d_kv == 0 and d_attn % d_kv == 0
    qb = _pick(seq, 4 * CHUNK, 2 * CHUNK)
    assert qb % WINDOW == 0 and seq % qb == 0
    chunks = qb // CHUNK
    nj = seq // qb
    r = qb // WINDOW
    kcol = (d_ssm + d_attn) // d_kv
    gq = d_attn // HEAD_DIM // KV_HEADS
    band = WINDOW + CHUNK
    bucket = jnp.asarray(_rel_buckets(CHUNK, WINDOW, band).T)

    def prev(b, j):
        return b * (seq // WINDOW) + jnp.maximum(j * r - 1, 0)

    return pl.pallas_call(
        functools.partial(_attn_prompt_body, chunks=chunks),
        grid=(n_b, nj),
        in_specs=[pl.BlockSpec((qb, d_attn), lambda b, j: (b * nj + j, d_ssm // d_attn)),
                  pl.BlockSpec((WINDOW, d_kv), lambda b, j: (prev(b, j), kcol)),
                  pl.BlockSpec((qb, d_kv), lambda b, j: (b * nj + j, kcol)),
                  pl.BlockSpec((WINDOW, d_kv), lambda b, j: (prev(b, j), kcol + 1)),
                  pl.BlockSpec((qb, d_kv), lambda b, j: (b * nj + j, kcol + 1)),
                  pl.BlockSpec((band, CHUNK), lambda b, j: (0, 0)),
                  pl.BlockSpec(memory_space=pltpu.SMEM),
                  pl.BlockSpec(memory_space=pltpu.SMEM)],
        out_specs=pl.BlockSpec((qb, d_attn), lambda b, j: (b * nj + j, 0)),
        out_shape=jax.ShapeDtypeStruct((n_b * seq, d_attn), F32),
        scratch_shapes=[pltpu.VMEM((KV_HEADS, band, gq * CHUNK), F32),
                        pltpu.VMEM((KV_HEADS, 1, gq * CHUNK), F32)],
        compiler_params=_params(("arbitrary", "arbitrary"), 40),
        name="attn_prompt",
    )(z, z, z, z, z, bucket, table, sinks)


def _attn_sample_body(q_ref, kc_ref, kn_ref, vc_ref, vn_ref, bucket_ref, table_ref, sinks_ref,
                      o_ref, bias_ref, sink_ref, *, n_bs, n_new, n_past):
    @pl.when(pl.program_id(0) == 0)
    def _():
        _build_bias(bias_ref, sink_ref, bucket_ref, table_ref, sinks_ref)

    q = q_ref[...].astype(BF16)
    kc, kn = kc_ref[...].astype(BF16), kn_ref[...].astype(BF16)
    vc, vn = vc_ref[...].astype(BF16), vn_ref[...].astype(BF16)
    items = []
    for b in range(n_bs):
        new = slice(b * n_new, (b + 1) * n_new)
        past = slice(b * n_past, (b + 1) * n_past)
        items.append((q[new], jnp.concatenate([kc[past], kn[new]], axis=0),
                      jnp.concatenate([vc[past], vn[new]], axis=0), None))
    for b, o in enumerate(_attn_items(items, bias_ref, sink_ref)):
        o_ref[b * n_new:(b + 1) * n_new, :] = o


def _attn_sample(z, row0, n_b, n_new, d_ssm, d_attn, cache_k, cache_v, lyr, table, sinks):
    d_kv = KV_HEADS * HEAD_DIM
    n_past = cache_k.shape[2]
    n_bs = _pick(n_b, 4, 1)
    rows = n_bs * n_new
    assert row0 % rows == 0
    kcol = (d_ssm + d_attn) // d_kv
    gq = d_attn // HEAD_DIM // KV_HEADS
    n_k = n_past + n_new
    bucket = jnp.asarray(_rel_buckets(n_new, n_past, n_k).T)
    ck = cache_k.reshape(-1, d_kv)
    cv = cache_v.reshape(-1, d_kv)
    rb = row0 // rows
    nsteps = n_b // n_bs
    cache_spec = pl.BlockSpec((n_bs * n_past, d_kv), lambda b: (lyr * nsteps + b, 0))
    return pl.pallas_call(
        functools.partial(_attn_sample_body, n_bs=n_bs, n_new=n_new, n_past=n_past),
        grid=(nsteps,),
        in_specs=[pl.BlockSpec((rows, d_attn), lambda b: (rb + b, d_ssm // d_attn)),
                  cache_spec,
                  pl.BlockSpec((rows, d_kv), lambda b: (rb + b, kcol)),
                  cache_spec,
                  pl.BlockSpec((rows, d_kv), lambda b: (rb + b, kcol + 1)),
                  pl.BlockSpec((n_k, n_new), lambda b: (0, 0)),
                  pl.BlockSpec(memory_space=pltpu.SMEM),
                  pl.BlockSpec(memory_space=pltpu.SMEM)],
        out_specs=pl.BlockSpec((rows, d_attn), lambda b: (b, 0)),
        out_shape=jax.ShapeDtypeStruct((n_b * n_new, d_attn), F32),
        scratch_shapes=[pltpu.VMEM((KV_HEADS, n_k, gq * n_new), F32),
                        pltpu.VMEM((KV_HEADS, 1, gq * n_new), F32)],
        compiler_params=_params(("arbitrary",), 40),
        name="attn_sample",
    )(z, ck, z, cv, z, bucket, table, sinks)


def _tile_lanes(x, reps):
    return jnp.concatenate([x] * reps, axis=-1)


def _s5_tables_body(are_ref, aim_ref, ldt_ref, btr_ref, bti_ref, cre_ref, cim_ref, d_ref,
                    dk_ref, we_ref, ws_ref, a_ref):
    rows, p = cre_ref.shape
    gb = rows // SSM_GROUP_CH
    a_re, a_im = are_ref[...], aim_ref[...]
    dt = jnp.exp(ldt_ref[...])
    mag = jnp.exp(a_re * dt)
    ab_re, ab_im = mag * jnp.cos(a_im * dt), mag * jnp.sin(a_im * dt)

    x, y = ab_re - 1.0, ab_im
    den = a_re * a_re + a_im * a_im
    co_re = (x * a_re + y * a_im) / den
    co_im = (y * a_re - x * a_im) / den
    bb_re = co_re * btr_ref[...] - co_im * bti_ref[...]
    bb_im = co_re * bti_ref[...] + co_im * btr_ref[...]
    c_re, c_im = cre_ref[...], cim_ref[...]

    r_idx = lax.broadcasted_iota(jnp.int32, (rows, 1), 0)
    own_state = (r_idx // SSM_GROUP_CH) == (lax.broadcasted_iota(jnp.int32, (1, gb * p), 1) // p)
    same_group = (r_idx // SSM_GROUP_CH) == (lax.broadcasted_iota(jnp.int32, (1, rows), 1) // SSM_GROUP_CH)
    diag = r_idx == lax.broadcasted_iota(jnp.int32, (1, rows), 1)

    def place(v_re, v_im):
        return jnp.concatenate([jnp.where(own_state, _tile_lanes(v_re, gb), 0.0),
                                jnp.where(own_state, _tile_lanes(v_im, gb), 0.0)], axis=-1)

    dims = (((1,), (1,)), ((), ()))
    bb_cat = jnp.concatenate([bb_re, -bb_im], axis=-1)
    pw_re, pw_im = jnp.ones_like(ab_re), jnp.zeros_like(ab_re)
    for j in range(S5_T + 1):
        ca_re = c_re * pw_re - c_im * pw_im
        ca_im = c_re * pw_im + c_im * pw_re
        if j < S5_T:
            kj = lax.dot_general(bb_cat, jnp.concatenate([ca_re, ca_im], axis=-1), dims,
                                 precision=HIGHEST, preferred_element_type=F32)
            kj = jnp.where(same_group, kj, 0.0)
            if j == 0:
                kj = kj + jnp.where(diag, d_ref[...], 0.0)
            dk_ref[j] = kj.astype(dk_ref.dtype)
            s = S5_T - 1 - j
            we_ref[s * rows:(s + 1) * rows, :] = place(
                pw_re * bb_re - pw_im * bb_im, pw_re * bb_im + pw_im * bb_re).astype(we_ref.dtype)
        if j >= 1:
            ws_ref[(j - 1) * rows:j * rows, :] = place(ca_re, -ca_im).astype(ws_ref.dtype)
        if j == S5_T:
            first = r_idx == (lax.broadcasted_iota(jnp.int32, (1, gb * p), 1) // p) * SSM_GROUP_CH
            a_ref[...] = jnp.concatenate(
                [jnp.sum(jnp.where(first, _tile_lanes(pw_re, gb), 0.0), axis=0, keepdims=True),
                 jnp.sum(jnp.where(first, _tile_lanes(pw_im, gb), 0.0), axis=0, keepdims=True)], axis=-1)
        pw_re, pw_im = pw_re * ab_re - pw_im * ab_im, pw_re * ab_im + pw_im * ab_re


def _s5_tables(a_re, a_im, log_dt, b_re, b_im, c_re, c_im, d_skip):
    h, p = a_re.shape
    g = SSM_GROUP_CH
    hb = h // S5_GB
    sw = 2 * S5_GB * p
    rep = lambda v: jnp.repeat(v, g, axis=0).reshape(hb, LANES, -1)
    blk = lambda w: pl.BlockSpec((None, LANES, w), lambda i: (i, 0, 0))
    wide = pl.BlockSpec((None, S5_F, sw), lambda i: (i, 0, 0))
    return pl.pallas_call(
        _s5_tables_body,
        grid=(hb,),
        in_specs=[blk(p), blk(p), blk(1), blk(p), blk(p), blk(p), blk(p), blk(1)],
        out_specs=[pl.BlockSpec((None, S5_T, LANES, LANES), lambda i: (i, 0, 0, 0)), wide, wide,
                   pl.BlockSpec((None, 1, sw), lambda i: (i, 0, 0))],
        out_shape=[jax.ShapeDtypeStruct((hb, S5_T, LANES, LANES), BF16),
                   jax.ShapeDtypeStruct((hb, S5_F, sw), BF16),
                   jax.ShapeDtypeStruct((hb, S5_F, sw), BF16),
                   jax.ShapeDtypeStruct((hb, 1, sw), F32)],
        compiler_params=_params(("arbitrary",), 40),
        name="s5_tables",
    )(rep(a_re), rep(a_im), rep(log_dt.reshape(h, 1)),
      jnp.swapaxes(b_re, 1, 2).reshape(hb, LANES, p), jnp.swapaxes(b_im, 1, 2).reshape(hb, LANES, p),
      c_re.reshape(hb, LANES, p), c_im.reshape(hb, LANES, p), d_skip.reshape(hb, LANES, 1))


def _split_mul(a, p):
    lane = lax.broadcasted_iota(jnp.int32, a.shape, a.ndim - 1)
    sw = pltpu.roll(a, p, axis=a.ndim - 1)
    return jnp.where(lane < p, a, sw), jnp.where(lane < p, -sw, a)


def _cmul(x, mr, mi, p):
    return x * mr + pltpu.roll(x, p, axis=x.ndim - 1) * mi


def _build_toeplitz(m_ref, dk_ref):
    m_ref[...] = jnp.zeros_like(m_ref)
    for s in range(S5_T):
        for t in range(s, S5_T):
            m_ref[s * LANES:(s + 1) * LANES, t * LANES:(t + 1) * LANES] = dk_ref[t - s]


def _s5_outputs(ucat, enter, m_ref, ws_ref):
    return (jnp.dot(ucat, m_ref[...], preferred_element_type=F32)
            + lax.dot_general(enter.astype(BF16), ws_ref[...], (((1,), (1,)), ((), ())),
                              preferred_element_type=F32))


def _glu_rows(o_ref, ga_ref, gb_ref, tile):
    ga, gb = ga_ref[...], gb_ref[...]
    tile = _pick(o_ref.shape[0], tile)
    for r0 in range(0, o_ref.shape[0], tile):
        y = o_ref[r0:r0 + tile, :].astype(BF16)
        o_ref[r0:r0 + tile, :] = (jnp.dot(y, ga, preferred_element_type=F32)
                                  * jax.nn.sigmoid(jnp.dot(y, gb, preferred_element_type=F32)))


def _s5_prompt_body(z_ref, dk_ref, we_ref, ws_ref, a_ref, ga_ref, gb_ref, o_ref, fin_ref, m_ref,
                    *, nb, nc):
    @pl.when(pl.program_id(1) == 0)
    def _():
        _build_toeplitz(m_ref, dk_ref)

    r = nb * nc
    p = a_ref.shape[-1] // 2
    ucat = jnp.concatenate([z_ref[pl.ds(t, r, stride=S5_T), :].astype(BF16) for t in range(S5_T)], axis=-1)
    x = jnp.dot(ucat, we_ref[...], preferred_element_type=F32)
    a = a_ref[...]
    pos = lax.rem(lax.broadcasted_iota(jnp.int32, (r, 1), 0), nc)
    sh = 1
    while sh < nc:
        mr, mi = _split_mul(a, p)
        shifted = jnp.where(pos >= sh, pltpu.roll(x, sh, axis=0), 0.0)
        x = x + _cmul(shifted, mr, mi, p)
        a = _cmul(a, mr, mi, p)
        sh *= 2
    enter = jnp.where(pos >= 1, pltpu.roll(x, 1, axis=0), 0.0)
    fin_ref[...] = jnp.concatenate([x[(b + 1) * nc - 1:(b + 1) * nc] for b in range(nb)], axis=0)
    y = _s5_outputs(ucat, enter, m_ref, ws_ref)
    for t in range(S5_T):
        o_ref[pl.ds(t, r, stride=S5_T), :] = y[:, t * LANES:(t + 1) * LANES]
    _glu_rows(o_ref, ga_ref, gb_ref, 512)


def _s5_sample_body(z_ref, dk_ref, we_ref, ws_ref, a_ref, ga_ref, gb_ref, h0_ref, o_ref, fin_ref, m_ref,
                    *, nb, nc):
    _build_toeplitz(m_ref, dk_ref)
    p = a_ref.shape[-1] // 2
    seq = nc * S5_T
    ucat = jnp.concatenate(
        [jnp.concatenate([z_ref[pl.ds(c * S5_T + t, nb, stride=seq), :].astype(BF16) for t in range(S5_T)],
                         axis=-1) for c in range(nc)], axis=0)
    e = jnp.dot(ucat, we_ref[...], preferred_element_type=F32)
    mr, mi = _split_mul(a_ref[...], p)
    state = h0_ref[...]
    enter = []
    for c in range(nc):
        enter.append(state)
        state = _cmul(state, mr, mi, p) + e[c * nb:(c + 1) * nb]
    fin_ref[...] = state
    y = _s5_outputs(ucat, jnp.concatenate(enter, axis=0), m_ref, ws_ref)
    for c in range(nc):
        for t in range(S5_T):
            o_ref[pl.ds(c * S5_T + t, nb, stride=seq), :] = y[c * nb:(c + 1) * nb, t * LANES:(t + 1) * LANES]
    _glu_rows(o_ref, ga_ref, gb_ref, 512)


def _s5_prompt(z, n_b, seq, tabs, g_a, g_b, lyr, hb):
    dk, w_e, w_s, a_t = tabs
    sw = a_t.shape[-1]
    nbs = _pick(n_b, 2, 1)
    nc = seq // S5_T
    tab = lambda arr: pl.BlockSpec((None,) + arr.shape[1:],
                                   lambda i, b: (lyr * hb + i,) + (0,) * (arr.ndim - 1))
    return pl.pallas_call(
        functools.partial(_s5_prompt_body, nb=nbs, nc=nc),
        grid=(hb, n_b // nbs),
        in_specs=[pl.BlockSpec((nbs * seq, LANES), lambda i, b: (b, i)),
                  tab(dk), tab(w_e), tab(w_s), tab(a_t), tab(g_a), tab(g_b)],
        out_specs=[pl.BlockSpec((nbs * seq, LANES), lambda i, b: (b, i)),
                   pl.BlockSpec((None, None, nbs, sw), lambda i, b: (i, b, 0, 0))],
        out_shape=[jax.ShapeDtypeStruct((n_b * seq, hb * LANES), F32),
                   jax.ShapeDtypeStruct((hb, n_b // nbs, nbs, sw), F32)],
        scratch_shapes=[pltpu.VMEM((S5_F, S5_F), BF16)],
        compiler_params=_params(("arbitrary", "arbitrary"), 52),
        name="s5_prompt",
    )(z, dk, w_e, w_s, a_t, g_a, g_b)


def _s5_sample(z, row0, n_b, n_new, tabs, g_a, g_b, h0, lyr, hb):
    dk, w_e, w_s, a_t = tabs
    sw = a_t.shape[-1]
    rows = n_b * n_new
    assert row0 % rows == 0
    rb = row0 // rows
    tab = lambda arr, off=lyr * hb: pl.BlockSpec((None,) + arr.shape[1:],
                                                 lambda i: (off + i,) + (0,) * (arr.ndim - 1))
    return pl.pallas_call(
        functools.partial(_s5_sample_body, nb=n_b, nc=n_new // S5_T),
        grid=(hb,),
        in_specs=[pl.BlockSpec((rows, LANES), lambda i: (rb, i)),
                  tab(dk), tab(w_e), tab(w_s), tab(a_t), tab(g_a), tab(g_b), tab(h0, 0)],
        out_specs=[pl.BlockSpec((rows, LANES), lambda i: (0, i)),
                   pl.BlockSpec((None, n_b, sw), lambda i: (i, 0, 0))],
        out_shape=[jax.ShapeDtypeStruct((rows, hb * LANES), F32),
                   jax.ShapeDtypeStruct((hb, n_b, sw), F32)],
        scratch_shapes=[pltpu.VMEM((S5_F, S5_F), BF16)],
        compiler_params=_params(("arbitrary",), 52),
        name="s5_sample",
    )(z, dk, w_e, w_s, a_t, g_a, g_b, h0)


def _glu_blocks(glu_w):
    h, g, _ = glu_w.shape
    hb = h // S5_GB
    eye = jnp.eye(S5_GB, dtype=F32)

    def diag(wm):
        wm = wm.reshape(hb, S5_GB, g, g)
        return (eye[None, :, None, :, None] * wm[:, :, :, None, :]).reshape(hb, LANES, LANES).astype(BF16)

    return diag(glu_w[..., :g]), diag(glu_w[..., g:])


def _state_to_lanes(re, im):
    n, h, p = re.shape
    hb = h // S5_GB
    f = lambda v: v.reshape(n, hb, S5_GB * p).transpose(1, 0, 2)
    return jnp.concatenate([f(re), f(im)], axis=-1)


def _state_from_lanes(s):
    nl, hb, n, sw = s.shape
    half = sw // 2
    f = lambda v: v.transpose(0, 2, 1, 3).reshape(nl, n, hb * S5_GB, half // S5_GB)
    return f(s[..., :half]), f(s[..., half:])


def kernel(x_prompt, x_sample, cache_k, cache_v, state_ssm_re, state_ssm_im, p_prompt, p_sample,
           norm_mix, w_in, ssm_a_re, ssm_a_im, ssm_log_dt, ssm_b_re, ssm_b_im, ssm_c_re, ssm_c_im,
           ssm_d, ssm_glu_w, attn_sinks, rel_bias, norm_grp_ssm, norm_grp_attn, w_out, norm_ffn,
           ffn_w_gate, ffn_w_up, ffn_w_down, router_w, moe_w_gate, moe_w_up, moe_w_down,
           norm_ple, ple_w_gate, ple_w_proj, norm_final):
    nb_p, seq, d = x_prompt.shape
    nb_s, n_new, _ = x_sample.shape
    depth = w_in.shape[0]
    d_ssm = norm_grp_ssm.shape[-1]
    d_attn = norm_grp_attn.shape[-1]
    mp, ms = nb_p * seq, nb_s * n_new
    win = min(WINDOW, seq)
    assert d_ssm % LANES == 0 and seq % S5_T == 0 and n_new % S5_T == 0

    h = jnp.concatenate([x_prompt.reshape(mp, d), x_sample.reshape(ms, d)], axis=0)
    p_all = jnp.concatenate([p_prompt.reshape(depth, mp, -1), p_sample.reshape(depth, ms, -1)], axis=1)

    n_grp = ssm_a_re.shape[1]
    hb = n_grp // S5_GB
    flat = lambda v: v.reshape((depth * n_grp,) + v.shape[2:])
    tabs = _s5_tables(flat(ssm_a_re), flat(ssm_a_im), ssm_log_dt.reshape(-1), flat(ssm_b_re), flat(ssm_b_im),
                      flat(ssm_c_re), flat(ssm_c_im), ssm_d.reshape(-1))
    g_a, g_b = _glu_blocks(flat(ssm_glu_w))

    kv_p, kv_s, fin_ps, fin_ss = [], [], [], []
    for i in range(depth):
        xn = _rmsnorm(h, norm_mix[i], BF16)
        z = _mm(xn, w_in, i, tm=1024, tn=1280, vmem_mib=56)
        ssm_p, fin_p = _s5_prompt(z, nb_p, seq, tabs, g_a, g_b, i, hb)
        ssm_s, fin_s = _s5_sample(z, mp, nb_s, n_new, tabs, g_a, g_b,
                                  _state_to_lanes(state_ssm_re[i], state_ssm_im[i]), i, hb)
        att_p = _attn_prompt(z, nb_p, seq, d_ssm, d_attn, rel_bias, attn_sinks[i])
        att_s = _attn_sample(z, mp, nb_s, n_new, d_ssm, d_attn, cache_k, cache_v, i, rel_bias,
                             attn_sinks[i])
        merged = _merge_norm(ssm_p, ssm_s, att_p, att_s, norm_grp_ssm[i], norm_grp_attn[i])
        h = _mm(merged, w_out, i, tm=1024, tn=1024, res=h, vmem_mib=56)

        kv_p.append(jnp.stack([z[(b + 1) * seq - win:(b + 1) * seq, d_ssm + d_attn:] for b in range(nb_p)]))
        kv_s.append(z[mp:, d_ssm + d_attn:])
        fin_ps.append(fin_p.reshape(hb, nb_p, -1))
        fin_ss.append(fin_s)

        j = i // 2
        if i % 2 == 0:
            hn = _rmsnorm(h, norm_ffn[i], BF16)
            t = _swiglu(hn, ffn_w_gate, ffn_w_up, j, tm=1024, tn=512)
            h = _mm(t, ffn_w_down, j, tm=512, tn=512, res=h, vmem_mib=52)
        else:
            hn, e_idx, gates = _rmsnorm_router(h, norm_ffn[i], router_w[j])
            h = h + _moe_ffn(hn, e_idx, gates, moe_w_gate, moe_w_up, moe_w_down, j)

        hn = _rmsnorm(h, norm_ple[i], BF16)
        h = _ple(hn, p_all, ple_w_gate, ple_w_proj, h, i, tm=1024, tn=1024, vmem_mib=56)

    y_p, y_s = _rmsnorm_split(h, norm_final, mp)
    kv_p =jnp.stack(kv_p).reshape(depth, nb_p, win, 2, KV_HEADS, HEAD_DIM)
    kv_s = jnp.stack(kv_s).reshape(depth, nb_s, n_new, 2, KV_HEADS, HEAD_DIM)
    p_re, p_im = _state_from_lanes(jnp.stack(fin_ps))
    s_re, s_im = _state_from_lanes(jnp.stack(fin_ss))
    return (y_p.reshape(nb_p, seq, d), y_s.reshape(nb_s, n_new, d),
            kv_p[:, :, :, 0], kv_p[:, :, :, 1], p_re, p_im,
            kv_s[:, :, :, 0], kv_s[:, :, :, 1], s_re, s_im)
```

```python
import functools
import math

import numpy as np
import jax
import jax.numpy as jnp
from jax import lax
from jax.experimental import pallas as pl
from jax.experimental.pallas import tpu as pltpu

F32 = jnp.float32
BF16 = jnp.bfloat16
HIGHEST = lax.Precision.HIGHEST

CHUNK = 64
WINDOW = 128
HEAD_DIM = 64
KV_HEADS = 4
SSM_GROUP_CH = 16
SSM_STATE = 64
N_BUCKETS = 32
REL_MAX_DIST = 128
EPS = 1e-6
NEG_INF = -1e30
ATTN_SCALE = HEAD_DIM ** -0.5

LANES = 128
S5_T = 16
S5_GB = LANES // SSM_GROUP_CH
S5_F = S5_T * LANES
MOE_TM = 512
MOE_SUB = 256
MIB = 1024 * 1024


def _pick(n, pref, mult=8):
    best = None
    for d in range(mult, min(n, pref) + 1, mult):
        if n % d == 0:
            best = d
    return best if best is not None else n


def _params(sem, vmem_mib):
    return pltpu.CompilerParams(dimension_semantics=sem, vmem_limit_bytes=int(vmem_mib * MIB))


def _rms(x, g):
    return x * lax.rsqrt(jnp.mean(x * x, axis=-1, keepdims=True) + EPS) * g


def _norm_body(x_ref, g_ref, o_ref):
    o_ref[...] = _rms(x_ref[...], g_ref[...]).astype(o_ref.dtype)


def _rmsnorm(x, g, out_dtype):
    m, d = x.shape
    tm = _pick(m, 512)
    return pl.pallas_call(
        _norm_body,
        grid=(m // tm,),
        in_specs=[pl.BlockSpec((tm, d), lambda i: (i, 0)), pl.BlockSpec((1, d), lambda i: (0, 0))],
        out_specs=pl.BlockSpec((tm, d), lambda i: (i, 0)),
        out_shape=jax.ShapeDtypeStruct((m, d), out_dtype),
        compiler_params=_params(("arbitrary",), 40),
        name="rmsnorm",
    )(x, g.reshape(1, d))


def _norm_split_body(x_ref, g_ref, op_ref, os_ref, *, np_tiles):
    y = _rms(x_ref[...], g_ref[...])

    @pl.when(pl.program_id(0) < np_tiles)
    def _():
        op_ref[...] = y

    @pl.when(pl.program_id(0) >= np_tiles)
    def _():
        os_ref[...] = y


def _rmsnorm_split(x, g, mp):
    m, d = x.shape
    ms = m - mp
    tm = _pick(math.gcd(mp, ms), 512)
    npt = mp // tm
    return pl.pallas_call(
        functools.partial(_norm_split_body, np_tiles=npt),
        grid=(m // tm,),
        in_specs=[pl.BlockSpec((tm, d), lambda i: (i, 0)), pl.BlockSpec((1, d), lambda i: (0, 0))],
        out_specs=[pl.BlockSpec((tm, d), lambda i: (jnp.minimum(i, npt - 1), 0)),
                   pl.BlockSpec((tm, d), lambda i: (jnp.maximum(i - npt, 0), 0))],
        out_shape=[jax.ShapeDtypeStruct((mp, d), F32), jax.ShapeDtypeStruct((ms, d), F32)],
        compiler_params=_params(("arbitrary",), 40),
        name="rmsnorm_split",
    )(x, g.reshape(1, d))


def _norm_router_body(x_ref, g_ref, rw_ref, o_ref, idx_ref, gate_ref):
    y = _rms(x_ref[...], g_ref[...])
    tm, d = y.shape
    pieces = d // LANES
    for s in range(pieces):
        o_ref[pl.ds(s, tm, stride=pieces), :] = y[:, s * LANES:(s + 1) * LANES]
    n_exp = rw_ref.shape[0]
    logits = jnp.concatenate([jnp.sum(y * rw_ref[e:e + 1, :], axis=-1, keepdims=True) for e in range(n_exp)],
                             axis=-1)
    lane = lax.broadcasted_iota(jnp.int32, logits.shape, 1)
    m1 = jnp.max(logits, axis=-1, keepdims=True)
    i1 = jnp.min(jnp.where(logits == m1, lane, n_exp), axis=-1, keepdims=True)
    rest = jnp.where(lane == i1, -jnp.inf, logits)
    m2 = jnp.max(rest, axis=-1, keepdims=True)
    i2 = jnp.min(jnp.where(rest == m2, lane, n_exp), axis=-1, keepdims=True)
    e2 = jnp.exp(m2 - m1)
    den = 1.0 + e2
    idx_ref[...] = jnp.concatenate([i1, i2], axis=-1)
    gate_ref[...] = jnp.concatenate([1.0 / den, e2 / den], axis=-1)


def _rmsnorm_router(x, g, router_w):
    m, d = x.shape
    n_exp = router_w.shape[-1]
    tm = _pick(m, 512)
    return pl.pallas_call(
        _norm_router_body,
        grid=(m // tm,),
        in_specs=[pl.BlockSpec((tm, d), lambda i: (i, 0)),
                  pl.BlockSpec((1, d), lambda i: (0, 0)),
                  pl.BlockSpec((n_exp, d), lambda i: (0, 0))],
        out_specs=[pl.BlockSpec((tm * (d // LANES), LANES), lambda i: (i, 0)),
                   pl.BlockSpec((tm, 2), lambda i: (i, 0)),
                   pl.BlockSpec((tm, 2), lambda i: (i, 0))],
        out_shape=[jax.ShapeDtypeStruct((m * (d // LANES), LANES), F32),
                   jax.ShapeDtypeStruct((m, 2), jnp.int32),
                   jax.ShapeDtypeStruct((m, 2), F32)],
        compiler_params=_params(("arbitrary",), 40),
        name="rmsnorm_router",
    )(x, g.reshape(1, d), router_w.T)


def _merge_norm_body(ap_ref, as_ref, bp_ref, bs_ref, ga_ref, gb_ref, o_ref, *, np_tiles):
    da = ap_ref.shape[-1]

    def emit(a_ref, b_ref):
        o_ref[:, :da] = _rms(a_ref[...], ga_ref[...]).astype(o_ref.dtype)
        o_ref[:, da:] = _rms(b_ref[...], gb_ref[...]).astype(o_ref.dtype)

    @pl.when(pl.program_id(0) < np_tiles)
    def _():
        emit(ap_ref, bp_ref)

    @pl.when(pl.program_id(0) >= np_tiles)
    def _():
        emit(as_ref, bs_ref)


def _merge_norm(a_p, a_s, b_p, b_s, ga, gb):
    (mp, da), (ms, db) = a_p.shape, b_s.shape
    tm = _pick(math.gcd(mp, ms), 512)
    npt, nst = mp // tm, ms // tm
    pspec = lambda w: pl.BlockSpec((tm, w), lambda i: (jnp.minimum(i, npt - 1), 0))
    sspec = lambda w: pl.BlockSpec((tm, w), lambda i: (jnp.maximum(i - npt, 0), 0))
    return pl.pallas_call(
        functools.partial(_merge_norm_body, np_tiles=npt),
        grid=(npt + nst,),
        in_specs=[pspec(da), sspec(da), pspec(db), sspec(db),
                  pl.BlockSpec((1, da), lambda i: (0, 0)), pl.BlockSpec((1, db), lambda i: (0, 0))],
        out_specs=pl.BlockSpec((tm, da + db), lambda i: (i, 0)),
        out_shape=jax.ShapeDtypeStruct((mp + ms, da + db), BF16),
        compiler_params=_params(("arbitrary",), 40),
        name="merge_norm",
    )(a_p, a_s, b_p, b_s, ga.reshape(1, da), gb.reshape(1, db))


def _mm_body(*refs, has_res):
    if has_res:
        x_ref, w_ref, r_ref, o_ref, wb_ref = refs
    else:
        x_ref, w_ref, o_ref, wb_ref = refs

    @pl.when(pl.program_id(1) == 0)
    def _():
        wb_ref[...] = w_ref[...].astype(BF16)

    acc = jnp.dot(x_ref[...], wb_ref[...], preferred_element_type=F32)
    if has_res:
        acc = r_ref[...] + acc
    o_ref[...] = acc.astype(o_ref.dtype)


def _mm(x, w, lyr, *, tm, tn, out_dtype=F32, res=None, vmem_mib=48):
    m, k = x.shape
    n = w.shape[-1]
    tm, tn = _pick(m, tm), _pick(n, tn, 128)
    in_specs = [pl.BlockSpec((tm, k), lambda j, i: (i, 0)),
                pl.BlockSpec((None, k, tn), lambda j, i: (lyr, 0, j))]
    args = [x, w]
    if res is not None:
        in_specs.append(pl.BlockSpec((tm, tn), lambda j, i: (i, j)))
        args.append(res)
    return pl.pallas_call(
        functools.partial(_mm_body, has_res=res is not None),
        grid=(n // tn, m // tm),
        in_specs=in_specs,
        out_specs=pl.BlockSpec((tm, tn), lambda j, i: (i, j)),
        out_shape=jax.ShapeDtypeStruct((m, n), out_dtype),
        scratch_shapes=[pltpu.VMEM((k, tn), BF16)],
        compiler_params=_params(("arbitrary", "arbitrary"), vmem_mib),
        name="mm_res" if res is not None else "mm",
    )(*args)


def _mm_rows_body(*refs, has_norm):
    if has_norm:
        x_ref, w_ref, r_ref, g_ref, o_ref, n_ref, wb_ref = refs
    else:
        x_ref, w_ref, r_ref, o_ref, wb_ref = refs

    @pl.when(pl.program_id(0) == 0)
    def _():
        wb_ref[...] = w_ref[...].astype(BF16)

    acc = r_ref[...] + jnp.dot(x_ref[...], wb_ref[...], preferred_element_type=F32)
    o_ref[...] = acc
    if has_norm:
        n_ref[...] = _rms(acc, g_ref[...]).astype(n_ref.dtype)


def _mm_rows(x, w, lyr, res, *, tm, gain=None, vmem_mib=56):
    m, k = x.shape
    n = w.shape[-1]
    tm = _pick(m, tm)
    row = lambda width: pl.BlockSpec((tm, width), lambda i: (i, 0))
    in_specs = [row(k), pl.BlockSpec((None, k, n), lambda i: (lyr, 0, 0), pipeline_mode=pl.Buffered(1)), row(n)]
    args = [x, w, res]
    out_specs, out_shape = [row(n)], [jax.ShapeDtypeStruct((m, n), F32)]
    if gain is not None:
        in_specs.append(pl.BlockSpec((1, n), lambda i: (0, 0)))
        args.append(gain.reshape(1, n))
        out_specs.append(row(n))
        out_shape.append(jax.ShapeDtypeStruct((m, n), BF16))
    out = pl.pallas_call(
        functools.partial(_mm_rows_body, has_norm=gain is not None),
        grid=(m // tm,),
        in_specs=in_specs,
        out_specs=out_specs,
        out_shape=out_shape,
        scratch_shapes=[pltpu.VMEM((k, n), BF16)],
        compiler_params=_params(("arbitrary",), vmem_mib),
        name="mm_rows",
    )(*args)
    return out if gain is not None else out[0]


def _swiglu_math(x, wg, wu):
    a = jnp.dot(x, wg, preferred_element_type=F32)
    b = jnp.dot(x, wu, preferred_element_type=F32)
    return (a * jax.nn.sigmoid(a) * b).astype(BF16)


def _swiglu_body(x_ref, wg_ref, wu_ref, o_ref, wgb_ref, wub_ref):
    @pl.when(pl.program_id(1) == 0)
    def _():
        wgb_ref[...] = wg_ref[...].astype(BF16)
        wub_ref[...] = wu_ref[...].astype(BF16)

    o_ref[...] = _swiglu_math(x_ref[...], wgb_ref[...], wub_ref[...])


def _swiglu(x, wg, wu, lyr, *, tm, tn, vmem_mib=48):
    m, k = x.shape
    n = wg.shape[-1]
    tm, tn = _pick(m, tm), _pick(n, tn, 128)
    wspec = pl.BlockSpec((None, k, tn), lambda j, i: (lyr, 0, j))
    return pl.pallas_call(
        _swiglu_body,
        grid=(n // tn, m // tm),
        in_specs=[pl.BlockSpec((tm, k), lambda j, i: (i, 0)), wspec, wspec],
        out_specs=pl.BlockSpec((tm, tn), lambda j, i: (i, j)),
        out_shape=jax.ShapeDtypeStruct((m, n), BF16),
        scratch_shapes=[pltpu.VMEM((k, tn), BF16), pltpu.VMEM((k, tn), BF16)],
        compiler_params=_params(("arbitrary", "arbitrary"), vmem_mib),
        name="swiglu",
    )(x, wg, wu)


def _ple_body(x_ref, pp_ref, ps_ref, wg_ref, wp_ref, r_ref, o_ref, wgb_ref, wpb_ref, pb_ref, *, np_tiles):
    i = pl.program_id(1)

    @pl.when(i == 0)
    def _():
        wgb_ref[...] = wg_ref[...].astype(BF16)
        wpb_ref[...] = wp_ref[...].astype(BF16)

    @pl.when(i < np_tiles)
    def _():
        pb_ref[...] = pp_ref[...].astype(BF16)

    @pl.when(i >= np_tiles)
    def _():
        pb_ref[...] = ps_ref[...].astype(BF16)

    a = jnp.dot(x_ref[...], wgb_ref[...], preferred_element_type=F32)
    b = jnp.dot(pb_ref[...], wpb_ref[...], preferred_element_type=F32)
    o_ref[...] = r_ref[...] + jax.nn.sigmoid(a) * b


def _ple(x, p_p, p_s, wg, wp, res, lyr, *, tm, tn, vmem_mib=48):
    m, k = x.shape
    mp, ms, kp = p_p.shape[1], p_s.shape[1], p_p.shape[-1]
    n = wg.shape[-1]
    tm, tn = _pick(math.gcd(mp, ms), tm), _pick(n, tn, 128)
    npt = mp // tm
    return pl.pallas_call(
        functools.partial(_ple_body, np_tiles=npt),
        grid=(n // tn, m // tm),
        in_specs=[pl.BlockSpec((tm, k), lambda j, i: (i, 0)),
                  pl.BlockSpec((None, tm, kp), lambda j, i: (lyr, jnp.minimum(i, npt - 1), 0)),
                  pl.BlockSpec((None, tm, kp), lambda j, i: (lyr, jnp.maximum(i - npt, 0), 0)),
                  pl.BlockSpec((None, k, tn), lambda j, i: (lyr, 0, j)),
                  pl.BlockSpec((None, kp, tn), lambda j, i: (lyr, 0, j)),
                  pl.BlockSpec((tm, tn), lambda j, i: (i, j))],
        out_specs=pl.BlockSpec((tm, tn), lambda j, i: (i, j)),
        out_shape=jax.ShapeDtypeStruct((m, n), F32),
        scratch_shapes=[pltpu.VMEM((k, tn), BF16), pltpu.VMEM((kp, tn), BF16), pltpu.VMEM((tm, kp), BF16)],
        compiler_params=_params(("arbitrary", "arbitrary"), vmem_mib),
        name="ple",
    )(x, p_p, p_s, wg, wp, res)


def _moe_weights(w_hbms, stages, bf16s, sem, e_ref, start_ref, nxt_ref, lyr):
    j, i = pl.program_id(0), pl.program_id(1)
    tn = stages[0].shape[1]

    def copies(e, col):
        col = pl.multiple_of(col, LANES)
        return [pltpu.make_async_copy(w.at[lyr, e, :, pl.ds(col, tn)], st, sem.at[n])
                for n, (w, st) in enumerate(zip(w_hbms, stages))]

    @pl.when(jnp.logical_and(j == 0, i == 0))
    def _():
        for cp in copies(e_ref[0], 0):
            cp.start()

    @pl.when(start_ref[i] == 1)
    def _():
        for cp in copies(e_ref[i], j * tn):
            cp.wait()
        rows = 128

        def cast_chunk(c, carry):
            r = pl.multiple_of(c * rows, rows)
            for st, wb in zip(stages, bf16s):
                wb[pl.ds(r, rows), :] = st[pl.ds(r, rows), :].astype(BF16)
            return carry
        lax.fori_loop(0, stages[0].shape[0] // rows, cast_chunk, 0)
        nxt = nxt_ref[i]

        @pl.when(nxt >= 0)
        def _():
            for cp in copies(nxt, j * tn):
                cp.start()

        @pl.when(jnp.logical_and(nxt < 0, j + 1 < pl.num_programs(0)))
        def _():
            for cp in copies(e_ref[0], (j + 1) * tn):
                cp.start()


def _active_prefix(act_ref, i, per):
    n_act = act_ref[i * per]
    for s in range(1, per):
        n_act = n_act + act_ref[i * per + s]
    return n_act


def _prefix_rows(o_ref, n_act, fn):
    tm = o_ref.shape[0]
    for c in range(tm // MOE_SUB + 1):
        rows = c * MOE_SUB

        @pl.when(n_act == c)
        def _():
            if rows:
                o_ref[:rows, :] = fn(rows)
            if rows < tm:
                o_ref[rows:, :] = jnp.zeros((tm - rows, o_ref.shape[1]), o_ref.dtype)


def _moe_up_body(e_ref, act_ref, start_ref, nxt_ref, x_ref, wg_hbm, wu_hbm, o_ref,
                 wgb_ref, wub_ref, sg_ref, su_ref, sem, *, lyr):
    _moe_weights((wg_hbm, wu_hbm), (sg_ref, su_ref), (wgb_ref, wub_ref), sem, e_ref, start_ref, nxt_ref, lyr)
    n_act = _active_prefix(act_ref, pl.program_id(1), x_ref.shape[0] // MOE_SUB)
    _prefix_rows(o_ref, n_act, lambda rows: _swiglu_math(x_ref[:rows, :], wgb_ref[...], wub_ref[...]))


def _moe_down_body(e_ref, act_ref, start_ref, nxt_ref, h_ref, wd_hbm, o_ref, wdb_ref, sd_ref, sem, *, lyr):
    _moe_weights((wd_hbm,), (sd_ref,), (wdb_ref,), sem, e_ref, start_ref, nxt_ref, lyr)
    n_act = _active_prefix(act_ref, pl.program_id(1), h_ref.shape[0] // MOE_SUB)
    _prefix_rows(o_ref, n_act,
                 lambda rows: jnp.dot(h_ref[:rows, :], wdb_ref[...], preferred_element_type=F32))


def _moe_matmul(body, x, ws, sched, lyr, *, tn, out_dtype, vmem_mib, name):
    ns, k = x.shape
    n = ws[0].shape[-1]
    tm, tn = MOE_TM, _pick(n, tn, 128)
    grid_spec = pltpu.PrefetchScalarGridSpec(
        num_scalar_prefetch=len(sched),
        grid=(n // tn, ns // tm),
        in_specs=[pl.BlockSpec((tm, k), lambda j, i, *_: (i, 0))]
                 + [pl.BlockSpec(memory_space=pl.ANY)] * len(ws),
        out_specs=pl.BlockSpec((tm, tn), lambda j, i, *_: (i, j)),
        scratch_shapes=[pltpu.VMEM((k, tn), BF16)] * len(ws) + [pltpu.VMEM((k, tn), F32)] * len(ws)
                       + [pltpu.SemaphoreType.DMA((len(ws),))],
    )
    return pl.pallas_call(
        functools.partial(body, lyr=lyr),
        grid_spec=grid_spec,
        out_shape=jax.ShapeDtypeStruct((ns, n), out_dtype),
        compiler_params=_params(("arbitrary", "arbitrary"), vmem_mib),
        name=name,
    )(*sched, x, *ws)


def _row_gather_copy(x_hbm, buf, sem, slot, src_row, dst_row, rows):
    return pltpu.make_async_copy(x_hbm.at[pl.ds(src_row, rows)], buf.at[slot, pl.ds(dst_row, rows)],
                                 sem.at[slot])


def _moe_gather_body(tok_ref, x_hbm, o_ref, buf, sem):
    i = pl.program_id(0)
    tm = o_ref.shape[0]
    pieces = o_ref.shape[1] // LANES

    def issue(blk, slot):
        def body(r2, carry):
            for pri in range(2):
                r = r2 * 2 + pri
                _row_gather_copy(x_hbm, buf, sem, slot, tok_ref[blk * tm + r] * pieces, r * pieces,
                                 pieces).start(priority=pri)
            return carry
        lax.fori_loop(0, tm // 2, body, 0, unroll=4)

    @pl.when(i == 0)
    def _():
        issue(0, 0)

    @pl.when(i + 1 < pl.num_programs(0))
    def _():
        issue(i + 1, (i + 1) % 2)

    slot = i % 2
    _row_gather_copy(x_hbm, buf, sem, slot, 0, 0, tm * pieces).wait()
    cur = buf.at[slot]
    o_ref[...] = jnp.concatenate([cur[pl.ds(s, tm, stride=pieces), :] for s in range(pieces)],
                                 axis=-1).astype(o_ref.dtype)


def _moe_gather(x_rows, slot_tok, d):
    n_slots = slot_tok.shape[0]
    tm = MOE_TM
    grid_spec = pltpu.PrefetchScalarGridSpec(
        num_scalar_prefetch=1,
        grid=(n_slots // tm,),
        in_specs=[pl.BlockSpec(memory_space=pl.ANY)],
        out_specs=pl.BlockSpec((tm, d), lambda i, tok: (i, 0)),
        scratch_shapes=[pltpu.VMEM((2, tm * (d // LANES), LANES), F32), pltpu.SemaphoreType.DMA((2,))],
    )
    return pl.pallas_call(
        _moe_gather_body,
        grid_spec=grid_spec,
        out_shape=jax.ShapeDtypeStruct((n_slots, d), BF16),
        compiler_params=_params(("arbitrary",), 32),
        name="moe_gather",
    )(slot_tok, x_rows)


def _moe_ffn(hn_rows, e_idx, gates, wg, wu, wd, lyr):
    m = e_idx.shape[0]
    d = wg.shape[2]
    n_exp = wg.shape[1]
    top_k = e_idx.shape[-1]
    tm = MOE_TM
    n_asg = m * top_k
    e_flat = e_idx.reshape(n_asg)
    onehot = (e_flat[None, :] == jnp.arange(n_exp, dtype=jnp.int32)[:, None]).astype(jnp.int32)
    counts = jnp.sum(onehot, axis=1)
    start = jnp.cumsum(counts) - counts
    padded = (counts + tm - 1) // tm * tm
    pad_end = jnp.cumsum(padded)
    pad_start = pad_end - padded
    order = jnp.argsort(e_flat).astype(jnp.int32)
    rank = jnp.argsort(order).astype(jnp.int32)
    slot_of = rank + jnp.sum(onehot * (pad_start - start)[:, None], axis=0).astype(jnp.int32)
    n_blocks = n_asg // tm + n_exp
    n_slots = n_blocks * tm
    n_used = (pad_end[-1] // tm).astype(jnp.int32)
    blk = jnp.arange(n_blocks, dtype=jnp.int32)
    blk_start = jnp.minimum(blk, n_used - 1) * tm
    blk_expert = jnp.sum((pad_end[None, :] <= blk_start[:, None]).astype(jnp.int32), axis=1)
    blk_expert = jnp.minimum(blk_expert, n_exp - 1).astype(jnp.int32)
    sub_start = jnp.arange(n_slots // MOE_SUB, dtype=jnp.int32) * MOE_SUB
    sub_expert = jnp.repeat(blk_expert, tm // MOE_SUB)
    real_end = (pad_start + counts)[sub_expert]
    sub_active = jnp.logical_and(sub_start < real_end, sub_start < pad_end[-1]).astype(jnp.int32)
    slot_r = jnp.arange(n_slots, dtype=jnp.int32) - jnp.repeat(pad_start[sub_expert], MOE_SUB)
    slot_src = jnp.repeat(start[sub_expert], MOE_SUB) + slot_r
    slot_real = jnp.logical_and(slot_r >= 0, slot_r < jnp.repeat(counts[sub_expert], MOE_SUB))
    slot_tok = jnp.where(slot_real, order[jnp.clip(slot_src, 0, n_asg - 1)] // top_k, 0).astype(jnp.int32)

    run_start = jnp.concatenate([jnp.ones((1,), jnp.int32),
                                 (blk_expert[1:] != blk_expert[:-1]).astype(jnp.int32)])
    later_start = jnp.where(run_start == 1, blk, n_blocks)
    nxt_blk = lax.cummin(jnp.concatenate([later_start[1:], jnp.full((1,), n_blocks, jnp.int32)]),
                         axis=0, reverse=True)
    run_next = jnp.where(nxt_blk < n_blocks, blk_expert[jnp.minimum(nxt_blk, n_blocks - 1)], -1)
    sched = (blk_expert, sub_active, run_start, run_next.astype(jnp.int32))

    x_sorted = _moe_gather(hn_rows, slot_tok, d)
    h_sorted = _moe_matmul(_moe_up_body, x_sorted, (wg, wu), sched, lyr, tn=1024, out_dtype=BF16,
                           vmem_mib=48, name="moe_up")
    y_sorted = _moe_matmul(_moe_down_body, h_sorted, (wd,), sched, lyr, tn=512, out_dtype=F32,
                           vmem_mib=48, name="moe_down")
    sl = slot_of.reshape(m, top_k)
    y = y_sorted[sl[:, 0]] * gates[:, 0:1]
    for t in range(1, top_k):
        y = y + y_sorted[sl[:, t]] * gates[:, t:t + 1]
    return y


def _rel_buckets(n_q, n_k_past, n_k):
    rel = (np.arange(n_k) - n_k_past)[None, :] - np.arange(n_q)[:, None]
    half = N_BUCKETS // 2
    max_exact = half // 2
    n = np.abs(rel)
    nf = np.maximum(n, 1).astype(np.float32)
    large = max_exact + (np.log(nf / np.float32(max_exact)) / np.float32(math.log(REL_MAX_DIST / max_exact))
                         * np.float32(half - max_exact)).astype(np.int32)
    large = np.minimum(large, half - 1)
    return (np.where(rel > 0, half, 0) + np.where(n < max_exact, n, large)).astype(np.int32)


def _build_bias(bias_ref, sink_ref, bucket_ref, table_ref, sinks_ref):
    bucket = bucket_ref[...]
    nq = bucket.shape[1]
    gq = bias_ref.shape[2] // nq
    for kh in range(bias_ref.shape[0]):
        for g in range(gq):
            n = kh * gq + g
            acc = jnp.zeros(bucket.shape, F32)
            for b in range(N_BUCKETS):
                acc = jnp.where(bucket == b, table_ref[b, n], acc)
            bias_ref[kh, :, g * nq:(g + 1) * nq] = acc
            sink_ref[kh, :, g * nq:(g + 1) * nq] = jnp.full((1, nq), sinks_ref[n], F32)


def _attn_items(items, bias_ref, sink_ref):
    nq = items[0][0].shape[0]
    gq = items[0][0].shape[-1] // HEAD_DIM // KV_HEADS
    head = lambda x, n: x[:, n * HEAD_DIM:(n + 1) * HEAD_DIM]
    scores = []
    for q, k, _, valid in items:
        for kh in range(KV_HEADS):
            q4 = jnp.concatenate([head(q, kh * gq + g) for g in range(gq)], axis=0)
            s = lax.dot_general(head(k, kh), q4, (((1,), (1,)), ((), ())), preferred_element_type=F32)
            scores.append((s, kh, valid))
    probs = []
    for s, kh, valid in scores:
        s = s + bias_ref[kh]
        if valid is not None:
            s = jnp.where(valid, s, NEG_INF)
        sink = sink_ref[kh]
        mx = jnp.maximum(jnp.max(s, axis=0, keepdims=True), sink)
        e = jnp.exp(s - mx)
        den = jnp.sum(e, axis=0, keepdims=True) + jnp.exp(sink - mx)
        probs.append((e * (1.0 / den)).astype(BF16))
    outs = []
    for i, (_, _, v, _) in enumerate(items):
        heads = []
        for kh in range(KV_HEADS):
            o4 = lax.dot_general(probs[i * KV_HEADS + kh], head(v, kh), (((0,), (0,)), ((), ())),
                                 preferred_element_type=F32)
            heads.extend(o4[g * nq:(g + 1) * nq] for g in range(gq))
        outs.append(jnp.concatenate(heads, axis=-1))
    return outs


def _attn_prompt_body(q_ref, kp_ref, kc_ref, vp_ref, vc_ref, bucket_ref, table_ref, sinks_ref,
                      o_ref, bias_ref, sink_ref, *, chunks):
    j = pl.program_id(1)

    @pl.when(jnp.logical_and(pl.program_id(0) == 0, j == 0))
    def _():
        _build_bias(bias_ref, sink_ref, bucket_ref, table_ref, sinks_ref)

    q = (q_ref[...] * ATTN_SCALE).astype(BF16)
    k = jnp.concatenate([kp_ref[...], kc_ref[...]], axis=0).astype(BF16)
    v = jnp.concatenate([vp_ref[...], vc_ref[...]], axis=0).astype(BF16)
    band = WINDOW + CHUNK
    items = []
    for c in range(chunks):
        lo = c * CHUNK
        key_valid = None
        if lo < WINDOW:
            key_pos = j * (chunks * CHUNK) + lo - WINDOW + lax.broadcasted_iota(jnp.int32, (band, 1), 0)
            key_valid = key_pos >= 0
        items.append((q[lo:lo + CHUNK], k[lo:lo + band], v[lo:lo + band], key_valid))
    for c, o in enumerate(_attn_items(items, bias_ref, sink_ref)):
        o_ref[c * CHUNK:(c + 1) * CHUNK, :] = o


def _attn_prompt(z, n_b, seq, d_ssm, d_attn, table, sinks):
    d_kv = KV_HEADS * HEAD_DIM
    assert d_ssm % d_attn == 0 and d_ssm % d_kv == 0 and d_attn % d_kv == 0
    qb = _pick(seq, 4 * CHUNK, 2 * CHUNK)
    assert qb % WINDOW == 0 and seq % qb == 0
    chunks = qb // CHUNK
    nj = seq // qb
    r = qb // WINDOW
    kcol = (d_ssm + d_attn) // d_kv
    gq = d_attn // HEAD_DIM // KV_HEADS
    band = WINDOW + CHUNK
    bucket = jnp.asarray(_rel_buckets(CHUNK, WINDOW, band).T)

    def prev(b, j):
        return b * (seq // WINDOW) + jnp.maximum(j * r - 1, 0)

    return pl.pallas_call(
        functools.partial(_attn_prompt_body, chunks=chunks),
        grid=(n_b, nj),
        in_specs=[pl.BlockSpec((qb, d_attn), lambda b, j: (b * nj + j, d_ssm // d_attn)),
                  pl.BlockSpec((WINDOW, d_kv), lambda b, j: (prev(b, j), kcol)),
                  pl.BlockSpec((qb, d_kv), lambda b, j: (b * nj + j, kcol)),
                  pl.BlockSpec((WINDOW, d_kv), lambda b, j: (prev(b, j), kcol + 1)),
                  pl.BlockSpec((qb, d_kv), lambda b, j: (b * nj + j, kcol + 1)),
                  pl.BlockSpec((band, CHUNK), lambda b, j: (0, 0)),
                  pl.BlockSpec(memory_space=pltpu.SMEM),
                  pl.BlockSpec(memory_space=pltpu.SMEM)],
        out_specs=pl.BlockSpec((qb, d_attn), lambda b, j: (b * nj + j, 0)),
        out_shape=jax.ShapeDtypeStruct((n_b * seq, d_attn), F32),
        scratch_shapes=[pltpu.VMEM((KV_HEADS, band, gq * CHUNK), F32),
                        pltpu.VMEM((KV_HEADS, 1, gq * CHUNK), F32)],
        compiler_params=_params(("arbitrary", "arbitrary"), 40),
        name="attn_prompt",
    )(z, z, z, z, z, bucket, table, sinks)


def _attn_sample_body(q_ref, kc_ref, kn_ref, vc_ref, vn_ref, bucket_ref, table_ref, sinks_ref,
                      o_ref, bias_ref, sink_ref, *, n_bs, n_new, n_past):
    @pl.when(pl.program_id(0) == 0)
    def _():
        _build_bias(bias_ref, sink_ref, bucket_ref, table_ref, sinks_ref)

    q = (q_ref[...] * ATTN_SCALE).astype(BF16)
    kc, kn = kc_ref[...].astype(BF16), kn_ref[...].astype(BF16)
    vc, vn = vc_ref[...].astype(BF16), vn_ref[...].astype(BF16)
    items = []
    for b in range(n_bs):
        new = slice(b * n_new, (b + 1) * n_new)
        past = slice(b * n_past, (b + 1) * n_past)
        items.append((q[new], jnp.concatenate([kc[past], kn[new]], axis=0),
                      jnp.concatenate([vc[past], vn[new]], axis=0), None))
    for b, o in enumerate(_attn_items(items, bias_ref, sink_ref)):
        o_ref[b * n_new:(b + 1) * n_new, :] = o


def _attn_sample(z, row0, n_b, n_new, d_ssm, d_attn, cache_k, cache_v, lyr, table, sinks):
    d_kv = KV_HEADS * HEAD_DIM
    n_past = cache_k.shape[2]
    n_bs = _pick(n_b, 4, 1)
    rows = n_bs * n_new
    assert row0 % rows == 0
    kcol = (d_ssm + d_attn) // d_kv
    gq = d_attn // HEAD_DIM // KV_HEADS
    n_k = n_past + n_new
    bucket = jnp.asarray(_rel_buckets(n_new, n_past, n_k).T)
    ck = cache_k.reshape(-1, d_kv)
    cv = cache_v.reshape(-1, d_kv)
    rb = row0 // rows
    nsteps = n_b // n_bs
    cache_spec = pl.BlockSpec((n_bs * n_past, d_kv), lambda b: (lyr * nsteps + b, 0))
    return pl.pallas_call(
        functools.partial(_attn_sample_body, n_bs=n_bs, n_new=n_new, n_past=n_past),
        grid=(nsteps,),
        in_specs=[pl.BlockSpec((rows, d_attn), lambda b: (rb + b, d_ssm // d_attn)),
                  cache_spec,
                  pl.BlockSpec((rows, d_kv), lambda b: (rb + b, kcol)),
                  cache_spec,
                  pl.BlockSpec((rows, d_kv), lambda b: (rb + b, kcol + 1)),
                  pl.BlockSpec((n_k, n_new), lambda b: (0, 0)),
                  pl.BlockSpec(memory_space=pltpu.SMEM),
                  pl.BlockSpec(memory_space=pltpu.SMEM)],
        out_specs=pl.BlockSpec((rows, d_attn), lambda b: (b, 0)),
        out_shape=jax.ShapeDtypeStruct((n_b * n_new, d_attn), F32),
        scratch_shapes=[pltpu.VMEM((KV_HEADS, n_k, gq * n_new), F32),
                        pltpu.VMEM((KV_HEADS, 1, gq * n_new), F32)],
        compiler_params=_params(("arbitrary",), 40),
        name="attn_sample",
    )(z, ck, z, cv, z, bucket, table, sinks)


def _tile_lanes(x, reps):
    return jnp.concatenate([x] * reps, axis=-1)


def _s5_tables_body(are_ref, aim_ref, ldt_ref, btr_ref, bti_ref, cre_ref, cim_ref, d_ref,
                    dk_ref, we_ref, ws_ref, a_ref):
    rows, p = cre_ref.shape
    gb = rows // SSM_GROUP_CH
    a_re, a_im = are_ref[...], aim_ref[...]
    dt = jnp.exp(ldt_ref[...])
    mag = jnp.exp(a_re * dt)
    ab_re, ab_im = mag * jnp.cos(a_im * dt), mag * jnp.sin(a_im * dt)

    x, y = ab_re - 1.0, ab_im
    den = a_re * a_re + a_im * a_im
    co_re = (x * a_re + y * a_im) / den
    co_im = (y * a_re - x * a_im) / den
    bb_re = co_re * btr_ref[...] - co_im * bti_ref[...]
    bb_im = co_re * bti_ref[...] + co_im * btr_ref[...]
    c_re, c_im = cre_ref[...], cim_ref[...]

    r_idx = lax.broadcasted_iota(jnp.int32, (rows, 1), 0)
    own_state = (r_idx // SSM_GROUP_CH) == (lax.broadcasted_iota(jnp.int32, (1, gb * p), 1) // p)
    same_group = (r_idx // SSM_GROUP_CH) == (lax.broadcasted_iota(jnp.int32, (1, rows), 1) // SSM_GROUP_CH)
    diag = r_idx == lax.broadcasted_iota(jnp.int32, (1, rows), 1)

    def place(v_re, v_im):
        return jnp.concatenate([jnp.where(own_state, _tile_lanes(v_re, gb), 0.0),
                                jnp.where(own_state, _tile_lanes(v_im, gb), 0.0)], axis=-1)

    dims = (((1,), (1,)), ((), ()))
    bb_cat = jnp.concatenate([bb_re, -bb_im], axis=-1)
    pw_re, pw_im = jnp.ones_like(ab_re), jnp.zeros_like(ab_re)
    for j in range(S5_T + 1):
        ca_re = c_re * pw_re - c_im * pw_im
        ca_im = c_re * pw_im + c_im * pw_re
        if j < S5_T:
            kj = lax.dot_general(bb_cat, jnp.concatenate([ca_re, ca_im], axis=-1), dims,
                                 precision=HIGHEST, preferred_element_type=F32)
            kj = jnp.where(same_group, kj, 0.0)
            if j == 0:
                kj = kj + jnp.where(diag, d_ref[...], 0.0)
            dk_ref[j] = kj.astype(dk_ref.dtype)
            s = S5_T - 1 - j
            we_ref[s * rows:(s + 1) * rows, :] = place(
                pw_re * bb_re - pw_im * bb_im, pw_re * bb_im + pw_im * bb_re).astype(we_ref.dtype)
        if j >= 1:
            ws_ref[(j - 1) * rows:j * rows, :] = place(ca_re, -ca_im).astype(ws_ref.dtype)
        if j == S5_T:
            first = r_idx == (lax.broadcasted_iota(jnp.int32, (1, gb * p), 1) // p) * SSM_GROUP_CH
            a_ref[...] = jnp.concatenate(
                [jnp.sum(jnp.where(first, _tile_lanes(pw_re, gb), 0.0), axis=0, keepdims=True),
                 jnp.sum(jnp.where(first, _tile_lanes(pw_im, gb), 0.0), axis=0, keepdims=True)], axis=-1)
        pw_re, pw_im = pw_re * ab_re - pw_im * ab_im, pw_re * ab_im + pw_im * ab_re


def _s5_tables(a_re, a_im, log_dt, b_re, b_im, c_re, c_im, d_skip):
    h, p = a_re.shape
    g = SSM_GROUP_CH
    hb = h // S5_GB
    sw = 2 * S5_GB * p
    rep = lambda v: jnp.repeat(v, g, axis=0).reshape(hb, LANES, -1)
    blk = lambda w: pl.BlockSpec((None, LANES, w), lambda i: (i, 0, 0))
    wide = pl.BlockSpec((None, S5_F, sw), lambda i: (i, 0, 0))
    return pl.pallas_call(
        _s5_tables_body,
        grid=(hb,),
        in_specs=[blk(p), blk(p), blk(1), blk(p), blk(p), blk(p), blk(p), blk(1)],
        out_specs=[pl.BlockSpec((None, S5_T, LANES, LANES), lambda i: (i, 0, 0, 0)), wide, wide,
                   pl.BlockSpec((None, 1, sw), lambda i: (i, 0, 0))],
        out_shape=[jax.ShapeDtypeStruct((hb, S5_T, LANES, LANES), BF16),
                   jax.ShapeDtypeStruct((hb, S5_F, sw), BF16),
                   jax.ShapeDtypeStruct((hb, S5_F, sw), BF16),
                   jax.ShapeDtypeStruct((hb, 1, sw), F32)],
        compiler_params=_params(("arbitrary",), 40),
        name="s5_tables",
    )(rep(a_re), rep(a_im), rep(log_dt.reshape(h, 1)),
      jnp.swapaxes(b_re, 1, 2).reshape(hb, LANES, p), jnp.swapaxes(b_im, 1, 2).reshape(hb, LANES, p),
      c_re.reshape(hb, LANES, p), c_im.reshape(hb, LANES, p), d_skip.reshape(hb, LANES, 1))


def _split_mul(a, p):
    lane = lax.broadcasted_iota(jnp.int32, a.shape, a.ndim - 1)
    sw = pltpu.roll(a, p, axis=a.ndim - 1)
    return jnp.where(lane < p, a, sw), jnp.where(lane < p, -sw, a)


def _cmul(x, mr, mi, p):
    return x * mr + pltpu.roll(x, p, axis=x.ndim - 1) * mi


def _build_toeplitz(m_ref, dk_ref):
    m_ref[...] = jnp.zeros_like(m_ref)
    for s in range(S5_T):
        for t in range(s, S5_T):
            m_ref[s * LANES:(s + 1) * LANES, t * LANES:(t + 1) * LANES] = dk_ref[t - s]


def _s5_outputs(ucat, enter, m_ref, ws_ref):
    return (jnp.dot(ucat, m_ref[...], preferred_element_type=F32)
            + lax.dot_general(enter.astype(BF16), ws_ref[...], (((1,), (1,)), ((), ())),
                              preferred_element_type=F32))


def _glu_rows(o_ref, ga_ref, gb_ref, tile):
    ga, gb = ga_ref[...], gb_ref[...]
    tile = _pick(o_ref.shape[0], tile)
    for r0 in range(0, o_ref.shape[0], tile):
        y = o_ref[r0:r0 + tile, :].astype(BF16)
        o_ref[r0:r0 + tile, :] = (jnp.dot(y, ga, preferred_element_type=F32)
                                  * jax.nn.sigmoid(jnp.dot(y, gb, preferred_element_type=F32)))


def _s5_prompt_body(z_ref, dk_ref, we_ref, ws_ref, a_ref, ga_ref, gb_ref, o_ref, fin_ref, m_ref,
                    *, nb, nc):
    @pl.when(pl.program_id(1) == 0)
    def _():
        _build_toeplitz(m_ref, dk_ref)

    r = nb * nc
    p = a_ref.shape[-1] // 2
    ucat = jnp.concatenate([z_ref[pl.ds(t, r, stride=S5_T), :].astype(BF16) for t in range(S5_T)], axis=-1)
    x = jnp.dot(ucat, we_ref[...], preferred_element_type=F32)
    a = a_ref[...]
    pos = lax.rem(lax.broadcasted_iota(jnp.int32, (r, 1), 0), nc)
    sh = 1
    while sh < nc:
        mr, mi = _split_mul(a, p)
        shifted = jnp.where(pos >= sh, pltpu.roll(x, sh, axis=0), 0.0)
        x = x + _cmul(shifted, mr, mi, p)
        a = _cmul(a, mr, mi, p)
        sh *= 2
    enter = jnp.where(pos >= 1, pltpu.roll(x, 1, axis=0), 0.0)
    fin_ref[...] = jnp.concatenate([x[(b + 1) * nc - 1:(b + 1) * nc] for b in range(nb)], axis=0)
    y = _s5_outputs(ucat, enter, m_ref, ws_ref)
    for t in range(S5_T):
        o_ref[pl.ds(t, r, stride=S5_T), :] = y[:, t * LANES:(t + 1) * LANES]
    _glu_rows(o_ref, ga_ref, gb_ref, 512)


def _s5_sample_body(z_ref, dk_ref, we_ref, ws_ref, a_ref, ga_ref, gb_ref, h0_ref, o_ref, fin_ref, m_ref,
                    *, nb, nc):
    _build_toeplitz(m_ref, dk_ref)
    p = a_ref.shape[-1] // 2
    seq = nc * S5_T
    ucat = jnp.concatenate(
        [jnp.concatenate([z_ref[pl.ds(c * S5_T + t, nb, stride=seq), :].astype(BF16) for t in range(S5_T)],
                         axis=-1) for c in range(nc)], axis=0)
    e = jnp.dot(ucat, we_ref[...], preferred_element_type=F32)
    mr, mi = _split_mul(a_ref[...], p)
    state = h0_ref[...]
    enter = []
    for c in range(nc):
        enter.append(state)
        state = _cmul(state, mr, mi, p) + e[c * nb:(c + 1) * nb]
    fin_ref[...] = state
    y = _s5_outputs(ucat, jnp.concatenate(enter, axis=0), m_ref, ws_ref)
    for c in range(nc):
        for t in range(S5_T):
            o_ref[pl.ds(c * S5_T + t, nb, stride=seq), :] = y[c * nb:(c + 1) * nb, t * LANES:(t + 1) * LANES]
    _glu_rows(o_ref, ga_ref, gb_ref, 512)


def _s5_prompt(z, n_b, seq, tabs, g_a, g_b, lyr, hb):
    dk, w_e, w_s, a_t = tabs
    sw = a_t.shape[-1]
    nbs = _pick(n_b, 2, 1)
    nc = seq // S5_T
    tab = lambda arr: pl.BlockSpec((None,) + arr.shape[1:],
                                   lambda i, b: (lyr * hb + i,) + (0,) * (arr.ndim - 1))
    return pl.pallas_call(
        functools.partial(_s5_prompt_body, nb=nbs, nc=nc),
        grid=(hb, n_b // nbs),
        in_specs=[pl.BlockSpec((nbs * seq, LANES), lambda i, b: (b, i)),
                  tab(dk), tab(w_e), tab(w_s), tab(a_t), tab(g_a), tab(g_b)],
        out_specs=[pl.BlockSpec((nbs * seq, LANES), lambda i, b: (b, i)),
                   pl.BlockSpec((None, None, nbs, sw), lambda i, b: (i, b, 0, 0))],
        out_shape=[jax.ShapeDtypeStruct((n_b * seq, hb * LANES), F32),
                   jax.ShapeDtypeStruct((hb, n_b // nbs, nbs, sw), F32)],
        scratch_shapes=[pltpu.VMEM((S5_F, S5_F), BF16)],
        compiler_params=_params(("arbitrary", "arbitrary"), 52),
        name="s5_prompt",
    )(z, dk, w_e, w_s, a_t, g_a, g_b)


def _s5_sample(z, row0, n_b, n_new, tabs, g_a, g_b, h0, lyr, hb):
    dk, w_e, w_s, a_t = tabs
    sw = a_t.shape[-1]
    rows = n_b * n_new
    assert row0 % rows == 0
    rb = row0 // rows
    tab = lambda arr, off=lyr * hb: pl.BlockSpec((None,) + arr.shape[1:],
                                                 lambda i: (off + i,) + (0,) * (arr.ndim - 1))
    return pl.pallas_call(
        functools.partial(_s5_sample_body, nb=n_b, nc=n_new // S5_T),
        grid=(hb,),
        in_specs=[pl.BlockSpec((rows, LANES), lambda i: (rb, i)),
                  tab(dk), tab(w_e), tab(w_s), tab(a_t), tab(g_a), tab(g_b), tab(h0, 0)],
        out_specs=[pl.BlockSpec((rows, LANES), lambda i: (0, i)),
                   pl.BlockSpec((None, n_b, sw), lambda i: (i, 0, 0))],
        out_shape=[jax.ShapeDtypeStruct((rows, hb * LANES), F32),
                   jax.ShapeDtypeStruct((hb, n_b, sw), F32)],
        scratch_shapes=[pltpu.VMEM((S5_F, S5_F), BF16)],
        compiler_params=_params(("arbitrary",), 52),
        name="s5_sample",
    )(z, dk, w_e, w_s, a_t, g_a, g_b, h0)


def _glu_blocks(glu_w):
    h, g, _ = glu_w.shape
    hb = h // S5_GB
    eye = jnp.eye(S5_GB, dtype=F32)

    def diag(wm):
        wm = wm.reshape(hb, S5_GB, g, g)
        return (eye[None, :, None, :, None] * wm[:, :, :, None, :]).reshape(hb, LANES, LANES).astype(BF16)

    return diag(glu_w[..., :g]), diag(glu_w[..., g:])


def _state_to_lanes(re, im):
    n, h, p = re.shape
    hb = h // S5_GB
    f = lambda v: v.reshape(n, hb, S5_GB * p).transpose(1, 0, 2)
    return jnp.concatenate([f(re), f(im)], axis=-1)


def _state_from_lanes(s):
    nl, hb, n, sw = s.shape
    half = sw // 2
    f = lambda v: v.transpose(0, 2, 1, 3).reshape(nl, n, hb * S5_GB, half // S5_GB)
    return f(s[..., :half]), f(s[..., half:])


def kernel(x_prompt, x_sample, cache_k, cache_v, state_ssm_re, state_ssm_im, p_prompt, p_sample,
           norm_mix, w_in, ssm_a_re, ssm_a_im, ssm_log_dt, ssm_b_re, ssm_b_im, ssm_c_re, ssm_c_im,
           ssm_d, ssm_glu_w, attn_sinks, rel_bias, norm_grp_ssm, norm_grp_attn, w_out, norm_ffn,
           ffn_w_gate, ffn_w_up, ffn_w_down, router_w, moe_w_gate, moe_w_up, moe_w_down,
           norm_ple, ple_w_gate, ple_w_proj, norm_final):
    nb_p, seq, d = x_prompt.shape
    nb_s, n_new, _ = x_sample.shape
    depth = w_in.shape[0]
    d_ssm = norm_grp_ssm.shape[-1]
    d_attn = norm_grp_attn.shape[-1]
    mp, ms = nb_p * seq, nb_s * n_new
    win = min(WINDOW, seq)
    assert d_ssm % LANES == 0 and seq % S5_T == 0 and n_new % S5_T == 0

    h = jnp.concatenate([x_prompt.reshape(mp, d), x_sample.reshape(ms, d)], axis=0)
    p_p, p_s = p_prompt.reshape(depth, mp, -1), p_sample.reshape(depth, ms, -1)

    n_grp = ssm_a_re.shape[1]
    hb = n_grp // S5_GB
    flat = lambda v: v.reshape((depth * n_grp,) + v.shape[2:])
    tabs = _s5_tables(flat(ssm_a_re), flat(ssm_a_im), ssm_log_dt.reshape(-1), flat(ssm_b_re), flat(ssm_b_im),
                      flat(ssm_c_re), flat(ssm_c_im), ssm_d.reshape(-1))
    g_a, g_b = _glu_blocks(flat(ssm_glu_w))

    kv_p, kv_s, fin_ps, fin_ss = [], [], [], []
    for i in range(depth):
        xn = _rmsnorm(h, norm_mix[i], BF16)
        z = _mm(xn, w_in, i, tm=1024, tn=1280, vmem_mib=56)
        ssm_p, fin_p = _s5_prompt(z, nb_p, seq, tabs, g_a, g_b, i, hb)
        ssm_s, fin_s = _s5_sample(z, mp, nb_s, n_new, tabs, g_a, g_b,
                                  _state_to_lanes(state_ssm_re[i], state_ssm_im[i]), i, hb)
        att_p = _attn_prompt(z, nb_p, seq, d_ssm, d_attn, rel_bias, attn_sinks[i])
        att_s = _attn_sample(z, mp, nb_s, n_new, d_ssm, d_attn, cache_k, cache_v, i, rel_bias,
                             attn_sinks[i])
        merged = _merge_norm(ssm_p, ssm_s, att_p, att_s, norm_grp_ssm[i], norm_grp_attn[i])
        if i % 2 == 0:
            h, hn = _mm_rows(merged, w_out, i, h, tm=512, gain=norm_ffn[i])
        else:
            h = _mm_rows(merged, w_out, i, h, tm=512)

        kv_p.append(jnp.stack([z[(b + 1) * seq - win:(b + 1) * seq, d_ssm + d_attn:] for b in range(nb_p)]))
        kv_s.append(z[mp:, d_ssm + d_attn:])
        fin_ps.append(fin_p.reshape(hb, nb_p, -1))
        fin_ss.append(fin_s)

        j = i // 2
        if i % 2 == 0:
            t = _swiglu(hn, ffn_w_gate, ffn_w_up, j, tm=1024, tn=512)
            h = _mm(t, ffn_w_down, j, tm=512, tn=512, res=h, vmem_mib=52)
        else:
            hn, e_idx, gates = _rmsnorm_router(h, norm_ffn[i], router_w[j])
            h = h + _moe_ffn(hn, e_idx, gates, moe_w_gate, moe_w_up, moe_w_down, j)

        hn = _rmsnorm(h, norm_ple[i], BF16)
        h = _ple(hn, p_p, p_s, ple_w_gate, ple_w_proj, h, i, tm=1024, tn=1024, vmem_mib=58)

    y_p, y_s = _rmsnorm_split(h, norm_final, mp)
    kv_p =jnp.stack(kv_p).reshape(depth, nb_p, win, 2, KV_HEADS, HEAD_DIM)
    kv_s = jnp.stack(kv_s).reshape(depth, nb_s, n_new, 2, KV_HEADS, HEAD_DIM)
    p_re, p_im = _state_from_lanes(jnp.stack(fin_ps))
    s_re, s_im = _state_from_lanes(jnp.stack(fin_ss))
    return (y_p.reshape(nb_p, seq, d), y_s.reshape(nb_s, n_new, d),
            kv_p[:, :, :, 0], kv_p[:, :, :, 1], p_re, p_im,
            kv_s[:, :, :, 0], kv_s[:, :, :, 1], s_re, s_im)
```

```python
import functools
import math

import numpy as np
import jax
import jax.numpy as jnp
from jax import lax
from jax.experimental import pallas as pl
from jax.experimental.pallas import tpu as pltpu

F32 = jnp.float32
BF16 = jnp.bfloat16
HIGHEST = lax.Precision.HIGHEST

CHUNK = 64
WINDOW = 128
HEAD_DIM = 64
KV_HEADS = 4
SSM_GROUP_CH = 16
SSM_STATE = 64
N_BUCKETS = 32
REL_MAX_DIST = 128
EPS = 1e-6
NEG_INF = -1e30
ATTN_SCALE = HEAD_DIM ** -0.5

LANES = 128
S5_T = 16
S5_GB = LANES // SSM_GROUP_CH
S5_F = S5_T * LANES
MOE_TM = 512
MOE_SUB = 256
MIB = 1024 * 1024


def _pick(n, pref, mult=8):
    best = None
    for d in range(mult, min(n, pref) + 1, mult):
        if n % d == 0:
            best = d
    return best if best is not None else n


def _params(sem, vmem_mib):
    return pltpu.CompilerParams(dimension_semantics=sem, vmem_limit_bytes=int(vmem_mib * MIB))


def _rms(x, g):
    return x * lax.rsqrt(jnp.mean(x * x, axis=-1, keepdims=True) + EPS) * g


def _norm_body(x_ref, g_ref, o_ref):
    o_ref[...] = _rms(x_ref[...], g_ref[...]).astype(o_ref.dtype)


def _rmsnorm(x, g, out_dtype):
    m, d = x.shape
    tm = _pick(m, 512)
    return pl.pallas_call(
        _norm_body,
        grid=(m // tm,),
        in_specs=[pl.BlockSpec((tm, d), lambda i: (i, 0)), pl.BlockSpec((1, d), lambda i: (0, 0))],
        out_specs=pl.BlockSpec((tm, d), lambda i: (i, 0)),
        out_shape=jax.ShapeDtypeStruct((m, d), out_dtype),
        compiler_params=_params(("arbitrary",), 40),
        name="rmsnorm",
    )(x, g.reshape(1, d))


def _norm_split_body(x_ref, g_ref, op_ref, os_ref, *, np_tiles):
    y = _rms(x_ref[...], g_ref[...])

    @pl.when(pl.program_id(0) < np_tiles)
    def _():
        op_ref[...] = y

    @pl.when(pl.program_id(0) >= np_tiles)
    def _():
        os_ref[...] = y


def _rmsnorm_split(x, g, mp):
    m, d = x.shape
    ms = m - mp
    tm = _pick(math.gcd(mp, ms), 512)
    npt = mp // tm
    return pl.pallas_call(
        functools.partial(_norm_split_body, np_tiles=npt),
        grid=(m // tm,),
        in_specs=[pl.BlockSpec((tm, d), lambda i: (i, 0)), pl.BlockSpec((1, d), lambda i: (0, 0))],
        out_specs=[pl.BlockSpec((tm, d), lambda i: (jnp.minimum(i, npt - 1), 0)),
                   pl.BlockSpec((tm, d), lambda i: (jnp.maximum(i - npt, 0), 0))],
        out_shape=[jax.ShapeDtypeStruct((mp, d), F32), jax.ShapeDtypeStruct((ms, d), F32)],
        compiler_params=_params(("arbitrary",), 40),
        name="rmsnorm_split",
    )(x, g.reshape(1, d))


def _norm_router_body(x_ref, g_ref, rw_ref, o_ref, idx_ref, gate_ref):
    y = _rms(x_ref[...], g_ref[...])
    tm, d = y.shape
    pieces = d // LANES
    for s in range(pieces):
        o_ref[pl.ds(s, tm, stride=pieces), :] = y[:, s * LANES:(s + 1) * LANES]
    n_exp = rw_ref.shape[0]
    logits = jnp.concatenate([jnp.sum(y * rw_ref[e:e + 1, :], axis=-1, keepdims=True) for e in range(n_exp)],
                             axis=-1)
    lane = lax.broadcasted_iota(jnp.int32, logits.shape, 1)
    m1 = jnp.max(logits, axis=-1, keepdims=True)
    i1 = jnp.min(jnp.where(logits == m1, lane, n_exp), axis=-1, keepdims=True)
    rest = jnp.where(lane == i1, -jnp.inf, logits)
    m2 = jnp.max(rest, axis=-1, keepdims=True)
    i2 = jnp.min(jnp.where(rest == m2, lane, n_exp), axis=-1, keepdims=True)
    e2 = jnp.exp(m2 - m1)
    den = 1.0 + e2
    idx_ref[...] = jnp.concatenate([i1, i2], axis=-1)
    gate_ref[...] = jnp.concatenate([1.0 / den, e2 / den], axis=-1)


def _rmsnorm_router(x, g, router_w):
    m, d = x.shape
    n_exp = router_w.shape[-1]
    tm = _pick(m, 512)
    return pl.pallas_call(
        _norm_router_body,
        grid=(m // tm,),
        in_specs=[pl.BlockSpec((tm, d), lambda i: (i, 0)),
                  pl.BlockSpec((1, d), lambda i: (0, 0)),
                  pl.BlockSpec((n_exp, d), lambda i: (0, 0))],
        out_specs=[pl.BlockSpec((tm * (d // LANES), LANES), lambda i: (i, 0)),
                   pl.BlockSpec((tm, 2), lambda i: (i, 0)),
                   pl.BlockSpec((tm, 2), lambda i: (i, 0))],
        out_shape=[jax.ShapeDtypeStruct((m * (d // LANES), LANES), F32),
                   jax.ShapeDtypeStruct((m, 2), jnp.int32),
                   jax.ShapeDtypeStruct((m, 2), F32)],
        compiler_params=_params(("arbitrary",), 40),
        name="rmsnorm_router",
    )(x, g.reshape(1, d), router_w.T)


def _merge_norm_body(ap_ref, as_ref, bp_ref, bs_ref, ga_ref, gb_ref, o_ref, *, np_tiles):
    da = ap_ref.shape[-1]

    def emit(a_ref, b_ref):
        o_ref[:, :da] = _rms(a_ref[...], ga_ref[...]).astype(o_ref.dtype)
        o_ref[:, da:] = _rms(b_ref[...], gb_ref[...]).astype(o_ref.dtype)

    @pl.when(pl.program_id(0) < np_tiles)
    def _():
        emit(ap_ref, bp_ref)

    @pl.when(pl.program_id(0) >= np_tiles)
    def _():
        emit(as_ref, bs_ref)


def _merge_norm(a_p, a_s, b_p, b_s, ga, gb):
    (mp, da), (ms, db) = a_p.shape, b_s.shape
    tm = _pick(math.gcd(mp, ms), 512)
    npt, nst = mp // tm, ms // tm
    pspec = lambda w: pl.BlockSpec((tm, w), lambda i: (jnp.minimum(i, npt - 1), 0))
    sspec = lambda w: pl.BlockSpec((tm, w), lambda i: (jnp.maximum(i - npt, 0), 0))
    return pl.pallas_call(
        functools.partial(_merge_norm_body, np_tiles=npt),
        grid=(npt + nst,),
        in_specs=[pspec(da), sspec(da), pspec(db), sspec(db),
                  pl.BlockSpec((1, da), lambda i: (0, 0)), pl.BlockSpec((1, db), lambda i: (0, 0))],
        out_specs=pl.BlockSpec((tm, da + db), lambda i: (i, 0)),
        out_shape=jax.ShapeDtypeStruct((mp + ms, da + db), BF16),
        compiler_params=_params(("arbitrary",), 40),
        name="merge_norm",
    )(a_p, a_s, b_p, b_s, ga.reshape(1, da), gb.reshape(1, db))


def _mm_body(*refs, has_res):
    if has_res:
        x_ref, w_ref, r_ref, o_ref, wb_ref = refs
    else:
        x_ref, w_ref, o_ref, wb_ref = refs

    @pl.when(pl.program_id(1) == 0)
    def _():
        wb_ref[...] = w_ref[...].astype(BF16)

    acc = jnp.dot(x_ref[...], wb_ref[...], preferred_element_type=F32)
    if has_res:
        acc = r_ref[...] + acc
    o_ref[...] = acc.astype(o_ref.dtype)


def _mm(x, w, lyr, *, tm, tn, out_dtype=F32, res=None, vmem_mib=48):
    m, k = x.shape
    n = w.shape[-1]
    tm, tn = _pick(m, tm), _pick(n, tn, 128)
    in_specs = [pl.BlockSpec((tm, k), lambda j, i: (i, 0)),
                pl.BlockSpec((None, k, tn), lambda j, i: (lyr, 0, j))]
    args = [x, w]
    if res is not None:
        in_specs.append(pl.BlockSpec((tm, tn), lambda j, i: (i, j)))
        args.append(res)
    return pl.pallas_call(
        functools.partial(_mm_body, has_res=res is not None),
        grid=(n // tn, m // tm),
        in_specs=in_specs,
        out_specs=pl.BlockSpec((tm, tn), lambda j, i: (i, j)),
        out_shape=jax.ShapeDtypeStruct((m, n), out_dtype),
        scratch_shapes=[pltpu.VMEM((k, tn), BF16)],
        compiler_params=_params(("arbitrary", "arbitrary"), vmem_mib),
        name="mm_res" if res is not None else "mm",
    )(*args)


def _mm_rows_body(*refs, has_norm):
    if has_norm:
        x_ref, w_ref, r_ref, g_ref, o_ref, n_ref, wb_ref = refs
    else:
        x_ref, w_ref, r_ref, o_ref, wb_ref = refs

    @pl.when(pl.program_id(0) == 0)
    def _():
        wb_ref[...] = w_ref[...].astype(BF16)

    acc = r_ref[...] + jnp.dot(x_ref[...], wb_ref[...], preferred_element_type=F32)
    o_ref[...] = acc
    if has_norm:
        n_ref[...] = _rms(acc, g_ref[...]).astype(n_ref.dtype)


def _mm_rows(x, w, lyr, res, *, tm, gain=None, vmem_mib=56):
    m, k = x.shape
    n = w.shape[-1]
    tm = _pick(m, tm)
    row = lambda width: pl.BlockSpec((tm, width), lambda i: (i, 0))
    in_specs = [row(k), pl.BlockSpec((None, k, n), lambda i: (lyr, 0, 0), pipeline_mode=pl.Buffered(1)), row(n)]
    args = [x, w, res]
    out_specs, out_shape = [row(n)], [jax.ShapeDtypeStruct((m, n), F32)]
    if gain is not None:
        in_specs.append(pl.BlockSpec((1, n), lambda i: (0, 0)))
        args.append(gain.reshape(1, n))
        out_specs.append(row(n))
        out_shape.append(jax.ShapeDtypeStruct((m, n), BF16))
    out = pl.pallas_call(
        functools.partial(_mm_rows_body, has_norm=gain is not None),
        grid=(m // tm,),
        in_specs=in_specs,
        out_specs=out_specs,
        out_shape=out_shape,
        scratch_shapes=[pltpu.VMEM((k, n), BF16)],
        compiler_params=_params(("arbitrary",), vmem_mib),
        name="mm_rows",
    )(*args)
    return out if gain is not None else out[0]


def _swiglu_math(x, wg, wu):
    a = jnp.dot(x, wg, preferred_element_type=F32)
    b = jnp.dot(x, wu, preferred_element_type=F32)
    return (a * jax.nn.sigmoid(a) * b).astype(BF16)


def _swiglu_body(x_ref, wg_ref, wu_ref, o_ref, wgb_ref, wub_ref):
    @pl.when(pl.program_id(1) == 0)
    def _():
        wgb_ref[...] = wg_ref[...].astype(BF16)
        wub_ref[...] = wu_ref[...].astype(BF16)

    o_ref[...] = _swiglu_math(x_ref[...], wgb_ref[...], wub_ref[...])


def _swiglu(x, wg, wu, lyr, *, tm, tn, vmem_mib=48):
    m, k = x.shape
    n = wg.shape[-1]
    tm, tn = _pick(m, tm), _pick(n, tn, 128)
    wspec = pl.BlockSpec((None, k, tn), lambda j, i: (lyr, 0, j))
    return pl.pallas_call(
        _swiglu_body,
        grid=(n // tn, m // tm),
        in_specs=[pl.BlockSpec((tm, k), lambda j, i: (i, 0)), wspec, wspec],
        out_specs=pl.BlockSpec((tm, tn), lambda j, i: (i, j)),
        out_shape=jax.ShapeDtypeStruct((m, n), BF16),
        scratch_shapes=[pltpu.VMEM((k, tn), BF16), pltpu.VMEM((k, tn), BF16)],
        compiler_params=_params(("arbitrary", "arbitrary"), vmem_mib),
        name="swiglu",
    )(x, wg, wu)


def _ple_body(x_ref, pp_ref, ps_ref, wg_ref, wp_ref, r_ref, o_ref, wgb_ref, wpb_ref, pb_ref, *, np_tiles):
    i = pl.program_id(1)

    @pl.when(i == 0)
    def _():
        wgb_ref[...] = wg_ref[...].astype(BF16)
        wpb_ref[...] = wp_ref[...].astype(BF16)

    @pl.when(i < np_tiles)
    def _():
        pb_ref[...] = pp_ref[...].astype(BF16)

    @pl.when(i >= np_tiles)
    def _():
        pb_ref[...] = ps_ref[...].astype(BF16)

    a = jnp.dot(x_ref[...], wgb_ref[...], preferred_element_type=F32)
    b = jnp.dot(pb_ref[...], wpb_ref[...], preferred_element_type=F32)
    o_ref[...] = r_ref[...] + jax.nn.sigmoid(a) * b


def _ple_rows_body(x_ref, pp_ref, ps_ref, wg_ref, wp_ref, r_ref, g_ref, o_ref, n_ref,
                   wgb_ref, wpb_ref, pb_ref, *, np_tiles):
    i = pl.program_id(0)

    @pl.when(i == 0)
    def _():
        wgb_ref[...] = wg_ref[...].astype(BF16)
        wpb_ref[...] = wp_ref[...].astype(BF16)

    @pl.when(i < np_tiles)
    def _():
        pb_ref[...] = pp_ref[...].astype(BF16)

    @pl.when(i >= np_tiles)
    def _():
        pb_ref[...] = ps_ref[...].astype(BF16)

    a = jnp.dot(x_ref[...], wgb_ref[...], preferred_element_type=F32)
    b = jnp.dot(pb_ref[...], wpb_ref[...], preferred_element_type=F32)
    out = r_ref[...] + jax.nn.sigmoid(a) * b
    o_ref[...] = out
    n_ref[...] = _rms(out, g_ref[...]).astype(n_ref.dtype)


def _ple_rows(x, p_p, p_s, wg, wp, res, lyr, gain, *, tm, vmem_mib=58):
    m, k = x.shape
    mp, ms, kp = p_p.shape[1], p_s.shape[1], p_p.shape[-1]
    n = wg.shape[-1]
    tm = _pick(math.gcd(mp, ms), tm)
    npt = mp // tm
    row = lambda width: pl.BlockSpec((tm, width), lambda i: (i, 0))
    once = lambda rows: pl.BlockSpec((None, rows, n), lambda i: (lyr, 0, 0), pipeline_mode=pl.Buffered(1))
    return pl.pallas_call(
        functools.partial(_ple_rows_body, np_tiles=npt),
        grid=(m // tm,),
        in_specs=[row(k),
                  pl.BlockSpec((None, tm, kp), lambda i: (lyr, jnp.minimum(i, npt - 1), 0)),
                  pl.BlockSpec((None, tm, kp), lambda i: (lyr, jnp.maximum(i - npt, 0), 0)),
                  once(k), once(kp), row(n), pl.BlockSpec((1, n), lambda i: (0, 0))],
        out_specs=[row(n), row(n)],
        out_shape=[jax.ShapeDtypeStruct((m, n), F32), jax.ShapeDtypeStruct((m, n), BF16)],
        scratch_shapes=[pltpu.VMEM((k, n), BF16), pltpu.VMEM((kp, n), BF16), pltpu.VMEM((tm, kp), BF16)],
        compiler_params=_params(("arbitrary",), vmem_mib),
        name="ple_rows",
    )(x, p_p, p_s, wg, wp, res, gain.reshape(1, n))


def _ple(x, p_p, p_s, wg, wp, res, lyr, *, tm, tn, vmem_mib=48):
    m, k = x.shape
    mp, ms, kp = p_p.shape[1], p_s.shape[1], p_p.shape[-1]
    n = wg.shape[-1]
    tm, tn = _pick(math.gcd(mp, ms), tm), _pick(n, tn, 128)
    npt = mp // tm
    return pl.pallas_call(
        functools.partial(_ple_body, np_tiles=npt),
        grid=(n // tn, m // tm),
        in_specs=[pl.BlockSpec((tm, k), lambda j, i: (i, 0)),
                  pl.BlockSpec((None, tm, kp), lambda j, i: (lyr, jnp.minimum(i, npt - 1), 0)),
                  pl.BlockSpec((None, tm, kp), lambda j, i: (lyr, jnp.maximum(i - npt, 0), 0)),
                  pl.BlockSpec((None, k, tn), lambda j, i: (lyr, 0, j)),
                  pl.BlockSpec((None, kp, tn), lambda j, i: (lyr, 0, j)),
                  pl.BlockSpec((tm, tn), lambda j, i: (i, j))],
        out_specs=pl.BlockSpec((tm, tn), lambda j, i: (i, j)),
        out_shape=jax.ShapeDtypeStruct((m, n), F32),
        scratch_shapes=[pltpu.VMEM((k, tn), BF16), pltpu.VMEM((kp, tn), BF16), pltpu.VMEM((tm, kp), BF16)],
        compiler_params=_params(("arbitrary", "arbitrary"), vmem_mib),
        name="ple",
    )(x, p_p, p_s, wg, wp, res)


def _moe_weights(w_hbms, stages, bf16s, sem, e_ref, start_ref, nxt_ref, lyr):
    j, i = pl.program_id(0), pl.program_id(1)
    tn = stages[0].shape[1]

    def copies(e, col):
        col = pl.multiple_of(col, LANES)
        return [pltpu.make_async_copy(w.at[lyr, e, :, pl.ds(col, tn)], st, sem.at[n])
                for n, (w, st) in enumerate(zip(w_hbms, stages))]

    @pl.when(jnp.logical_and(j == 0, i == 0))
    def _():
        for cp in copies(e_ref[0], 0):
            cp.start()

    @pl.when(start_ref[i] == 1)
    def _():
        for cp in copies(e_ref[i], j * tn):
            cp.wait()
        rows = 128

        def cast_chunk(c, carry):
            r = pl.multiple_of(c * rows, rows)
            for st, wb in zip(stages, bf16s):
                wb[pl.ds(r, rows), :] = st[pl.ds(r, rows), :].astype(BF16)
            return carry
        lax.fori_loop(0, stages[0].shape[0] // rows, cast_chunk, 0)
        nxt = nxt_ref[i]

        @pl.when(nxt >= 0)
        def _():
            for cp in copies(nxt, j * tn):
                cp.start()

        @pl.when(jnp.logical_and(nxt < 0, j + 1 < pl.num_programs(0)))
        def _():
            for cp in copies(e_ref[0], (j + 1) * tn):
                cp.start()


def _active_prefix(act_ref, i, per):
    n_act = act_ref[i * per]
    for s in range(1, per):
        n_act = n_act + act_ref[i * per + s]
    return n_act


def _prefix_rows(o_ref, n_act, fn):
    tm = o_ref.shape[0]
    for c in range(tm // MOE_SUB + 1):
        rows = c * MOE_SUB

        @pl.when(n_act == c)
        def _():
            if rows:
                o_ref[:rows, :] = fn(rows)
            if rows < tm:
                o_ref[rows:, :] = jnp.zeros((tm - rows, o_ref.shape[1]), o_ref.dtype)


def _moe_up_body(e_ref, act_ref, start_ref, nxt_ref, x_ref, wg_hbm, wu_hbm, o_ref,
                 wgb_ref, wub_ref, sg_ref, su_ref, sem, *, lyr):
    _moe_weights((wg_hbm, wu_hbm), (sg_ref, su_ref), (wgb_ref, wub_ref), sem, e_ref, start_ref, nxt_ref, lyr)
    n_act = _active_prefix(act_ref, pl.program_id(1), x_ref.shape[0] // MOE_SUB)
    _prefix_rows(o_ref, n_act, lambda rows: _swiglu_math(x_ref[:rows, :], wgb_ref[...], wub_ref[...]))


def _moe_down_body(e_ref, act_ref, start_ref, nxt_ref, h_ref, wd_hbm, o_ref, wdb_ref, sd_ref, sem, *, lyr):
    _moe_weights((wd_hbm,), (sd_ref,), (wdb_ref,), sem, e_ref, start_ref, nxt_ref, lyr)
    n_act = _active_prefix(act_ref, pl.program_id(1), h_ref.shape[0] // MOE_SUB)
    _prefix_rows(o_ref, n_act,
                 lambda rows: jnp.dot(h_ref[:rows, :], wdb_ref[...], preferred_element_type=F32))


def _moe_matmul(body, x, ws, sched, lyr, *, tn, out_dtype, vmem_mib, name):
    ns, k = x.shape
    n = ws[0].shape[-1]
    tm, tn = MOE_TM, _pick(n, tn, 128)
    grid_spec = pltpu.PrefetchScalarGridSpec(
        num_scalar_prefetch=len(sched),
        grid=(n // tn, ns // tm),
        in_specs=[pl.BlockSpec((tm, k), lambda j, i, *_: (i, 0))]
                 + [pl.BlockSpec(memory_space=pl.ANY)] * len(ws),
        out_specs=pl.BlockSpec((tm, tn), lambda j, i, *_: (i, j)),
        scratch_shapes=[pltpu.VMEM((k, tn), BF16)] * len(ws) + [pltpu.VMEM((k, tn), F32)] * len(ws)
                       + [pltpu.SemaphoreType.DMA((len(ws),))],
    )
    return pl.pallas_call(
        functools.partial(body, lyr=lyr),
        grid_spec=grid_spec,
        out_shape=jax.ShapeDtypeStruct((ns, n), out_dtype),
        compiler_params=_params(("arbitrary", "arbitrary"), vmem_mib),
        name=name,
    )(*sched, x, *ws)


def _row_gather_copy(x_hbm, buf, sem, slot, src_row, dst_row, rows):
    return pltpu.make_async_copy(x_hbm.at[pl.ds(src_row, rows)], buf.at[slot, pl.ds(dst_row, rows)],
                                 sem.at[slot])


def _moe_gather_body(tok_ref, x_hbm, o_ref, buf, sem):
    i = pl.program_id(0)
    tm = o_ref.shape[0]
    pieces = o_ref.shape[1] // LANES

    def issue(blk, slot):
        def body(r2, carry):
            for pri in range(2):
                r = r2 * 2 + pri
                _row_gather_copy(x_hbm, buf, sem, slot, tok_ref[blk * tm + r] * pieces, r * pieces,
                                 pieces).start(priority=pri)
            return carry
        lax.fori_loop(0, tm // 2, body, 0, unroll=4)

    @pl.when(i == 0)
    def _():
        issue(0, 0)

    @pl.when(i + 1 < pl.num_programs(0))
    def _():
        issue(i + 1, (i + 1) % 2)

    slot = i % 2
    _row_gather_copy(x_hbm, buf, sem, slot, 0, 0, tm * pieces).wait()
    cur = buf.at[slot]
    o_ref[...] = jnp.concatenate([cur[pl.ds(s, tm, stride=pieces), :] for s in range(pieces)],
                                 axis=-1).astype(o_ref.dtype)


def _moe_gather(x_rows, slot_tok, d):
    n_slots = slot_tok.shape[0]
    tm = MOE_TM
    grid_spec = pltpu.PrefetchScalarGridSpec(
        num_scalar_prefetch=1,
        grid=(n_slots // tm,),
        in_specs=[pl.BlockSpec(memory_space=pl.ANY)],
        out_specs=pl.BlockSpec((tm, d), lambda i, tok: (i, 0)),
        scratch_shapes=[pltpu.VMEM((2, tm * (d // LANES), LANES), F32), pltpu.SemaphoreType.DMA((2,))],
    )
    return pl.pallas_call(
        _moe_gather_body,
        grid_spec=grid_spec,
        out_shape=jax.ShapeDtypeStruct((n_slots, d), BF16),
        compiler_params=_params(("arbitrary",), 32),
        name="moe_gather",
    )(slot_tok, x_rows)


def _moe_ffn(hn_rows, e_idx, gates, wg, wu, wd, lyr):
    m = e_idx.shape[0]
    d = wg.shape[2]
    n_exp = wg.shape[1]
    top_k = e_idx.shape[-1]
    tm = MOE_TM
    n_asg = m * top_k
    e_flat = e_idx.reshape(n_asg)
    onehot = (e_flat[None, :] == jnp.arange(n_exp, dtype=jnp.int32)[:, None]).astype(jnp.int32)
    counts = jnp.sum(onehot, axis=1)
    start = jnp.cumsum(counts) - counts
    padded = (counts + tm - 1) // tm * tm
    pad_end = jnp.cumsum(padded)
    pad_start = pad_end - padded
    order = jnp.argsort(e_flat).astype(jnp.int32)
    rank = jnp.argsort(order).astype(jnp.int32)
    slot_of = rank + jnp.sum(onehot * (pad_start - start)[:, None], axis=0).astype(jnp.int32)
    n_blocks = n_asg // tm + n_exp
    n_slots = n_blocks * tm
    n_used = (pad_end[-1] // tm).astype(jnp.int32)
    blk = jnp.arange(n_blocks, dtype=jnp.int32)
    blk_start = jnp.minimum(blk, n_used - 1) * tm
    blk_expert = jnp.sum((pad_end[None, :] <= blk_start[:, None]).astype(jnp.int32), axis=1)
    blk_expert = jnp.minimum(blk_expert, n_exp - 1).astype(jnp.int32)
    sub_start = jnp.arange(n_slots // MOE_SUB, dtype=jnp.int32) * MOE_SUB
    sub_expert = jnp.repeat(blk_expert, tm // MOE_SUB)
    real_end = (pad_start + counts)[sub_expert]
    sub_active = jnp.logical_and(sub_start < real_end, sub_start < pad_end[-1]).astype(jnp.int32)
    slot_r = jnp.arange(n_slots, dtype=jnp.int32) - jnp.repeat(pad_start[sub_expert], MOE_SUB)
    slot_src = jnp.repeat(start[sub_expert], MOE_SUB) + slot_r
    slot_real = jnp.logical_and(slot_r >= 0, slot_r < jnp.repeat(counts[sub_expert], MOE_SUB))
    slot_tok = jnp.where(slot_real, order[jnp.clip(slot_src, 0, n_asg - 1)] // top_k, 0).astype(jnp.int32)

    run_start = jnp.concatenate([jnp.ones((1,), jnp.int32),
                                 (blk_expert[1:] != blk_expert[:-1]).astype(jnp.int32)])
    later_start = jnp.where(run_start == 1, blk, n_blocks)
    nxt_blk = lax.cummin(jnp.concatenate([later_start[1:], jnp.full((1,), n_blocks, jnp.int32)]),
                         axis=0, reverse=True)
    run_next = jnp.where(nxt_blk < n_blocks, blk_expert[jnp.minimum(nxt_blk, n_blocks - 1)], -1)
    sched = (blk_expert, sub_active, run_start, run_next.astype(jnp.int32))

    x_sorted = _moe_gather(hn_rows, slot_tok, d)
    h_sorted = _moe_matmul(_moe_up_body, x_sorted, (wg, wu), sched, lyr, tn=1024, out_dtype=BF16,
                           vmem_mib=48, name="moe_up")
    y_sorted = _moe_matmul(_moe_down_body, h_sorted, (wd,), sched, lyr, tn=512, out_dtype=F32,
                           vmem_mib=48, name="moe_down")
    sl = slot_of.reshape(m, top_k)
    y = y_sorted[sl[:, 0]] * gates[:, 0:1]
    for t in range(1, top_k):
        y = y + y_sorted[sl[:, t]] * gates[:, t:t + 1]
    return y


def _rel_buckets(n_q, n_k_past, n_k):
    rel = (np.arange(n_k) - n_k_past)[None, :] - np.arange(n_q)[:, None]
    half = N_BUCKETS // 2
    max_exact = half // 2
    n = np.abs(rel)
    nf = np.maximum(n, 1).astype(np.float32)
    large = max_exact + (np.log(nf / np.float32(max_exact)) / np.float32(math.log(REL_MAX_DIST / max_exact))
                         * np.float32(half - max_exact)).astype(np.int32)
    large = np.minimum(large, half - 1)
    return (np.where(rel > 0, half, 0) + np.where(n < max_exact, n, large)).astype(np.int32)


def _build_bias(bias_ref, sink_ref, bucket_ref, table_ref, sinks_ref):
    bucket = bucket_ref[...]
    nq = bucket.shape[1]
    gq = bias_ref.shape[2] // nq
    for kh in range(bias_ref.shape[0]):
        for g in range(gq):
            n = kh * gq + g
            acc = jnp.zeros(bucket.shape, F32)
            for b in range(N_BUCKETS):
                acc = jnp.where(bucket == b, table_ref[b, n], acc)
            bias_ref[kh, :, g * nq:(g + 1) * nq] = acc
            sink_ref[kh, :, g * nq:(g + 1) * nq] = jnp.full((1, nq), sinks_ref[n], F32)


def _attn_items(items, bias_ref, sink_ref):
    nq = items[0][0].shape[0]
    gq = items[0][0].shape[-1] // HEAD_DIM // KV_HEADS
    head = lambda x, n: x[:, n * HEAD_DIM:(n + 1) * HEAD_DIM]
    scores = []
    for q, k, _, valid in items:
        for kh in range(KV_HEADS):
            q4 = jnp.concatenate([head(q, kh * gq + g) for g in range(gq)], axis=0)
            s = lax.dot_general(head(k, kh), q4, (((1,), (1,)), ((), ())), preferred_element_type=F32)
            scores.append((s, kh, valid))
    probs = []
    for s, kh, valid in scores:
        s = s + bias_ref[kh]
        if valid is not None:
            s = jnp.where(valid, s, NEG_INF)
        sink = sink_ref[kh]
        mx = jnp.maximum(jnp.max(s, axis=0, keepdims=True), sink)
        e = jnp.exp(s - mx)
        den = jnp.sum(e, axis=0, keepdims=True) + jnp.exp(sink - mx)
        probs.append((e * (1.0 / den)).astype(BF16))
    outs = []
    for i, (_, _, v, _) in enumerate(items):
        heads = []
        for kh in range(KV_HEADS):
            o4 = lax.dot_general(probs[i * KV_HEADS + kh], head(v, kh), (((0,), (0,)), ((), ())),
                                 preferred_element_type=F32)
            heads.extend(o4[g * nq:(g + 1) * nq] for g in range(gq))
        outs.append(jnp.concatenate(heads, axis=-1))
    return outs


def _attn_prompt_body(q_ref, kp_ref, kc_ref, vp_ref, vc_ref, bucket_ref, table_ref, sinks_ref,
                      o_ref, bias_ref, sink_ref, *, chunks):
    j = pl.program_id(1)

    @pl.when(jnp.logical_and(pl.program_id(0) == 0, j == 0))
    def _():
        _build_bias(bias_ref, sink_ref, bucket_ref, table_ref, sinks_ref)

    q = (q_ref[...] * ATTN_SCALE).astype(BF16)
    k = jnp.concatenate([kp_ref[...], kc_ref[...]], axis=0).astype(BF16)
    v = jnp.concatenate([vp_ref[...], vc_ref[...]], axis=0).astype(BF16)
    band = WINDOW + CHUNK
    items = []
    for c in range(chunks):
        lo = c * CHUNK
        key_valid = None
        if lo < WINDOW:
            key_pos = j * (chunks * CHUNK) + lo - WINDOW + lax.broadcasted_iota(jnp.int32, (band, 1), 0)
            key_valid = key_pos >= 0
        items.append((q[lo:lo + CHUNK], k[lo:lo + band], v[lo:lo + band], key_valid))
    for c, o in enumerate(_attn_items(items, bias_ref, sink_ref)):
        o_ref[c * CHUNK:(c + 1) * CHUNK, :] = o


def _attn_prompt(z, n_b, seq, d_ssm, d_attn, table, sinks):
    d_kv = KV_HEADS * HEAD_DIM
    assert d_ssm % d_attn == 0 and d_ssm % d_kv == 0 and d_attn % d_kv == 0
    qb = _pick(seq, 4 * CHUNK, 2 * CHUNK)
    assert qb % WINDOW == 0 and seq % qb == 0
    chunks = qb // CHUNK
    nj = seq // qb
    r = qb // WINDOW
    kcol = (d_ssm + d_attn) // d_kv
    gq = d_attn // HEAD_DIM // KV_HEADS
    band = WINDOW + CHUNK
    bucket = jnp.asarray(_rel_buckets(CHUNK, WINDOW, band).T)

    def prev(b, j):
        return b * (seq // WINDOW) + jnp.maximum(j * r - 1, 0)

    return pl.pallas_call(
        functools.partial(_attn_prompt_body, chunks=chunks),
        grid=(n_b, nj),
        in_specs=[pl.BlockSpec((qb, d_attn), lambda b, j: (b * nj + j, d_ssm // d_attn)),
                  pl.BlockSpec((WINDOW, d_kv), lambda b, j: (prev(b, j), kcol)),
                  pl.BlockSpec((qb, d_kv), lambda b, j: (b * nj + j, kcol)),
                  pl.BlockSpec((WINDOW, d_kv), lambda b, j: (prev(b, j), kcol + 1)),
                  pl.BlockSpec((qb, d_kv), lambda b, j: (b * nj + j, kcol + 1)),
                  pl.BlockSpec((band, CHUNK), lambda b, j: (0, 0)),
                  pl.BlockSpec(memory_space=pltpu.SMEM),
                  pl.BlockSpec(memory_space=pltpu.SMEM)],
        out_specs=pl.BlockSpec((qb, d_attn), lambda b, j: (b * nj + j, 0)),
        out_shape=jax.ShapeDtypeStruct((n_b * seq, d_attn), F32),
        scratch_shapes=[pltpu.VMEM((KV_HEADS, band, gq * CHUNK), F32),
                        pltpu.VMEM((KV_HEADS, 1, gq * CHUNK), F32)],
        compiler_params=_params(("arbitrary", "arbitrary"), 40),
        name="attn_prompt",
    )(z, z, z, z, z, bucket, table, sinks)


def _attn_sample_body(q_ref, kc_ref, kn_ref, vc_ref, vn_ref, bucket_ref, table_ref, sinks_ref,
                      o_ref, bias_ref, sink_ref, *, n_bs, n_new, n_past):
    @pl.when(pl.program_id(0) == 0)
    def _():
        _build_bias(bias_ref, sink_ref, bucket_ref, table_ref, sinks_ref)

    q = (q_ref[...] * ATTN_SCALE).astype(BF16)
    kc, kn = kc_ref[...].astype(BF16), kn_ref[...].astype(BF16)
    vc, vn = vc_ref[...].astype(BF16), vn_ref[...].astype(BF16)
    items = []
    for b in range(n_bs):
        new = slice(b * n_new, (b + 1) * n_new)
        past = slice(b * n_past, (b + 1) * n_past)
        items.append((q[new], jnp.concatenate([kc[past], kn[new]], axis=0),
                      jnp.concatenate([vc[past], vn[new]], axis=0), None))
    for b, o in enumerate(_attn_items(items, bias_ref, sink_ref)):
        o_ref[b * n_new:(b + 1) * n_new, :] = o


def _attn_sample(z, row0, n_b, n_new, d_ssm, d_attn, cache_k, cache_v, lyr, table, sinks):
    d_kv = KV_HEADS * HEAD_DIM
    n_past = cache_k.shape[2]
    n_bs = _pick(n_b, 4, 1)
    rows = n_bs * n_new
    assert row0 % rows == 0
    kcol = (d_ssm + d_attn) // d_kv
    gq = d_attn // HEAD_DIM // KV_HEADS
    n_k = n_past + n_new
    bucket = jnp.asarray(_rel_buckets(n_new, n_past, n_k).T)
    ck = cache_k.reshape(-1, d_kv)
    cv = cache_v.reshape(-1, d_kv)
    rb = row0 // rows
    nsteps = n_b // n_bs
    cache_spec = pl.BlockSpec((n_bs * n_past, d_kv), lambda b: (lyr * nsteps + b, 0))
    return pl.pallas_call(
        functools.partial(_attn_sample_body, n_bs=n_bs, n_new=n_new, n_past=n_past),
        grid=(nsteps,),
        in_specs=[pl.BlockSpec((rows, d_attn), lambda b: (rb + b, d_ssm // d_attn)),
                  cache_spec,
                  pl.BlockSpec((rows, d_kv), lambda b: (rb + b, kcol)),
                  cache_spec,
                  pl.BlockSpec((rows, d_kv), lambda b: (rb + b, kcol + 1)),
                  pl.BlockSpec((n_k, n_new), lambda b: (0, 0)),
                  pl.BlockSpec(memory_space=pltpu.SMEM),
                  pl.BlockSpec(memory_space=pltpu.SMEM)],
        out_specs=pl.BlockSpec((rows, d_attn), lambda b: (b, 0)),
        out_shape=jax.ShapeDtypeStruct((n_b * n_new, d_attn), F32),
        scratch_shapes=[pltpu.VMEM((KV_HEADS, n_k, gq * n_new), F32),
                        pltpu.VMEM((KV_HEADS, 1, gq * n_new), F32)],
        compiler_params=_params(("arbitrary",), 40),
        name="attn_sample",
    )(z, ck, z, cv, z, bucket, table, sinks)


def _tile_lanes(x, reps):
    return jnp.concatenate([x] * reps, axis=-1)


def _s5_tables_body(are_ref, aim_ref, ldt_ref, btr_ref, bti_ref, cre_ref, cim_ref, d_ref,
                    dk_ref, we_ref, ws_ref, a_ref):
    rows, p = cre_ref.shape
    gb = rows // SSM_GROUP_CH
    a_re, a_im = are_ref[...], aim_ref[...]
    dt = jnp.exp(ldt_ref[...])
    mag = jnp.exp(a_re * dt)
    ab_re, ab_im = mag * jnp.cos(a_im * dt), mag * jnp.sin(a_im * dt)

    x, y = ab_re - 1.0, ab_im
    den = a_re * a_re + a_im * a_im
    co_re = (x * a_re + y * a_im) / den
    co_im = (y * a_re - x * a_im) / den
    bb_re = co_re * btr_ref[...] - co_im * bti_ref[...]
    bb_im = co_re * bti_ref[...] + co_im * btr_ref[...]
    c_re, c_im = cre_ref[...], cim_ref[...]

    r_idx = lax.broadcasted_iota(jnp.int32, (rows, 1), 0)
    own_state = (r_idx // SSM_GROUP_CH) == (lax.broadcasted_iota(jnp.int32, (1, gb * p), 1) // p)
    same_group = (r_idx // SSM_GROUP_CH) == (lax.broadcasted_iota(jnp.int32, (1, rows), 1) // SSM_GROUP_CH)
    diag = r_idx == lax.broadcasted_iota(jnp.int32, (1, rows), 1)

    def place(v_re, v_im):
        return jnp.concatenate([jnp.where(own_state, _tile_lanes(v_re, gb), 0.0),
                                jnp.where(own_state, _tile_lanes(v_im, gb), 0.0)], axis=-1)

    dims = (((1,), (1,)), ((), ()))
    bb_cat = jnp.concatenate([bb_re, -bb_im], axis=-1)
    pw_re, pw_im = jnp.ones_like(ab_re), jnp.zeros_like(ab_re)
    for j in range(S5_T + 1):
        ca_re = c_re * pw_re - c_im * pw_im
        ca_im = c_re * pw_im + c_im * pw_re
        if j < S5_T:
            kj = lax.dot_general(bb_cat, jnp.concatenate([ca_re, ca_im], axis=-1), dims,
                                 precision=HIGHEST, preferred_element_type=F32)
            kj = jnp.where(same_group, kj, 0.0)
            if j == 0:
                kj = kj + jnp.where(diag, d_ref[...], 0.0)
            dk_ref[j] = kj.astype(dk_ref.dtype)
            s = S5_T - 1 - j
            we_ref[s * rows:(s + 1) * rows, :] = place(
                pw_re * bb_re - pw_im * bb_im, pw_re * bb_im + pw_im * bb_re).astype(we_ref.dtype)
        if j >= 1:
            ws_ref[(j - 1) * rows:j * rows, :] = place(ca_re, -ca_im).astype(ws_ref.dtype)
        if j == S5_T:
            first = r_idx == (lax.broadcasted_iota(jnp.int32, (1, gb * p), 1) // p) * SSM_GROUP_CH
            a_ref[...] = jnp.concatenate(
                [jnp.sum(jnp.where(first, _tile_lanes(pw_re, gb), 0.0), axis=0, keepdims=True),
                 jnp.sum(jnp.where(first, _tile_lanes(pw_im, gb), 0.0), axis=0, keepdims=True)], axis=-1)
        pw_re, pw_im = pw_re * ab_re - pw_im * ab_im, pw_re * ab_im + pw_im * ab_re


def _s5_tables(a_re, a_im, log_dt, b_re, b_im, c_re, c_im, d_skip):
    h, p = a_re.shape
    g = SSM_GROUP_CH
    hb = h // S5_GB
    sw = 2 * S5_GB * p
    rep = lambda v: jnp.repeat(v, g, axis=0).reshape(hb, LANES, -1)
    blk = lambda w: pl.BlockSpec((None, LANES, w), lambda i: (i, 0, 0))
    wide = pl.BlockSpec((None, S5_F, sw), lambda i: (i, 0, 0))
    return pl.pallas_call(
        _s5_tables_body,
        grid=(hb,),
        in_specs=[blk(p), blk(p), blk(1), blk(p), blk(p), blk(p), blk(p), blk(1)],
        out_specs=[pl.BlockSpec((None, S5_T, LANES, LANES), lambda i: (i, 0, 0, 0)), wide, wide,
                   pl.BlockSpec((None, 1, sw), lambda i: (i, 0, 0))],
        out_shape=[jax.ShapeDtypeStruct((hb, S5_T, LANES, LANES), BF16),
                   jax.ShapeDtypeStruct((hb, S5_F, sw), BF16),
                   jax.ShapeDtypeStruct((hb, S5_F, sw), BF16),
                   jax.ShapeDtypeStruct((hb, 1, sw), F32)],
        compiler_params=_params(("arbitrary",), 40),
        name="s5_tables",
    )(rep(a_re), rep(a_im), rep(log_dt.reshape(h, 1)),
      jnp.swapaxes(b_re, 1, 2).reshape(hb, LANES, p), jnp.swapaxes(b_im, 1, 2).reshape(hb, LANES, p),
      c_re.reshape(hb, LANES, p), c_im.reshape(hb, LANES, p), d_skip.reshape(hb, LANES, 1))


def _split_mul(a, p):
    lane = lax.broadcasted_iota(jnp.int32, a.shape, a.ndim - 1)
    sw = pltpu.roll(a, p, axis=a.ndim - 1)
    return jnp.where(lane < p, a, sw), jnp.where(lane < p, -sw, a)


def _cmul(x, mr, mi, p):
    return x * mr + pltpu.roll(x, p, axis=x.ndim - 1) * mi


def _build_toeplitz(m_ref, dk_ref):
    m_ref[...] = jnp.zeros_like(m_ref)
    for s in range(S5_T):
        for t in range(s, S5_T):
            m_ref[s * LANES:(s + 1) * LANES, t * LANES:(t + 1) * LANES] = dk_ref[t - s]


def _s5_outputs(ucat, enter, m_ref, ws_ref):
    return (jnp.dot(ucat, m_ref[...], preferred_element_type=F32)
            + lax.dot_general(enter.astype(BF16), ws_ref[...], (((1,), (1,)), ((), ())),
                              preferred_element_type=F32))


def _glu_rows(o_ref, ga_ref, gb_ref, tile):
    ga, gb = ga_ref[...], gb_ref[...]
    tile = _pick(o_ref.shape[0], tile)
    for r0 in range(0, o_ref.shape[0], tile):
        y = o_ref[r0:r0 + tile, :].astype(BF16)
        o_ref[r0:r0 + tile, :] = (jnp.dot(y, ga, preferred_element_type=F32)
                                  * jax.nn.sigmoid(jnp.dot(y, gb, preferred_element_type=F32)))


def _s5_prompt_body(z_ref, dk_ref, we_ref, ws_ref, a_ref, ga_ref, gb_ref, o_ref, fin_ref, m_ref,
                    *, nb, nc):
    @pl.when(pl.program_id(1) == 0)
    def _():
        _build_toeplitz(m_ref, dk_ref)

    r = nb * nc
    p = a_ref.shape[-1] // 2
    ucat = jnp.concatenate([z_ref[pl.ds(t, r, stride=S5_T), :].astype(BF16) for t in range(S5_T)], axis=-1)
    x = jnp.dot(ucat, we_ref[...], preferred_element_type=F32)
    a = a_ref[...]
    pos = lax.rem(lax.broadcasted_iota(jnp.int32, (r, 1), 0), nc)
    sh = 1
    while sh < nc:
        mr, mi = _split_mul(a, p)
        shifted = jnp.where(pos >= sh, pltpu.roll(x, sh, axis=0), 0.0)
        x = x + _cmul(shifted, mr, mi, p)
        a = _cmul(a, mr, mi, p)
        sh *= 2
    enter = jnp.where(pos >= 1, pltpu.roll(x, 1, axis=0), 0.0)
    fin_ref[...] = jnp.concatenate([x[(b + 1) * nc - 1:(b + 1) * nc] for b in range(nb)], axis=0)
    y = _s5_outputs(ucat, enter, m_ref, ws_ref)
    for t in range(S5_T):
        o_ref[pl.ds(t, r, stride=S5_T), :] = y[:, t * LANES:(t + 1) * LANES]
    _glu_rows(o_ref, ga_ref, gb_ref, 512)


def _s5_sample_body(z_ref, dk_ref, we_ref, ws_ref, a_ref, ga_ref, gb_ref, h0_ref, o_ref, fin_ref, m_ref,
                    *, nb, nc):
    _build_toeplitz(m_ref, dk_ref)
    p = a_ref.shape[-1] // 2
    seq = nc * S5_T
    ucat = jnp.concatenate(
        [jnp.concatenate([z_ref[pl.ds(c * S5_T + t, nb, stride=seq), :].astype(BF16) for t in range(S5_T)],
                         axis=-1) for c in range(nc)], axis=0)
    e = jnp.dot(ucat, we_ref[...], preferred_element_type=F32)
    mr, mi = _split_mul(a_ref[...], p)
    state = h0_ref[...]
    enter = []
    for c in range(nc):
        enter.append(state)
        state = _cmul(state, mr, mi, p) + e[c * nb:(c + 1) * nb]
    fin_ref[...] = state
    y = _s5_outputs(ucat, jnp.concatenate(enter, axis=0), m_ref, ws_ref)
    for c in range(nc):
        for t in range(S5_T):
            o_ref[pl.ds(c * S5_T + t, nb, stride=seq), :] = y[c * nb:(c + 1) * nb, t * LANES:(t + 1) * LANES]
    _glu_rows(o_ref, ga_ref, gb_ref, 512)


def _s5_prompt(z, n_b, seq, tabs, g_a, g_b, lyr, hb):
    dk, w_e, w_s, a_t = tabs
    sw = a_t.shape[-1]
    nbs = _pick(n_b, 2, 1)
    nc = seq // S5_T
    tab = lambda arr: pl.BlockSpec((None,) + arr.shape[1:],
                                   lambda i, b: (lyr * hb + i,) + (0,) * (arr.ndim - 1))
    return pl.pallas_call(
        functools.partial(_s5_prompt_body, nb=nbs, nc=nc),
        grid=(hb, n_b // nbs),
        in_specs=[pl.BlockSpec((nbs * seq, LANES), lambda i, b: (b, i)),
                  tab(dk), tab(w_e), tab(w_s), tab(a_t), tab(g_a), tab(g_b)],
        out_specs=[pl.BlockSpec((nbs * seq, LANES), lambda i, b: (b, i)),
                   pl.BlockSpec((None, None, nbs, sw), lambda i, b: (i, b, 0, 0))],
        out_shape=[jax.ShapeDtypeStruct((n_b * seq, hb * LANES), F32),
                   jax.ShapeDtypeStruct((hb, n_b // nbs, nbs, sw), F32)],
        scratch_shapes=[pltpu.VMEM((S5_F, S5_F), BF16)],
        compiler_params=_params(("arbitrary", "arbitrary"), 52),
        name="s5_prompt",
    )(z, dk, w_e, w_s, a_t, g_a, g_b)


def _s5_sample(z, row0, n_b, n_new, tabs, g_a, g_b, h0, lyr, hb):
    dk, w_e, w_s, a_t = tabs
    sw = a_t.shape[-1]
    rows = n_b * n_new
    assert row0 % rows == 0
    rb = row0 // rows
    tab = lambda arr, off=lyr * hb: pl.BlockSpec((None,) + arr.shape[1:],
                                                 lambda i: (off + i,) + (0,) * (arr.ndim - 1))
    return pl.pallas_call(
        functools.partial(_s5_sample_body, nb=n_b, nc=n_new // S5_T),
        grid=(hb,),
        in_specs=[pl.BlockSpec((rows, LANES), lambda i: (rb, i)),
                  tab(dk), tab(w_e), tab(w_s), tab(a_t), tab(g_a), tab(g_b), tab(h0, 0)],
        out_specs=[pl.BlockSpec((rows, LANES), lambda i: (0, i)),
                   pl.BlockSpec((None, n_b, sw), lambda i: (i, 0, 0))],
        out_shape=[jax.ShapeDtypeStruct((rows, hb * LANES), F32),
                   jax.ShapeDtypeStruct((hb, n_b, sw), F32)],
        scratch_shapes=[pltpu.VMEM((S5_F, S5_F), BF16)],
        compiler_params=_params(("arbitrary",), 52),
        name="s5_sample",
    )(z, dk, w_e, w_s, a_t, g_a, g_b, h0)


def _glu_blocks(glu_w):
    h, g, _ = glu_w.shape
    hb = h // S5_GB
    eye = jnp.eye(S5_GB, dtype=F32)

    def diag(wm):
        wm = wm.reshape(hb, S5_GB, g, g)
        return (eye[None, :, None, :, None] * wm[:, :, :, None, :]).reshape(hb, LANES, LANES).astype(BF16)

    return diag(glu_w[..., :g]), diag(glu_w[..., g:])


def _state_to_lanes(re, im):
    n, h, p = re.shape
    hb = h // S5_GB
    f = lambda v: v.reshape(n, hb, S5_GB * p).transpose(1, 0, 2)
    return jnp.concatenate([f(re), f(im)], axis=-1)


def _state_from_lanes(s):
    nl, hb, n, sw = s.shape
    half = sw // 2
    f = lambda v: v.transpose(0, 2, 1, 3).reshape(nl, n, hb * S5_GB, half // S5_GB)
    return f(s[..., :half]), f(s[..., half:])


def kernel(x_prompt, x_sample, cache_k, cache_v, state_ssm_re, state_ssm_im, p_prompt, p_sample,
           norm_mix, w_in, ssm_a_re, ssm_a_im, ssm_log_dt, ssm_b_re, ssm_b_im, ssm_c_re, ssm_c_im,
           ssm_d, ssm_glu_w, attn_sinks, rel_bias, norm_grp_ssm, norm_grp_attn, w_out, norm_ffn,
           ffn_w_gate, ffn_w_up, ffn_w_down, router_w, moe_w_gate, moe_w_up, moe_w_down,
           norm_ple, ple_w_gate, ple_w_proj, norm_final):
    nb_p, seq, d = x_prompt.shape
    nb_s, n_new, _ = x_sample.shape
    depth = w_in.shape[0]
    d_ssm = norm_grp_ssm.shape[-1]
    d_attn = norm_grp_attn.shape[-1]
    mp, ms = nb_p * seq, nb_s * n_new
    win = min(WINDOW, seq)
    assert d_ssm % LANES == 0 and seq % S5_T == 0 and n_new % S5_T == 0

    h = jnp.concatenate([x_prompt.reshape(mp, d), x_sample.reshape(ms, d)], axis=0)
    p_p, p_s = p_prompt.reshape(depth, mp, -1), p_sample.reshape(depth, ms, -1)

    n_grp = ssm_a_re.shape[1]
    hb = n_grp // S5_GB
    flat = lambda v: v.reshape((depth * n_grp,) + v.shape[2:])
    tabs = _s5_tables(flat(ssm_a_re), flat(ssm_a_im), ssm_log_dt.reshape(-1), flat(ssm_b_re), flat(ssm_b_im),
                      flat(ssm_c_re), flat(ssm_c_im), ssm_d.reshape(-1))
    g_a, g_b = _glu_blocks(flat(ssm_glu_w))

    kv_p, kv_s, fin_ps, fin_ss = [], [], [], []
    xn = _rmsnorm(h, norm_mix[0], BF16)
    for i in range(depth):
        z = _mm(xn, w_in, i, tm=1024, tn=1280, vmem_mib=56)
        ssm_p, fin_p = _s5_prompt(z, nb_p, seq, tabs, g_a, g_b, i, hb)
        ssm_s, fin_s = _s5_sample(z, mp, nb_s, n_new, tabs, g_a, g_b,
                                  _state_to_lanes(state_ssm_re[i], state_ssm_im[i]), i, hb)
        att_p = _attn_prompt(z, nb_p, seq, d_ssm, d_attn, rel_bias, attn_sinks[i])
        att_s = _attn_sample(z, mp, nb_s, n_new, d_ssm, d_attn, cache_k, cache_v, i, rel_bias,
                             attn_sinks[i])
        merged = _merge_norm(ssm_p, ssm_s, att_p, att_s, norm_grp_ssm[i], norm_grp_attn[i])
        if i % 2 == 0:
            h, hn = _mm_rows(merged, w_out, i, h, tm=512, gain=norm_ffn[i])
        else:
            h = _mm_rows(merged, w_out, i, h, tm=512)

        kv_p.append(jnp.stack([z[(b + 1) * seq - win:(b + 1) * seq, d_ssm + d_attn:] for b in range(nb_p)]))
        kv_s.append(z[mp:, d_ssm + d_attn:])
        fin_ps.append(fin_p.reshape(hb, nb_p, -1))
        fin_ss.append(fin_s)

        j = i // 2
        if i % 2 == 0:
            t = _swiglu(hn, ffn_w_gate, ffn_w_up, j, tm=1024, tn=512)
            h = _mm(t, ffn_w_down, j, tm=512, tn=512, res=h, vmem_mib=52)
        else:
            hn, e_idx, gates = _rmsnorm_router(h, norm_ffn[i], router_w[j])
            h = h + _moe_ffn(hn, e_idx, gates, moe_w_gate, moe_w_up, moe_w_down, j)

        hn = _rmsnorm(h, norm_ple[i], BF16)
        if i + 1 < depth:
            h, xn = _ple_rows(hn, p_p, p_s, ple_w_gate, ple_w_proj, h, i, norm_mix[i + 1], tm=512)
        else:
            h = _ple(hn, p_p, p_s, ple_w_gate, ple_w_proj, h, i, tm=1024, tn=1024, vmem_mib=58)

    y_p, y_s = _rmsnorm_split(h, norm_final, mp)
    kv_p = jnp.stack(kv_p).reshape(depth, nb_p, win, 2, KV_HEADS, HEAD_DIM)
    kv_s = jnp.stack(kv_s).reshape(depth, nb_s, n_new, 2, KV_HEADS, HEAD_DIM)
    p_re, p_im = _state_from_lanes(jnp.stack(fin_ps))
    s_re, s_im = _state_from_lanes(jnp.stack(fin_ss))
    return (y_p.reshape(nb_p, seq, d), y_s.reshape(nb_s, n_new, d),
            kv_p[:, :, :, 0], kv_p[:, :, :, 1], p_re, p_im,
            kv_s[:, :, :, 0], kv_s[:, :, :, 1], s_re, s_im)
```

```python
import functools
import math

import numpy as np
import jax
import jax.numpy as jnp
from jax import lax
from jax.experimental import pallas as pl
from jax.experimental.pallas import tpu as pltpu

F32 = jnp.float32
BF16 = jnp.bfloat16
HIGHEST = lax.Precision.HIGHEST

CHUNK = 64
WINDOW = 128
HEAD_DIM = 64
KV_HEADS = 4
SSM_GROUP_CH = 16
SSM_STATE = 64
N_BUCKETS = 32
REL_MAX_DIST = 128
EPS = 1e-6
NEG_INF = -1e30
ATTN_SCALE = HEAD_DIM ** -0.5

LANES = 128
S5_T = 16
S5_GB = LANES // SSM_GROUP_CH
S5_F = S5_T * LANES
MOE_TM = 512
MOE_SUB = 256
MIB = 1024 * 1024


def _pick(n, pref, mult=8):
    best = None
    for d in range(mult, min(n, pref) + 1, mult):
        if n % d == 0:
            best = d
    return best if best is not None else n


def _params(sem, vmem_mib):
    return pltpu.CompilerParams(dimension_semantics=sem, vmem_limit_bytes=int(vmem_mib * MIB))


def _rms(x, g):
    return x * lax.rsqrt(jnp.mean(x * x, axis=-1, keepdims=True) + EPS) * g


def _norm_body(x_ref, g_ref, o_ref):
    o_ref[...] = _rms(x_ref[...], g_ref[...]).astype(o_ref.dtype)


def _rmsnorm(x, g, out_dtype):
    m, d = x.shape
    tm = _pick(m, 512)
    return pl.pallas_call(
        _norm_body,
        grid=(m // tm,),
        in_specs=[pl.BlockSpec((tm, d), lambda i: (i, 0)), pl.BlockSpec((1, d), lambda i: (0, 0))],
        out_specs=pl.BlockSpec((tm, d), lambda i: (i, 0)),
        out_shape=jax.ShapeDtypeStruct((m, d), out_dtype),
        compiler_params=_params(("arbitrary",), 40),
        name="rmsnorm",
    )(x, g.reshape(1, d))


def _norm_split_body(x_ref, g_ref, op_ref, os_ref, *, np_tiles):
    y = _rms(x_ref[...], g_ref[...])

    @pl.when(pl.program_id(0) < np_tiles)
    def _():
        op_ref[...] = y

    @pl.when(pl.program_id(0) >= np_tiles)
    def _():
        os_ref[...] = y


def _rmsnorm_split(x, g, mp):
    m, d = x.shape
    ms = m - mp
    tm = _pick(math.gcd(mp, ms), 512)
    npt = mp // tm
    return pl.pallas_call(
        functools.partial(_norm_split_body, np_tiles=npt),
        grid=(m // tm,),
        in_specs=[pl.BlockSpec((tm, d), lambda i: (i, 0)), pl.BlockSpec((1, d), lambda i: (0, 0))],
        out_specs=[pl.BlockSpec((tm, d), lambda i: (jnp.minimum(i, npt - 1), 0)),
                   pl.BlockSpec((tm, d), lambda i: (jnp.maximum(i - npt, 0), 0))],
        out_shape=[jax.ShapeDtypeStruct((mp, d), F32), jax.ShapeDtypeStruct((ms, d), F32)],
        compiler_params=_params(("arbitrary",), 40),
        name="rmsnorm_split",
    )(x, g.reshape(1, d))


def _norm_router_body(x_ref, g_ref, rw_ref, o_ref, idx_ref, gate_ref):
    y = _rms(x_ref[...], g_ref[...])
    tm, d = y.shape
    pieces = d // LANES
    for s in range(pieces):
        o_ref[pl.ds(s, tm, stride=pieces), :] = y[:, s * LANES:(s + 1) * LANES]
    n_exp = rw_ref.shape[0]
    logits = jnp.concatenate([jnp.sum(y * rw_ref[e:e + 1, :], axis=-1, keepdims=True) for e in range(n_exp)],
                             axis=-1)
    lane = lax.broadcasted_iota(jnp.int32, logits.shape, 1)
    m1 = jnp.max(logits, axis=-1, keepdims=True)
    i1 = jnp.min(jnp.where(logits == m1, lane, n_exp), axis=-1, keepdims=True)
    rest = jnp.where(lane == i1, -jnp.inf, logits)
    m2 = jnp.max(rest, axis=-1, keepdims=True)
    i2 = jnp.min(jnp.where(rest == m2, lane, n_exp), axis=-1, keepdims=True)
    e2 = jnp.exp(m2 - m1)
    den = 1.0 + e2
    idx_ref[...] = jnp.concatenate([i1, i2], axis=-1)
    gate_ref[...] = jnp.concatenate([1.0 / den, e2 / den], axis=-1)


def _rmsnorm_router(x, g, router_w):
    m, d = x.shape
    n_exp = router_w.shape[-1]
    tm = _pick(m, 512)
    return pl.pallas_call(
        _norm_router_body,
        grid=(m // tm,),
        in_specs=[pl.BlockSpec((tm, d), lambda i: (i, 0)),
                  pl.BlockSpec((1, d), lambda i: (0, 0)),
                  pl.BlockSpec((n_exp, d), lambda i: (0, 0))],
        out_specs=[pl.BlockSpec((tm * (d // LANES), LANES), lambda i: (i, 0)),
                   pl.BlockSpec((tm, 2), lambda i: (i, 0)),
                   pl.BlockSpec((tm, 2), lambda i: (i, 0))],
        out_shape=[jax.ShapeDtypeStruct((m * (d // LANES), LANES), F32),
                   jax.ShapeDtypeStruct((m, 2), jnp.int32),
                   jax.ShapeDtypeStruct((m, 2), F32)],
        compiler_params=_params(("arbitrary",), 40),
        name="rmsnorm_router",
    )(x, g.reshape(1, d), router_w.T)


def _merge_norm_body(ap_ref, as_ref, bp_ref, bs_ref, ga_ref, gb_ref, o_ref, *, np_tiles):
    da = ap_ref.shape[-1]

    def emit(a_ref, b_ref):
        o_ref[:, :da] = _rms(a_ref[...], ga_ref[...]).astype(o_ref.dtype)
        o_ref[:, da:] = _rms(b_ref[...], gb_ref[...]).astype(o_ref.dtype)

    @pl.when(pl.program_id(0) < np_tiles)
    def _():
        emit(ap_ref, bp_ref)

    @pl.when(pl.program_id(0) >= np_tiles)
    def _():
        emit(as_ref, bs_ref)


def _merge_norm(a_p, a_s, b_p, b_s, ga, gb):
    (mp, da), (ms, db) = a_p.shape, b_s.shape
    tm = _pick(math.gcd(mp, ms), 512)
    npt, nst = mp // tm, ms // tm
    pspec = lambda w: pl.BlockSpec((tm, w), lambda i: (jnp.minimum(i, npt - 1), 0))
    sspec = lambda w: pl.BlockSpec((tm, w), lambda i: (jnp.maximum(i - npt, 0), 0))
    return pl.pallas_call(
        functools.partial(_merge_norm_body, np_tiles=npt),
        grid=(npt + nst,),
        in_specs=[pspec(da), sspec(da), pspec(db), sspec(db),
                  pl.BlockSpec((1, da), lambda i: (0, 0)), pl.BlockSpec((1, db), lambda i: (0, 0))],
        out_specs=pl.BlockSpec((tm, da + db), lambda i: (i, 0)),
        out_shape=jax.ShapeDtypeStruct((mp + ms, da + db), BF16),
        compiler_params=_params(("arbitrary",), 40),
        name="merge_norm",
    )(a_p, a_s, b_p, b_s, ga.reshape(1, da), gb.reshape(1, db))


def _mm_body(*refs, has_res):
    if has_res:
        x_ref, w_ref, r_ref, o_ref, wb_ref = refs
    else:
        x_ref, w_ref, o_ref, wb_ref = refs

    @pl.when(pl.program_id(1) == 0)
    def _():
        wb_ref[...] = w_ref[...].astype(BF16)

    acc = jnp.dot(x_ref[...], wb_ref[...], preferred_element_type=F32)
    if has_res:
        acc = r_ref[...] + acc
    o_ref[...] = acc.astype(o_ref.dtype)


def _mm(x, w, lyr, *, tm, tn, out_dtype=F32, res=None, vmem_mib=48):
    m, k = x.shape
    n = w.shape[-1]
    tm, tn = _pick(m, tm), _pick(n, tn, 128)
    in_specs = [pl.BlockSpec((tm, k), lambda j, i: (i, 0)),
                pl.BlockSpec((None, k, tn), lambda j, i: (lyr, 0, j))]
    args = [x, w]
    if res is not None:
        in_specs.append(pl.BlockSpec((tm, tn), lambda j, i: (i, j)))
        args.append(res)
    return pl.pallas_call(
        functools.partial(_mm_body, has_res=res is not None),
        grid=(n // tn, m // tm),
        in_specs=in_specs,
        out_specs=pl.BlockSpec((tm, tn), lambda j, i: (i, j)),
        out_shape=jax.ShapeDtypeStruct((m, n), out_dtype),
        scratch_shapes=[pltpu.VMEM((k, tn), BF16)],
        compiler_params=_params(("arbitrary", "arbitrary"), vmem_mib),
        name="mm_res" if res is not None else "mm",
    )(*args)


def _mm_rows_body(*refs, has_norm):
    if has_norm:
        x_ref, w_ref, r_ref, g_ref, o_ref, n_ref, wb_ref = refs
    else:
        x_ref, w_ref, r_ref, o_ref, wb_ref = refs

    @pl.when(pl.program_id(0) == 0)
    def _():
        wb_ref[...] = w_ref[...].astype(BF16)

    acc = r_ref[...] + jnp.dot(x_ref[...], wb_ref[...], preferred_element_type=F32)
    o_ref[...] = acc
    if has_norm:
        n_ref[...] = _rms(acc, g_ref[...]).astype(n_ref.dtype)


def _mm_rows(x, w, lyr, res, *, tm, gain=None, vmem_mib=56):
    m, k = x.shape
    n = w.shape[-1]
    tm = _pick(m, tm)
    row = lambda width: pl.BlockSpec((tm, width), lambda i: (i, 0))
    in_specs = [row(k), pl.BlockSpec((None, k, n), lambda i: (lyr, 0, 0), pipeline_mode=pl.Buffered(1)), row(n)]
    args = [x, w, res]
    out_specs, out_shape = [row(n)], [jax.ShapeDtypeStruct((m, n), F32)]
    if gain is not None:
        in_specs.append(pl.BlockSpec((1, n), lambda i: (0, 0)))
        args.append(gain.reshape(1, n))
        out_specs.append(row(n))
        out_shape.append(jax.ShapeDtypeStruct((m, n), BF16))
    out = pl.pallas_call(
        functools.partial(_mm_rows_body, has_norm=gain is not None),
        grid=(m // tm,),
        in_specs=in_specs,
        out_specs=out_specs,
        out_shape=out_shape,
        scratch_shapes=[pltpu.VMEM((k, n), BF16)],
        compiler_params=_params(("arbitrary",), vmem_mib),
        name="mm_rows",
    )(*args)
    return out if gain is not None else out[0]


def _swiglu_math(x, wg, wu):
    a = jnp.dot(x, wg, preferred_element_type=F32)
    b = jnp.dot(x, wu, preferred_element_type=F32)
    return (a * jax.nn.sigmoid(a) * b).astype(BF16)


def _swiglu_body(x_ref, wg_ref, wu_ref, o_ref, wgb_ref, wub_ref):
    @pl.when(pl.program_id(1) == 0)
    def _():
        wgb_ref[...] = wg_ref[...].astype(BF16)
        wub_ref[...] = wu_ref[...].astype(BF16)

    o_ref[...] = _swiglu_math(x_ref[...], wgb_ref[...], wub_ref[...])


def _swiglu(x, wg, wu, lyr, *, tm, tn, vmem_mib=48):
    m, k = x.shape
    n = wg.shape[-1]
    tm, tn = _pick(m, tm), _pick(n, tn, 128)
    wspec = pl.BlockSpec((None, k, tn), lambda j, i: (lyr, 0, j))
    return pl.pallas_call(
        _swiglu_body,
        grid=(n // tn, m // tm),
        in_specs=[pl.BlockSpec((tm, k), lambda j, i: (i, 0)), wspec, wspec],
        out_specs=pl.BlockSpec((tm, tn), lambda j, i: (i, j)),
        out_shape=jax.ShapeDtypeStruct((m, n), BF16),
        scratch_shapes=[pltpu.VMEM((k, tn), BF16), pltpu.VMEM((k, tn), BF16)],
        compiler_params=_params(("arbitrary", "arbitrary"), vmem_mib),
        name="swiglu",
    )(x, wg, wu)


def _ple_body(x_ref, pp_ref, ps_ref, wg_ref, wp_ref, r_ref, o_ref, wgb_ref, wpb_ref, pb_ref, *, np_tiles):
    i = pl.program_id(1)

    @pl.when(i == 0)
    def _():
        wgb_ref[...] = wg_ref[...].astype(BF16)
        wpb_ref[...] = wp_ref[...].astype(BF16)

    @pl.when(i < np_tiles)
    def _():
        pb_ref[...] = pp_ref[...].astype(BF16)

    @pl.when(i >= np_tiles)
    def _():
        pb_ref[...] = ps_ref[...].astype(BF16)

    a = jnp.dot(x_ref[...], wgb_ref[...], preferred_element_type=F32)
    b = jnp.dot(pb_ref[...], wpb_ref[...], preferred_element_type=F32)
    o_ref[...] = r_ref[...] + jax.nn.sigmoid(a) * b


def _ple_rows_body(x_ref, pp_ref, ps_ref, wg_ref, wp_ref, r_ref, g_ref, o_ref, n_ref,
                   wgb_ref, wpb_ref, pb_ref, *, np_tiles):
    i = pl.program_id(0)

    @pl.when(i == 0)
    def _():
        wgb_ref[...] = wg_ref[...].astype(BF16)
        wpb_ref[...] = wp_ref[...].astype(BF16)

    @pl.when(i < np_tiles)
    def _():
        pb_ref[...] = pp_ref[...].astype(BF16)

    @pl.when(i >= np_tiles)
    def _():
        pb_ref[...] = ps_ref[...].astype(BF16)

    a = jnp.dot(x_ref[...], wgb_ref[...], preferred_element_type=F32)
    b = jnp.dot(pb_ref[...], wpb_ref[...], preferred_element_type=F32)
    out = r_ref[...] + jax.nn.sigmoid(a) * b
    o_ref[...] = out
    n_ref[...] = _rms(out, g_ref[...]).astype(n_ref.dtype)


def _ple_rows(x, p_p, p_s, wg, wp, res, lyr, gain, *, tm, vmem_mib=58):
    m, k = x.shape
    mp, ms, kp = p_p.shape[1], p_s.shape[1], p_p.shape[-1]
    n = wg.shape[-1]
    tm = _pick(math.gcd(mp, ms), tm)
    npt = mp // tm
    row = lambda width: pl.BlockSpec((tm, width), lambda i: (i, 0))
    once = lambda rows: pl.BlockSpec((None, rows, n), lambda i: (lyr, 0, 0), pipeline_mode=pl.Buffered(1))
    return pl.pallas_call(
        functools.partial(_ple_rows_body, np_tiles=npt),
        grid=(m // tm,),
        in_specs=[row(k),
                  pl.BlockSpec((None, tm, kp), lambda i: (lyr, jnp.minimum(i, npt - 1), 0)),
                  pl.BlockSpec((None, tm, kp), lambda i: (lyr, jnp.maximum(i - npt, 0), 0)),
                  once(k), once(kp), row(n), pl.BlockSpec((1, n), lambda i: (0, 0))],
        out_specs=[row(n), row(n)],
        out_shape=[jax.ShapeDtypeStruct((m, n), F32), jax.ShapeDtypeStruct((m, n), BF16)],
        scratch_shapes=[pltpu.VMEM((k, n), BF16), pltpu.VMEM((kp, n), BF16), pltpu.VMEM((tm, kp), BF16)],
        compiler_params=_params(("arbitrary",), vmem_mib),
        name="ple_rows",
    )(x, p_p, p_s, wg, wp, res, gain.reshape(1, n))


def _ple(x, p_p, p_s, wg, wp, res, lyr, *, tm, tn, vmem_mib=48):
    m, k = x.shape
    mp, ms, kp = p_p.shape[1], p_s.shape[1], p_p.shape[-1]
    n = wg.shape[-1]
    tm, tn = _pick(math.gcd(mp, ms), tm), _pick(n, tn, 128)
    npt = mp // tm
    return pl.pallas_call(
        functools.partial(_ple_body, np_tiles=npt),
        grid=(n // tn, m // tm),
        in_specs=[pl.BlockSpec((tm, k), lambda j, i: (i, 0)),
                  pl.BlockSpec((None, tm, kp), lambda j, i: (lyr, jnp.minimum(i, npt - 1), 0)),
                  pl.BlockSpec((None, tm, kp), lambda j, i: (lyr, jnp.maximum(i - npt, 0), 0)),
                  pl.BlockSpec((None, k, tn), lambda j, i: (lyr, 0, j)),
                  pl.BlockSpec((None, kp, tn), lambda j, i: (lyr, 0, j)),
                  pl.BlockSpec((tm, tn), lambda j, i: (i, j))],
        out_specs=pl.BlockSpec((tm, tn), lambda j, i: (i, j)),
        out_shape=jax.ShapeDtypeStruct((m, n), F32),
        scratch_shapes=[pltpu.VMEM((k, tn), BF16), pltpu.VMEM((kp, tn), BF16), pltpu.VMEM((tm, kp), BF16)],
        compiler_params=_params(("arbitrary", "arbitrary"), vmem_mib),
        name="ple",
    )(x, p_p, p_s, wg, wp, res)


def _moe_weights(w_hbms, stages, bf16s, sem, e_ref, start_ref, nxt_ref, lyr):
    j, i = pl.program_id(0), pl.program_id(1)
    tn = stages[0].shape[1]

    def copies(e, col):
        col = pl.multiple_of(col, LANES)
        return [pltpu.make_async_copy(w.at[lyr, e, :, pl.ds(col, tn)], st, sem.at[n])
                for n, (w, st) in enumerate(zip(w_hbms, stages))]

    @pl.when(jnp.logical_and(j == 0, i == 0))
    def _():
        for cp in copies(e_ref[0], 0):
            cp.start()

    @pl.when(start_ref[i] == 1)
    def _():
        for cp in copies(e_ref[i], j * tn):
            cp.wait()
        rows = 128

        def cast_chunk(c, carry):
            r = pl.multiple_of(c * rows, rows)
            for st, wb in zip(stages, bf16s):
                wb[pl.ds(r, rows), :] = st[pl.ds(r, rows), :].astype(BF16)
            return carry
        lax.fori_loop(0, stages[0].shape[0] // rows, cast_chunk, 0)
        nxt = nxt_ref[i]

        @pl.when(nxt >= 0)
        def _():
            for cp in copies(nxt, j * tn):
                cp.start()

        @pl.when(jnp.logical_and(nxt < 0, j + 1 < pl.num_programs(0)))
        def _():
            for cp in copies(e_ref[0], (j + 1) * tn):
                cp.start()


def _active_prefix(act_ref, i, per):
    n_act = act_ref[i * per]
    for s in range(1, per):
        n_act = n_act + act_ref[i * per + s]
    return n_act


def _prefix_rows(o_ref, n_act, fn):
    tm = o_ref.shape[0]
    for c in range(tm // MOE_SUB + 1):
        rows = c * MOE_SUB

        @pl.when(n_act == c)
        def _():
            if rows:
                o_ref[:rows, :] = fn(rows)
            if rows < tm:
                o_ref[rows:, :] = jnp.zeros((tm - rows, o_ref.shape[1]), o_ref.dtype)


def _moe_up_body(e_ref, act_ref, start_ref, nxt_ref, x_ref, wg_hbm, wu_hbm, o_ref,
                 wgb_ref, wub_ref, sg_ref, su_ref, sem, *, lyr):
    _moe_weights((wg_hbm, wu_hbm), (sg_ref, su_ref), (wgb_ref, wub_ref), sem, e_ref, start_ref, nxt_ref, lyr)
    n_act = _active_prefix(act_ref, pl.program_id(1), x_ref.shape[0] // MOE_SUB)
    _prefix_rows(o_ref, n_act, lambda rows: _swiglu_math(x_ref[:rows, :], wgb_ref[...], wub_ref[...]))


def _moe_down_body(e_ref, act_ref, start_ref, nxt_ref, h_ref, wd_hbm, o_ref, wdb_ref, sd_ref, sem, *, lyr):
    _moe_weights((wd_hbm,), (sd_ref,), (wdb_ref,), sem, e_ref, start_ref, nxt_ref, lyr)
    n_act = _active_prefix(act_ref, pl.program_id(1), h_ref.shape[0] // MOE_SUB)
    _prefix_rows(o_ref, n_act,
                 lambda rows: jnp.dot(h_ref[:rows, :], wdb_ref[...], preferred_element_type=F32))


def _moe_matmul(body, x, ws, sched, lyr, *, tn, out_dtype, vmem_mib, name):
    ns, k = x.shape
    n = ws[0].shape[-1]
    tm, tn = MOE_TM, _pick(n, tn, 128)
    grid_spec = pltpu.PrefetchScalarGridSpec(
        num_scalar_prefetch=len(sched),
        grid=(n // tn, ns // tm),
        in_specs=[pl.BlockSpec((tm, k), lambda j, i, *_: (i, 0))]
                 + [pl.BlockSpec(memory_space=pl.ANY)] * len(ws),
        out_specs=pl.BlockSpec((tm, tn), lambda j, i, *_: (i, j)),
        scratch_shapes=[pltpu.VMEM((k, tn), BF16)] * len(ws) + [pltpu.VMEM((k, tn), F32)] * len(ws)
                       + [pltpu.SemaphoreType.DMA((len(ws),))],
    )
    return pl.pallas_call(
        functools.partial(body, lyr=lyr),
        grid_spec=grid_spec,
        out_shape=jax.ShapeDtypeStruct((ns, n), out_dtype),
        compiler_params=_params(("arbitrary", "arbitrary"), vmem_mib),
        name=name,
    )(*sched, x, *ws)


def _row_gather_copy(x_hbm, buf, sem, slot, src_row, dst_row, rows):
    return pltpu.make_async_copy(x_hbm.at[pl.ds(src_row, rows)], buf.at[slot, pl.ds(dst_row, rows)],
                                 sem.at[slot])


GATHER_GROUP = 8


def _moe_gather_body(tok_ref, real_ref, x_hbm, o_ref, buf, sem):
    i = pl.program_id(0)
    tm = o_ref.shape[0]
    pieces = o_ref.shape[1] // LANES
    grp_rows = GATHER_GROUP * pieces

    def groups(blk):
        return (real_ref[blk] + GATHER_GROUP - 1) // GATHER_GROUP

    def issue(blk, slot):
        def body(g, carry):
            for u in range(GATHER_GROUP):
                r = g * GATHER_GROUP + u
                _row_gather_copy(x_hbm, buf, sem, slot, tok_ref[blk * tm + r] * pieces, r * pieces,
                                 pieces).start(priority=u % 2)
            return carry
        lax.fori_loop(0, groups(blk), body, 0)

    @pl.when(i == 0)
    def _():
        issue(0, 0)

    @pl.when(i + 1 < pl.num_programs(0))
    def _():
        issue(i + 1, (i + 1) % 2)

    slot = i % 2
    cur = buf.at[slot]
    n_grp = groups(i)

    @pl.when(n_grp > 0)
    def _():
        _row_gather_copy(x_hbm, buf, sem, slot, 0, 0, n_grp * grp_rows).wait()

    def zero(g, carry):
        cur[pl.ds(pl.multiple_of(g * grp_rows, grp_rows), grp_rows), :] = jnp.zeros((grp_rows, LANES), F32)
        return carry
    lax.fori_loop(n_grp, tm // GATHER_GROUP, zero, 0)

    o_ref[...] = jnp.concatenate([cur[pl.ds(s, tm, stride=pieces), :] for s in range(pieces)],
                                 axis=-1).astype(o_ref.dtype)


def _moe_gather(x_rows, slot_tok, blk_real, d):
    n_slots = slot_tok.shape[0]
    tm = MOE_TM
    grid_spec = pltpu.PrefetchScalarGridSpec(
        num_scalar_prefetch=2,
        grid=(n_slots // tm,),
        in_specs=[pl.BlockSpec(memory_space=pl.ANY)],
        out_specs=pl.BlockSpec((tm, d), lambda i, *_: (i, 0)),
        scratch_shapes=[pltpu.VMEM((2, tm * (d // LANES), LANES), F32), pltpu.SemaphoreType.DMA((2,))],
    )
    return pl.pallas_call(
        _moe_gather_body,
        grid_spec=grid_spec,
        out_shape=jax.ShapeDtypeStruct((n_slots, d), BF16),
        compiler_params=_params(("arbitrary",), 32),
        name="moe_gather",
    )(slot_tok, blk_real, x_rows)


def _moe_ffn(hn_rows, e_idx, gates, wg, wu, wd, lyr):
    m = e_idx.shape[0]
    d = wg.shape[2]
    n_exp = wg.shape[1]
    top_k = e_idx.shape[-1]
    tm = MOE_TM
    n_asg = m * top_k
    e_flat = e_idx.reshape(n_asg)
    onehot = (e_flat[None, :] == jnp.arange(n_exp, dtype=jnp.int32)[:, None]).astype(jnp.int32)
    counts = jnp.sum(onehot, axis=1)
    start = jnp.cumsum(counts) - counts
    padded = (counts + tm - 1) // tm * tm
    pad_end = jnp.cumsum(padded)
    pad_start = pad_end - padded
    order = jnp.argsort(e_flat).astype(jnp.int32)
    rank = jnp.argsort(order).astype(jnp.int32)
    slot_of = rank + jnp.sum(onehot * (pad_start - start)[:, None], axis=0).astype(jnp.int32)
    n_blocks = n_asg // tm + n_exp
    n_slots = n_blocks * tm
    n_used = (pad_end[-1] // tm).astype(jnp.int32)
    blk = jnp.arange(n_blocks, dtype=jnp.int32)
    blk_start = jnp.minimum(blk, n_used - 1) * tm
    blk_expert = jnp.sum((pad_end[None, :] <= blk_start[:, None]).astype(jnp.int32), axis=1)
    blk_expert = jnp.minimum(blk_expert, n_exp - 1).astype(jnp.int32)
    sub_start = jnp.arange(n_slots // MOE_SUB, dtype=jnp.int32) * MOE_SUB
    sub_expert = jnp.repeat(blk_expert, tm // MOE_SUB)
    real_end = (pad_start + counts)[sub_expert]
    sub_active = jnp.logical_and(sub_start < real_end, sub_start < pad_end[-1]).astype(jnp.int32)
    slot_r = jnp.arange(n_slots, dtype=jnp.int32) - jnp.repeat(pad_start[sub_expert], MOE_SUB)
    slot_src = jnp.repeat(start[sub_expert], MOE_SUB) + slot_r
    slot_real = jnp.logical_and(slot_r >= 0, slot_r < jnp.repeat(counts[sub_expert], MOE_SUB))
    slot_tok = jnp.where(slot_real, order[jnp.clip(slot_src, 0, n_asg - 1)] // top_k, 0).astype(jnp.int32)

    run_start = jnp.concatenate([jnp.ones((1,), jnp.int32),
                                 (blk_expert[1:] != blk_expert[:-1]).astype(jnp.int32)])
    later_start = jnp.where(run_start == 1, blk, n_blocks)
    nxt_blk = lax.cummin(jnp.concatenate([later_start[1:], jnp.full((1,), n_blocks, jnp.int32)]),
                         axis=0, reverse=True)
    run_next = jnp.where(nxt_blk < n_blocks, blk_expert[jnp.minimum(nxt_blk, n_blocks - 1)], -1)
    sched = (blk_expert, sub_active, run_start, run_next.astype(jnp.int32))

    blk_real = jnp.clip((pad_start + counts)[blk_expert] - blk * tm, 0, tm)
    blk_real = jnp.where(blk < n_used, blk_real, 0).astype(jnp.int32)
    x_sorted = _moe_gather(hn_rows, slot_tok, blk_real, d)
    h_sorted = _moe_matmul(_moe_up_body, x_sorted, (wg, wu), sched, lyr, tn=1024, out_dtype=BF16,
                           vmem_mib=48, name="moe_up")
    y_sorted = _moe_matmul(_moe_down_body, h_sorted, (wd,), sched, lyr, tn=512, out_dtype=F32,
                           vmem_mib=48, name="moe_down")
    sl = slot_of.reshape(m, top_k)
    y = y_sorted[sl[:, 0]] * gates[:, 0:1]
    for t in range(1, top_k):
        y = y + y_sorted[sl[:, t]] * gates[:, t:t + 1]
    return y


def _rel_buckets(n_q, n_k_past, n_k):
    rel = (np.arange(n_k) - n_k_past)[None, :] - np.arange(n_q)[:, None]
    half = N_BUCKETS // 2
    max_exact = half // 2
    n = np.abs(rel)
    nf = np.maximum(n, 1).astype(np.float32)
    large = max_exact + (np.log(nf / np.float32(max_exact)) / np.float32(math.log(REL_MAX_DIST / max_exact))
                         * np.float32(half - max_exact)).astype(np.int32)
    large = np.minimum(large, half - 1)
    return (np.where(rel > 0, half, 0) + np.where(n < max_exact, n, large)).astype(np.int32)


def _build_bias(bias_ref, sink_ref, bucket_ref, table_ref, sinks_ref):
    bucket = bucket_ref[...]
    nq = bucket.shape[1]
    gq = bias_ref.shape[2] // nq
    for kh in range(bias_ref.shape[0]):
        for g in range(gq):
            n = kh * gq + g
            acc = jnp.zeros(bucket.shape, F32)
            for b in range(N_BUCKETS):
                acc = jnp.where(bucket == b, table_ref[b, n], acc)
            bias_ref[kh, :, g * nq:(g + 1) * nq] = acc
            sink_ref[kh, :, g * nq:(g + 1) * nq] = jnp.full((1, nq), sinks_ref[n], F32)


def _attn_items(items, bias_ref, sink_ref):
    nq = items[0][0].shape[0]
    gq = items[0][0].shape[-1] // HEAD_DIM // KV_HEADS
    head = lambda x, n: x[:, n * HEAD_DIM:(n + 1) * HEAD_DIM]
    scores = []
    for q, k, _, valid in items:
        for kh in range(KV_HEADS):
            q4 = jnp.concatenate([head(q, kh * gq + g) for g in range(gq)], axis=0)
            s = lax.dot_general(head(k, kh), q4, (((1,), (1,)), ((), ())), preferred_element_type=F32)
            scores.append((s, kh, valid))
    probs = []
    for s, kh, valid in scores:
        s = s + bias_ref[kh]
        if valid is not None:
            s = jnp.where(valid, s, NEG_INF)
        sink = sink_ref[kh]
        mx = jnp.maximum(jnp.max(s, axis=0, keepdims=True), sink)
        e = jnp.exp(s - mx)
        den = jnp.sum(e, axis=0, keepdims=True) + jnp.exp(sink - mx)
        probs.append((e * (1.0 / den)).astype(BF16))
    outs = []
    for i, (_, _, v, _) in enumerate(items):
        heads = []
        for kh in range(KV_HEADS):
            o4 = lax.dot_general(probs[i * KV_HEADS + kh], head(v, kh), (((0,), (0,)), ((), ())),
                                 preferred_element_type=F32)
            heads.extend(o4[g * nq:(g + 1) * nq] for g in range(gq))
        outs.append(jnp.concatenate(heads, axis=-1))
    return outs


def _attn_prompt_body(q_ref, kp_ref, kc_ref, vp_ref, vc_ref, bucket_ref, table_ref, sinks_ref,
                      o_ref, bias_ref, sink_ref, *, chunks):
    j = pl.program_id(1)

    @pl.when(jnp.logical_and(pl.program_id(0) == 0, j == 0))
    def _():
        _build_bias(bias_ref, sink_ref, bucket_ref, table_ref, sinks_ref)

    q = (q_ref[...] * ATTN_SCALE).astype(BF16)
    k = jnp.concatenate([kp_ref[...], kc_ref[...]], axis=0).astype(BF16)
    v = jnp.concatenate([vp_ref[...], vc_ref[...]], axis=0).astype(BF16)
    band = WINDOW + CHUNK
    items = []
    for c in range(chunks):
        lo = c * CHUNK
        key_valid = None
        if lo < WINDOW:
            key_pos = j * (chunks * CHUNK) + lo - WINDOW + lax.broadcasted_iota(jnp.int32, (band, 1), 0)
            key_valid = key_pos >= 0
        items.append((q[lo:lo + CHUNK], k[lo:lo + band], v[lo:lo + band], key_valid))
    for c, o in enumerate(_attn_items(items, bias_ref, sink_ref)):
        o_ref[c * CHUNK:(c + 1) * CHUNK, :] = o


def _attn_prompt(z, n_b, seq, d_ssm, d_attn, table, sinks):
    d_kv = KV_HEADS * HEAD_DIM
    assert d_ssm % d_attn == 0 and d_ssm % d_kv == 0 and d_attn % d_kv == 0
    qb = _pick(seq, 4 * CHUNK, 2 * CHUNK)
    assert qb % WINDOW == 0 and seq % qb == 0
    chunks = qb // CHUNK
    nj = seq // qb
    r = qb // WINDOW
    kcol = (d_ssm + d_attn) // d_kv
    gq = d_attn // HEAD_DIM // KV_HEADS
    band = WINDOW + CHUNK
    bucket = jnp.asarray(_rel_buckets(CHUNK, WINDOW, band).T)

    def prev(b, j):
        return b * (seq // WINDOW) + jnp.maximum(j * r - 1, 0)

    return pl.pallas_call(
        functools.partial(_attn_prompt_body, chunks=chunks),
        grid=(n_b, nj),
        in_specs=[pl.BlockSpec((qb, d_attn), lambda b, j: (b * nj + j, d_ssm // d_attn)),
                  pl.BlockSpec((WINDOW, d_kv), lambda b, j: (prev(b, j), kcol)),
                  pl.BlockSpec((qb, d_kv), lambda b, j: (b * nj + j, kcol)),
                  pl.BlockSpec((WINDOW, d_kv), lambda b, j: (prev(b, j), kcol + 1)),
                  pl.BlockSpec((qb, d_kv), lambda b, j: (b * nj + j, kcol + 1)),
                  pl.BlockSpec((band, CHUNK), lambda b, j: (0, 0)),
                  pl.BlockSpec(memory_space=pltpu.SMEM),
                  pl.BlockSpec(memory_space=pltpu.SMEM)],
        out_specs=pl.BlockSpec((qb, d_attn), lambda b, j: (b * nj + j, 0)),
        out_shape=jax.ShapeDtypeStruct((n_b * seq, d_attn), F32),
        scratch_shapes=[pltpu.VMEM((KV_HEADS, band, gq * CHUNK), F32),
                        pltpu.VMEM((KV_HEADS, 1, gq * CHUNK), F32)],
        compiler_params=_params(("arbitrary", "arbitrary"), 40),
        name="attn_prompt",
    )(z, z, z, z, z, bucket, table, sinks)


def _attn_sample_body(q_ref, kc_ref, kn_ref, vc_ref, vn_ref, bucket_ref, table_ref, sinks_ref,
                      o_ref, bias_ref, sink_ref, *, n_bs, n_new, n_past):
    @pl.when(pl.program_id(0) == 0)
    def _():
        _build_bias(bias_ref, sink_ref, bucket_ref, table_ref, sinks_ref)

    q = (q_ref[...] * ATTN_SCALE).astype(BF16)
    kc, kn = kc_ref[...].astype(BF16), kn_ref[...].astype(BF16)
    vc, vn = vc_ref[...].astype(BF16), vn_ref[...].astype(BF16)
    items = []
    for b in range(n_bs):
        new = slice(b * n_new, (b + 1) * n_new)
        past = slice(b * n_past, (b + 1) * n_past)
        items.append((q[new], jnp.concatenate([kc[past], kn[new]], axis=0),
                      jnp.concatenate([vc[past], vn[new]], axis=0), None))
    for b, o in enumerate(_attn_items(items, bias_ref, sink_ref)):
        o_ref[b * n_new:(b + 1) * n_new, :] = o


def _attn_sample(z, row0, n_b, n_new, d_ssm, d_attn, cache_k, cache_v, lyr, table, sinks):
    d_kv = KV_HEADS * HEAD_DIM
    n_past = cache_k.shape[2]
    n_bs = _pick(n_b, 4, 1)
    rows = n_bs * n_new
    assert row0 % rows == 0
    kcol = (d_ssm + d_attn) // d_kv
    gq = d_attn // HEAD_DIM // KV_HEADS
    n_k = n_past + n_new
    bucket = jnp.asarray(_rel_buckets(n_new, n_past, n_k).T)
    ck = cache_k.reshape(-1, d_kv)
    cv = cache_v.reshape(-1, d_kv)
    rb = row0 // rows
    nsteps = n_b // n_bs
    cache_spec = pl.BlockSpec((n_bs * n_past, d_kv), lambda b: (lyr * nsteps + b, 0))
    return pl.pallas_call(
        functools.partial(_attn_sample_body, n_bs=n_bs, n_new=n_new, n_past=n_past),
        grid=(nsteps,),
        in_specs=[pl.BlockSpec((rows, d_attn), lambda b: (rb + b, d_ssm // d_attn)),
                  cache_spec,
                  pl.BlockSpec((rows, d_kv), lambda b: (rb + b, kcol)),
                  cache_spec,
                  pl.BlockSpec((rows, d_kv), lambda b: (rb + b, kcol + 1)),
                  pl.BlockSpec((n_k, n_new), lambda b: (0, 0)),
                  pl.BlockSpec(memory_space=pltpu.SMEM),
                  pl.BlockSpec(memory_space=pltpu.SMEM)],
        out_specs=pl.BlockSpec((rows, d_attn), lambda b: (b, 0)),
        out_shape=jax.ShapeDtypeStruct((n_b * n_new, d_attn), F32),
        scratch_shapes=[pltpu.VMEM((KV_HEADS, n_k, gq * n_new), F32),
                        pltpu.VMEM((KV_HEADS, 1, gq * n_new), F32)],
        compiler_params=_params(("arbitrary",), 40),
        name="attn_sample",
    )(z, ck, z, cv, z, bucket, table, sinks)


def _tile_lanes(x, reps):
    return jnp.concatenate([x] * reps, axis=-1)


def _s5_tables_body(are_ref, aim_ref, ldt_ref, btr_ref, bti_ref, cre_ref, cim_ref, d_ref,
                    dk_ref, we_ref, ws_ref, a_ref):
    rows, p = cre_ref.shape
    gb = rows // SSM_GROUP_CH
    a_re, a_im = are_ref[...], aim_ref[...]
    dt = jnp.exp(ldt_ref[...])
    mag = jnp.exp(a_re * dt)
    ab_re, ab_im = mag * jnp.cos(a_im * dt), mag * jnp.sin(a_im * dt)

    x, y = ab_re - 1.0, ab_im
    den = a_re * a_re + a_im * a_im
    co_re = (x * a_re + y * a_im) / den
    co_im = (y * a_re - x * a_im) / den
    bb_re = co_re * btr_ref[...] - co_im * bti_ref[...]
    bb_im = co_re * bti_ref[...] + co_im * btr_ref[...]
    c_re, c_im = cre_ref[...], cim_ref[...]

    r_idx = lax.broadcasted_iota(jnp.int32, (rows, 1), 0)
    own_state = (r_idx // SSM_GROUP_CH) == (lax.broadcasted_iota(jnp.int32, (1, gb * p), 1) // p)
    same_group = (r_idx // SSM_GROUP_CH) == (lax.broadcasted_iota(jnp.int32, (1, rows), 1) // SSM_GROUP_CH)
    diag = r_idx == lax.broadcasted_iota(jnp.int32, (1, rows), 1)

    def place(v_re, v_im):
        return jnp.concatenate([jnp.where(own_state, _tile_lanes(v_re, gb), 0.0),
                                jnp.where(own_state, _tile_lanes(v_im, gb), 0.0)], axis=-1)

    dims = (((1,), (1,)), ((), ()))
    bb_cat = jnp.concatenate([bb_re, -bb_im], axis=-1)
    pw_re, pw_im = jnp.ones_like(ab_re), jnp.zeros_like(ab_re)
    for j in range(S5_T + 1):
        ca_re = c_re * pw_re - c_im * pw_im
        ca_im = c_re * pw_im + c_im * pw_re
        if j < S5_T:
            kj = lax.dot_general(bb_cat, jnp.concatenate([ca_re, ca_im], axis=-1), dims,
                                 precision=HIGHEST, preferred_element_type=F32)
            kj = jnp.where(same_group, kj, 0.0)
            if j == 0:
                kj = kj + jnp.where(diag, d_ref[...], 0.0)
            dk_ref[j] = kj.astype(dk_ref.dtype)
            s = S5_T - 1 - j
            we_ref[s * rows:(s + 1) * rows, :] = place(
                pw_re * bb_re - pw_im * bb_im, pw_re * bb_im + pw_im * bb_re).astype(we_ref.dtype)
        if j >= 1:
            ws_ref[(j - 1) * rows:j * rows, :] = place(ca_re, -ca_im).astype(ws_ref.dtype)
        if j == S5_T:
            first = r_idx == (lax.broadcasted_iota(jnp.int32, (1, gb * p), 1) // p) * SSM_GROUP_CH
            a_ref[...] = jnp.concatenate(
                [jnp.sum(jnp.where(first, _tile_lanes(pw_re, gb), 0.0), axis=0, keepdims=True),
                 jnp.sum(jnp.where(first, _tile_lanes(pw_im, gb), 0.0), axis=0, keepdims=True)], axis=-1)
        pw_re, pw_im = pw_re * ab_re - pw_im * ab_im, pw_re * ab_im + pw_im * ab_re


def _s5_tables(a_re, a_im, log_dt, b_re, b_im, c_re, c_im, d_skip):
    h, p = a_re.shape
    g = SSM_GROUP_CH
    hb = h // S5_GB
    sw = 2 * S5_GB * p
    rep = lambda v: jnp.repeat(v, g, axis=0).reshape(hb, LANES, -1)
    blk = lambda w: pl.BlockSpec((None, LANES, w), lambda i: (i, 0, 0))
    wide = pl.BlockSpec((None, S5_F, sw), lambda i: (i, 0, 0))
    return pl.pallas_call(
        _s5_tables_body,
        grid=(hb,),
        in_specs=[blk(p), blk(p), blk(1), blk(p), blk(p), blk(p), blk(p), blk(1)],
        out_specs=[pl.BlockSpec((None, S5_T, LANES, LANES), lambda i: (i, 0, 0, 0)), wide, wide,
                   pl.BlockSpec((None, 1, sw), lambda i: (i, 0, 0))],
        out_shape=[jax.ShapeDtypeStruct((hb, S5_T, LANES, LANES), BF16),
                   jax.ShapeDtypeStruct((hb, S5_F, sw), BF16),
                   jax.ShapeDtypeStruct((hb, S5_F, sw), BF16),
                   jax.ShapeDtypeStruct((hb, 1, sw), F32)],
        compiler_params=_params(("arbitrary",), 40),
        name="s5_tables",
    )(rep(a_re), rep(a_im), rep(log_dt.reshape(h, 1)),
      jnp.swapaxes(b_re, 1, 2).reshape(hb, LANES, p), jnp.swapaxes(b_im, 1, 2).reshape(hb, LANES, p),
      c_re.reshape(hb, LANES, p), c_im.reshape(hb, LANES, p), d_skip.reshape(hb, LANES, 1))


def _split_mul(a, p):
    lane = lax.broadcasted_iota(jnp.int32, a.shape, a.ndim - 1)
    sw = pltpu.roll(a, p, axis=a.ndim - 1)
    return jnp.where(lane < p, a, sw), jnp.where(lane < p, -sw, a)


def _cmul(x, mr, mi, p):
    return x * mr + pltpu.roll(x, p, axis=x.ndim - 1) * mi


def _build_toeplitz(m_ref, dk_ref):
    m_ref[...] = jnp.zeros_like(m_ref)
    for s in range(S5_T):
        for t in range(s, S5_T):
            m_ref[s * LANES:(s + 1) * LANES, t * LANES:(t + 1) * LANES] = dk_ref[t - s]


def _s5_outputs(ucat, enter, m_ref, ws_ref):
    return (jnp.dot(ucat, m_ref[...], preferred_element_type=F32)
            + lax.dot_general(enter.astype(BF16), ws_ref[...], (((1,), (1,)), ((), ())),
                              preferred_element_type=F32))


def _glu_rows(o_ref, ga_ref, gb_ref, tile):
    ga, gb = ga_ref[...], gb_ref[...]
    tile = _pick(o_ref.shape[0], tile)
    for r0 in range(0, o_ref.shape[0], tile):
        y = o_ref[r0:r0 + tile, :].astype(BF16)
        o_ref[r0:r0 + tile, :] = (jnp.dot(y, ga, preferred_element_type=F32)
                                  * jax.nn.sigmoid(jnp.dot(y, gb, preferred_element_type=F32)))


def _s5_prompt_body(z_ref, dk_ref, we_ref, ws_ref, a_ref, ga_ref, gb_ref, o_ref, fin_ref, m_ref,
                    *, nb, nc):
    @pl.when(pl.program_id(1) == 0)
    def _():
        _build_toeplitz(m_ref, dk_ref)

    r = nb * nc
    p = a_ref.shape[-1] // 2
    ucat = jnp.concatenate([z_ref[pl.ds(t, r, stride=S5_T), :].astype(BF16) for t in range(S5_T)], axis=-1)
    x = jnp.dot(ucat, we_ref[...], preferred_element_type=F32)
    a = a_ref[...]
    pos = lax.rem(lax.broadcasted_iota(jnp.int32, (r, 1), 0), nc)
    sh = 1
    while sh < nc:
        mr, mi = _split_mul(a, p)
        shifted = jnp.where(pos >= sh, pltpu.roll(x, sh, axis=0), 0.0)
        x = x + _cmul(shifted, mr, mi, p)
        a = _cmul(a, mr, mi, p)
        sh *= 2
    enter = jnp.where(pos >= 1, pltpu.roll(x, 1, axis=0), 0.0)
    fin_ref[...] = jnp.concatenate([x[(b + 1) * nc - 1:(b + 1) * nc] for b in range(nb)], axis=0)
    y = _s5_outputs(ucat, enter, m_ref, ws_ref)
    for t in range(S5_T):
        o_ref[pl.ds(t, r, stride=S5_T), :] = y[:, t * LANES:(t + 1) * LANES]
    _glu_rows(o_ref, ga_ref, gb_ref, 512)


def _s5_sample_body(z_ref, dk_ref, we_ref, ws_ref, a_ref, ga_ref, gb_ref, h0_ref, o_ref, fin_ref, m_ref,
                    *, nb, nc):
    _build_toeplitz(m_ref, dk_ref)
    p = a_ref.shape[-1] // 2
    seq = nc * S5_T
    ucat = jnp.concatenate(
        [jnp.concatenate([z_ref[pl.ds(c * S5_T + t, nb, stride=seq), :].astype(BF16) for t in range(S5_T)],
                         axis=-1) for c in range(nc)], axis=0)
    e = jnp.dot(ucat, we_ref[...], preferred_element_type=F32)
    mr, mi = _split_mul(a_ref[...], p)
    state = h0_ref[...]
    enter = []
    for c in range(nc):
        enter.append(state)
        state = _cmul(state, mr, mi, p) + e[c * nb:(c + 1) * nb]
    fin_ref[...] = state
    y = _s5_outputs(ucat, jnp.concatenate(enter, axis=0), m_ref, ws_ref)
    for c in range(nc):
        for t in range(S5_T):
            o_ref[pl.ds(c * S5_T + t, nb, stride=seq), :] = y[c * nb:(c + 1) * nb, t * LANES:(t + 1) * LANES]
    _glu_rows(o_ref, ga_ref, gb_ref, 512)


def _s5_prompt(z, n_b, seq, tabs, g_a, g_b, lyr, hb):
    dk, w_e, w_s, a_t = tabs
    sw = a_t.shape[-1]
    nbs = _pick(n_b, 2, 1)
    nc = seq // S5_T
    tab = lambda arr: pl.BlockSpec((None,) + arr.shape[1:],
                                   lambda i, b: (lyr * hb + i,) + (0,) * (arr.ndim - 1))
    return pl.pallas_call(
        functools.partial(_s5_prompt_body, nb=nbs, nc=nc),
        grid=(hb, n_b // nbs),
        in_specs=[pl.BlockSpec((nbs * seq, LANES), lambda i, b: (b, i)),
                  tab(dk), tab(w_e), tab(w_s), tab(a_t), tab(g_a), tab(g_b)],
        out_specs=[pl.BlockSpec((nbs * seq, LANES), lambda i, b: (b, i)),
                   pl.BlockSpec((None, None, nbs, sw), lambda i, b: (i, b, 0, 0))],
        out_shape=[jax.ShapeDtypeStruct((n_b * seq, hb * LANES), F32),
                   jax.ShapeDtypeStruct((hb, n_b // nbs, nbs, sw), F32)],
        scratch_shapes=[pltpu.VMEM((S5_F, S5_F), BF16)],
        compiler_params=_params(("arbitrary", "arbitrary"), 52),
        name="s5_prompt",
    )(z, dk, w_e, w_s, a_t, g_a, g_b)


def _s5_sample(z, row0, n_b, n_new, tabs, g_a, g_b, h0, lyr, hb):
    dk, w_e, w_s, a_t = tabs
    sw = a_t.shape[-1]
    rows = n_b * n_new
    assert row0 % rows == 0
    rb = row0 // rows
    tab = lambda arr, off=lyr * hb: pl.BlockSpec((None,) + arr.shape[1:],
                                                 lambda i: (off + i,) + (0,) * (arr.ndim - 1))
    return pl.pallas_call(
        functools.partial(_s5_sample_body, nb=n_b, nc=n_new // S5_T),
        grid=(hb,),
        in_specs=[pl.BlockSpec((rows, LANES), lambda i: (rb, i)),
                  tab(dk), tab(w_e), tab(w_s), tab(a_t), tab(g_a), tab(g_b), tab(h0, 0)],
        out_specs=[pl.BlockSpec((rows, LANES), lambda i: (0, i)),
                   pl.BlockSpec((None, n_b, sw), lambda i: (i, 0, 0))],
        out_shape=[jax.ShapeDtypeStruct((rows, hb * LANES), F32),
                   jax.ShapeDtypeStruct((hb, n_b, sw), F32)],
        scratch_shapes=[pltpu.VMEM((S5_F, S5_F), BF16)],
        compiler_params=_params(("arbitrary",), 52),
        name="s5_sample",
    )(z, dk, w_e, w_s, a_t, g_a, g_b, h0)


def _glu_blocks(glu_w):
    h, g, _ = glu_w.shape
    hb = h // S5_GB
    eye = jnp.eye(S5_GB, dtype=F32)

    def diag(wm):
        wm = wm.reshape(hb, S5_GB, g, g)
        return (eye[None, :, None, :, None] * wm[:, :, :, None, :]).reshape(hb, LANES, LANES).astype(BF16)

    return diag(glu_w[..., :g]), diag(glu_w[..., g:])


def _state_to_lanes(re, im):
    n, h, p = re.shape
    hb = h // S5_GB
    f = lambda v: v.reshape(n, hb, S5_GB * p).transpose(1, 0, 2)
    return jnp.concatenate([f(re), f(im)], axis=-1)


def _state_from_lanes(s):
    nl, hb, n, sw = s.shape
    half = sw // 2
    f = lambda v: v.transpose(0, 2, 1, 3).reshape(nl, n, hb * S5_GB, half // S5_GB)
    return f(s[..., :half]), f(s[..., half:])


def kernel(x_prompt, x_sample, cache_k, cache_v, state_ssm_re, state_ssm_im, p_prompt, p_sample,
           norm_mix, w_in, ssm_a_re, ssm_a_im, ssm_log_dt, ssm_b_re, ssm_b_im, ssm_c_re, ssm_c_im,
           ssm_d, ssm_glu_w, attn_sinks, rel_bias, norm_grp_ssm, norm_grp_attn, w_out, norm_ffn,
           ffn_w_gate, ffn_w_up, ffn_w_down, router_w, moe_w_gate, moe_w_up, moe_w_down,
           norm_ple, ple_w_gate, ple_w_proj, norm_final):
    nb_p, seq, d = x_prompt.shape
    nb_s, n_new, _ = x_sample.shape
    depth = w_in.shape[0]
    d_ssm = norm_grp_ssm.shape[-1]
    d_attn = norm_grp_attn.shape[-1]
    mp, ms = nb_p * seq, nb_s * n_new
    win = min(WINDOW, seq)
    assert d_ssm % LANES == 0 and seq % S5_T == 0 and n_new % S5_T == 0

    h = jnp.concatenate([x_prompt.reshape(mp, d), x_sample.reshape(ms, d)], axis=0)
    p_p, p_s = p_prompt.reshape(depth, mp, -1), p_sample.reshape(depth, ms, -1)

    n_grp = ssm_a_re.shape[1]
    hb = n_grp // S5_GB
    flat = lambda v: v.reshape((depth * n_grp,) + v.shape[2:])
    tabs = _s5_tables(flat(ssm_a_re), flat(ssm_a_im), ssm_log_dt.reshape(-1), flat(ssm_b_re), flat(ssm_b_im),
                      flat(ssm_c_re), flat(ssm_c_im), ssm_d.reshape(-1))
    g_a, g_b = _glu_blocks(flat(ssm_glu_w))

    kv_p, kv_s, fin_ps, fin_ss = [], [], [], []
    xn = _rmsnorm(h, norm_mix[0], BF16)
    for i in range(depth):
        z = _mm(xn, w_in, i, tm=1024, tn=1280, vmem_mib=56)
        ssm_p, fin_p = _s5_prompt(z, nb_p, seq, tabs, g_a, g_b, i, hb)
        ssm_s, fin_s = _s5_sample(z, mp, nb_s, n_new, tabs, g_a, g_b,
                                  _state_to_lanes(state_ssm_re[i], state_ssm_im[i]), i, hb)
        att_p = _attn_prompt(z, nb_p, seq, d_ssm, d_attn, rel_bias, attn_sinks[i])
        att_s = _attn_sample(z, mp, nb_s, n_new, d_ssm, d_attn, cache_k, cache_v, i, rel_bias,
                             attn_sinks[i])
        merged = _merge_norm(ssm_p, ssm_s, att_p, att_s, norm_grp_ssm[i], norm_grp_attn[i])
        if i % 2 == 0:
            h, hn = _mm_rows(merged, w_out, i, h, tm=512, gain=norm_ffn[i])
        else:
            h = _mm_rows(merged, w_out, i, h, tm=512)

        kv_p.append(jnp.stack([z[(b + 1) * seq - win:(b + 1) * seq, d_ssm + d_attn:] for b in range(nb_p)]))
        kv_s.append(z[mp:, d_ssm + d_attn:])
        fin_ps.append(fin_p.reshape(hb, nb_p, -1))
        fin_ss.append(fin_s)

        j = i // 2
        if i % 2 == 0:
            t = _swiglu(hn, ffn_w_gate, ffn_w_up, j, tm=1024, tn=512)
            h = _mm(t, ffn_w_down, j, tm=512, tn=512, res=h, vmem_mib=52)
        else:
            hn, e_idx, gates = _rmsnorm_router(h, norm_ffn[i], router_w[j])
            h = h + _moe_ffn(hn, e_idx, gates, moe_w_gate, moe_w_up, moe_w_down, j)

        hn = _rmsnorm(h, norm_ple[i], BF16)
        if i + 1 < depth:
            h, xn = _ple_rows(hn, p_p, p_s, ple_w_gate, ple_w_proj, h, i, norm_mix[i + 1], tm=512)
        else:
            h = _ple(hn, p_p, p_s, ple_w_gate, ple_w_proj, h, i, tm=1024, tn=1024, vmem_mib=58)

    y_p, y_s = _rmsnorm_split(h, norm_final, mp)
    kv_p = jnp.stack(kv_p).reshape(depth, nb_p, win, 2, KV_HEADS, HEAD_DIM)
    kv_s = jnp.stack(kv_s).reshape(depth, nb_s, n_new, 2, KV_HEADS, HEAD_DIM)
    p_re, p_im = _state_from_lanes(jnp.stack(fin_ps))
    s_re, s_im = _state_from_lanes(jnp.stack(fin_ss))
    return (y_p.reshape(nb_p, seq, d), y_s.reshape(nb_s, n_new, d),
            kv_p[:, :, :, 0], kv_p[:, :, :, 1], p_re, p_im,
            kv_s[:, :, :, 0], kv_s[:, :, :, 1], s_re, s_im)
```

```python
import functools
import math

import numpy as np
import jax
import jax.numpy as jnp
from jax import lax
from jax.experimental import pallas as pl
from jax.experimental.pallas import tpu as pltpu

F32 = jnp.float32
BF16 = jnp.bfloat16
HIGHEST = lax.Precision.HIGHEST

CHUNK = 64
WINDOW = 128
HEAD_DIM = 64
KV_HEADS = 4
SSM_GROUP_CH = 16
SSM_STATE = 64
N_BUCKETS = 32
REL_MAX_DIST = 128
EPS = 1e-6
NEG_INF = -1e30
ATTN_SCALE = HEAD_DIM ** -0.5

LANES = 128
S5_T = 16
S5_GB = LANES // SSM_GROUP_CH
S5_F = S5_T * LANES
MOE_TM = 512
MOE_SUB = 256
MIB = 1024 * 1024


def _pick(n, pref, mult=8):
    best = None
    for d in range(mult, min(n, pref) + 1, mult):
        if n % d == 0:
            best = d
    return best if best is not None else n


def _params(sem, vmem_mib):
    return pltpu.CompilerParams(dimension_semantics=sem, vmem_limit_bytes=int(vmem_mib * MIB))


def _rms(x, g):
    return x * lax.rsqrt(jnp.mean(x * x, axis=-1, keepdims=True) + EPS) * g


def _norm_body(x_ref, g_ref, o_ref):
    o_ref[...] = _rms(x_ref[...], g_ref[...]).astype(o_ref.dtype)


def _rmsnorm(x, g, out_dtype):
    m, d = x.shape
    tm = _pick(m, 512)
    return pl.pallas_call(
        _norm_body,
        grid=(m // tm,),
        in_specs=[pl.BlockSpec((tm, d), lambda i: (i, 0)), pl.BlockSpec((1, d), lambda i: (0, 0))],
        out_specs=pl.BlockSpec((tm, d), lambda i: (i, 0)),
        out_shape=jax.ShapeDtypeStruct((m, d), out_dtype),
        compiler_params=_params(("arbitrary",), 40),
        name="rmsnorm",
    )(x, g.reshape(1, d))


def _norm_split_body(x_ref, g_ref, op_ref, os_ref, *, np_tiles):
    y = _rms(x_ref[...], g_ref[...])

    @pl.when(pl.program_id(0) < np_tiles)
    def _():
        op_ref[...] = y

    @pl.when(pl.program_id(0) >= np_tiles)
    def _():
        os_ref[...] = y


def _rmsnorm_split(x, g, mp):
    m, d = x.shape
    ms = m - mp
    tm = _pick(math.gcd(mp, ms), 512)
    npt = mp // tm
    return pl.pallas_call(
        functools.partial(_norm_split_body, np_tiles=npt),
        grid=(m // tm,),
        in_specs=[pl.BlockSpec((tm, d), lambda i: (i, 0)), pl.BlockSpec((1, d), lambda i: (0, 0))],
        out_specs=[pl.BlockSpec((tm, d), lambda i: (jnp.minimum(i, npt - 1), 0)),
                   pl.BlockSpec((tm, d), lambda i: (jnp.maximum(i - npt, 0), 0))],
        out_shape=[jax.ShapeDtypeStruct((mp, d), F32), jax.ShapeDtypeStruct((ms, d), F32)],
        compiler_params=_params(("arbitrary",), 40),
        name="rmsnorm_split",
    )(x, g.reshape(1, d))


def _norm_router_body(x_ref, g_ref, rw_ref, o_ref, idx_ref, gate_ref):
    y = _rms(x_ref[...], g_ref[...])
    tm, d = y.shape
    pieces = d // LANES
    for s in range(pieces):
        o_ref[pl.ds(s, tm, stride=pieces), :] = y[:, s * LANES:(s + 1) * LANES]
    n_exp = rw_ref.shape[0]
    logits = jnp.concatenate([jnp.sum(y * rw_ref[e:e + 1, :], axis=-1, keepdims=True) for e in range(n_exp)],
                             axis=-1)
    lane = lax.broadcasted_iota(jnp.int32, logits.shape, 1)
    m1 = jnp.max(logits, axis=-1, keepdims=True)
    i1 = jnp.min(jnp.where(logits == m1, lane, n_exp), axis=-1, keepdims=True)
    rest = jnp.where(lane == i1, -jnp.inf, logits)
    m2 = jnp.max(rest, axis=-1, keepdims=True)
    i2 = jnp.min(jnp.where(rest == m2, lane, n_exp), axis=-1, keepdims=True)
    e2 = jnp.exp(m2 - m1)
    den = 1.0 + e2
    idx_ref[...] = jnp.concatenate([i1, i2], axis=-1)
    gate_ref[...] = jnp.concatenate([1.0 / den, e2 / den], axis=-1)


def _rmsnorm_router(x, g, router_w):
    m, d = x.shape
    n_exp = router_w.shape[-1]
    tm = _pick(m, 512)
    return pl.pallas_call(
        _norm_router_body,
        grid=(m // tm,),
        in_specs=[pl.BlockSpec((tm, d), lambda i: (i, 0)),
                  pl.BlockSpec((1, d), lambda i: (0, 0)),
                  pl.BlockSpec((n_exp, d), lambda i: (0, 0))],
        out_specs=[pl.BlockSpec((tm * (d // LANES), LANES), lambda i: (i, 0)),
                   pl.BlockSpec((tm, 2), lambda i: (i, 0)),
                   pl.BlockSpec((tm, 2), lambda i: (i, 0))],
        out_shape=[jax.ShapeDtypeStruct((m * (d // LANES), LANES), F32),
                   jax.ShapeDtypeStruct((m, 2), jnp.int32),
                   jax.ShapeDtypeStruct((m, 2), F32)],
        compiler_params=_params(("arbitrary",), 40),
        name="rmsnorm_router",
    )(x, g.reshape(1, d), router_w.T)


def _merge_norm_body(ap_ref, as_ref, bp_ref, bs_ref, ga_ref, gb_ref, o_ref, *, np_tiles):
    da = ap_ref.shape[-1]

    def emit(a_ref, b_ref):
        o_ref[:, :da] = _rms(a_ref[...], ga_ref[...]).astype(o_ref.dtype)
        o_ref[:, da:] = _rms(b_ref[...], gb_ref[...]).astype(o_ref.dtype)

    @pl.when(pl.program_id(0) < np_tiles)
    def _():
        emit(ap_ref, bp_ref)

    @pl.when(pl.program_id(0) >= np_tiles)
    def _():
        emit(as_ref, bs_ref)


def _merge_norm(a_p, a_s, b_p, b_s, ga, gb):
    (mp, da), (ms, db) = a_p.shape, b_s.shape
    tm = _pick(math.gcd(mp, ms), 512)
    npt, nst = mp // tm, ms // tm
    pspec = lambda w: pl.BlockSpec((tm, w), lambda i: (jnp.minimum(i, npt - 1), 0))
    sspec = lambda w: pl.BlockSpec((tm, w), lambda i: (jnp.maximum(i - npt, 0), 0))
    return pl.pallas_call(
        functools.partial(_merge_norm_body, np_tiles=npt),
        grid=(npt + nst,),
        in_specs=[pspec(da), sspec(da), pspec(db), sspec(db),
                  pl.BlockSpec((1, da), lambda i: (0, 0)), pl.BlockSpec((1, db), lambda i: (0, 0))],
        out_specs=pl.BlockSpec((tm, da + db), lambda i: (i, 0)),
        out_shape=jax.ShapeDtypeStruct((mp + ms, da + db), BF16),
        compiler_params=_params(("arbitrary",), 40),
        name="merge_norm",
    )(a_p, a_s, b_p, b_s, ga.reshape(1, da), gb.reshape(1, db))


def _mm_body(*refs, has_res):
    if has_res:
        x_ref, w_ref, r_ref, o_ref, wb_ref = refs
    else:
        x_ref, w_ref, o_ref, wb_ref = refs

    @pl.when(pl.program_id(1) == 0)
    def _():
        wb_ref[...] = w_ref[...].astype(BF16)

    acc = jnp.dot(x_ref[...], wb_ref[...], preferred_element_type=F32)
    if has_res:
        acc = r_ref[...] + acc
    o_ref[...] = acc.astype(o_ref.dtype)


def _mm(x, w, lyr, *, tm, tn, out_dtype=F32, res=None, vmem_mib=48):
    m, k = x.shape
    n = w.shape[-1]
    tm, tn = _pick(m, tm), _pick(n, tn, 128)
    in_specs = [pl.BlockSpec((tm, k), lambda j, i: (i, 0)),
                pl.BlockSpec((None, k, tn), lambda j, i: (lyr, 0, j))]
    args = [x, w]
    if res is not None:
        in_specs.append(pl.BlockSpec((tm, tn), lambda j, i: (i, j)))
        args.append(res)
    return pl.pallas_call(
        functools.partial(_mm_body, has_res=res is not None),
        grid=(n // tn, m // tm),
        in_specs=in_specs,
        out_specs=pl.BlockSpec((tm, tn), lambda j, i: (i, j)),
        out_shape=jax.ShapeDtypeStruct((m, n), out_dtype),
        scratch_shapes=[pltpu.VMEM((k, tn), BF16)],
        compiler_params=_params(("arbitrary", "arbitrary"), vmem_mib),
        name="mm_res" if res is not None else "mm",
    )(*args)


def _mm_rows_body(*refs, has_norm):
    if has_norm:
        x_ref, w_ref, r_ref, g_ref, o_ref, n_ref, wb_ref = refs
    else:
        x_ref, w_ref, r_ref, o_ref, wb_ref = refs

    @pl.when(pl.program_id(0) == 0)
    def _():
        wb_ref[...] = w_ref[...].astype(BF16)

    acc = r_ref[...] + jnp.dot(x_ref[...], wb_ref[...], preferred_element_type=F32)
    o_ref[...] = acc
    if has_norm:
        n_ref[...] = _rms(acc, g_ref[...]).astype(n_ref.dtype)


def _mm_rows(x, w, lyr, res, *, tm, gain=None, vmem_mib=56):
    m, k = x.shape
    n = w.shape[-1]
    tm = _pick(m, tm)
    row = lambda width: pl.BlockSpec((tm, width), lambda i: (i, 0))
    in_specs = [row(k), pl.BlockSpec((None, k, n), lambda i: (lyr, 0, 0), pipeline_mode=pl.Buffered(1)), row(n)]
    args = [x, w, res]
    out_specs, out_shape = [row(n)], [jax.ShapeDtypeStruct((m, n), F32)]
    if gain is not None:
        in_specs.append(pl.BlockSpec((1, n), lambda i: (0, 0)))
        args.append(gain.reshape(1, n))
        out_specs.append(row(n))
        out_shape.append(jax.ShapeDtypeStruct((m, n), BF16))
    out = pl.pallas_call(
        functools.partial(_mm_rows_body, has_norm=gain is not None),
        grid=(m // tm,),
        in_specs=in_specs,
        out_specs=out_specs,
        out_shape=out_shape,
        scratch_shapes=[pltpu.VMEM((k, n), BF16)],
        compiler_params=_params(("arbitrary",), vmem_mib),
        name="mm_rows",
    )(*args)
    return out if gain is not None else out[0]


def _swiglu_math(x, wg, wu):
    a = jnp.dot(x, wg, preferred_element_type=F32)
    b = jnp.dot(x, wu, preferred_element_type=F32)
    return (a * jax.nn.sigmoid(a) * b).astype(BF16)


def _swiglu_body(x_ref, wg_ref, wu_ref, o_ref, wgb_ref, wub_ref):
    @pl.when(pl.program_id(1) == 0)
    def _():
        wgb_ref[...] = wg_ref[...].astype(BF16)
        wub_ref[...] = wu_ref[...].astype(BF16)

    o_ref[...] = _swiglu_math(x_ref[...], wgb_ref[...], wub_ref[...])


def _swiglu(x, wg, wu, lyr, *, tm, tn, vmem_mib=48):
    m, k = x.shape
    n = wg.shape[-1]
    tm, tn = _pick(m, tm), _pick(n, tn, 128)
    wspec = pl.BlockSpec((None, k, tn), lambda j, i: (lyr, 0, j))
    return pl.pallas_call(
        _swiglu_body,
        grid=(n // tn, m // tm),
        in_specs=[pl.BlockSpec((tm, k), lambda j, i: (i, 0)), wspec, wspec],
        out_specs=pl.BlockSpec((tm, tn), lambda j, i: (i, j)),
        out_shape=jax.ShapeDtypeStruct((m, n), BF16),
        scratch_shapes=[pltpu.VMEM((k, tn), BF16), pltpu.VMEM((k, tn), BF16)],
        compiler_params=_params(("arbitrary", "arbitrary"), vmem_mib),
        name="swiglu",
    )(x, wg, wu)


def _ple_body(x_ref, pp_ref, ps_ref, wg_ref, wp_ref, r_ref, o_ref, wgb_ref, wpb_ref, pb_ref, *, np_tiles):
    i = pl.program_id(1)

    @pl.when(i == 0)
    def _():
        wgb_ref[...] = wg_ref[...].astype(BF16)
        wpb_ref[...] = wp_ref[...].astype(BF16)

    @pl.when(i < np_tiles)
    def _():
        pb_ref[...] = pp_ref[...].astype(BF16)

    @pl.when(i >= np_tiles)
    def _():
        pb_ref[...] = ps_ref[...].astype(BF16)

    a = jnp.dot(x_ref[...], wgb_ref[...], preferred_element_type=F32)
    b = jnp.dot(pb_ref[...], wpb_ref[...], preferred_element_type=F32)
    o_ref[...] = r_ref[...] + jax.nn.sigmoid(a) * b


def _ple_rows_body(x_ref, pp_ref, ps_ref, wg_ref, wp_ref, r_ref, g_ref, o_ref, n_ref,
                   wgb_ref, wpb_ref, pb_ref, *, np_tiles):
    i = pl.program_id(0)

    @pl.when(i == 0)
    def _():
        wgb_ref[...] = wg_ref[...].astype(BF16)
        wpb_ref[...] = wp_ref[...].astype(BF16)

    @pl.when(i < np_tiles)
    def _():
        pb_ref[...] = pp_ref[...].astype(BF16)

    @pl.when(i >= np_tiles)
    def _():
        pb_ref[...] = ps_ref[...].astype(BF16)

    a = jnp.dot(x_ref[...], wgb_ref[...], preferred_element_type=F32)
    b = jnp.dot(pb_ref[...], wpb_ref[...], preferred_element_type=F32)
    out = r_ref[...] + jax.nn.sigmoid(a) * b
    o_ref[...] = out
    n_ref[...] = _rms(out, g_ref[...]).astype(n_ref.dtype)


def _ple_rows(x, p_p, p_s, wg, wp, res, lyr, gain, *, tm, vmem_mib=58):
    m, k = x.shape
    mp, ms, kp = p_p.shape[1], p_s.shape[1], p_p.shape[-1]
    n = wg.shape[-1]
    tm = _pick(math.gcd(mp, ms), tm)
    npt = mp // tm
    row = lambda width: pl.BlockSpec((tm, width), lambda i: (i, 0))
    once = lambda rows: pl.BlockSpec((None, rows, n), lambda i: (lyr, 0, 0), pipeline_mode=pl.Buffered(1))
    return pl.pallas_call(
        functools.partial(_ple_rows_body, np_tiles=npt),
        grid=(m // tm,),
        in_specs=[row(k),
                  pl.BlockSpec((None, tm, kp), lambda i: (lyr, jnp.minimum(i, npt - 1), 0)),
                  pl.BlockSpec((None, tm, kp), lambda i: (lyr, jnp.maximum(i - npt, 0), 0)),
                  once(k), once(kp), row(n), pl.BlockSpec((1, n), lambda i: (0, 0))],
        out_specs=[row(n), row(n)],
        out_shape=[jax.ShapeDtypeStruct((m, n), F32), jax.ShapeDtypeStruct((m, n), BF16)],
        scratch_shapes=[pltpu.VMEM((k, n), BF16), pltpu.VMEM((kp, n), BF16), pltpu.VMEM((tm, kp), BF16)],
        compiler_params=_params(("arbitrary",), vmem_mib),
        name="ple_rows",
    )(x, p_p, p_s, wg, wp, res, gain.reshape(1, n))


def _ple(x, p_p, p_s, wg, wp, res, lyr, *, tm, tn, vmem_mib=48):
    m, k = x.shape
    mp, ms, kp = p_p.shape[1], p_s.shape[1], p_p.shape[-1]
    n = wg.shape[-1]
    tm, tn = _pick(math.gcd(mp, ms), tm), _pick(n, tn, 128)
    npt = mp // tm
    return pl.pallas_call(
        functools.partial(_ple_body, np_tiles=npt),
        grid=(n // tn, m // tm),
        in_specs=[pl.BlockSpec((tm, k), lambda j, i: (i, 0)),
                  pl.BlockSpec((None, tm, kp), lambda j, i: (lyr, jnp.minimum(i, npt - 1), 0)),
                  pl.BlockSpec((None, tm, kp), lambda j, i: (lyr, jnp.maximum(i - npt, 0), 0)),
                  pl.BlockSpec((None, k, tn), lambda j, i: (lyr, 0, j)),
                  pl.BlockSpec((None, kp, tn), lambda j, i: (lyr, 0, j)),
                  pl.BlockSpec((tm, tn), lambda j, i: (i, j))],
        out_specs=pl.BlockSpec((tm, tn), lambda j, i: (i, j)),
        out_shape=jax.ShapeDtypeStruct((m, n), F32),
        scratch_shapes=[pltpu.VMEM((k, tn), BF16), pltpu.VMEM((kp, tn), BF16), pltpu.VMEM((tm, kp), BF16)],
        compiler_params=_params(("arbitrary", "arbitrary"), vmem_mib),
        name="ple",
    )(x, p_p, p_s, wg, wp, res)


def _moe_weights(w_hbms, stages, bf16s, sem, e_ref, start_ref, nxt_ref, lyr):
    j, i = pl.program_id(0), pl.program_id(1)
    tn = stages[0].shape[1]

    def copies(e, col):
        col = pl.multiple_of(col, LANES)
        return [pltpu.make_async_copy(w.at[lyr, e, :, pl.ds(col, tn)], st, sem.at[n])
                for n, (w, st) in enumerate(zip(w_hbms, stages))]

    @pl.when(jnp.logical_and(j == 0, i == 0))
    def _():
        for cp in copies(e_ref[0], 0):
            cp.start()

    @pl.when(start_ref[i] == 1)
    def _():
        for cp in copies(e_ref[i], j * tn):
            cp.wait()
        rows = 128

        def cast_chunk(c, carry):
            r = pl.multiple_of(c * rows, rows)
            for st, wb in zip(stages, bf16s):
                wb[pl.ds(r, rows), :] = st[pl.ds(r, rows), :].astype(BF16)
            return carry
        lax.fori_loop(0, stages[0].shape[0] // rows, cast_chunk, 0)
        nxt = nxt_ref[i]

        @pl.when(nxt >= 0)
        def _():
            for cp in copies(nxt, j * tn):
                cp.start()

        @pl.when(jnp.logical_and(nxt < 0, j + 1 < pl.num_programs(0)))
        def _():
            for cp in copies(e_ref[0], (j + 1) * tn):
                cp.start()


def _active_prefix(act_ref, i, per):
    n_act = act_ref[i * per]
    for s in range(1, per):
        n_act = n_act + act_ref[i * per + s]
    return n_act


def _prefix_rows(o_ref, n_act, fn):
    tm = o_ref.shape[0]
    for c in range(tm // MOE_SUB + 1):
        rows = c * MOE_SUB

        @pl.when(n_act == c)
        def _():
            if rows:
                o_ref[:rows, :] = fn(rows)
            if rows < tm:
                o_ref[rows:, :] = jnp.zeros((tm - rows, o_ref.shape[1]), o_ref.dtype)


def _moe_up_body(e_ref, act_ref, start_ref, nxt_ref, x_ref, wg_hbm, wu_hbm, o_ref,
                 wgb_ref, wub_ref, sg_ref, su_ref, sem, *, lyr):
    _moe_weights((wg_hbm, wu_hbm), (sg_ref, su_ref), (wgb_ref, wub_ref), sem, e_ref, start_ref, nxt_ref, lyr)
    n_act = _active_prefix(act_ref, pl.program_id(1), x_ref.shape[0] // MOE_SUB)
    _prefix_rows(o_ref, n_act, lambda rows: _swiglu_math(x_ref[:rows, :], wgb_ref[...], wub_ref[...]))


def _moe_down_body(e_ref, act_ref, start_ref, nxt_ref, h_ref, wd_hbm, o_ref, wdb_ref, sd_ref, sem, *, lyr):
    _moe_weights((wd_hbm,), (sd_ref,), (wdb_ref,), sem, e_ref, start_ref, nxt_ref, lyr)
    n_act = _active_prefix(act_ref, pl.program_id(1), h_ref.shape[0] // MOE_SUB)
    _prefix_rows(o_ref, n_act,
                 lambda rows: jnp.dot(h_ref[:rows, :], wdb_ref[...], preferred_element_type=F32))


def _moe_matmul(body, x, ws, sched, lyr, *, tn, out_dtype, vmem_mib, name):
    ns, k = x.shape
    n = ws[0].shape[-1]
    tm, tn = MOE_TM, _pick(n, tn, 128)
    grid_spec = pltpu.PrefetchScalarGridSpec(
        num_scalar_prefetch=len(sched),
        grid=(n // tn, ns // tm),
        in_specs=[pl.BlockSpec((tm, k), lambda j, i, *_: (i, 0))]
                 + [pl.BlockSpec(memory_space=pl.ANY)] * len(ws),
        out_specs=pl.BlockSpec((tm, tn), lambda j, i, *_: (i, j)),
        scratch_shapes=[pltpu.VMEM((k, tn), BF16)] * len(ws) + [pltpu.VMEM((k, tn), F32)] * len(ws)
                       + [pltpu.SemaphoreType.DMA((len(ws),))],
    )
    return pl.pallas_call(
        functools.partial(body, lyr=lyr),
        grid_spec=grid_spec,
        out_shape=jax.ShapeDtypeStruct((ns, n), out_dtype),
        compiler_params=_params(("arbitrary", "arbitrary"), vmem_mib),
        name=name,
    )(*sched, x, *ws)


def _row_gather_copy(x_hbm, buf, sem, slot, src_row, dst_row, rows):
    return pltpu.make_async_copy(x_hbm.at[pl.ds(src_row, rows)], buf.at[slot, pl.ds(dst_row, rows)],
                                 sem.at[slot])


GATHER_GROUP = 8


def _moe_gather_body(tok_ref, real_ref, x_hbm, o_ref, buf, sem):
    i = pl.program_id(0)
    tm = o_ref.shape[0]
    pieces = o_ref.shape[1] // LANES
    grp_rows = GATHER_GROUP * pieces

    def groups(blk):
        return (real_ref[blk] + GATHER_GROUP - 1) // GATHER_GROUP

    def issue(blk, slot):
        def body(g, carry):
            for u in range(GATHER_GROUP):
                r = g * GATHER_GROUP + u
                _row_gather_copy(x_hbm, buf, sem, slot, tok_ref[blk * tm + r] * pieces, r * pieces,
                                 pieces).start(priority=u % 2)
            return carry
        lax.fori_loop(0, groups(blk), body, 0)

    @pl.when(i == 0)
    def _():
        issue(0, 0)

    @pl.when(i + 1 < pl.num_programs(0))
    def _():
        issue(i + 1, (i + 1) % 2)

    slot = i % 2
    cur = buf.at[slot]
    n_grp = groups(i)

    @pl.when(n_grp > 0)
    def _():
        _row_gather_copy(x_hbm, buf, sem, slot, 0, 0, n_grp * grp_rows).wait()

    def zero(g, carry):
        cur[pl.ds(pl.multiple_of(g * grp_rows, grp_rows), grp_rows), :] = jnp.zeros((grp_rows, LANES), F32)
        return carry
    lax.fori_loop(n_grp, tm // GATHER_GROUP, zero, 0)

    o_ref[...] = jnp.concatenate([cur[pl.ds(s, tm, stride=pieces), :] for s in range(pieces)],
                                 axis=-1).astype(o_ref.dtype)


def _moe_gather(x_rows, slot_tok, blk_real, d):
    n_slots = slot_tok.shape[0]
    tm = MOE_TM
    grid_spec = pltpu.PrefetchScalarGridSpec(
        num_scalar_prefetch=2,
        grid=(n_slots // tm,),
        in_specs=[pl.BlockSpec(memory_space=pl.ANY)],
        out_specs=pl.BlockSpec((tm, d), lambda i, *_: (i, 0)),
        scratch_shapes=[pltpu.VMEM((2, tm * (d // LANES), LANES), F32), pltpu.SemaphoreType.DMA((2,))],
    )
    return pl.pallas_call(
        _moe_gather_body,
        grid_spec=grid_spec,
        out_shape=jax.ShapeDtypeStruct((n_slots, d), BF16),
        compiler_params=_params(("arbitrary",), 32),
        name="moe_gather",
    )(slot_tok, blk_real, x_rows)


def _moe_ffn(hn_rows, e_idx, gates, wg, wu, wd, lyr):
    m = e_idx.shape[0]
    d = wg.shape[2]
    n_exp = wg.shape[1]
    top_k = e_idx.shape[-1]
    tm = MOE_TM
    n_asg = m * top_k
    e_flat = e_idx.reshape(n_asg)
    onehot = (e_flat[None, :] == jnp.arange(n_exp, dtype=jnp.int32)[:, None]).astype(jnp.int32)
    counts = jnp.sum(onehot, axis=1)
    start = jnp.cumsum(counts) - counts
    padded = (counts + tm - 1) // tm * tm
    pad_end = jnp.cumsum(padded)
    pad_start = pad_end - padded
    order = jnp.argsort(e_flat).astype(jnp.int32)
    rank = jnp.argsort(order).astype(jnp.int32)
    slot_of = rank + jnp.sum(onehot * (pad_start - start)[:, None], axis=0).astype(jnp.int32)
    n_blocks = n_asg // tm + n_exp
    n_slots = n_blocks * tm
    n_used = (pad_end[-1] // tm).astype(jnp.int32)
    blk = jnp.arange(n_blocks, dtype=jnp.int32)
    blk_start = jnp.minimum(blk, n_used - 1) * tm
    blk_expert = jnp.sum((pad_end[None, :] <= blk_start[:, None]).astype(jnp.int32), axis=1)
    blk_expert = jnp.minimum(blk_expert, n_exp - 1).astype(jnp.int32)
    sub_start = jnp.arange(n_slots // MOE_SUB, dtype=jnp.int32) * MOE_SUB
    sub_expert = jnp.repeat(blk_expert, tm // MOE_SUB)
    real_end = (pad_start + counts)[sub_expert]
    sub_active = jnp.logical_and(sub_start < real_end, sub_start < pad_end[-1]).astype(jnp.int32)
    slot_r = jnp.arange(n_slots, dtype=jnp.int32) - jnp.repeat(pad_start[sub_expert], MOE_SUB)
    slot_src = jnp.repeat(start[sub_expert], MOE_SUB) + slot_r
    slot_real = jnp.logical_and(slot_r >= 0, slot_r < jnp.repeat(counts[sub_expert], MOE_SUB))
    slot_tok = jnp.where(slot_real, order[jnp.clip(slot_src, 0, n_asg - 1)] // top_k, 0).astype(jnp.int32)

    run_start = jnp.concatenate([jnp.ones((1,), jnp.int32),
                                 (blk_expert[1:] != blk_expert[:-1]).astype(jnp.int32)])
    later_start = jnp.where(run_start == 1, blk, n_blocks)
    nxt_blk = lax.cummin(jnp.concatenate([later_start[1:], jnp.full((1,), n_blocks, jnp.int32)]),
                         axis=0, reverse=True)
    run_next = jnp.where(nxt_blk < n_blocks, blk_expert[jnp.minimum(nxt_blk, n_blocks - 1)], -1)
    sched = (blk_expert, sub_active, run_start, run_next.astype(jnp.int32))

    blk_real = jnp.clip((pad_start + counts)[blk_expert] - blk * tm, 0, tm)
    blk_real = jnp.where(blk < n_used, blk_real, 0).astype(jnp.int32)
    x_sorted = _moe_gather(hn_rows, slot_tok, blk_real, d)
    h_sorted = _moe_matmul(_moe_up_body, x_sorted, (wg, wu), sched, lyr, tn=1024, out_dtype=BF16,
                           vmem_mib=48, name="moe_up")
    y_sorted = _moe_matmul(_moe_down_body, h_sorted, (wd,), sched, lyr, tn=512, out_dtype=F32,
                           vmem_mib=48, name="moe_down")
    sl = slot_of.reshape(m, top_k)
    y = y_sorted[sl[:, 0]] * gates[:, 0:1]
    for t in range(1, top_k):
        y = y + y_sorted[sl[:, t]] * gates[:, t:t + 1]
    return y


def _rel_buckets(n_q, n_k_past, n_k):
    rel = (np.arange(n_k) - n_k_past)[None, :] - np.arange(n_q)[:, None]
    half = N_BUCKETS // 2
    max_exact = half // 2
    n = np.abs(rel)
    nf = np.maximum(n, 1).astype(np.float32)
    large = max_exact + (np.log(nf / np.float32(max_exact)) / np.float32(math.log(REL_MAX_DIST / max_exact))
                         * np.float32(half - max_exact)).astype(np.int32)
    large = np.minimum(large, half - 1)
    return (np.where(rel > 0, half, 0) + np.where(n < max_exact, n, large)).astype(np.int32)


def _build_bias(bias_ref, sink_ref, bucket_ref, table_ref, sinks_ref):
    bucket = bucket_ref[...]
    nq = bucket.shape[1]
    gq = bias_ref.shape[2] // nq
    for kh in range(bias_ref.shape[0]):
        for g in range(gq):
            n = kh * gq + g
            acc = jnp.zeros(bucket.shape, F32)
            for b in range(N_BUCKETS):
                acc = jnp.where(bucket == b, table_ref[b, n], acc)
            bias_ref[kh, :, g * nq:(g + 1) * nq] = acc
            sink_ref[kh, :, g * nq:(g + 1) * nq] = jnp.full((1, nq), sinks_ref[n], F32)


def _attn_items(items, bias_ref, sink_ref):
    nq = items[0][0].shape[0]
    gq = items[0][0].shape[-1] // HEAD_DIM // KV_HEADS
    head = lambda x, n: x[:, n * HEAD_DIM:(n + 1) * HEAD_DIM]
    scores = []
    for q, k, _, valid in items:
        for kh in range(KV_HEADS):
            q4 = jnp.concatenate([head(q, kh * gq + g) for g in range(gq)], axis=0)
            s = lax.dot_general(head(k, kh), q4, (((1,), (1,)), ((), ())), preferred_element_type=F32)
            scores.append((s, kh, valid))
    probs = []
    for s, kh, valid in scores:
        s = s + bias_ref[kh]
        if valid is not None:
            s = jnp.where(valid, s, NEG_INF)
        sink = sink_ref[kh]
        mx = jnp.maximum(jnp.max(s, axis=0, keepdims=True), sink)
        e = jnp.exp(s - mx)
        den = jnp.sum(e, axis=0, keepdims=True) + jnp.exp(sink - mx)
        probs.append((e * (1.0 / den)).astype(BF16))
    outs = []
    for i, (_, _, v, _) in enumerate(items):
        heads = []
        for kh in range(KV_HEADS):
            o4 = lax.dot_general(probs[i * KV_HEADS + kh], head(v, kh), (((0,), (0,)), ((), ())),
                                 preferred_element_type=F32)
            heads.extend(o4[g * nq:(g + 1) * nq] for g in range(gq))
        outs.append(jnp.concatenate(heads, axis=-1))
    return outs


def _attn_prompt_body(q_ref, kp_ref, kc_ref, vp_ref, vc_ref, bucket_ref, table_ref, sinks_ref,
                      o_ref, bias_ref, sink_ref, *, chunks):
    j = pl.program_id(1)

    @pl.when(jnp.logical_and(pl.program_id(0) == 0, j == 0))
    def _():
        _build_bias(bias_ref, sink_ref, bucket_ref, table_ref, sinks_ref)

    q = (q_ref[...] * ATTN_SCALE).astype(BF16)
    k = jnp.concatenate([kp_ref[...], kc_ref[...]], axis=0).astype(BF16)
    v = jnp.concatenate([vp_ref[...], vc_ref[...]], axis=0).astype(BF16)
    band = WINDOW + CHUNK
    items = []
    for c in range(chunks):
        lo = c * CHUNK
        key_valid = None
        if lo < WINDOW:
            key_pos = j * (chunks * CHUNK) + lo - WINDOW + lax.broadcasted_iota(jnp.int32, (band, 1), 0)
            key_valid = key_pos >= 0
        items.append((q[lo:lo + CHUNK], k[lo:lo + band], v[lo:lo + band], key_valid))
    for c, o in enumerate(_attn_items(items, bias_ref, sink_ref)):
        o_ref[c * CHUNK:(c + 1) * CHUNK, :] = o


def _attn_prompt(z, n_b, seq, d_ssm, d_attn, table, sinks):
    d_kv = KV_HEADS * HEAD_DIM
    assert d_ssm % d_attn == 0 and d_ssm % d_kv == 0 and d_attn % d_kv == 0
    qb = _pick(seq, 4 * CHUNK, 2 * CHUNK)
    assert qb % WINDOW == 0 and seq % qb == 0
    chunks = qb // CHUNK
    nj = seq // qb
    r = qb // WINDOW
    kcol = (d_ssm + d_attn) // d_kv
    gq = d_attn // HEAD_DIM // KV_HEADS
    band = WINDOW + CHUNK
    bucket = jnp.asarray(_rel_buckets(CHUNK, WINDOW, band).T)

    def prev(b, j):
        return b * (seq // WINDOW) + jnp.maximum(j * r - 1, 0)

    return pl.pallas_call(
        functools.partial(_attn_prompt_body, chunks=chunks),
        grid=(n_b, nj),
        in_specs=[pl.BlockSpec((qb, d_attn), lambda b, j: (b * nj + j, d_ssm // d_attn)),
                  pl.BlockSpec((WINDOW, d_kv), lambda b, j: (prev(b, j), kcol)),
                  pl.BlockSpec((qb, d_kv), lambda b, j: (b * nj + j, kcol)),
                  pl.BlockSpec((WINDOW, d_kv), lambda b, j: (prev(b, j), kcol + 1)),
                  pl.BlockSpec((qb, d_kv), lambda b, j: (b * nj + j, kcol + 1)),
                  pl.BlockSpec((band, CHUNK), lambda b, j: (0, 0)),
                  pl.BlockSpec(memory_space=pltpu.SMEM),
                  pl.BlockSpec(memory_space=pltpu.SMEM)],
        out_specs=pl.BlockSpec((qb, d_attn), lambda b, j: (b * nj + j, 0)),
        out_shape=jax.ShapeDtypeStruct((n_b * seq, d_attn), F32),
        scratch_shapes=[pltpu.VMEM((KV_HEADS, band, gq * CHUNK), F32),
                        pltpu.VMEM((KV_HEADS, 1, gq * CHUNK), F32)],
        compiler_params=_params(("arbitrary", "arbitrary"), 40),
        name="attn_prompt",
    )(z, z, z, z, z, bucket, table, sinks)


def _attn_sample_body(q_ref, kc_ref, kn_ref, vc_ref, vn_ref, bucket_ref, table_ref, sinks_ref,
                      o_ref, bias_ref, sink_ref, *, n_bs, n_new, n_past):
    @pl.when(pl.program_id(0) == 0)
    def _():
        _build_bias(bias_ref, sink_ref, bucket_ref, table_ref, sinks_ref)

    q = (q_ref[...] * ATTN_SCALE).astype(BF16)
    kc, kn = kc_ref[...].astype(BF16), kn_ref[...].astype(BF16)
    vc, vn = vc_ref[...].astype(BF16), vn_ref[...].astype(BF16)
    items = []
    for b in range(n_bs):
        new = slice(b * n_new, (b + 1) * n_new)
        past = slice(b * n_past, (b + 1) * n_past)
        items.append((q[new], jnp.concatenate([kc[past], kn[new]], axis=0),
                      jnp.concatenate([vc[past], vn[new]], axis=0), None))
    for b, o in enumerate(_attn_items(items, bias_ref, sink_ref)):
        o_ref[b * n_new:(b + 1) * n_new, :] = o


def _attn_sample(z, row0, n_b, n_new, d_ssm, d_attn, cache_k, cache_v, lyr, table, sinks):
    d_kv = KV_HEADS * HEAD_DIM
    n_past = cache_k.shape[2]
    n_bs = _pick(n_b, 4, 1)
    rows = n_bs * n_new
    assert row0 % rows == 0
    kcol = (d_ssm + d_attn) // d_kv
    gq = d_attn // HEAD_DIM // KV_HEADS
    n_k = n_past + n_new
    bucket = jnp.asarray(_rel_buckets(n_new, n_past, n_k).T)
    ck = cache_k.reshape(-1, d_kv)
    cv = cache_v.reshape(-1, d_kv)
    rb = row0 // rows
    nsteps = n_b // n_bs
    cache_spec = pl.BlockSpec((n_bs * n_past, d_kv), lambda b: (lyr * nsteps + b, 0))
    return pl.pallas_call(
        functools.partial(_attn_sample_body, n_bs=n_bs, n_new=n_new, n_past=n_past),
        grid=(nsteps,),
        in_specs=[pl.BlockSpec((rows, d_attn), lambda b: (rb + b, d_ssm // d_attn)),
                  cache_spec,
                  pl.BlockSpec((rows, d_kv), lambda b: (rb + b, kcol)),
                  cache_spec,
                  pl.BlockSpec((rows, d_kv), lambda b: (rb + b, kcol + 1)),
                  pl.BlockSpec((n_k, n_new), lambda b: (0, 0)),
                  pl.BlockSpec(memory_space=pltpu.SMEM),
                  pl.BlockSpec(memory_space=pltpu.SMEM)],
        out_specs=pl.BlockSpec((rows, d_attn), lambda b: (b, 0)),
        out_shape=jax.ShapeDtypeStruct((n_b * n_new, d_attn), F32),
        scratch_shapes=[pltpu.VMEM((KV_HEADS, n_k, gq * n_new), F32),
                        pltpu.VMEM((KV_HEADS, 1, gq * n_new), F32)],
        compiler_params=_params(("arbitrary",), 40),
        name="attn_sample",
    )(z, ck, z, cv, z, bucket, table, sinks)


def _tile_lanes(x, reps):
    return jnp.concatenate([x] * reps, axis=-1)


def _s5_tables_body(are_ref, aim_ref, ldt_ref, btr_ref, bti_ref, cre_ref, cim_ref, d_ref,
                    dk_ref, we_ref, ws_ref, a_ref):
    rows, p = cre_ref.shape
    gb = rows // SSM_GROUP_CH
    a_re, a_im = are_ref[...], aim_ref[...]
    dt = jnp.exp(ldt_ref[...])
    mag = jnp.exp(a_re * dt)
    ab_re, ab_im = mag * jnp.cos(a_im * dt), mag * jnp.sin(a_im * dt)

    x, y = ab_re - 1.0, ab_im
    den = a_re * a_re + a_im * a_im
    co_re = (x * a_re + y * a_im) / den
    co_im = (y * a_re - x * a_im) / den
    bb_re = co_re * btr_ref[...] - co_im * bti_ref[...]
    bb_im = co_re * bti_ref[...] + co_im * btr_ref[...]
    c_re, c_im = cre_ref[...], cim_ref[...]

    r_idx = lax.broadcasted_iota(jnp.int32, (rows, 1), 0)
    own_state = (r_idx // SSM_GROUP_CH) == (lax.broadcasted_iota(jnp.int32, (1, gb * p), 1) // p)
    same_group = (r_idx // SSM_GROUP_CH) == (lax.broadcasted_iota(jnp.int32, (1, rows), 1) // SSM_GROUP_CH)
    diag = r_idx == lax.broadcasted_iota(jnp.int32, (1, rows), 1)

    def place(v_re, v_im):
        return jnp.concatenate([jnp.where(own_state, _tile_lanes(v_re, gb), 0.0),
                                jnp.where(own_state, _tile_lanes(v_im, gb), 0.0)], axis=-1)

    dims = (((1,), (1,)), ((), ()))
    bb_cat = jnp.concatenate([bb_re, -bb_im], axis=-1)
    pw_re, pw_im = jnp.ones_like(ab_re), jnp.zeros_like(ab_re)
    for j in range(S5_T + 1):
        ca_re = c_re * pw_re - c_im * pw_im
        ca_im = c_re * pw_im + c_im * pw_re
        if j < S5_T:
            kj = lax.dot_general(bb_cat, jnp.concatenate([ca_re, ca_im], axis=-1), dims,
                                 precision=HIGHEST, preferred_element_type=F32)
            kj = jnp.where(same_group, kj, 0.0)
            if j == 0:
                kj = kj + jnp.where(diag, d_ref[...], 0.0)
            dk_ref[j] = kj.astype(dk_ref.dtype)
            s = S5_T - 1 - j
            we_ref[s * rows:(s + 1) * rows, :] = place(
                pw_re * bb_re - pw_im * bb_im, pw_re * bb_im + pw_im * bb_re).astype(we_ref.dtype)
        if j >= 1:
            ws_ref[(j - 1) * rows:j * rows, :] = place(ca_re, -ca_im).astype(ws_ref.dtype)
        if j == S5_T:
            first = r_idx == (lax.broadcasted_iota(jnp.int32, (1, gb * p), 1) // p) * SSM_GROUP_CH
            a_ref[...] = jnp.concatenate(
                [jnp.sum(jnp.where(first, _tile_lanes(pw_re, gb), 0.0), axis=0, keepdims=True),
                 jnp.sum(jnp.where(first, _tile_lanes(pw_im, gb), 0.0), axis=0, keepdims=True)], axis=-1)
        pw_re, pw_im = pw_re * ab_re - pw_im * ab_im, pw_re * ab_im + pw_im * ab_re


def _s5_tables(a_re, a_im, log_dt, b_re, b_im, c_re, c_im, d_skip):
    h, p = a_re.shape
    g = SSM_GROUP_CH
    hb = h // S5_GB
    sw = 2 * S5_GB * p
    rep = lambda v: jnp.repeat(v, g, axis=0).reshape(hb, LANES, -1)
    blk = lambda w: pl.BlockSpec((None, LANES, w), lambda i: (i, 0, 0))
    wide = pl.BlockSpec((None, S5_F, sw), lambda i: (i, 0, 0))
    return pl.pallas_call(
        _s5_tables_body,
        grid=(hb,),
        in_specs=[blk(p), blk(p), blk(1), blk(p), blk(p), blk(p), blk(p), blk(1)],
        out_specs=[pl.BlockSpec((None, S5_T, LANES, LANES), lambda i: (i, 0, 0, 0)), wide, wide,
                   pl.BlockSpec((None, 1, sw), lambda i: (i, 0, 0))],
        out_shape=[jax.ShapeDtypeStruct((hb, S5_T, LANES, LANES), BF16),
                   jax.ShapeDtypeStruct((hb, S5_F, sw), BF16),
                   jax.ShapeDtypeStruct((hb, S5_F, sw), BF16),
                   jax.ShapeDtypeStruct((hb, 1, sw), F32)],
        compiler_params=_params(("arbitrary",), 40),
        name="s5_tables",
    )(rep(a_re), rep(a_im), rep(log_dt.reshape(h, 1)),
      jnp.swapaxes(b_re, 1, 2).reshape(hb, LANES, p), jnp.swapaxes(b_im, 1, 2).reshape(hb, LANES, p),
      c_re.reshape(hb, LANES, p), c_im.reshape(hb, LANES, p), d_skip.reshape(hb, LANES, 1))


def _split_mul(a, p):
    lane = lax.broadcasted_iota(jnp.int32, a.shape, a.ndim - 1)
    sw = pltpu.roll(a, p, axis=a.ndim - 1)
    return jnp.where(lane < p, a, sw), jnp.where(lane < p, -sw, a)


def _cmul(x, mr, mi, p):
    return x * mr + pltpu.roll(x, p, axis=x.ndim - 1) * mi


def _build_toeplitz(m_ref, dk_ref):
    m_ref[...] = jnp.zeros_like(m_ref)
    for s in range(S5_T):
        for t in range(s, S5_T):
            m_ref[s * LANES:(s + 1) * LANES, t * LANES:(t + 1) * LANES] = dk_ref[t - s]


def _s5_outputs(ucat, enter, m_ref, ws_ref):
    return (jnp.dot(ucat, m_ref[...], preferred_element_type=F32)
            + lax.dot_general(enter.astype(BF16), ws_ref[...], (((1,), (1,)), ((), ())),
                              preferred_element_type=F32))


def _glu_rows(o_ref, ga_ref, gb_ref, tile):
    ga, gb = ga_ref[...], gb_ref[...]
    tile = _pick(o_ref.shape[0], tile)
    for r0 in range(0, o_ref.shape[0], tile):
        y = o_ref[r0:r0 + tile, :].astype(BF16)
        o_ref[r0:r0 + tile, :] = (jnp.dot(y, ga, preferred_element_type=F32)
                                  * jax.nn.sigmoid(jnp.dot(y, gb, preferred_element_type=F32)))


def _s5_prompt_body(z_ref, dk_ref, we_ref, ws_ref, a_ref, ga_ref, gb_ref, o_ref, fin_ref, m_ref,
                    *, nb, nc):
    @pl.when(pl.program_id(1) == 0)
    def _():
        _build_toeplitz(m_ref, dk_ref)

    r = nb * nc
    p = a_ref.shape[-1] // 2
    ucat = jnp.concatenate([z_ref[pl.ds(t, r, stride=S5_T), :].astype(BF16) for t in range(S5_T)], axis=-1)
    x = jnp.dot(ucat, we_ref[...], preferred_element_type=F32)
    a = a_ref[...]
    pos = lax.rem(lax.broadcasted_iota(jnp.int32, (r, 1), 0), nc)
    sh = 1
    while sh < nc:
        mr, mi = _split_mul(a, p)
        shifted = jnp.where(pos >= sh, pltpu.roll(x, sh, axis=0), 0.0)
        x = x + _cmul(shifted, mr, mi, p)
        a = _cmul(a, mr, mi, p)
        sh *= 2
    enter = jnp.where(pos >= 1, pltpu.roll(x, 1, axis=0), 0.0)
    fin_ref[...] = jnp.concatenate([x[(b + 1) * nc - 1:(b + 1) * nc] for b in range(nb)], axis=0)
    y = _s5_outputs(ucat, enter, m_ref, ws_ref)
    for t in range(S5_T):
        o_ref[pl.ds(t, r, stride=S5_T), :] = y[:, t * LANES:(t + 1) * LANES]
    _glu_rows(o_ref, ga_ref, gb_ref, 512)


def _s5_sample_body(z_ref, dk_ref, we_ref, ws_ref, a_ref, ga_ref, gb_ref, h0_ref, o_ref, fin_ref, m_ref,
                    *, nb, nc):
    _build_toeplitz(m_ref, dk_ref)
    p = a_ref.shape[-1] // 2
    seq = nc * S5_T
    ucat = jnp.concatenate(
        [jnp.concatenate([z_ref[pl.ds(c * S5_T + t, nb, stride=seq), :].astype(BF16) for t in range(S5_T)],
                         axis=-1) for c in range(nc)], axis=0)
    e = jnp.dot(ucat, we_ref[...], preferred_element_type=F32)
    mr, mi = _split_mul(a_ref[...], p)
    state = h0_ref[...]
    enter = []
    for c in range(nc):
        enter.append(state)
        state = _cmul(state, mr, mi, p) + e[c * nb:(c + 1) * nb]
    fin_ref[...] = state
    y = _s5_outputs(ucat, jnp.concatenate(enter, axis=0), m_ref, ws_ref)
    for c in range(nc):
        for t in range(S5_T):
            o_ref[pl.ds(c * S5_T + t, nb, stride=seq), :] = y[c * nb:(c + 1) * nb, t * LANES:(t + 1) * LANES]
    _glu_rows(o_ref, ga_ref, gb_ref, 512)


def _s5_prompt(z, n_b, seq, tabs, g_a, g_b, lyr, hb):
    dk, w_e, w_s, a_t = tabs
    sw = a_t.shape[-1]
    nbs = _pick(n_b, 4, 1)
    nc = seq // S5_T
    tab = lambda arr: pl.BlockSpec((None,) + arr.shape[1:],
                                   lambda i, b: (lyr * hb + i,) + (0,) * (arr.ndim - 1))
    return pl.pallas_call(
        functools.partial(_s5_prompt_body, nb=nbs, nc=nc),
        grid=(hb, n_b // nbs),
        in_specs=[pl.BlockSpec((nbs * seq, LANES), lambda i, b: (b, i)),
                  tab(dk), tab(w_e), tab(w_s), tab(a_t), tab(g_a), tab(g_b)],
        out_specs=[pl.BlockSpec((nbs * seq, LANES), lambda i, b: (b, i)),
                   pl.BlockSpec((None, None, nbs, sw), lambda i, b: (i, b, 0, 0))],
        out_shape=[jax.ShapeDtypeStruct((n_b * seq, hb * LANES), F32),
                   jax.ShapeDtypeStruct((hb, n_b // nbs, nbs, sw), F32)],
        scratch_shapes=[pltpu.VMEM((S5_F, S5_F), BF16)],
        compiler_params=_params(("arbitrary", "arbitrary"), 58),
        name="s5_prompt",
    )(z, dk, w_e, w_s, a_t, g_a, g_b)


def _s5_sample(z, row0, n_b, n_new, tabs, g_a, g_b, h0, lyr, hb):
    dk, w_e, w_s, a_t = tabs
    sw = a_t.shape[-1]
    rows = n_b * n_new
    assert row0 % rows == 0
    rb = row0 // rows
    tab = lambda arr, off=lyr * hb: pl.BlockSpec((None,) + arr.shape[1:],
                                                 lambda i: (off + i,) + (0,) * (arr.ndim - 1))
    return pl.pallas_call(
        functools.partial(_s5_sample_body, nb=n_b, nc=n_new // S5_T),
        grid=(hb,),
        in_specs=[pl.BlockSpec((rows, LANES), lambda i: (rb, i)),
                  tab(dk), tab(w_e), tab(w_s), tab(a_t), tab(g_a), tab(g_b), tab(h0, 0)],
        out_specs=[pl.BlockSpec((rows, LANES), lambda i: (0, i)),
                   pl.BlockSpec((None, n_b, sw), lambda i: (i, 0, 0))],
        out_shape=[jax.ShapeDtypeStruct((rows, hb * LANES), F32),
                   jax.ShapeDtypeStruct((hb, n_b, sw), F32)],
        scratch_shapes=[pltpu.VMEM((S5_F, S5_F), BF16)],
        compiler_params=_params(("arbitrary",), 52),
        name="s5_sample",
    )(z, dk, w_e, w_s, a_t, g_a, g_b, h0)


def _glu_blocks(glu_w):
    h, g, _ = glu_w.shape
    hb = h // S5_GB
    eye = jnp.eye(S5_GB, dtype=F32)

    def diag(wm):
        wm = wm.reshape(hb, S5_GB, g, g)
        return (eye[None, :, None, :, None] * wm[:, :, :, None, :]).reshape(hb, LANES, LANES).astype(BF16)

    return diag(glu_w[..., :g]), diag(glu_w[..., g:])


def _state_to_lanes(re, im):
    n, h, p = re.shape
    hb = h // S5_GB
    f = lambda v: v.reshape(n, hb, S5_GB * p).transpose(1, 0, 2)
    return jnp.concatenate([f(re), f(im)], axis=-1)


def _state_from_lanes(s):
    nl, hb, n, sw = s.shape
    half = sw // 2
    f = lambda v: v.transpose(0, 2, 1, 3).reshape(nl, n, hb * S5_GB, half // S5_GB)
    return f(s[..., :half]), f(s[..., half:])


def kernel(x_prompt, x_sample, cache_k, cache_v, state_ssm_re, state_ssm_im, p_prompt, p_sample,
           norm_mix, w_in, ssm_a_re, ssm_a_im, ssm_log_dt, ssm_b_re, ssm_b_im, ssm_c_re, ssm_c_im,
           ssm_d, ssm_glu_w, attn_sinks, rel_bias, norm_grp_ssm, norm_grp_attn, w_out, norm_ffn,
           ffn_w_gate, ffn_w_up, ffn_w_down, router_w, moe_w_gate, moe_w_up, moe_w_down,
           norm_ple, ple_w_gate, ple_w_proj, norm_final):
    nb_p, seq, d = x_prompt.shape
    nb_s, n_new, _ = x_sample.shape
    depth = w_in.shape[0]
    d_ssm = norm_grp_ssm.shape[-1]
    d_attn = norm_grp_attn.shape[-1]
    mp, ms = nb_p * seq, nb_s * n_new
    win = min(WINDOW, seq)
    assert d_ssm % LANES == 0 and seq % S5_T == 0 and n_new % S5_T == 0

    h = jnp.concatenate([x_prompt.reshape(mp, d), x_sample.reshape(ms, d)], axis=0)
    p_p, p_s = p_prompt.reshape(depth, mp, -1), p_sample.reshape(depth, ms, -1)

    n_grp = ssm_a_re.shape[1]
    hb = n_grp // S5_GB
    flat = lambda v: v.reshape((depth * n_grp,) + v.shape[2:])
    tabs = _s5_tables(flat(ssm_a_re), flat(ssm_a_im), ssm_log_dt.reshape(-1), flat(ssm_b_re), flat(ssm_b_im),
                      flat(ssm_c_re), flat(ssm_c_im), ssm_d.reshape(-1))
    g_a, g_b = _glu_blocks(flat(ssm_glu_w))

    kv_p, kv_s, fin_ps, fin_ss = [], [], [], []
    xn = _rmsnorm(h, norm_mix[0], BF16)
    for i in range(depth):
        z = _mm(xn, w_in, i, tm=1024, tn=1280, vmem_mib=56)
        ssm_p, fin_p = _s5_prompt(z, nb_p, seq, tabs, g_a, g_b, i, hb)
        ssm_s, fin_s = _s5_sample(z, mp, nb_s, n_new, tabs, g_a, g_b,
                                  _state_to_lanes(state_ssm_re[i], state_ssm_im[i]), i, hb)
        att_p = _attn_prompt(z, nb_p, seq, d_ssm, d_attn, rel_bias, attn_sinks[i])
        att_s = _attn_sample(z, mp, nb_s, n_new, d_ssm, d_attn, cache_k, cache_v, i, rel_bias,
                             attn_sinks[i])
        merged = _merge_norm(ssm_p, ssm_s, att_p, att_s, norm_grp_ssm[i], norm_grp_attn[i])
        if i % 2 == 0:
            h, hn = _mm_rows(merged, w_out, i, h, tm=512, gain=norm_ffn[i])
        else:
            h = _mm_rows(merged, w_out, i, h, tm=512)

        kv_p.append(jnp.stack([z[(b + 1) * seq - win:(b + 1) * seq, d_ssm + d_attn:] for b in range(nb_p)]))
        kv_s.append(z[mp:, d_ssm + d_attn:])
        fin_ps.append(fin_p.reshape(hb, nb_p, -1))
        fin_ss.append(fin_s)

        j = i // 2
        if i % 2 == 0:
            t = _swiglu(hn, ffn_w_gate, ffn_w_up, j, tm=1024, tn=512)
            h = _mm(t, ffn_w_down, j, tm=512, tn=512, res=h, vmem_mib=52)
        else:
            hn, e_idx, gates = _rmsnorm_router(h, norm_ffn[i], router_w[j])
            h = h + _moe_ffn(hn, e_idx, gates, moe_w_gate, moe_w_up, moe_w_down, j)

        hn = _rmsnorm(h, norm_ple[i], BF16)
        if i + 1 < depth:
            h, xn = _ple_rows(hn, p_p, p_s, ple_w_gate, ple_w_proj, h, i, norm_mix[i + 1], tm=512)
        else:
            h = _ple(hn, p_p, p_s, ple_w_gate, ple_w_proj, h, i, tm=1024, tn=1024, vmem_mib=58)

    y_p, y_s = _rmsnorm_split(h, norm_final, mp)
    kv_p = jnp.stack(kv_p).reshape(depth, nb_p, win, 2, KV_HEADS, HEAD_DIM)
    kv_s = jnp.stack(kv_s).reshape(depth, nb_s, n_new, 2, KV_HEADS, HEAD_DIM)
    p_re, p_im = _state_from_lanes(jnp.stack(fin_ps))
    s_re, s_im = _state_from_lanes(jnp.stack(fin_ss))
    return (y_p.reshape(nb_p, seq, d), y_s.reshape(nb_s, n_new, d),
            kv_p[:, :, :, 0], kv_p[:, :, :, 1], p_re, p_im,
            kv_s[:, :, :, 0], kv_s[:, :, :, 1], s_re, s_im)
```

```python
import functools
import math

import numpy as np
import jax
import jax.numpy as jnp
from jax import lax
from jax.experimental import pallas as pl
from jax.experimental.pallas import tpu as pltpu

F32 = jnp.float32
BF16 = jnp.bfloat16
HIGHEST = lax.Precision.HIGHEST

CHUNK = 64
WINDOW = 128
HEAD_DIM = 64
KV_HEADS = 4
SSM_GROUP_CH = 16
SSM_STATE = 64
N_BUCKETS = 32
REL_MAX_DIST = 128
EPS = 1e-6
NEG_INF = -1e30
ATTN_SCALE = HEAD_DIM ** -0.5

LANES = 128
S5_T = 16
S5_GB = LANES // SSM_GROUP_CH
S5_F = S5_T * LANES
MOE_TM = 512
MOE_SUB = 128
MIB = 1024 * 1024


def _pick(n, pref, mult=8):
    best = None
    for d in range(mult, min(n, pref) + 1, mult):
        if n % d == 0:
            best = d
    return best if best is not None else n


def _params(sem, vmem_mib):
    return pltpu.CompilerParams(dimension_semantics=sem, vmem_limit_bytes=int(vmem_mib * MIB))


def _rms(x, g):
    return x * lax.rsqrt(jnp.mean(x * x, axis=-1, keepdims=True) + EPS) * g


def _norm_body(x_ref, g_ref, o_ref):
    o_ref[...] = _rms(x_ref[...], g_ref[...]).astype(o_ref.dtype)


def _rmsnorm(x, g, out_dtype):
    m, d = x.shape
    tm = _pick(m, 512)
    return pl.pallas_call(
        _norm_body,
        grid=(m // tm,),
        in_specs=[pl.BlockSpec((tm, d), lambda i: (i, 0)), pl.BlockSpec((1, d), lambda i: (0, 0))],
        out_specs=pl.BlockSpec((tm, d), lambda i: (i, 0)),
        out_shape=jax.ShapeDtypeStruct((m, d), out_dtype),
        compiler_params=_params(("arbitrary",), 40),
        name="rmsnorm",
    )(x, g.reshape(1, d))


def _norm_split_body(x_ref, g_ref, op_ref, os_ref, *, np_tiles):
    y = _rms(x_ref[...], g_ref[...])

    @pl.when(pl.program_id(0) < np_tiles)
    def _():
        op_ref[...] = y

    @pl.when(pl.program_id(0) >= np_tiles)
    def _():
        os_ref[...] = y


def _rmsnorm_split(x, g, mp):
    m, d = x.shape
    ms = m - mp
    tm = _pick(math.gcd(mp, ms), 512)
    npt = mp // tm
    return pl.pallas_call(
        functools.partial(_norm_split_body, np_tiles=npt),
        grid=(m // tm,),
        in_specs=[pl.BlockSpec((tm, d), lambda i: (i, 0)), pl.BlockSpec((1, d), lambda i: (0, 0))],
        out_specs=[pl.BlockSpec((tm, d), lambda i: (jnp.minimum(i, npt - 1), 0)),
                   pl.BlockSpec((tm, d), lambda i: (jnp.maximum(i - npt, 0), 0))],
        out_shape=[jax.ShapeDtypeStruct((mp, d), F32), jax.ShapeDtypeStruct((ms, d), F32)],
        compiler_params=_params(("arbitrary",), 40),
        name="rmsnorm_split",
    )(x, g.reshape(1, d))


def _norm_router_body(x_ref, g_ref, rw_ref, o_ref, idx_ref, gate_ref):
    y = _rms(x_ref[...], g_ref[...])
    tm, d = y.shape
    pieces = d // LANES
    for s in range(pieces):
        o_ref[pl.ds(s, tm, stride=pieces), :] = y[:, s * LANES:(s + 1) * LANES]
    n_exp = rw_ref.shape[0]
    logits = jnp.concatenate([jnp.sum(y * rw_ref[e:e + 1, :], axis=-1, keepdims=True) for e in range(n_exp)],
                             axis=-1)
    lane = lax.broadcasted_iota(jnp.int32, logits.shape, 1)
    m1 = jnp.max(logits, axis=-1, keepdims=True)
    i1 = jnp.min(jnp.where(logits == m1, lane, n_exp), axis=-1, keepdims=True)
    rest = jnp.where(lane == i1, -jnp.inf, logits)
    m2 = jnp.max(rest, axis=-1, keepdims=True)
    i2 = jnp.min(jnp.where(rest == m2, lane, n_exp), axis=-1, keepdims=True)
    e2 = jnp.exp(m2 - m1)
    den = 1.0 + e2
    idx_ref[...] = jnp.concatenate([i1, i2], axis=-1)
    gate_ref[...] = jnp.concatenate([1.0 / den, e2 / den], axis=-1)


def _rmsnorm_router(x, g, router_w):
    m, d = x.shape
    n_exp = router_w.shape[-1]
    tm = _pick(m, 512)
    return pl.pallas_call(
        _norm_router_body,
        grid=(m // tm,),
        in_specs=[pl.BlockSpec((tm, d), lambda i: (i, 0)),
                  pl.BlockSpec((1, d), lambda i: (0, 0)),
                  pl.BlockSpec((n_exp, d), lambda i: (0, 0))],
        out_specs=[pl.BlockSpec((tm * (d // LANES), LANES), lambda i: (i, 0)),
                   pl.BlockSpec((tm, 2), lambda i: (i, 0)),
                   pl.BlockSpec((tm, 2), lambda i: (i, 0))],
        out_shape=[jax.ShapeDtypeStruct((m * (d // LANES), LANES), F32),
                   jax.ShapeDtypeStruct((m, 2), jnp.int32),
                   jax.ShapeDtypeStruct((m, 2), F32)],
        compiler_params=_params(("arbitrary",), 40),
        name="rmsnorm_router",
    )(x, g.reshape(1, d), router_w.T)


def _merge_norm_body(ap_ref, as_ref, bp_ref, bs_ref, ga_ref, gb_ref, o_ref, *, np_tiles):
    da = ap_ref.shape[-1]

    def emit(a_ref, b_ref):
        o_ref[:, :da] = _rms(a_ref[...], ga_ref[...]).astype(o_ref.dtype)
        o_ref[:, da:] = _rms(b_ref[...], gb_ref[...]).astype(o_ref.dtype)

    @pl.when(pl.program_id(0) < np_tiles)
    def _():
        emit(ap_ref, bp_ref)

    @pl.when(pl.program_id(0) >= np_tiles)
    def _():
        emit(as_ref, bs_ref)


def _merge_norm(a_p, a_s, b_p, b_s, ga, gb):
    (mp, da), (ms, db) = a_p.shape, b_s.shape
    tm = _pick(math.gcd(mp, ms), 512)
    npt, nst = mp // tm, ms // tm
    pspec = lambda w: pl.BlockSpec((tm, w), lambda i: (jnp.minimum(i, npt - 1), 0))
    sspec = lambda w: pl.BlockSpec((tm, w), lambda i: (jnp.maximum(i - npt, 0), 0))
    return pl.pallas_call(
        functools.partial(_merge_norm_body, np_tiles=npt),
        grid=(npt + nst,),
        in_specs=[pspec(da), sspec(da), pspec(db), sspec(db),
                  pl.BlockSpec((1, da), lambda i: (0, 0)), pl.BlockSpec((1, db), lambda i: (0, 0))],
        out_specs=pl.BlockSpec((tm, da + db), lambda i: (i, 0)),
        out_shape=jax.ShapeDtypeStruct((mp + ms, da + db), BF16),
        compiler_params=_params(("arbitrary",), 40),
        name="merge_norm",
    )(a_p, a_s, b_p, b_s, ga.reshape(1, da), gb.reshape(1, db))


def _mm_body(*refs, has_res):
    if has_res:
        x_ref, w_ref, r_ref, o_ref, wb_ref = refs
    else:
        x_ref, w_ref, o_ref, wb_ref = refs

    @pl.when(pl.program_id(1) == 0)
    def _():
        wb_ref[...] = w_ref[...].astype(BF16)

    acc = jnp.dot(x_ref[...], wb_ref[...], preferred_element_type=F32)
    if has_res:
        acc = r_ref[...] + acc
    o_ref[...] = acc.astype(o_ref.dtype)


def _mm(x, w, lyr, *, tm, tn, out_dtype=F32, res=None, vmem_mib=48):
    m, k = x.shape
    n = w.shape[-1]
    tm, tn = _pick(m, tm), _pick(n, tn, 128)
    in_specs = [pl.BlockSpec((tm, k), lambda j, i: (i, 0)),
                pl.BlockSpec((None, k, tn), lambda j, i: (lyr, 0, j))]
    args = [x, w]
    if res is not None:
        in_specs.append(pl.BlockSpec((tm, tn), lambda j, i: (i, j)))
        args.append(res)
    return pl.pallas_call(
        functools.partial(_mm_body, has_res=res is not None),
        grid=(n // tn, m // tm),
        in_specs=in_specs,
        out_specs=pl.BlockSpec((tm, tn), lambda j, i: (i, j)),
        out_shape=jax.ShapeDtypeStruct((m, n), out_dtype),
        scratch_shapes=[pltpu.VMEM((k, tn), BF16)],
        compiler_params=_params(("arbitrary", "arbitrary"), vmem_mib),
        name="mm_res" if res is not None else "mm",
    )(*args)


def _mm_rows_body(*refs, has_norm):
    if has_norm:
        x_ref, w_ref, r_ref, g_ref, o_ref, n_ref, wb_ref = refs
    else:
        x_ref, w_ref, r_ref, o_ref, wb_ref = refs

    @pl.when(pl.program_id(0) == 0)
    def _():
        wb_ref[...] = w_ref[...].astype(BF16)

    acc = r_ref[...] + jnp.dot(x_ref[...], wb_ref[...], preferred_element_type=F32)
    o_ref[...] = acc
    if has_norm:
        n_ref[...] = _rms(acc, g_ref[...]).astype(n_ref.dtype)


def _mm_rows(x, w, lyr, res, *, tm, gain=None, vmem_mib=56):
    m, k = x.shape
    n = w.shape[-1]
    tm = _pick(m, tm)
    row = lambda width: pl.BlockSpec((tm, width), lambda i: (i, 0))
    in_specs = [row(k), pl.BlockSpec((None, k, n), lambda i: (lyr, 0, 0), pipeline_mode=pl.Buffered(1)), row(n)]
    args = [x, w, res]
    out_specs, out_shape = [row(n)], [jax.ShapeDtypeStruct((m, n), F32)]
    if gain is not None:
        in_specs.append(pl.BlockSpec((1, n), lambda i: (0, 0)))
        args.append(gain.reshape(1, n))
        out_specs.append(row(n))
        out_shape.append(jax.ShapeDtypeStruct((m, n), BF16))
    out = pl.pallas_call(
        functools.partial(_mm_rows_body, has_norm=gain is not None),
        grid=(m // tm,),
        in_specs=in_specs,
        out_specs=out_specs,
        out_shape=out_shape,
        scratch_shapes=[pltpu.VMEM((k, n), BF16)],
        compiler_params=_params(("arbitrary",), vmem_mib),
        name="mm_rows",
    )(*args)
    return out if gain is not None else out[0]


def _swiglu_math(x, wg, wu):
    a = jnp.dot(x, wg, preferred_element_type=F32)
    b = jnp.dot(x, wu, preferred_element_type=F32)
    return (a * jax.nn.sigmoid(a) * b).astype(BF16)


def _swiglu_body(x_ref, wg_ref, wu_ref, o_ref, wgb_ref, wub_ref):
    @pl.when(pl.program_id(1) == 0)
    def _():
        wgb_ref[...] = wg_ref[...].astype(BF16)
        wub_ref[...] = wu_ref[...].astype(BF16)

    o_ref[...] = _swiglu_math(x_ref[...], wgb_ref[...], wub_ref[...])


def _swiglu(x, wg, wu, lyr, *, tm, tn, vmem_mib=48):
    m, k = x.shape
    n = wg.shape[-1]
    tm, tn = _pick(m, tm), _pick(n, tn, 128)
    wspec = pl.BlockSpec((None, k, tn), lambda j, i: (lyr, 0, j))
    return pl.pallas_call(
        _swiglu_body,
        grid=(n // tn, m // tm),
        in_specs=[pl.BlockSpec((tm, k), lambda j, i: (i, 0)), wspec, wspec],
        out_specs=pl.BlockSpec((tm, tn), lambda j, i: (i, j)),
        out_shape=jax.ShapeDtypeStruct((m, n), BF16),
        scratch_shapes=[pltpu.VMEM((k, tn), BF16), pltpu.VMEM((k, tn), BF16)],
        compiler_params=_params(("arbitrary", "arbitrary"), vmem_mib),
        name="swiglu",
    )(x, wg, wu)


def _ple_body(x_ref, pp_ref, ps_ref, wg_ref, wp_ref, r_ref, o_ref, wgb_ref, wpb_ref, pb_ref, *, np_tiles):
    i = pl.program_id(1)

    @pl.when(i == 0)
    def _():
        wgb_ref[...] = wg_ref[...].astype(BF16)
        wpb_ref[...] = wp_ref[...].astype(BF16)

    @pl.when(i < np_tiles)
    def _():
        pb_ref[...] = pp_ref[...].astype(BF16)

    @pl.when(i >= np_tiles)
    def _():
        pb_ref[...] = ps_ref[...].astype(BF16)

    a = jnp.dot(x_ref[...], wgb_ref[...], preferred_element_type=F32)
    b = jnp.dot(pb_ref[...], wpb_ref[...], preferred_element_type=F32)
    o_ref[...] = r_ref[...] + jax.nn.sigmoid(a) * b


def _ple_rows_body(x_ref, pp_ref, ps_ref, wg_ref, wp_ref, r_ref, g_ref, o_ref, n_ref,
                   wgb_ref, wpb_ref, pb_ref, *, np_tiles):
    i = pl.program_id(0)

    @pl.when(i == 0)
    def _():
        wgb_ref[...] = wg_ref[...].astype(BF16)
        wpb_ref[...] = wp_ref[...].astype(BF16)

    @pl.when(i < np_tiles)
    def _():
        pb_ref[...] = pp_ref[...].astype(BF16)

    @pl.when(i >= np_tiles)
    def _():
        pb_ref[...] = ps_ref[...].astype(BF16)

    a = jnp.dot(x_ref[...], wgb_ref[...], preferred_element_type=F32)
    b = jnp.dot(pb_ref[...], wpb_ref[...], preferred_element_type=F32)
    out = r_ref[...] + jax.nn.sigmoid(a) * b
    o_ref[...] = out
    n_ref[...] = _rms(out, g_ref[...]).astype(n_ref.dtype)


def _ple_rows(x, p_p, p_s, wg, wp, res, lyr, gain, *, tm, vmem_mib=58):
    m, k = x.shape
    mp, ms, kp = p_p.shape[1], p_s.shape[1], p_p.shape[-1]
    n = wg.shape[-1]
    tm = _pick(math.gcd(mp, ms), tm)
    npt = mp // tm
    row = lambda width: pl.BlockSpec((tm, width), lambda i: (i, 0))
    once = lambda rows: pl.BlockSpec((None, rows, n), lambda i: (lyr, 0, 0), pipeline_mode=pl.Buffered(1))
    return pl.pallas_call(
        functools.partial(_ple_rows_body, np_tiles=npt),
        grid=(m // tm,),
        in_specs=[row(k),
                  pl.BlockSpec((None, tm, kp), lambda i: (lyr, jnp.minimum(i, npt - 1), 0)),
                  pl.BlockSpec((None, tm, kp), lambda i: (lyr, jnp.maximum(i - npt, 0), 0)),
                  once(k), once(kp), row(n), pl.BlockSpec((1, n), lambda i: (0, 0))],
        out_specs=[row(n), row(n)],
        out_shape=[jax.ShapeDtypeStruct((m, n), F32), jax.ShapeDtypeStruct((m, n), BF16)],
        scratch_shapes=[pltpu.VMEM((k, n), BF16), pltpu.VMEM((kp, n), BF16), pltpu.VMEM((tm, kp), BF16)],
        compiler_params=_params(("arbitrary",), vmem_mib),
        name="ple_rows",
    )(x, p_p, p_s, wg, wp, res, gain.reshape(1, n))


def _ple(x, p_p, p_s, wg, wp, res, lyr, *, tm, tn, vmem_mib=48):
    m, k = x.shape
    mp, ms, kp = p_p.shape[1], p_s.shape[1], p_p.shape[-1]
    n = wg.shape[-1]
    tm, tn = _pick(math.gcd(mp, ms), tm), _pick(n, tn, 128)
    npt = mp // tm
    return pl.pallas_call(
        functools.partial(_ple_body, np_tiles=npt),
        grid=(n // tn, m // tm),
        in_specs=[pl.BlockSpec((tm, k), lambda j, i: (i, 0)),
                  pl.BlockSpec((None, tm, kp), lambda j, i: (lyr, jnp.minimum(i, npt - 1), 0)),
                  pl.BlockSpec((None, tm, kp), lambda j, i: (lyr, jnp.maximum(i - npt, 0), 0)),
                  pl.BlockSpec((None, k, tn), lambda j, i: (lyr, 0, j)),
                  pl.BlockSpec((None, kp, tn), lambda j, i: (lyr, 0, j)),
                  pl.BlockSpec((tm, tn), lambda j, i: (i, j))],
        out_specs=pl.BlockSpec((tm, tn), lambda j, i: (i, j)),
        out_shape=jax.ShapeDtypeStruct((m, n), F32),
        scratch_shapes=[pltpu.VMEM((k, tn), BF16), pltpu.VMEM((kp, tn), BF16), pltpu.VMEM((tm, kp), BF16)],
        compiler_params=_params(("arbitrary", "arbitrary"), vmem_mib),
        name="ple",
    )(x, p_p, p_s, wg, wp, res)


def _moe_weights(w_hbms, stages, bf16s, sem, e_ref, start_ref, nxt_ref, lyr):
    j, i = pl.program_id(0), pl.program_id(1)
    tn = stages[0].shape[1]

    def copies(e, col):
        col = pl.multiple_of(col, LANES)
        return [pltpu.make_async_copy(w.at[lyr, e, :, pl.ds(col, tn)], st, sem.at[n])
                for n, (w, st) in enumerate(zip(w_hbms, stages))]

    @pl.when(jnp.logical_and(j == 0, i == 0))
    def _():
        for cp in copies(e_ref[0], 0):
            cp.start()

    @pl.when(start_ref[i] == 1)
    def _():
        for cp in copies(e_ref[i], j * tn):
            cp.wait()
        rows = 128

        def cast_chunk(c, carry):
            r = pl.multiple_of(c * rows, rows)
            for st, wb in zip(stages, bf16s):
                wb[pl.ds(r, rows), :] = st[pl.ds(r, rows), :].astype(BF16)
            return carry
        lax.fori_loop(0, stages[0].shape[0] // rows, cast_chunk, 0)
        nxt = nxt_ref[i]

        @pl.when(nxt >= 0)
        def _():
            for cp in copies(nxt, j * tn):
                cp.start()

        @pl.when(jnp.logical_and(nxt < 0, j + 1 < pl.num_programs(0)))
        def _():
            for cp in copies(e_ref[0], (j + 1) * tn):
                cp.start()


def _active_prefix(act_ref, i, per):
    n_act = act_ref[i * per]
    for s in range(1, per):
        n_act = n_act + act_ref[i * per + s]
    return n_act


def _prefix_rows(o_ref, n_act, fn):
    tm = o_ref.shape[0]
    for c in range(tm // MOE_SUB + 1):
        rows = c * MOE_SUB

        @pl.when(n_act == c)
        def _():
            if rows:
                o_ref[:rows, :] = fn(rows)
            if rows < tm:
                o_ref[rows:, :] = jnp.zeros((tm - rows, o_ref.shape[1]), o_ref.dtype)


def _moe_up_body(e_ref, act_ref, start_ref, nxt_ref, x_ref, wg_hbm, wu_hbm, o_ref,
                 wgb_ref, wub_ref, sg_ref, su_ref, sem, *, lyr):
    _moe_weights((wg_hbm, wu_hbm), (sg_ref, su_ref), (wgb_ref, wub_ref), sem, e_ref, start_ref, nxt_ref, lyr)
    n_act = _active_prefix(act_ref, pl.program_id(1), x_ref.shape[0] // MOE_SUB)
    _prefix_rows(o_ref, n_act, lambda rows: _swiglu_math(x_ref[:rows, :], wgb_ref[...], wub_ref[...]))


def _moe_down_body(e_ref, act_ref, start_ref, nxt_ref, h_ref, wd_hbm, o_ref, wdb_ref, sd_ref, sem, *, lyr):
    _moe_weights((wd_hbm,), (sd_ref,), (wdb_ref,), sem, e_ref, start_ref, nxt_ref, lyr)
    n_act = _active_prefix(act_ref, pl.program_id(1), h_ref.shape[0] // MOE_SUB)
    _prefix_rows(o_ref, n_act,
                 lambda rows: jnp.dot(h_ref[:rows, :], wdb_ref[...], preferred_element_type=F32))


def _moe_matmul(body, x, ws, sched, lyr, *, tn, out_dtype, vmem_mib, name):
    ns, k = x.shape
    n = ws[0].shape[-1]
    tm, tn = MOE_TM, _pick(n, tn, 128)
    grid_spec = pltpu.PrefetchScalarGridSpec(
        num_scalar_prefetch=len(sched),
        grid=(n // tn, ns // tm),
        in_specs=[pl.BlockSpec((tm, k), lambda j, i, *_: (i, 0))]
                 + [pl.BlockSpec(memory_space=pl.ANY)] * len(ws),
        out_specs=pl.BlockSpec((tm, tn), lambda j, i, *_: (i, j)),
        scratch_shapes=[pltpu.VMEM((k, tn), BF16)] * len(ws) + [pltpu.VMEM((k, tn), F32)] * len(ws)
                       + [pltpu.SemaphoreType.DMA((len(ws),))],
    )
    return pl.pallas_call(
        functools.partial(body, lyr=lyr),
        grid_spec=grid_spec,
        out_shape=jax.ShapeDtypeStruct((ns, n), out_dtype),
        compiler_params=_params(("arbitrary", "arbitrary"), vmem_mib),
        name=name,
    )(*sched, x, *ws)


def _row_gather_copy(x_hbm, buf, sem, slot, src_row, dst_row, rows):
    return pltpu.make_async_copy(x_hbm.at[pl.ds(src_row, rows)], buf.at[slot, pl.ds(dst_row, rows)],
                                 sem.at[slot])


GATHER_GROUP = 8


def _moe_gather_body(tok_ref, real_ref, x_hbm, o_ref, buf, sem):
    i = pl.program_id(0)
    tm = o_ref.shape[0]
    pieces = o_ref.shape[1] // LANES
    grp_rows = GATHER_GROUP * pieces

    def groups(blk):
        return (real_ref[blk] + GATHER_GROUP - 1) // GATHER_GROUP

    def issue(blk, slot):
        def body(g, carry):
            for u in range(GATHER_GROUP):
                r = g * GATHER_GROUP + u
                _row_gather_copy(x_hbm, buf, sem, slot, tok_ref[blk * tm + r] * pieces, r * pieces,
                                 pieces).start(priority=u % 2)
            return carry
        lax.fori_loop(0, groups(blk), body, 0)

    @pl.when(i == 0)
    def _():
        issue(0, 0)

    @pl.when(i + 1 < pl.num_programs(0))
    def _():
        issue(i + 1, (i + 1) % 2)

    slot = i % 2
    cur = buf.at[slot]
    n_grp = groups(i)

    @pl.when(n_grp > 0)
    def _():
        _row_gather_copy(x_hbm, buf, sem, slot, 0, 0, n_grp * grp_rows).wait()

    def zero(g, carry):
        cur[pl.ds(pl.multiple_of(g * grp_rows, grp_rows), grp_rows), :] = jnp.zeros((grp_rows, LANES), F32)
        return carry
    lax.fori_loop(n_grp, tm // GATHER_GROUP, zero, 0)

    o_ref[...] = jnp.concatenate([cur[pl.ds(s, tm, stride=pieces), :] for s in range(pieces)],
                                 axis=-1).astype(o_ref.dtype)


def _moe_gather(x_rows, slot_tok, blk_real, d):
    n_slots = slot_tok.shape[0]
    tm = MOE_TM
    grid_spec = pltpu.PrefetchScalarGridSpec(
        num_scalar_prefetch=2,
        grid=(n_slots // tm,),
        in_specs=[pl.BlockSpec(memory_space=pl.ANY)],
        out_specs=pl.BlockSpec((tm, d), lambda i, *_: (i, 0)),
        scratch_shapes=[pltpu.VMEM((2, tm * (d // LANES), LANES), F32), pltpu.SemaphoreType.DMA((2,))],
    )
    return pl.pallas_call(
        _moe_gather_body,
        grid_spec=grid_spec,
        out_shape=jax.ShapeDtypeStruct((n_slots, d), BF16),
        compiler_params=_params(("arbitrary",), 32),
        name="moe_gather",
    )(slot_tok, blk_real, x_rows)


def _moe_ffn(hn_rows, e_idx, gates, wg, wu, wd, lyr):
    m = e_idx.shape[0]
    d = wg.shape[2]
    n_exp = wg.shape[1]
    top_k = e_idx.shape[-1]
    tm = MOE_TM
    n_asg = m * top_k
    e_flat = e_idx.reshape(n_asg)
    onehot = (e_flat[None, :] == jnp.arange(n_exp, dtype=jnp.int32)[:, None]).astype(jnp.int32)
    counts = jnp.sum(onehot, axis=1)
    start = jnp.cumsum(counts) - counts
    padded = (counts + tm - 1) // tm * tm
    pad_end = jnp.cumsum(padded)
    pad_start = pad_end - padded
    order = jnp.argsort(e_flat).astype(jnp.int32)
    rank = jnp.argsort(order).astype(jnp.int32)
    slot_of = rank + jnp.sum(onehot * (pad_start - start)[:, None], axis=0).astype(jnp.int32)
    n_blocks = n_asg // tm + n_exp
    n_slots = n_blocks * tm
    n_used = (pad_end[-1] // tm).astype(jnp.int32)
    blk = jnp.arange(n_blocks, dtype=jnp.int32)
    blk_start = jnp.minimum(blk, n_used - 1) * tm
    blk_expert = jnp.sum((pad_end[None, :] <= blk_start[:, None]).astype(jnp.int32), axis=1)
    blk_expert = jnp.minimum(blk_expert, n_exp - 1).astype(jnp.int32)
    sub_start = jnp.arange(n_slots // MOE_SUB, dtype=jnp.int32) * MOE_SUB
    sub_expert = jnp.repeat(blk_expert, tm // MOE_SUB)
    real_end = (pad_start + counts)[sub_expert]
    sub_active = jnp.logical_and(sub_start < real_end, sub_start < pad_end[-1]).astype(jnp.int32)
    slot_r = jnp.arange(n_slots, dtype=jnp.int32) - jnp.repeat(pad_start[sub_expert], MOE_SUB)
    slot_src = jnp.repeat(start[sub_expert], MOE_SUB) + slot_r
    slot_real = jnp.logical_and(slot_r >= 0, slot_r < jnp.repeat(counts[sub_expert], MOE_SUB))
    slot_tok = jnp.where(slot_real, order[jnp.clip(slot_src, 0, n_asg - 1)] // top_k, 0).astype(jnp.int32)

    run_start = jnp.concatenate([jnp.ones((1,), jnp.int32),
                                 (blk_expert[1:] != blk_expert[:-1]).astype(jnp.int32)])
    later_start = jnp.where(run_start == 1, blk, n_blocks)
    nxt_blk = lax.cummin(jnp.concatenate([later_start[1:], jnp.full((1,), n_blocks, jnp.int32)]),
                         axis=0, reverse=True)
    run_next = jnp.where(nxt_blk < n_blocks, blk_expert[jnp.minimum(nxt_blk, n_blocks - 1)], -1)
    sched = (blk_expert, sub_active, run_start, run_next.astype(jnp.int32))

    blk_real = jnp.clip((pad_start + counts)[blk_expert] - blk * tm, 0, tm)
    blk_real = jnp.where(blk < n_used, blk_real, 0).astype(jnp.int32)
    x_sorted = _moe_gather(hn_rows, slot_tok, blk_real, d)
    h_sorted = _moe_matmul(_moe_up_body, x_sorted, (wg, wu), sched, lyr, tn=1024, out_dtype=BF16,
                           vmem_mib=48, name="moe_up")
    y_sorted = _moe_matmul(_moe_down_body, h_sorted, (wd,), sched, lyr, tn=512, out_dtype=F32,
                           vmem_mib=48, name="moe_down")
    sl = slot_of.reshape(m, top_k)
    y = y_sorted[sl[:, 0]] * gates[:, 0:1]
    for t in range(1, top_k):
        y = y + y_sorted[sl[:, t]] * gates[:, t:t + 1]
    return y


def _rel_buckets(n_q, n_k_past, n_k):
    rel = (np.arange(n_k) - n_k_past)[None, :] - np.arange(n_q)[:, None]
    half = N_BUCKETS // 2
    max_exact = half // 2
    n = np.abs(rel)
    nf = np.maximum(n, 1).astype(np.float32)
    large = max_exact + (np.log(nf / np.float32(max_exact)) / np.float32(math.log(REL_MAX_DIST / max_exact))
                         * np.float32(half - max_exact)).astype(np.int32)
    large = np.minimum(large, half - 1)
    return (np.where(rel > 0, half, 0) + np.where(n < max_exact, n, large)).astype(np.int32)


def _build_bias(bias_ref, sink_ref, bucket_ref, table_ref, sinks_ref):
    bucket = bucket_ref[...]
    nq = bucket.shape[1]
    gq = bias_ref.shape[2] // nq
    for kh in range(bias_ref.shape[0]):
        for g in range(gq):
            n = kh * gq + g
            acc = jnp.zeros(bucket.shape, F32)
            for b in range(N_BUCKETS):
                acc = jnp.where(bucket == b, table_ref[b, n], acc)
            bias_ref[kh, :, g * nq:(g + 1) * nq] = acc
            sink_ref[kh, :, g * nq:(g + 1) * nq] = jnp.full((1, nq), sinks_ref[n], F32)


def _attn_items(items, bias_ref, sink_ref):
    nq = items[0][0].shape[0]
    gq = items[0][0].shape[-1] // HEAD_DIM // KV_HEADS
    head = lambda x, n: x[:, n * HEAD_DIM:(n + 1) * HEAD_DIM]
    scores = []
    for q, k, _, valid in items:
        for kh in range(KV_HEADS):
            q4 = jnp.concatenate([head(q, kh * gq + g) for g in range(gq)], axis=0)
            s = lax.dot_general(head(k, kh), q4, (((1,), (1,)), ((), ())), preferred_element_type=F32)
            scores.append((s, kh, valid))
    probs = []
    for s, kh, valid in scores:
        s = s + bias_ref[kh]
        if valid is not None:
            s = jnp.where(valid, s, NEG_INF)
        sink = sink_ref[kh]
        mx = jnp.maximum(jnp.max(s, axis=0, keepdims=True), sink)
        e = jnp.exp(s - mx)
        den = jnp.sum(e, axis=0, keepdims=True) + jnp.exp(sink - mx)
        probs.append((e * (1.0 / den)).astype(BF16))
    outs = []
    for i, (_, _, v, _) in enumerate(items):
        heads = []
        for kh in range(KV_HEADS):
            o4 = lax.dot_general(probs[i * KV_HEADS + kh], head(v, kh), (((0,), (0,)), ((), ())),
                                 preferred_element_type=F32)
            heads.extend(o4[g * nq:(g + 1) * nq] for g in range(gq))
        outs.append(jnp.concatenate(heads, axis=-1))
    return outs


def _attn_prompt_body(q_ref, kp_ref, kc_ref, vp_ref, vc_ref, bucket_ref, table_ref, sinks_ref,
                      o_ref, bias_ref, sink_ref, *, chunks):
    j = pl.program_id(1)

    @pl.when(jnp.logical_and(pl.program_id(0) == 0, j == 0))
    def _():
        _build_bias(bias_ref, sink_ref, bucket_ref, table_ref, sinks_ref)

    q = (q_ref[...] * ATTN_SCALE).astype(BF16)
    k = jnp.concatenate([kp_ref[...], kc_ref[...]], axis=0).astype(BF16)
    v = jnp.concatenate([vp_ref[...], vc_ref[...]], axis=0).astype(BF16)
    band = WINDOW + CHUNK
    items = []
    for c in range(chunks):
        lo = c * CHUNK
        key_valid = None
        if lo < WINDOW:
            key_pos = j * (chunks * CHUNK) + lo - WINDOW + lax.broadcasted_iota(jnp.int32, (band, 1), 0)
            key_valid = key_pos >= 0
        items.append((q[lo:lo + CHUNK], k[lo:lo + band], v[lo:lo + band], key_valid))
    for c, o in enumerate(_attn_items(items, bias_ref, sink_ref)):
        o_ref[c * CHUNK:(c + 1) * CHUNK, :] = o


def _attn_prompt(z, n_b, seq, d_ssm, d_attn, table, sinks):
    d_kv = KV_HEADS * HEAD_DIM
    assert d_ssm % d_attn == 0 and d_ssm % d_kv == 0 and d_attn % d_kv == 0
    qb = _pick(seq, 4 * CHUNK, 2 * CHUNK)
    assert qb % WINDOW == 0 and seq % qb == 0
    chunks = qb // CHUNK
    nj = seq // qb
    r = qb // WINDOW
    kcol = (d_ssm + d_attn) // d_kv
    gq = d_attn // HEAD_DIM // KV_HEADS
    band = WINDOW + CHUNK
    bucket = jnp.asarray(_rel_buckets(CHUNK, WINDOW, band).T)

    def prev(b, j):
        return b * (seq // WINDOW) + jnp.maximum(j * r - 1, 0)

    return pl.pallas_call(
        functools.partial(_attn_prompt_body, chunks=chunks),
        grid=(n_b, nj),
        in_specs=[pl.BlockSpec((qb, d_attn), lambda b, j: (b * nj + j, d_ssm // d_attn)),
                  pl.BlockSpec((WINDOW, d_kv), lambda b, j: (prev(b, j), kcol)),
                  pl.BlockSpec((qb, d_kv), lambda b, j: (b * nj + j, kcol)),
                  pl.BlockSpec((WINDOW, d_kv), lambda b, j: (prev(b, j), kcol + 1)),
                  pl.BlockSpec((qb, d_kv), lambda b, j: (b * nj + j, kcol + 1)),
                  pl.BlockSpec((band, CHUNK), lambda b, j: (0, 0)),
                  pl.BlockSpec(memory_space=pltpu.SMEM),
                  pl.BlockSpec(memory_space=pltpu.SMEM)],
        out_specs=pl.BlockSpec((qb, d_attn), lambda b, j: (b * nj + j, 0)),
        out_shape=jax.ShapeDtypeStruct((n_b * seq, d_attn), F32),
        scratch_shapes=[pltpu.VMEM((KV_HEADS, band, gq * CHUNK), F32),
                        pltpu.VMEM((KV_HEADS, 1, gq * CHUNK), F32)],
        compiler_params=_params(("arbitrary", "arbitrary"), 40),
        name="attn_prompt",
    )(z, z, z, z, z, bucket, table, sinks)


def _attn_sample_body(q_ref, kc_ref, kn_ref, vc_ref, vn_ref, bucket_ref, table_ref, sinks_ref,
                      o_ref, bias_ref, sink_ref, *, n_bs, n_new, n_past):
    @pl.when(pl.program_id(0) == 0)
    def _():
        _build_bias(bias_ref, sink_ref, bucket_ref, table_ref, sinks_ref)

    q = (q_ref[...] * ATTN_SCALE).astype(BF16)
    kc, kn = kc_ref[...].astype(BF16), kn_ref[...].astype(BF16)
    vc, vn = vc_ref[...].astype(BF16), vn_ref[...].astype(BF16)
    items = []
    for b in range(n_bs):
        new = slice(b * n_new, (b + 1) * n_new)
        past = slice(b * n_past, (b + 1) * n_past)
        items.append((q[new], jnp.concatenate([kc[past], kn[new]], axis=0),
                      jnp.concatenate([vc[past], vn[new]], axis=0), None))
    for b, o in enumerate(_attn_items(items, bias_ref, sink_ref)):
        o_ref[b * n_new:(b + 1) * n_new, :] = o


def _attn_sample(z, row0, n_b, n_new, d_ssm, d_attn, cache_k, cache_v, lyr, table, sinks):
    d_kv = KV_HEADS * HEAD_DIM
    n_past = cache_k.shape[2]
    n_bs = _pick(n_b, 4, 1)
    rows = n_bs * n_new
    assert row0 % rows == 0
    kcol = (d_ssm + d_attn) // d_kv
    gq = d_attn // HEAD_DIM // KV_HEADS
    n_k = n_past + n_new
    bucket = jnp.asarray(_rel_buckets(n_new, n_past, n_k).T)
    ck = cache_k.reshape(-1, d_kv)
    cv = cache_v.reshape(-1, d_kv)
    rb = row0 // rows
    nsteps = n_b // n_bs
    cache_spec = pl.BlockSpec((n_bs * n_past, d_kv), lambda b: (lyr * nsteps + b, 0))
    return pl.pallas_call(
        functools.partial(_attn_sample_body, n_bs=n_bs, n_new=n_new, n_past=n_past),
        grid=(nsteps,),
        in_specs=[pl.BlockSpec((rows, d_attn), lambda b: (rb + b, d_ssm // d_attn)),
                  cache_spec,
                  pl.BlockSpec((rows, d_kv), lambda b: (rb + b, kcol)),
                  cache_spec,
                  pl.BlockSpec((rows, d_kv), lambda b: (rb + b, kcol + 1)),
                  pl.BlockSpec((n_k, n_new), lambda b: (0, 0)),
                  pl.BlockSpec(memory_space=pltpu.SMEM),
                  pl.BlockSpec(memory_space=pltpu.SMEM)],
        out_specs=pl.BlockSpec((rows, d_attn), lambda b: (b, 0)),
        out_shape=jax.ShapeDtypeStruct((n_b * n_new, d_attn), F32),
        scratch_shapes=[pltpu.VMEM((KV_HEADS, n_k, gq * n_new), F32),
                        pltpu.VMEM((KV_HEADS, 1, gq * n_new), F32)],
        compiler_params=_params(("arbitrary",), 40),
        name="attn_sample",
    )(z, ck, z, cv, z, bucket, table, sinks)


def _tile_lanes(x, reps):
    return jnp.concatenate([x] * reps, axis=-1)


def _s5_tables_body(are_ref, aim_ref, ldt_ref, btr_ref, bti_ref, cre_ref, cim_ref, d_ref,
                    dk_ref, we_ref, ws_ref, a_ref):
    rows, p = cre_ref.shape
    gb = rows // SSM_GROUP_CH
    a_re, a_im = are_ref[...], aim_ref[...]
    dt = jnp.exp(ldt_ref[...])
    mag = jnp.exp(a_re * dt)
    ab_re, ab_im = mag * jnp.cos(a_im * dt), mag * jnp.sin(a_im * dt)

    x, y = ab_re - 1.0, ab_im
    den = a_re * a_re + a_im * a_im
    co_re = (x * a_re + y * a_im) / den
    co_im = (y * a_re - x * a_im) / den
    bb_re = co_re * btr_ref[...] - co_im * bti_ref[...]
    bb_im = co_re * bti_ref[...] + co_im * btr_ref[...]
    c_re, c_im = cre_ref[...], cim_ref[...]

    r_idx = lax.broadcasted_iota(jnp.int32, (rows, 1), 0)
    own_state = (r_idx // SSM_GROUP_CH) == (lax.broadcasted_iota(jnp.int32, (1, gb * p), 1) // p)
    same_group = (r_idx // SSM_GROUP_CH) == (lax.broadcasted_iota(jnp.int32, (1, rows), 1) // SSM_GROUP_CH)
    diag = r_idx == lax.broadcasted_iota(jnp.int32, (1, rows), 1)

    def place(v_re, v_im):
        return jnp.concatenate([jnp.where(own_state, _tile_lanes(v_re, gb), 0.0),
                                jnp.where(own_state, _tile_lanes(v_im, gb), 0.0)], axis=-1)

    dims = (((1,), (1,)), ((), ()))
    bb_cat = jnp.concatenate([bb_re, -bb_im], axis=-1)
    pw_re, pw_im = jnp.ones_like(ab_re), jnp.zeros_like(ab_re)
    for j in range(S5_T + 1):
        ca_re = c_re * pw_re - c_im * pw_im
        ca_im = c_re * pw_im + c_im * pw_re
        if j < S5_T:
            kj = lax.dot_general(bb_cat, jnp.concatenate([ca_re, ca_im], axis=-1), dims,
                                 precision=HIGHEST, preferred_element_type=F32)
            kj = jnp.where(same_group, kj, 0.0)
            if j == 0:
                kj = kj + jnp.where(diag, d_ref[...], 0.0)
            dk_ref[j] = kj.astype(dk_ref.dtype)
            s = S5_T - 1 - j
            we_ref[s * rows:(s + 1) * rows, :] = place(
                pw_re * bb_re - pw_im * bb_im, pw_re * bb_im + pw_im * bb_re).astype(we_ref.dtype)
        if j >= 1:
            ws_ref[(j - 1) * rows:j * rows, :] = place(ca_re, -ca_im).astype(ws_ref.dtype)
        if j == S5_T:
            first = r_idx == (lax.broadcasted_iota(jnp.int32, (1, gb * p), 1) // p) * SSM_GROUP_CH
            a_ref[...] = jnp.concatenate(
                [jnp.sum(jnp.where(first, _tile_lanes(pw_re, gb), 0.0), axis=0, keepdims=True),
                 jnp.sum(jnp.where(first, _tile_lanes(pw_im, gb), 0.0), axis=0, keepdims=True)], axis=-1)
        pw_re, pw_im = pw_re * ab_re - pw_im * ab_im, pw_re * ab_im + pw_im * ab_re


def _s5_tables(a_re, a_im, log_dt, b_re, b_im, c_re, c_im, d_skip):
    h, p = a_re.shape
    g = SSM_GROUP_CH
    hb = h // S5_GB
    sw = 2 * S5_GB * p
    rep = lambda v: jnp.repeat(v, g, axis=0).reshape(hb, LANES, -1)
    blk = lambda w: pl.BlockSpec((None, LANES, w), lambda i: (i, 0, 0))
    wide = pl.BlockSpec((None, S5_F, sw), lambda i: (i, 0, 0))
    return pl.pallas_call(
        _s5_tables_body,
        grid=(hb,),
        in_specs=[blk(p), blk(p), blk(1), blk(p), blk(p), blk(p), blk(p), blk(1)],
        out_specs=[pl.BlockSpec((None, S5_T, LANES, LANES), lambda i: (i, 0, 0, 0)), wide, wide,
                   pl.BlockSpec((None, 1, sw), lambda i: (i, 0, 0))],
        out_shape=[jax.ShapeDtypeStruct((hb, S5_T, LANES, LANES), BF16),
                   jax.ShapeDtypeStruct((hb, S5_F, sw), BF16),
                   jax.ShapeDtypeStruct((hb, S5_F, sw), BF16),
                   jax.ShapeDtypeStruct((hb, 1, sw), F32)],
        compiler_params=_params(("arbitrary",), 40),
        name="s5_tables",
    )(rep(a_re), rep(a_im), rep(log_dt.reshape(h, 1)),
      jnp.swapaxes(b_re, 1, 2).reshape(hb, LANES, p), jnp.swapaxes(b_im, 1, 2).reshape(hb, LANES, p),
      c_re.reshape(hb, LANES, p), c_im.reshape(hb, LANES, p), d_skip.reshape(hb, LANES, 1))


def _split_mul(a, p):
    lane = lax.broadcasted_iota(jnp.int32, a.shape, a.ndim - 1)
    sw = pltpu.roll(a, p, axis=a.ndim - 1)
    return jnp.where(lane < p, a, sw), jnp.where(lane < p, -sw, a)


def _cmul(x, mr, mi, p):
    return x * mr + pltpu.roll(x, p, axis=x.ndim - 1) * mi


def _build_toeplitz(m_ref, dk_ref):
    m_ref[...] = jnp.zeros_like(m_ref)
    for s in range(S5_T):
        for t in range(s, S5_T):
            m_ref[s * LANES:(s + 1) * LANES, t * LANES:(t + 1) * LANES] = dk_ref[t - s]


def _s5_outputs(ucat, enter, m_ref, ws_ref):
    return (jnp.dot(ucat, m_ref[...], preferred_element_type=F32)
            + lax.dot_general(enter.astype(BF16), ws_ref[...], (((1,), (1,)), ((), ())),
                              preferred_element_type=F32))


def _glu_rows(o_ref, ga_ref, gb_ref, tile):
    ga, gb = ga_ref[...], gb_ref[...]
    tile = _pick(o_ref.shape[0], tile)
    for r0 in range(0, o_ref.shape[0], tile):
        y = o_ref[r0:r0 + tile, :].astype(BF16)
        o_ref[r0:r0 + tile, :] = (jnp.dot(y, ga, preferred_element_type=F32)
                                  * jax.nn.sigmoid(jnp.dot(y, gb, preferred_element_type=F32)))


def _s5_prompt_body(z_ref, dk_ref, we_ref, ws_ref, a_ref, ga_ref, gb_ref, o_ref, fin_ref, m_ref,
                    *, nb, nc):
    @pl.when(pl.program_id(1) == 0)
    def _():
        _build_toeplitz(m_ref, dk_ref)

    r = nb * nc
    p = a_ref.shape[-1] // 2
    ucat = jnp.concatenate([z_ref[pl.ds(t, r, stride=S5_T), :].astype(BF16) for t in range(S5_T)], axis=-1)
    x = jnp.dot(ucat, we_ref[...], preferred_element_type=F32)
    a = a_ref[...]
    pos = lax.rem(lax.broadcasted_iota(jnp.int32, (r, 1), 0), nc)
    sh = 1
    while sh < nc:
        mr, mi = _split_mul(a, p)
        shifted = jnp.where(pos >= sh, pltpu.roll(x, sh, axis=0), 0.0)
        x = x + _cmul(shifted, mr, mi, p)
        a = _cmul(a, mr, mi, p)
        sh *= 2
    enter = jnp.where(pos >= 1, pltpu.roll(x, 1, axis=0), 0.0)
    fin_ref[...] = jnp.concatenate([x[(b + 1) * nc - 1:(b + 1) * nc] for b in range(nb)], axis=0)
    y = _s5_outputs(ucat, enter, m_ref, ws_ref)
    for t in range(S5_T):
        o_ref[pl.ds(t, r, stride=S5_T), :] = y[:, t * LANES:(t + 1) * LANES]
    _glu_rows(o_ref, ga_ref, gb_ref, 512)


def _s5_sample_body(z_ref, dk_ref, we_ref, ws_ref, a_ref, ga_ref, gb_ref, h0_ref, o_ref, fin_ref, m_ref,
                    *, nb, nc):
    _build_toeplitz(m_ref, dk_ref)
    p = a_ref.shape[-1] // 2
    seq = nc * S5_T
    ucat = jnp.concatenate(
        [jnp.concatenate([z_ref[pl.ds(c * S5_T + t, nb, stride=seq), :].astype(BF16) for t in range(S5_T)],
                         axis=-1) for c in range(nc)], axis=0)
    e = jnp.dot(ucat, we_ref[...], preferred_element_type=F32)
    mr, mi = _split_mul(a_ref[...], p)
    state = h0_ref[...]
    enter = []
    for c in range(nc):
        enter.append(state)
        state = _cmul(state, mr, mi, p) + e[c * nb:(c + 1) * nb]
    fin_ref[...] = state
    y = _s5_outputs(ucat, jnp.concatenate(enter, axis=0), m_ref, ws_ref)
    for c in range(nc):
        for t in range(S5_T):
            o_ref[pl.ds(c * S5_T + t, nb, stride=seq), :] = y[c * nb:(c + 1) * nb, t * LANES:(t + 1) * LANES]
    _glu_rows(o_ref, ga_ref, gb_ref, 512)


def _s5_prompt(z, n_b, seq, tabs, g_a, g_b, lyr, hb):
    dk, w_e, w_s, a_t = tabs
    sw = a_t.shape[-1]
    nbs = _pick(n_b, 4, 1)
    nc = seq // S5_T
    tab = lambda arr: pl.BlockSpec((None,) + arr.shape[1:],
                                   lambda i, b: (lyr * hb + i,) + (0,) * (arr.ndim - 1))
    return pl.pallas_call(
        functools.partial(_s5_prompt_body, nb=nbs, nc=nc),
        grid=(hb, n_b // nbs),
        in_specs=[pl.BlockSpec((nbs * seq, LANES), lambda i, b: (b, i)),
                  tab(dk), tab(w_e), tab(w_s), tab(a_t), tab(g_a), tab(g_b)],
        out_specs=[pl.BlockSpec((nbs * seq, LANES), lambda i, b: (b, i)),
                   pl.BlockSpec((None, None, nbs, sw), lambda i, b: (i, b, 0, 0))],
        out_shape=[jax.ShapeDtypeStruct((n_b * seq, hb * LANES), F32),
                   jax.ShapeDtypeStruct((hb, n_b // nbs, nbs, sw), F32)],
        scratch_shapes=[pltpu.VMEM((S5_F, S5_F), BF16)],
        compiler_params=_params(("arbitrary", "arbitrary"), 58),
        name="s5_prompt",
    )(z, dk, w_e, w_s, a_t, g_a, g_b)


def _s5_sample(z, row0, n_b, n_new, tabs, g_a, g_b, h0, lyr, hb):
    dk, w_e, w_s, a_t = tabs
    sw = a_t.shape[-1]
    rows = n_b * n_new
    assert row0 % rows == 0
    rb = row0 // rows
    tab = lambda arr, off=lyr * hb: pl.BlockSpec((None,) + arr.shape[1:],
                                                 lambda i: (off + i,) + (0,) * (arr.ndim - 1))
    return pl.pallas_call(
        functools.partial(_s5_sample_body, nb=n_b, nc=n_new // S5_T),
        grid=(hb,),
        in_specs=[pl.BlockSpec((rows, LANES), lambda i: (rb, i)),
                  tab(dk), tab(w_e), tab(w_s), tab(a_t), tab(g_a), tab(g_b), tab(h0, 0)],
        out_specs=[pl.BlockSpec((rows, LANES), lambda i: (0, i)),
                   pl.BlockSpec((None, n_b, sw), lambda i: (i, 0, 0))],
        out_shape=[jax.ShapeDtypeStruct((rows, hb * LANES), F32),
                   jax.ShapeDtypeStruct((hb, n_b, sw), F32)],
        scratch_shapes=[pltpu.VMEM((S5_F, S5_F), BF16)],
        compiler_params=_params(("arbitrary",), 52),
        name="s5_sample",
    )(z, dk, w_e, w_s, a_t, g_a, g_b, h0)


def _glu_blocks(glu_w):
    h, g, _ = glu_w.shape
    hb = h // S5_GB
    eye = jnp.eye(S5_GB, dtype=F32)

    def diag(wm):
        wm = wm.reshape(hb, S5_GB, g, g)
        return (eye[None, :, None, :, None] * wm[:, :, :, None, :]).reshape(hb, LANES, LANES).astype(BF16)

    return diag(glu_w[..., :g]), diag(glu_w[..., g:])


def _state_to_lanes(re, im):
    n, h, p = re.shape
    hb = h // S5_GB
    f = lambda v: v.reshape(n, hb, S5_GB * p).transpose(1, 0, 2)
    return jnp.concatenate([f(re), f(im)], axis=-1)


def _state_from_lanes(s):
    nl, hb, n, sw = s.shape
    half = sw // 2
    f = lambda v: v.transpose(0, 2, 1, 3).reshape(nl, n, hb * S5_GB, half // S5_GB)
    return f(s[..., :half]), f(s[..., half:])


def kernel(x_prompt, x_sample, cache_k, cache_v, state_ssm_re, state_ssm_im, p_prompt, p_sample,
           norm_mix, w_in, ssm_a_re, ssm_a_im, ssm_log_dt, ssm_b_re, ssm_b_im, ssm_c_re, ssm_c_im,
           ssm_d, ssm_glu_w, attn_sinks, rel_bias, norm_grp_ssm, norm_grp_attn, w_out, norm_ffn,
           ffn_w_gate, ffn_w_up, ffn_w_down, router_w, moe_w_gate, moe_w_up, moe_w_down,
           norm_ple, ple_w_gate, ple_w_proj, norm_final):
    nb_p, seq, d = x_prompt.shape
    nb_s, n_new, _ = x_sample.shape
    depth = w_in.shape[0]
    d_ssm = norm_grp_ssm.shape[-1]
    d_attn = norm_grp_attn.shape[-1]
    mp, ms = nb_p * seq, nb_s * n_new
    win = min(WINDOW, seq)
    assert d_ssm % LANES == 0 and seq % S5_T == 0 and n_new % S5_T == 0

    h = jnp.concatenate([x_prompt.reshape(mp, d), x_sample.reshape(ms, d)], axis=0)
    p_p, p_s = p_prompt.reshape(depth, mp, -1), p_sample.reshape(depth, ms, -1)

    n_grp = ssm_a_re.shape[1]
    hb = n_grp // S5_GB
    flat = lambda v: v.reshape((depth * n_grp,) + v.shape[2:])
    tabs = _s5_tables(flat(ssm_a_re), flat(ssm_a_im), ssm_log_dt.reshape(-1), flat(ssm_b_re), flat(ssm_b_im),
                      flat(ssm_c_re), flat(ssm_c_im), ssm_d.reshape(-1))
    g_a, g_b = _glu_blocks(flat(ssm_glu_w))

    kv_p, kv_s, fin_ps, fin_ss = [], [], [], []
    xn = _rmsnorm(h, norm_mix[0], BF16)
    for i in range(depth):
        z = _mm(xn, w_in, i, tm=1024, tn=1280, vmem_mib=56)
        ssm_p, fin_p = _s5_prompt(z, nb_p, seq, tabs, g_a, g_b, i, hb)
        ssm_s, fin_s = _s5_sample(z, mp, nb_s, n_new, tabs, g_a, g_b,
                                  _state_to_lanes(state_ssm_re[i], state_ssm_im[i]), i, hb)
        att_p = _attn_prompt(z, nb_p, seq, d_ssm, d_attn, rel_bias, attn_sinks[i])
        att_s = _attn_sample(z, mp, nb_s, n_new, d_ssm, d_attn, cache_k, cache_v, i, rel_bias,
                             attn_sinks[i])
        merged = _merge_norm(ssm_p, ssm_s, att_p, att_s, norm_grp_ssm[i], norm_grp_attn[i])
        if i % 2 == 0:
            h, hn = _mm_rows(merged, w_out, i, h, tm=512, gain=norm_ffn[i])
        else:
            h = _mm_rows(merged, w_out, i, h, tm=512)

        kv_p.append(jnp.stack([z[(b + 1) * seq - win:(b + 1) * seq, d_ssm + d_attn:] for b in range(nb_p)]))
        kv_s.append(z[mp:, d_ssm + d_attn:])
        fin_ps.append(fin_p.reshape(hb, nb_p, -1))
        fin_ss.append(fin_s)

        j = i // 2
        if i % 2 == 0:
            t = _swiglu(hn, ffn_w_gate, ffn_w_up, j, tm=1024, tn=512)
            h = _mm(t, ffn_w_down, j, tm=512, tn=512, res=h, vmem_mib=52)
        else:
            hn, e_idx, gates = _rmsnorm_router(h, norm_ffn[i], router_w[j])
            h = h + _moe_ffn(hn, e_idx, gates, moe_w_gate, moe_w_up, moe_w_down, j)

        hn = _rmsnorm(h, norm_ple[i], BF16)
        if i + 1 < depth:
            h, xn = _ple_rows(hn, p_p, p_s, ple_w_gate, ple_w_proj, h, i, norm_mix[i + 1], tm=512)
        else:
            h = _ple(hn, p_p, p_s, ple_w_gate, ple_w_proj, h, i, tm=1024, tn=1024, vmem_mib=58)

    y_p, y_s = _rmsnorm_split(h, norm_final, mp)
    kv_p = jnp.stack(kv_p).reshape(depth, nb_p, win, 2, KV_HEADS, HEAD_DIM)
    kv_s = jnp.stack(kv_s).reshape(depth, nb_s, n_new, 2, KV_HEADS, HEAD_DIM)
    p_re, p_im = _state_from_lanes(jnp.stack(fin_ps))
    s_re, s_im = _state_from_lanes(jnp.stack(fin_ss))
    return (y_p.reshape(nb_p, seq, d), y_s.reshape(nb_s, n_new, d),
            kv_p[:, :, :, 0], kv_p[:, :, :, 1], p_re, p_im,
            kv_s[:, :, :, 0], kv_s[:, :, :, 1], s_re, s_im)
```

```python
import functools
import math

import numpy as np
import jax
import jax.numpy as jnp
from jax import lax
from jax.experimental import pallas as pl
from jax.experimental.pallas import tpu as pltpu

F32 = jnp.float32
BF16 = jnp.bfloat16
HIGHEST = lax.Precision.HIGHEST

CHUNK = 64
WINDOW = 128
HEAD_DIM = 64
KV_HEADS = 4
SSM_GROUP_CH = 16
SSM_STATE = 64
N_BUCKETS = 32
REL_MAX_DIST = 128
EPS = 1e-6
NEG_INF = -1e30
ATTN_SCALE = HEAD_DIM ** -0.5

LANES = 128
S5_T = 16
S5_GB = LANES // SSM_GROUP_CH
S5_F = S5_T * LANES
MOE_TM = 512
MOE_SUB = 128
MIB = 1024 * 1024


def _pick(n, pref, mult=8):
    best = None
    for d in range(mult, min(n, pref) + 1, mult):
        if n % d == 0:
            best = d
    return best if best is not None else n


def _params(sem, vmem_mib):
    return pltpu.CompilerParams(dimension_semantics=sem, vmem_limit_bytes=int(vmem_mib * MIB))


def _rms(x, g):
    return x * lax.rsqrt(jnp.mean(x * x, axis=-1, keepdims=True) + EPS) * g


def _norm_body(x_ref, g_ref, o_ref):
    o_ref[...] = _rms(x_ref[...], g_ref[...]).astype(o_ref.dtype)


def _rmsnorm(x, g, out_dtype):
    m, d = x.shape
    tm = _pick(m, 512)
    return pl.pallas_call(
        _norm_body,
        grid=(m // tm,),
        in_specs=[pl.BlockSpec((tm, d), lambda i: (i, 0)), pl.BlockSpec((1, d), lambda i: (0, 0))],
        out_specs=pl.BlockSpec((tm, d), lambda i: (i, 0)),
        out_shape=jax.ShapeDtypeStruct((m, d), out_dtype),
        compiler_params=_params(("arbitrary",), 40),
        name="rmsnorm",
    )(x, g.reshape(1, d))


def _norm_split_body(x_ref, g_ref, op_ref, os_ref, *, np_tiles):
    y = _rms(x_ref[...], g_ref[...])

    @pl.when(pl.program_id(0) < np_tiles)
    def _():
        op_ref[...] = y

    @pl.when(pl.program_id(0) >= np_tiles)
    def _():
        os_ref[...] = y


def _rmsnorm_split(x, g, mp):
    m, d = x.shape
    ms = m - mp
    tm = _pick(math.gcd(mp, ms), 512)
    npt = mp // tm
    return pl.pallas_call(
        functools.partial(_norm_split_body, np_tiles=npt),
        grid=(m // tm,),
        in_specs=[pl.BlockSpec((tm, d), lambda i: (i, 0)), pl.BlockSpec((1, d), lambda i: (0, 0))],
        out_specs=[pl.BlockSpec((tm, d), lambda i: (jnp.minimum(i, npt - 1), 0)),
                   pl.BlockSpec((tm, d), lambda i: (jnp.maximum(i - npt, 0), 0))],
        out_shape=[jax.ShapeDtypeStruct((mp, d), F32), jax.ShapeDtypeStruct((ms, d), F32)],
        compiler_params=_params(("arbitrary",), 40),
        name="rmsnorm_split",
    )(x, g.reshape(1, d))


def _norm_router_body(x_ref, g_ref, rw_ref, o_ref, idx_ref, gate_ref):
    y = _rms(x_ref[...], g_ref[...])
    tm, d = y.shape
    pieces = d // LANES
    for s in range(pieces):
        o_ref[pl.ds(s, tm, stride=pieces), :] = y[:, s * LANES:(s + 1) * LANES]
    n_exp = rw_ref.shape[0]
    logits = jnp.concatenate([jnp.sum(y * rw_ref[e:e + 1, :], axis=-1, keepdims=True) for e in range(n_exp)],
                             axis=-1)
    lane = lax.broadcasted_iota(jnp.int32, logits.shape, 1)
    m1 = jnp.max(logits, axis=-1, keepdims=True)
    i1 = jnp.min(jnp.where(logits == m1, lane, n_exp), axis=-1, keepdims=True)
    rest = jnp.where(lane == i1, -jnp.inf, logits)
    m2 = jnp.max(rest, axis=-1, keepdims=True)
    i2 = jnp.min(jnp.where(rest == m2, lane, n_exp), axis=-1, keepdims=True)
    e2 = jnp.exp(m2 - m1)
    den = 1.0 + e2
    idx_ref[...] = jnp.concatenate([i1, i2], axis=-1)
    gate_ref[...] = jnp.concatenate([1.0 / den, e2 / den], axis=-1)


def _rmsnorm_router(x, g, router_w):
    m, d = x.shape
    n_exp = router_w.shape[-1]
    tm = _pick(m, 512)
    return pl.pallas_call(
        _norm_router_body,
        grid=(m // tm,),
        in_specs=[pl.BlockSpec((tm, d), lambda i: (i, 0)),
                  pl.BlockSpec((1, d), lambda i: (0, 0)),
                  pl.BlockSpec((n_exp, d), lambda i: (0, 0))],
        out_specs=[pl.BlockSpec((tm * (d // LANES), LANES), lambda i: (i, 0)),
                   pl.BlockSpec((tm, 2), lambda i: (i, 0)),
                   pl.BlockSpec((tm, 2), lambda i: (i, 0))],
        out_shape=[jax.ShapeDtypeStruct((m * (d // LANES), LANES), F32),
                   jax.ShapeDtypeStruct((m, 2), jnp.int32),
                   jax.ShapeDtypeStruct((m, 2), F32)],
        compiler_params=_params(("arbitrary",), 40),
        name="rmsnorm_router",
    )(x, g.reshape(1, d), router_w.T)


def _merge_norm_body(ap_ref, as_ref, bp_ref, bs_ref, ga_ref, gb_ref, o_ref, *, np_tiles):
    da = ap_ref.shape[-1]

    def emit(a_ref, b_ref):
        o_ref[:, :da] = _rms(a_ref[...], ga_ref[...]).astype(o_ref.dtype)
        o_ref[:, da:] = _rms(b_ref[...], gb_ref[...]).astype(o_ref.dtype)

    @pl.when(pl.program_id(0) < np_tiles)
    def _():
        emit(ap_ref, bp_ref)

    @pl.when(pl.program_id(0) >= np_tiles)
    def _():
        emit(as_ref, bs_ref)


def _merge_norm(a_p, a_s, b_p, b_s, ga, gb):
    (mp, da), (ms, db) = a_p.shape, b_s.shape
    tm = _pick(math.gcd(mp, ms), 512)
    npt, nst = mp // tm, ms // tm
    pspec = lambda w: pl.BlockSpec((tm, w), lambda i: (jnp.minimum(i, npt - 1), 0))
    sspec = lambda w: pl.BlockSpec((tm, w), lambda i: (jnp.maximum(i - npt, 0), 0))
    return pl.pallas_call(
        functools.partial(_merge_norm_body, np_tiles=npt),
        grid=(npt + nst,),
        in_specs=[pspec(da), sspec(da), pspec(db), sspec(db),
                  pl.BlockSpec((1, da), lambda i: (0, 0)), pl.BlockSpec((1, db), lambda i: (0, 0))],
        out_specs=pl.BlockSpec((tm, da + db), lambda i: (i, 0)),
        out_shape=jax.ShapeDtypeStruct((mp + ms, da + db), BF16),
        compiler_params=_params(("arbitrary",), 40),
        name="merge_norm",
    )(a_p, a_s, b_p, b_s, ga.reshape(1, da), gb.reshape(1, db))


def _mm_body(*refs, has_res):
    if has_res:
        x_ref, w_ref, r_ref, o_ref, wb_ref = refs
    else:
        x_ref, w_ref, o_ref, wb_ref = refs

    @pl.when(pl.program_id(1) == 0)
    def _():
        wb_ref[...] = w_ref[...].astype(BF16)

    acc = jnp.dot(x_ref[...], wb_ref[...], preferred_element_type=F32)
    if has_res:
        acc = r_ref[...] + acc
    o_ref[...] = acc.astype(o_ref.dtype)


def _mm(x, w, lyr, *, tm, tn, out_dtype=F32, res=None, vmem_mib=48):
    m, k = x.shape
    n = w.shape[-1]
    tm, tn = _pick(m, tm), _pick(n, tn, 128)
    in_specs = [pl.BlockSpec((tm, k), lambda j, i: (i, 0)),
                pl.BlockSpec((None, k, tn), lambda j, i: (lyr, 0, j))]
    args = [x, w]
    if res is not None:
        in_specs.append(pl.BlockSpec((tm, tn), lambda j, i: (i, j)))
        args.append(res)
    return pl.pallas_call(
        functools.partial(_mm_body, has_res=res is not None),
        grid=(n // tn, m // tm),
        in_specs=in_specs,
        out_specs=pl.BlockSpec((tm, tn), lambda j, i: (i, j)),
        out_shape=jax.ShapeDtypeStruct((m, n), out_dtype),
        scratch_shapes=[pltpu.VMEM((k, tn), BF16)],
        compiler_params=_params(("arbitrary", "arbitrary"), vmem_mib),
        name="mm_res" if res is not None else "mm",
    )(*args)


def _mm_rows_body(*refs, has_norm):
    if has_norm:
        x_ref, w_ref, r_ref, g_ref, o_ref, n_ref, wb_ref = refs
    else:
        x_ref, w_ref, r_ref, o_ref, wb_ref = refs

    @pl.when(pl.program_id(0) == 0)
    def _():
        wb_ref[...] = w_ref[...].astype(BF16)

    acc = r_ref[...] + jnp.dot(x_ref[...], wb_ref[...], preferred_element_type=F32)
    o_ref[...] = acc
    if has_norm:
        n_ref[...] = _rms(acc, g_ref[...]).astype(n_ref.dtype)


def _mm_rows(x, w, lyr, res, *, tm, gain=None, vmem_mib=56):
    m, k = x.shape
    n = w.shape[-1]
    tm = _pick(m, tm)
    row = lambda width: pl.BlockSpec((tm, width), lambda i: (i, 0))
    in_specs = [row(k), pl.BlockSpec((None, k, n), lambda i: (lyr, 0, 0), pipeline_mode=pl.Buffered(1)), row(n)]
    args = [x, w, res]
    out_specs, out_shape = [row(n)], [jax.ShapeDtypeStruct((m, n), F32)]
    if gain is not None:
        in_specs.append(pl.BlockSpec((1, n), lambda i: (0, 0)))
        args.append(gain.reshape(1, n))
        out_specs.append(row(n))
        out_shape.append(jax.ShapeDtypeStruct((m, n), BF16))
    out = pl.pallas_call(
        functools.partial(_mm_rows_body, has_norm=gain is not None),
        grid=(m // tm,),
        in_specs=in_specs,
        out_specs=out_specs,
        out_shape=out_shape,
        scratch_shapes=[pltpu.VMEM((k, n), BF16)],
        compiler_params=_params(("arbitrary",), vmem_mib),
        name="mm_rows",
    )(*args)
    return out if gain is not None else out[0]


def _swiglu_math(x, wg, wu):
    a = jnp.dot(x, wg, preferred_element_type=F32)
    b = jnp.dot(x, wu, preferred_element_type=F32)
    return (a * jax.nn.sigmoid(a) * b).astype(BF16)


def _swiglu_body(x_ref, wg_ref, wu_ref, o_ref, wgb_ref, wub_ref):
    @pl.when(pl.program_id(1) == 0)
    def _():
        wgb_ref[...] = wg_ref[...].astype(BF16)
        wub_ref[...] = wu_ref[...].astype(BF16)

    o_ref[...] = _swiglu_math(x_ref[...], wgb_ref[...], wub_ref[...])


def _swiglu(x, wg, wu, lyr, *, tm, tn, vmem_mib=48):
    m, k = x.shape
    n = wg.shape[-1]
    tm, tn = _pick(m, tm), _pick(n, tn, 128)
    wspec = pl.BlockSpec((None, k, tn), lambda j, i: (lyr, 0, j))
    return pl.pallas_call(
        _swiglu_body,
        grid=(n // tn, m // tm),
        in_specs=[pl.BlockSpec((tm, k), lambda j, i: (i, 0)), wspec, wspec],
        out_specs=pl.BlockSpec((tm, tn), lambda j, i: (i, j)),
        out_shape=jax.ShapeDtypeStruct((m, n), BF16),
        scratch_shapes=[pltpu.VMEM((k, tn), BF16), pltpu.VMEM((k, tn), BF16)],
        compiler_params=_params(("arbitrary", "arbitrary"), vmem_mib),
        name="swiglu",
    )(x, wg, wu)


def _ple_body(x_ref, pp_ref, ps_ref, wg_ref, wp_ref, r_ref, o_ref, wgb_ref, wpb_ref, pb_ref, *, np_tiles):
    i = pl.program_id(1)

    @pl.when(i == 0)
    def _():
        wgb_ref[...] = wg_ref[...].astype(BF16)
        wpb_ref[...] = wp_ref[...].astype(BF16)

    @pl.when(i < np_tiles)
    def _():
        pb_ref[...] = pp_ref[...].astype(BF16)

    @pl.when(i >= np_tiles)
    def _():
        pb_ref[...] = ps_ref[...].astype(BF16)

    a = jnp.dot(x_ref[...], wgb_ref[...], preferred_element_type=F32)
    b = jnp.dot(pb_ref[...], wpb_ref[...], preferred_element_type=F32)
    o_ref[...] = r_ref[...] + jax.nn.sigmoid(a) * b


def _ple_rows_body(x_ref, pp_ref, ps_ref, wg_ref, wp_ref, r_ref, g_ref, o_ref, n_ref,
                   wgb_ref, wpb_ref, pb_ref, *, np_tiles):
    i = pl.program_id(0)

    @pl.when(i == 0)
    def _():
        wgb_ref[...] = wg_ref[...].astype(BF16)
        wpb_ref[...] = wp_ref[...].astype(BF16)

    @pl.when(i < np_tiles)
    def _():
        pb_ref[...] = pp_ref[...].astype(BF16)

    @pl.when(i >= np_tiles)
    def _():
        pb_ref[...] = ps_ref[...].astype(BF16)

    a = jnp.dot(x_ref[...], wgb_ref[...], preferred_element_type=F32)
    b = jnp.dot(pb_ref[...], wpb_ref[...], preferred_element_type=F32)
    out = r_ref[...] + jax.nn.sigmoid(a) * b
    o_ref[...] = out
    n_ref[...] = _rms(out, g_ref[...]).astype(n_ref.dtype)


def _ple_rows(x, p_p, p_s, wg, wp, res, lyr, gain, *, tm, vmem_mib=58):
    m, k = x.shape
    mp, ms, kp = p_p.shape[1], p_s.shape[1], p_p.shape[-1]
    n = wg.shape[-1]
    tm = _pick(math.gcd(mp, ms), tm)
    npt = mp // tm
    row = lambda width: pl.BlockSpec((tm, width), lambda i: (i, 0))
    once = lambda rows: pl.BlockSpec((None, rows, n), lambda i: (lyr, 0, 0), pipeline_mode=pl.Buffered(1))
    return pl.pallas_call(
        functools.partial(_ple_rows_body, np_tiles=npt),
        grid=(m // tm,),
        in_specs=[row(k),
                  pl.BlockSpec((None, tm, kp), lambda i: (lyr, jnp.minimum(i, npt - 1), 0)),
                  pl.BlockSpec((None, tm, kp), lambda i: (lyr, jnp.maximum(i - npt, 0), 0)),
                  once(k), once(kp), row(n), pl.BlockSpec((1, n), lambda i: (0, 0))],
        out_specs=[row(n), row(n)],
        out_shape=[jax.ShapeDtypeStruct((m, n), F32), jax.ShapeDtypeStruct((m, n), BF16)],
        scratch_shapes=[pltpu.VMEM((k, n), BF16), pltpu.VMEM((kp, n), BF16), pltpu.VMEM((tm, kp), BF16)],
        compiler_params=_params(("arbitrary",), vmem_mib),
        name="ple_rows",
    )(x, p_p, p_s, wg, wp, res, gain.reshape(1, n))


def _ple(x, p_p, p_s, wg, wp, res, lyr, *, tm, tn, vmem_mib=48):
    m, k = x.shape
    mp, ms, kp = p_p.shape[1], p_s.shape[1], p_p.shape[-1]
    n = wg.shape[-1]
    tm, tn = _pick(math.gcd(mp, ms), tm), _pick(n, tn, 128)
    npt = mp // tm
    return pl.pallas_call(
        functools.partial(_ple_body, np_tiles=npt),
        grid=(n // tn, m // tm),
        in_specs=[pl.BlockSpec((tm, k), lambda j, i: (i, 0)),
                  pl.BlockSpec((None, tm, kp), lambda j, i: (lyr, jnp.minimum(i, npt - 1), 0)),
                  pl.BlockSpec((None, tm, kp), lambda j, i: (lyr, jnp.maximum(i - npt, 0), 0)),
                  pl.BlockSpec((None, k, tn), lambda j, i: (lyr, 0, j)),
                  pl.BlockSpec((None, kp, tn), lambda j, i: (lyr, 0, j)),
                  pl.BlockSpec((tm, tn), lambda j, i: (i, j))],
        out_specs=pl.BlockSpec((tm, tn), lambda j, i: (i, j)),
        out_shape=jax.ShapeDtypeStruct((m, n), F32),
        scratch_shapes=[pltpu.VMEM((k, tn), BF16), pltpu.VMEM((kp, tn), BF16), pltpu.VMEM((tm, kp), BF16)],
        compiler_params=_params(("arbitrary", "arbitrary"), vmem_mib),
        name="ple",
    )(x, p_p, p_s, wg, wp, res)


def _moe_weights(w_hbms, stages, bf16s, sem, e_ref, start_ref, nxt_ref, lyr):
    j, i = pl.program_id(0), pl.program_id(1)
    tn = stages[0].shape[1]

    def copies(e, col):
        col = pl.multiple_of(col, LANES)
        return [pltpu.make_async_copy(w.at[lyr, e, :, pl.ds(col, tn)], st, sem.at[n])
                for n, (w, st) in enumerate(zip(w_hbms, stages))]

    @pl.when(jnp.logical_and(j == 0, i == 0))
    def _():
        for cp in copies(e_ref[0], 0):
            cp.start()

    @pl.when(start_ref[i] == 1)
    def _():
        for cp in copies(e_ref[i], j * tn):
            cp.wait()
        rows = 128

        def cast_chunk(c, carry):
            r = pl.multiple_of(c * rows, rows)
            for st, wb in zip(stages, bf16s):
                wb[pl.ds(r, rows), :] = st[pl.ds(r, rows), :].astype(BF16)
            return carry
        lax.fori_loop(0, stages[0].shape[0] // rows, cast_chunk, 0)
        nxt = nxt_ref[i]

        @pl.when(nxt >= 0)
        def _():
            for cp in copies(nxt, j * tn):
                cp.start()

        @pl.when(jnp.logical_and(nxt < 0, j + 1 < pl.num_programs(0)))
        def _():
            for cp in copies(e_ref[0], (j + 1) * tn):
                cp.start()


def _active_prefix(act_ref, i, per):
    n_act = act_ref[i * per]
    for s in range(1, per):
        n_act = n_act + act_ref[i * per + s]
    return n_act


def _prefix_rows(o_ref, n_act, fn):
    tm = o_ref.shape[0]
    for c in range(tm // MOE_SUB + 1):
        rows = c * MOE_SUB

        @pl.when(n_act == c)
        def _():
            if rows:
                o_ref[:rows, :] = fn(rows)
            if rows < tm:
                o_ref[rows:, :] = jnp.zeros((tm - rows, o_ref.shape[1]), o_ref.dtype)


def _moe_up_body(e_ref, act_ref, start_ref, nxt_ref, x_ref, wg_hbm, wu_hbm, o_ref,
                 wgb_ref, wub_ref, sg_ref, su_ref, sem, *, lyr):
    _moe_weights((wg_hbm, wu_hbm), (sg_ref, su_ref), (wgb_ref, wub_ref), sem, e_ref, start_ref, nxt_ref, lyr)
    n_act = _active_prefix(act_ref, pl.program_id(1), x_ref.shape[0] // MOE_SUB)
    _prefix_rows(o_ref, n_act, lambda rows: _swiglu_math(x_ref[:rows, :], wgb_ref[...], wub_ref[...]))


def _moe_down_body(e_ref, act_ref, start_ref, nxt_ref, h_ref, wd_hbm, o_ref, wdb_ref, sd_ref, sem, *, lyr):
    _moe_weights((wd_hbm,), (sd_ref,), (wdb_ref,), sem, e_ref, start_ref, nxt_ref, lyr)
    n_act = _active_prefix(act_ref, pl.program_id(1), h_ref.shape[0] // MOE_SUB)
    _prefix_rows(o_ref, n_act,
                 lambda rows: jnp.dot(h_ref[:rows, :], wdb_ref[...], preferred_element_type=F32))


def _moe_matmul(body, x, ws, sched, lyr, *, tn, out_dtype, vmem_mib, name):
    ns, k = x.shape
    n = ws[0].shape[-1]
    tm, tn = MOE_TM, _pick(n, tn, 128)
    grid_spec = pltpu.PrefetchScalarGridSpec(
        num_scalar_prefetch=len(sched),
        grid=(n // tn, ns // tm),
        in_specs=[pl.BlockSpec((tm, k), lambda j, i, *_: (i, 0))]
                 + [pl.BlockSpec(memory_space=pl.ANY)] * len(ws),
        out_specs=pl.BlockSpec((tm, tn), lambda j, i, *_: (i, j)),
        scratch_shapes=[pltpu.VMEM((k, tn), BF16)] * len(ws) + [pltpu.VMEM((k, tn), F32)] * len(ws)
                       + [pltpu.SemaphoreType.DMA((len(ws),))],
    )
    return pl.pallas_call(
        functools.partial(body, lyr=lyr),
        grid_spec=grid_spec,
        out_shape=jax.ShapeDtypeStruct((ns, n), out_dtype),
        compiler_params=_params(("arbitrary", "arbitrary"), vmem_mib),
        name=name,
    )(*sched, x, *ws)


def _row_gather_copy(x_hbm, buf, sem, slot, src_row, dst_row, rows):
    return pltpu.make_async_copy(x_hbm.at[pl.ds(src_row, rows)], buf.at[slot, pl.ds(dst_row, rows)],
                                 sem.at[slot])


GATHER_GROUP = 8


def _moe_gather_body(tok_ref, real_ref, x_hbm, o_ref, buf, sem):
    i = pl.program_id(0)
    tm = o_ref.shape[0]
    pieces = o_ref.shape[1] // LANES
    grp_rows = GATHER_GROUP * pieces
    pitch = buf.shape[1] // tm

    def groups(blk):
        return (real_ref[blk] + GATHER_GROUP - 1) // GATHER_GROUP

    def issue(blk, slot):
        def body(g, carry):
            for u in range(GATHER_GROUP):
                r = g * GATHER_GROUP + u
                _row_gather_copy(x_hbm, buf, sem, slot, tok_ref[blk * tm + r] * pieces, r * pitch,
                                 pieces).start(priority=u % 2)
            return carry
        lax.fori_loop(0, groups(blk), body, 0)

    @pl.when(i == 0)
    def _():
        issue(0, 0)

    @pl.when(i + 1 < pl.num_programs(0))
    def _():
        issue(i + 1, (i + 1) % 2)

    slot = i % 2
    cur = buf.at[slot]
    n_grp = groups(i)

    @pl.when(n_grp > 0)
    def _():
        _row_gather_copy(x_hbm, buf, sem, slot, 0, 0, n_grp * grp_rows).wait()

    def zero(g, carry):
        rows = GATHER_GROUP * pitch
        cur[pl.ds(pl.multiple_of(g * rows, rows), rows), :] = jnp.zeros((rows, LANES), F32)
        return carry
    lax.fori_loop(n_grp, tm // GATHER_GROUP, zero, 0)

    o_ref[...] = jnp.concatenate([cur[pl.ds(s, tm, stride=pitch), :] for s in range(pieces)],
                                 axis=-1).astype(o_ref.dtype)


def _moe_gather(x_rows, slot_tok, blk_real, d):
    n_slots = slot_tok.shape[0]
    tm = MOE_TM
    grid_spec = pltpu.PrefetchScalarGridSpec(
        num_scalar_prefetch=2,
        grid=(n_slots // tm,),
        in_specs=[pl.BlockSpec(memory_space=pl.ANY)],
        out_specs=pl.BlockSpec((tm, d), lambda i, *_: (i, 0)),
        scratch_shapes=[pltpu.VMEM((2, tm * (d // LANES + 8), LANES), F32), pltpu.SemaphoreType.DMA((2,))],
    )
    return pl.pallas_call(
        _moe_gather_body,
        grid_spec=grid_spec,
        out_shape=jax.ShapeDtypeStruct((n_slots, d), BF16),
        compiler_params=_params(("arbitrary",), 32),
        name="moe_gather",
    )(slot_tok, blk_real, x_rows)


def _moe_ffn(hn_rows, e_idx, gates, wg, wu, wd, lyr):
    m = e_idx.shape[0]
    d = wg.shape[2]
    n_exp = wg.shape[1]
    top_k = e_idx.shape[-1]
    tm = MOE_TM
    n_asg = m * top_k
    e_flat = e_idx.reshape(n_asg)
    onehot = (e_flat[None, :] == jnp.arange(n_exp, dtype=jnp.int32)[:, None]).astype(jnp.int32)
    counts = jnp.sum(onehot, axis=1)
    start = jnp.cumsum(counts) - counts
    padded = (counts + tm - 1) // tm * tm
    pad_end = jnp.cumsum(padded)
    pad_start = pad_end - padded
    order = jnp.argsort(e_flat).astype(jnp.int32)
    rank = jnp.argsort(order).astype(jnp.int32)
    slot_of = rank + jnp.sum(onehot * (pad_start - start)[:, None], axis=0).astype(jnp.int32)
    n_blocks = n_asg // tm + n_exp
    n_slots = n_blocks * tm
    n_used = (pad_end[-1] // tm).astype(jnp.int32)
    blk = jnp.arange(n_blocks, dtype=jnp.int32)
    blk_start = jnp.minimum(blk, n_used - 1) * tm
    blk_expert = jnp.sum((pad_end[None, :] <= blk_start[:, None]).astype(jnp.int32), axis=1)
    blk_expert = jnp.minimum(blk_expert, n_exp - 1).astype(jnp.int32)
    sub_start = jnp.arange(n_slots // MOE_SUB, dtype=jnp.int32) * MOE_SUB
    sub_expert = jnp.repeat(blk_expert, tm // MOE_SUB)
    real_end = (pad_start + counts)[sub_expert]
    sub_active = jnp.logical_and(sub_start < real_end, sub_start < pad_end[-1]).astype(jnp.int32)
    slot_r = jnp.arange(n_slots, dtype=jnp.int32) - jnp.repeat(pad_start[sub_expert], MOE_SUB)
    slot_src = jnp.repeat(start[sub_expert], MOE_SUB) + slot_r
    slot_real = jnp.logical_and(slot_r >= 0, slot_r < jnp.repeat(counts[sub_expert], MOE_SUB))
    slot_tok = jnp.where(slot_real, order[jnp.clip(slot_src, 0, n_asg - 1)] // top_k, 0).astype(jnp.int32)

    run_start = jnp.concatenate([jnp.ones((1,), jnp.int32),
                                 (blk_expert[1:] != blk_expert[:-1]).astype(jnp.int32)])
    later_start = jnp.where(run_start == 1, blk, n_blocks)
    nxt_blk = lax.cummin(jnp.concatenate([later_start[1:], jnp.full((1,), n_blocks, jnp.int32)]),
                         axis=0, reverse=True)
    run_next = jnp.where(nxt_blk < n_blocks, blk_expert[jnp.minimum(nxt_blk, n_blocks - 1)], -1)
    sched = (blk_expert, sub_active, run_start, run_next.astype(jnp.int32))

    blk_real = jnp.clip((pad_start + counts)[blk_expert] - blk * tm, 0, tm)
    blk_real = jnp.where(blk < n_used, blk_real, 0).astype(jnp.int32)
    x_sorted = _moe_gather(hn_rows, slot_tok, blk_real, d)
    h_sorted = _moe_matmul(_moe_up_body, x_sorted, (wg, wu), sched, lyr, tn=1024, out_dtype=BF16,
                           vmem_mib=48, name="moe_up")
    y_sorted = _moe_matmul(_moe_down_body, h_sorted, (wd,), sched, lyr, tn=512, out_dtype=F32,
                           vmem_mib=48, name="moe_down")
    sl = slot_of.reshape(m, top_k)
    y = y_sorted[sl[:, 0]] * gates[:, 0:1]
    for t in range(1, top_k):
        y = y + y_sorted[sl[:, t]] * gates[:, t:t + 1]
    return y


def _rel_buckets(n_q, n_k_past, n_k):
    rel = (np.arange(n_k) - n_k_past)[None, :] - np.arange(n_q)[:, None]
    half = N_BUCKETS // 2
    max_exact = half // 2
    n = np.abs(rel)
    nf = np.maximum(n, 1).astype(np.float32)
    large = max_exact + (np.log(nf / np.float32(max_exact)) / np.float32(math.log(REL_MAX_DIST / max_exact))
                         * np.float32(half - max_exact)).astype(np.int32)
    large = np.minimum(large, half - 1)
    return (np.where(rel > 0, half, 0) + np.where(n < max_exact, n, large)).astype(np.int32)


def _build_bias(bias_ref, sink_ref, bucket_ref, table_ref, sinks_ref):
    bucket = bucket_ref[...]
    nq = bucket.shape[1]
    gq = bias_ref.shape[2] // nq
    for kh in range(bias_ref.shape[0]):
        for g in range(gq):
            n = kh * gq + g
            acc = jnp.zeros(bucket.shape, F32)
            for b in range(N_BUCKETS):
                acc = jnp.where(bucket == b, table_ref[b, n], acc)
            bias_ref[kh, :, g * nq:(g + 1) * nq] = acc
            sink_ref[kh, :, g * nq:(g + 1) * nq] = jnp.full((1, nq), sinks_ref[n], F32)


def _attn_items(items, bias_ref, sink_ref):
    nq = items[0][0].shape[0]
    gq = items[0][0].shape[-1] // HEAD_DIM // KV_HEADS
    head = lambda x, n: x[:, n * HEAD_DIM:(n + 1) * HEAD_DIM]
    scores = []
    for q, k, _, valid in items:
        for kh in range(KV_HEADS):
            q4 = jnp.concatenate([head(q, kh * gq + g) for g in range(gq)], axis=0)
            s = lax.dot_general(head(k, kh), q4, (((1,), (1,)), ((), ())), preferred_element_type=F32)
            scores.append((s, kh, valid))
    probs = []
    for s, kh, valid in scores:
        s = s + bias_ref[kh]
        if valid is not None:
            s = jnp.where(valid, s, NEG_INF)
        sink = sink_ref[kh]
        mx = jnp.maximum(jnp.max(s, axis=0, keepdims=True), sink)
        e = jnp.exp(s - mx)
        den = jnp.sum(e, axis=0, keepdims=True) + jnp.exp(sink - mx)
        probs.append((e * (1.0 / den)).astype(BF16))
    outs = []
    for i, (_, _, v, _) in enumerate(items):
        heads = []
        for kh in range(KV_HEADS):
            o4 = lax.dot_general(probs[i * KV_HEADS + kh], head(v, kh), (((0,), (0,)), ((), ())),
                                 preferred_element_type=F32)
            heads.extend(o4[g * nq:(g + 1) * nq] for g in range(gq))
        outs.append(jnp.concatenate(heads, axis=-1))
    return outs


def _attn_prompt_body(q_ref, kp_ref, kc_ref, vp_ref, vc_ref, bucket_ref, table_ref, sinks_ref,
                      o_ref, bias_ref, sink_ref, *, chunks):
    j = pl.program_id(1)

    @pl.when(jnp.logical_and(pl.program_id(0) == 0, j == 0))
    def _():
        _build_bias(bias_ref, sink_ref, bucket_ref, table_ref, sinks_ref)

    q = (q_ref[...] * ATTN_SCALE).astype(BF16)
    k = jnp.concatenate([kp_ref[...], kc_ref[...]], axis=0).astype(BF16)
    v = jnp.concatenate([vp_ref[...], vc_ref[...]], axis=0).astype(BF16)
    band = WINDOW + CHUNK
    items = []
    for c in range(chunks):
        lo = c * CHUNK
        key_valid = None
        if lo < WINDOW:
            key_pos = j * (chunks * CHUNK) + lo - WINDOW + lax.broadcasted_iota(jnp.int32, (band, 1), 0)
            key_valid = key_pos >= 0
        items.append((q[lo:lo + CHUNK], k[lo:lo + band], v[lo:lo + band], key_valid))
    for c, o in enumerate(_attn_items(items, bias_ref, sink_ref)):
        o_ref[c * CHUNK:(c + 1) * CHUNK, :] = o


def _attn_prompt(z, n_b, seq, d_ssm, d_attn, table, sinks):
    d_kv = KV_HEADS * HEAD_DIM
    assert d_ssm % d_attn == 0 and d_ssm % d_kv == 0 and d_attn % d_kv == 0
    qb = _pick(seq, 4 * CHUNK, 2 * CHUNK)
    assert qb % WINDOW == 0 and seq % qb == 0
    chunks = qb // CHUNK
    nj = seq // qb
    r = qb // WINDOW
    kcol = (d_ssm + d_attn) // d_kv
    gq = d_attn // HEAD_DIM // KV_HEADS
    band = WINDOW + CHUNK
    bucket = jnp.asarray(_rel_buckets(CHUNK, WINDOW, band).T)

    def prev(b, j):
        return b * (seq // WINDOW) + jnp.maximum(j * r - 1, 0)

    return pl.pallas_call(
        functools.partial(_attn_prompt_body, chunks=chunks),
        grid=(n_b, nj),
        in_specs=[pl.BlockSpec((qb, d_attn), lambda b, j: (b * nj + j, d_ssm // d_attn)),
                  pl.BlockSpec((WINDOW, d_kv), lambda b, j: (prev(b, j), kcol)),
                  pl.BlockSpec((qb, d_kv), lambda b, j: (b * nj + j, kcol)),
                  pl.BlockSpec((WINDOW, d_kv), lambda b, j: (prev(b, j), kcol + 1)),
                  pl.BlockSpec((qb, d_kv), lambda b, j: (b * nj + j, kcol + 1)),
                  pl.BlockSpec((band, CHUNK), lambda b, j: (0, 0)),
                  pl.BlockSpec(memory_space=pltpu.SMEM),
                  pl.BlockSpec(memory_space=pltpu.SMEM)],
        out_specs=pl.BlockSpec((qb, d_attn), lambda b, j: (b * nj + j, 0)),
        out_shape=jax.ShapeDtypeStruct((n_b * seq, d_attn), F32),
        scratch_shapes=[pltpu.VMEM((KV_HEADS, band, gq * CHUNK), F32),
                        pltpu.VMEM((KV_HEADS, 1, gq * CHUNK), F32)],
        compiler_params=_params(("arbitrary", "arbitrary"), 40),
        name="attn_prompt",
    )(z, z, z, z, z, bucket, table, sinks)


def _attn_sample_body(q_ref, kc_ref, kn_ref, vc_ref, vn_ref, bucket_ref, table_ref, sinks_ref,
                      o_ref, bias_ref, sink_ref, *, n_bs, n_new, n_past):
    @pl.when(pl.program_id(0) == 0)
    def _():
        _build_bias(bias_ref, sink_ref, bucket_ref, table_ref, sinks_ref)

    q = (q_ref[...] * ATTN_SCALE).astype(BF16)
    kc, kn = kc_ref[...].astype(BF16), kn_ref[...].astype(BF16)
    vc, vn = vc_ref[...].astype(BF16), vn_ref[...].astype(BF16)
    items = []
    for b in range(n_bs):
        new = slice(b * n_new, (b + 1) * n_new)
        past = slice(b * n_past, (b + 1) * n_past)
        items.append((q[new], jnp.concatenate([kc[past], kn[new]], axis=0),
                      jnp.concatenate([vc[past], vn[new]], axis=0), None))
    for b, o in enumerate(_attn_items(items, bias_ref, sink_ref)):
        o_ref[b * n_new:(b + 1) * n_new, :] = o


def _attn_sample(z, row0, n_b, n_new, d_ssm, d_attn, cache_k, cache_v, lyr, table, sinks):
    d_kv = KV_HEADS * HEAD_DIM
    n_past = cache_k.shape[2]
    n_bs = _pick(n_b, 4, 1)
    rows = n_bs * n_new
    assert row0 % rows == 0
    kcol = (d_ssm + d_attn) // d_kv
    gq = d_attn // HEAD_DIM // KV_HEADS
    n_k = n_past + n_new
    bucket = jnp.asarray(_rel_buckets(n_new, n_past, n_k).T)
    ck = cache_k.reshape(-1, d_kv)
    cv = cache_v.reshape(-1, d_kv)
    rb = row0 // rows
    nsteps = n_b // n_bs
    cache_spec = pl.BlockSpec((n_bs * n_past, d_kv), lambda b: (lyr * nsteps + b, 0))
    return pl.pallas_call(
        functools.partial(_attn_sample_body, n_bs=n_bs, n_new=n_new, n_past=n_past),
        grid=(nsteps,),
        in_specs=[pl.BlockSpec((rows, d_attn), lambda b: (rb + b, d_ssm // d_attn)),
                  cache_spec,
                  pl.BlockSpec((rows, d_kv), lambda b: (rb + b, kcol)),
                  cache_spec,
                  pl.BlockSpec((rows, d_kv), lambda b: (rb + b, kcol + 1)),
                  pl.BlockSpec((n_k, n_new), lambda b: (0, 0)),
                  pl.BlockSpec(memory_space=pltpu.SMEM),
                  pl.BlockSpec(memory_space=pltpu.SMEM)],
        out_specs=pl.BlockSpec((rows, d_attn), lambda b: (b, 0)),
        out_shape=jax.ShapeDtypeStruct((n_b * n_new, d_attn), F32),
        scratch_shapes=[pltpu.VMEM((KV_HEADS, n_k, gq * n_new), F32),
                        pltpu.VMEM((KV_HEADS, 1, gq * n_new), F32)],
        compiler_params=_params(("arbitrary",), 40),
        name="attn_sample",
    )(z, ck, z, cv, z, bucket, table, sinks)


def _tile_lanes(x, reps):
    return jnp.concatenate([x] * reps, axis=-1)


def _s5_tables_body(are_ref, aim_ref, ldt_ref, btr_ref, bti_ref, cre_ref, cim_ref, d_ref,
                    dk_ref, we_ref, ws_ref, a_ref):
    rows, p = cre_ref.shape
    gb = rows // SSM_GROUP_CH
    a_re, a_im = are_ref[...], aim_ref[...]
    dt = jnp.exp(ldt_ref[...])
    mag = jnp.exp(a_re * dt)
    ab_re, ab_im = mag * jnp.cos(a_im * dt), mag * jnp.sin(a_im * dt)

    x, y = ab_re - 1.0, ab_im
    den = a_re * a_re + a_im * a_im
    co_re = (x * a_re + y * a_im) / den
    co_im = (y * a_re - x * a_im) / den
    bb_re = co_re * btr_ref[...] - co_im * bti_ref[...]
    bb_im = co_re * bti_ref[...] + co_im * btr_ref[...]
    c_re, c_im = cre_ref[...], cim_ref[...]

    r_idx = lax.broadcasted_iota(jnp.int32, (rows, 1), 0)
    own_state = (r_idx // SSM_GROUP_CH) == (lax.broadcasted_iota(jnp.int32, (1, gb * p), 1) // p)
    same_group = (r_idx // SSM_GROUP_CH) == (lax.broadcasted_iota(jnp.int32, (1, rows), 1) // SSM_GROUP_CH)
    diag = r_idx == lax.broadcasted_iota(jnp.int32, (1, rows), 1)

    def place(v_re, v_im):
        return jnp.concatenate([jnp.where(own_state, _tile_lanes(v_re, gb), 0.0),
                                jnp.where(own_state, _tile_lanes(v_im, gb), 0.0)], axis=-1)

    dims = (((1,), (1,)), ((), ()))
    bb_cat = jnp.concatenate([bb_re, -bb_im], axis=-1)
    pw_re, pw_im = jnp.ones_like(ab_re), jnp.zeros_like(ab_re)
    for j in range(S5_T + 1):
        ca_re = c_re * pw_re - c_im * pw_im
        ca_im = c_re * pw_im + c_im * pw_re
        if j < S5_T:
            kj = lax.dot_general(bb_cat, jnp.concatenate([ca_re, ca_im], axis=-1), dims,
                                 precision=HIGHEST, preferred_element_type=F32)
            kj = jnp.where(same_group, kj, 0.0)
            if j == 0:
                kj = kj + jnp.where(diag, d_ref[...], 0.0)
            dk_ref[j] = kj.astype(dk_ref.dtype)
            s = S5_T - 1 - j
            we_ref[s * rows:(s + 1) * rows, :] = place(
                pw_re * bb_re - pw_im * bb_im, pw_re * bb_im + pw_im * bb_re).astype(we_ref.dtype)
        if j >= 1:
            ws_ref[(j - 1) * rows:j * rows, :] = place(ca_re, -ca_im).astype(ws_ref.dtype)
        if j == S5_T:
            first = r_idx == (lax.broadcasted_iota(jnp.int32, (1, gb * p), 1) // p) * SSM_GROUP_CH
            a_ref[...] = jnp.concatenate(
                [jnp.sum(jnp.where(first, _tile_lanes(pw_re, gb), 0.0), axis=0, keepdims=True),
                 jnp.sum(jnp.where(first, _tile_lanes(pw_im, gb), 0.0), axis=0, keepdims=True)], axis=-1)
        pw_re, pw_im = pw_re * ab_re - pw_im * ab_im, pw_re * ab_im + pw_im * ab_re


def _s5_tables(a_re, a_im, log_dt, b_re, b_im, c_re, c_im, d_skip):
    h, p = a_re.shape
    g = SSM_GROUP_CH
    hb = h // S5_GB
    sw = 2 * S5_GB * p
    rep = lambda v: jnp.repeat(v, g, axis=0).reshape(hb, LANES, -1)
    blk = lambda w: pl.BlockSpec((None, LANES, w), lambda i: (i, 0, 0))
    wide = pl.BlockSpec((None, S5_F, sw), lambda i: (i, 0, 0))
    return pl.pallas_call(
        _s5_tables_body,
        grid=(hb,),
        in_specs=[blk(p), blk(p), blk(1), blk(p), blk(p), blk(p), blk(p), blk(1)],
        out_specs=[pl.BlockSpec((None, S5_T, LANES, LANES), lambda i: (i, 0, 0, 0)), wide, wide,
                   pl.BlockSpec((None, 1, sw), lambda i: (i, 0, 0))],
        out_shape=[jax.ShapeDtypeStruct((hb, S5_T, LANES, LANES), BF16),
                   jax.ShapeDtypeStruct((hb, S5_F, sw), BF16),
                   jax.ShapeDtypeStruct((hb, S5_F, sw), BF16),
                   jax.ShapeDtypeStruct((hb, 1, sw), F32)],
        compiler_params=_params(("arbitrary",), 40),
        name="s5_tables",
    )(rep(a_re), rep(a_im), rep(log_dt.reshape(h, 1)),
      jnp.swapaxes(b_re, 1, 2).reshape(hb, LANES, p), jnp.swapaxes(b_im, 1, 2).reshape(hb, LANES, p),
      c_re.reshape(hb, LANES, p), c_im.reshape(hb, LANES, p), d_skip.reshape(hb, LANES, 1))


def _split_mul(a, p):
    lane = lax.broadcasted_iota(jnp.int32, a.shape, a.ndim - 1)
    sw = pltpu.roll(a, p, axis=a.ndim - 1)
    return jnp.where(lane < p, a, sw), jnp.where(lane < p, -sw, a)


def _cmul(x, mr, mi, p):
    return x * mr + pltpu.roll(x, p, axis=x.ndim - 1) * mi


def _build_toeplitz(m_ref, dk_ref):
    m_ref[...] = jnp.zeros_like(m_ref)
    for s in range(S5_T):
        for t in range(s, S5_T):
            m_ref[s * LANES:(s + 1) * LANES, t * LANES:(t + 1) * LANES] = dk_ref[t - s]


def _s5_outputs(ucat, enter, m_ref, ws_ref):
    return (jnp.dot(ucat, m_ref[...], preferred_element_type=F32)
            + lax.dot_general(enter.astype(BF16), ws_ref[...], (((1,), (1,)), ((), ())),
                              preferred_element_type=F32))


def _glu_rows(o_ref, ga_ref, gb_ref, tile):
    ga, gb = ga_ref[...], gb_ref[...]
    tile = _pick(o_ref.shape[0], tile)
    for r0 in range(0, o_ref.shape[0], tile):
        y = o_ref[r0:r0 + tile, :].astype(BF16)
        o_ref[r0:r0 + tile, :] = (jnp.dot(y, ga, preferred_element_type=F32)
                                  * jax.nn.sigmoid(jnp.dot(y, gb, preferred_element_type=F32)))


def _s5_prompt_body(z_ref, dk_ref, we_ref, ws_ref, a_ref, ga_ref, gb_ref, o_ref, fin_ref, m_ref,
                    *, nb, nc):
    @pl.when(pl.program_id(1) == 0)
    def _():
        _build_toeplitz(m_ref, dk_ref)

    r = nb * nc
    p = a_ref.shape[-1] // 2
    ucat = jnp.concatenate([z_ref[pl.ds(t, r, stride=S5_T), :].astype(BF16) for t in range(S5_T)], axis=-1)
    x = jnp.dot(ucat, we_ref[...], preferred_element_type=F32)
    a = a_ref[...]
    pos = lax.rem(lax.broadcasted_iota(jnp.int32, (r, 1), 0), nc)
    sh = 1
    while sh < nc:
        mr, mi = _split_mul(a, p)
        shifted = jnp.where(pos >= sh, pltpu.roll(x, sh, axis=0), 0.0)
        x = x + _cmul(shifted, mr, mi, p)
        a = _cmul(a, mr, mi, p)
        sh *= 2
    enter = jnp.where(pos >= 1, pltpu.roll(x, 1, axis=0), 0.0)
    fin_ref[...] = jnp.concatenate([x[(b + 1) * nc - 1:(b + 1) * nc] for b in range(nb)], axis=0)
    y = _s5_outputs(ucat, enter, m_ref, ws_ref)
    for t in range(S5_T):
        o_ref[pl.ds(t, r, stride=S5_T), :] = y[:, t * LANES:(t + 1) * LANES]
    _glu_rows(o_ref, ga_ref, gb_ref, 512)


def _s5_sample_body(z_ref, dk_ref, we_ref, ws_ref, a_ref, ga_ref, gb_ref, h0_ref, o_ref, fin_ref, m_ref,
                    *, nb, nc):
    _build_toeplitz(m_ref, dk_ref)
    p = a_ref.shape[-1] // 2
    seq = nc * S5_T
    ucat = jnp.concatenate(
        [jnp.concatenate([z_ref[pl.ds(c * S5_T + t, nb, stride=seq), :].astype(BF16) for t in range(S5_T)],
                         axis=-1) for c in range(nc)], axis=0)
    e = jnp.dot(ucat, we_ref[...], preferred_element_type=F32)
    mr, mi = _split_mul(a_ref[...], p)
    state = h0_ref[...]
    enter = []
    for c in range(nc):
        enter.append(state)
        state = _cmul(state, mr, mi, p) + e[c * nb:(c + 1) * nb]
    fin_ref[...] = state
    y = _s5_outputs(ucat, jnp.concatenate(enter, axis=0), m_ref, ws_ref)
    for c in range(nc):
        for t in range(S5_T):
            o_ref[pl.ds(c * S5_T + t, nb, stride=seq), :] = y[c * nb:(c + 1) * nb, t * LANES:(t + 1) * LANES]
    _glu_rows(o_ref, ga_ref, gb_ref, 512)


def _s5_prompt(z, n_b, seq, tabs, g_a, g_b, lyr, hb):
    dk, w_e, w_s, a_t = tabs
    sw = a_t.shape[-1]
    nbs = _pick(n_b, 4, 1)
    nc = seq // S5_T
    tab = lambda arr: pl.BlockSpec((None,) + arr.shape[1:],
                                   lambda i, b: (lyr * hb + i,) + (0,) * (arr.ndim - 1))
    return pl.pallas_call(
        functools.partial(_s5_prompt_body, nb=nbs, nc=nc),
        grid=(hb, n_b // nbs),
        in_specs=[pl.BlockSpec((nbs * seq, LANES), lambda i, b: (b, i)),
                  tab(dk), tab(w_e), tab(w_s), tab(a_t), tab(g_a), tab(g_b)],
        out_specs=[pl.BlockSpec((nbs * seq, LANES), lambda i, b: (b, i)),
                   pl.BlockSpec((None, None, nbs, sw), lambda i, b: (i, b, 0, 0))],
        out_shape=[jax.ShapeDtypeStruct((n_b * seq, hb * LANES), F32),
                   jax.ShapeDtypeStruct((hb, n_b // nbs, nbs, sw), F32)],
        scratch_shapes=[pltpu.VMEM((S5_F, S5_F), BF16)],
        compiler_params=_params(("arbitrary", "arbitrary"), 58),
        name="s5_prompt",
    )(z, dk, w_e, w_s, a_t, g_a, g_b)


def _s5_sample(z, row0, n_b, n_new, tabs, g_a, g_b, h0, lyr, hb):
    dk, w_e, w_s, a_t = tabs
    sw = a_t.shape[-1]
    rows = n_b * n_new
    assert row0 % rows == 0
    rb = row0 // rows
    tab = lambda arr, off=lyr * hb: pl.BlockSpec((None,) + arr.shape[1:],
                                                 lambda i: (off + i,) + (0,) * (arr.ndim - 1))
    return pl.pallas_call(
        functools.partial(_s5_sample_body, nb=n_b, nc=n_new // S5_T),
        grid=(hb,),
        in_specs=[pl.BlockSpec((rows, LANES), lambda i: (rb, i)),
                  tab(dk), tab(w_e), tab(w_s), tab(a_t), tab(g_a), tab(g_b), tab(h0, 0)],
        out_specs=[pl.BlockSpec((rows, LANES), lambda i: (0, i)),
                   pl.BlockSpec((None, n_b, sw), lambda i: (i, 0, 0))],
        out_shape=[jax.ShapeDtypeStruct((rows, hb * LANES), F32),
                   jax.ShapeDtypeStruct((hb, n_b, sw), F32)],
        scratch_shapes=[pltpu.VMEM((S5_F, S5_F), BF16)],
        compiler_params=_params(("arbitrary",), 52),
        name="s5_sample",
    )(z, dk, w_e, w_s, a_t, g_a, g_b, h0)


def _glu_blocks(glu_w):
    h, g, _ = glu_w.shape
    hb = h // S5_GB
    eye = jnp.eye(S5_GB, dtype=F32)

    def diag(wm):
        wm = wm.reshape(hb, S5_GB, g, g)
        return (eye[None, :, None, :, None] * wm[:, :, :, None, :]).reshape(hb, LANES, LANES).astype(BF16)

    return diag(glu_w[..., :g]), diag(glu_w[..., g:])


def _state_to_lanes(re, im):
    n, h, p = re.shape
    hb = h // S5_GB
    f = lambda v: v.reshape(n, hb, S5_GB * p).transpose(1, 0, 2)
    return jnp.concatenate([f(re), f(im)], axis=-1)


def _state_from_lanes(s):
    nl, hb, n, sw = s.shape
    half = sw // 2
    f = lambda v: v.transpose(0, 2, 1, 3).reshape(nl, n, hb * S5_GB, half // S5_GB)
    return f(s[..., :half]), f(s[..., half:])


def kernel(x_prompt, x_sample, cache_k, cache_v, state_ssm_re, state_ssm_im, p_prompt, p_sample,
           norm_mix, w_in, ssm_a_re, ssm_a_im, ssm_log_dt, ssm_b_re, ssm_b_im, ssm_c_re, ssm_c_im,
           ssm_d, ssm_glu_w, attn_sinks, rel_bias, norm_grp_ssm, norm_grp_attn, w_out, norm_ffn,
           ffn_w_gate, ffn_w_up, ffn_w_down, router_w, moe_w_gate, moe_w_up, moe_w_down,
           norm_ple, ple_w_gate, ple_w_proj, norm_final):
    nb_p, seq, d = x_prompt.shape
    nb_s, n_new, _ = x_sample.shape
    depth = w_in.shape[0]
    d_ssm = norm_grp_ssm.shape[-1]
    d_attn = norm_grp_attn.shape[-1]
    mp, ms = nb_p * seq, nb_s * n_new
    win = min(WINDOW, seq)
    assert d_ssm % LANES == 0 and seq % S5_T == 0 and n_new % S5_T == 0

    h = jnp.concatenate([x_prompt.reshape(mp, d), x_sample.reshape(ms, d)], axis=0)
    p_p, p_s = p_prompt.reshape(depth, mp, -1), p_sample.reshape(depth, ms, -1)

    n_grp = ssm_a_re.shape[1]
    hb = n_grp // S5_GB
    flat = lambda v: v.reshape((depth * n_grp,) + v.shape[2:])
    tabs = _s5_tables(flat(ssm_a_re), flat(ssm_a_im), ssm_log_dt.reshape(-1), flat(ssm_b_re), flat(ssm_b_im),
                      flat(ssm_c_re), flat(ssm_c_im), ssm_d.reshape(-1))
    g_a, g_b = _glu_blocks(flat(ssm_glu_w))

    kv_p, kv_s, fin_ps, fin_ss = [], [], [], []
    xn = _rmsnorm(h, norm_mix[0], BF16)
    for i in range(depth):
        z = _mm(xn, w_in, i, tm=1024, tn=1280, vmem_mib=56)
        ssm_p, fin_p = _s5_prompt(z, nb_p, seq, tabs, g_a, g_b, i, hb)
        ssm_s, fin_s = _s5_sample(z, mp, nb_s, n_new, tabs, g_a, g_b,
                                  _state_to_lanes(state_ssm_re[i], state_ssm_im[i]), i, hb)
        att_p = _attn_prompt(z, nb_p, seq, d_ssm, d_attn, rel_bias, attn_sinks[i])
        att_s = _attn_sample(z, mp, nb_s, n_new, d_ssm, d_attn, cache_k, cache_v, i, rel_bias,
                             attn_sinks[i])
        merged = _merge_norm(ssm_p, ssm_s, att_p, att_s, norm_grp_ssm[i], norm_grp_attn[i])
        if i % 2 == 0:
            h, hn = _mm_rows(merged, w_out, i, h, tm=512, gain=norm_ffn[i])
        else:
            h = _mm_rows(merged, w_out, i, h, tm=512)

        kv_p.append(jnp.stack([z[(b + 1) * seq - win:(b + 1) * seq, d_ssm + d_attn:] for b in range(nb_p)]))
        kv_s.append(z[mp:, d_ssm + d_attn:])
        fin_ps.append(fin_p.reshape(hb, nb_p, -1))
        fin_ss.append(fin_s)

        j = i // 2
        if i % 2 == 0:
            t = _swiglu(hn, ffn_w_gate, ffn_w_up, j, tm=1024, tn=512)
            h = _mm(t, ffn_w_down, j, tm=512, tn=512, res=h, vmem_mib=52)
        else:
            hn, e_idx, gates = _rmsnorm_router(h, norm_ffn[i], router_w[j])
            h = h + _moe_ffn(hn, e_idx, gates, moe_w_gate, moe_w_up, moe_w_down, j)

        hn = _rmsnorm(h, norm_ple[i], BF16)
        if i + 1 < depth:
            h, xn = _ple_rows(hn, p_p, p_s, ple_w_gate, ple_w_proj, h, i, norm_mix[i + 1], tm=512)
        else:
            h = _ple(hn, p_p, p_s, ple_w_gate, ple_w_proj, h, i, tm=1024, tn=1024, vmem_mib=58)

    y_p, y_s = _rmsnorm_split(h, norm_final, mp)
    kv_p = jnp.stack(kv_p).reshape(depth, nb_p, win, 2, KV_HEADS, HEAD_DIM)
    kv_s = jnp.stack(kv_s).reshape(depth, nb_s, n_new, 2, KV_HEADS, HEAD_DIM)
    p_re, p_im = _state_from_lanes(jnp.stack(fin_ps))
    s_re, s_im = _state_from_lanes(jnp.stack(fin_ss))
    return (y_p.reshape(nb_p, seq, d), y_s.reshape(nb_s, n_new, d),
            kv_p[:, :, :, 0], kv_p[:, :, :, 1], p_re, p_im,
            kv_s[:, :, :, 0], kv_s[:, :, :, 1], s_re, s_im)
```
